```python
import math
import jax, jax.numpy as jnp
from jax import lax
import numpy as np

D_MODEL = 1024
BATCH = 4
SEQ = 4096
DEPTH = 1

HEAD_DIM = 64
HEADS_PER_GROUP = 8
DILATION_GROUPS = ((128, 1), (512, 4), (2048, 16))
N_GROUPS = len(DILATION_GROUPS)
ATTN_QKV_WIDTH = N_GROUPS * HEADS_PER_GROUP * HEAD_DIM
ATTN_OUT_WIDTH = HEADS_PER_GROUP * HEAD_DIM
BAND_BLOCK = 128
ROPE_THETA = 10000.0
LRU_WIDTH = 1024
LRU_BLOCKS = 16
LRU_BLOCK_WIDTH = LRU_WIDTH // LRU_BLOCKS
CONV_WIDTH = 4
LRU_C = 8.0
N_BRANCHES = 2
SPLIT_SIZES = (ATTN_QKV_WIDTH, ATTN_QKV_WIDTH, ATTN_QKV_WIDTH, ATTN_OUT_WIDTH,
               LRU_WIDTH, LRU_WIDTH, N_BRANCHES * D_MODEL)
IN_WIDTH = sum(SPLIT_SIZES)
SPLIT_POINTS = [int(v) for v in np.cumsum(SPLIT_SIZES)[:-1]]
DEEPNORM_ALPHA = (2.0 * DEPTH) ** 0.25
DEEPNORM_BETA = (8.0 * DEPTH) ** -0.25
LN_EPS = 1e-5
NEG_INF = -1e30

kernel_name = "hybrid_dilated_attn_rglru_deepnorm"


def layer_norm(x, gain, bias):
    xf = x.astype(jnp.float32)
    mu = jnp.mean(xf, axis=-1, keepdims=True)
    var = jnp.mean(jnp.square(xf - mu), axis=-1, keepdims=True)
    y = (xf - mu) * lax.rsqrt(var + LN_EPS) * gain.astype(jnp.float32) + bias.astype(jnp.float32)
    return y.astype(x.dtype)


def rope(x):
    S = x.shape[1]
    half = HEAD_DIM // 2
    inv_freq = ROPE_THETA ** (-jnp.arange(half, dtype=jnp.float32) / half)
    ang = jnp.arange(S, dtype=jnp.float32)[:, None] * inv_freq[None, :]
    cos = jnp.cos(ang)[None, :, None, :]
    sin = jnp.sin(ang)[None, :, None, :]
    xf = x.astype(jnp.float32)
    x1, x2 = xf[..., :half], xf[..., half:]
    return jnp.concatenate([x1 * cos - x2 * sin, x2 * cos + x1 * sin], axis=-1)


def dilated_band_attention(q, k, v, window, dilation):
    B, S, H, Dh = q.shape
    span = window // dilation
    chunk = dilation * BAND_BLOCK
    Sp = -(-S // chunk) * chunk
    M = Sp // dilation
    nb = M // BAND_BLOCK

    def to_blocks(t):
        t = jnp.pad(t, ((0, 0), (0, Sp - S), (0, 0), (0, 0)))
        t = t.reshape(B, M, dilation, H, Dh).transpose(0, 2, 1, 3, 4)
        return t.reshape(B, dilation, nb, BAND_BLOCK, H, Dh)

    qb, kb, vb = to_blocks(q), to_blocks(k), to_blocks(v)

    def with_prev(t):
        prev = jnp.concatenate([jnp.zeros_like(t[:, :, :1]), t[:, :, :-1]], axis=2)
        return jnp.concatenate([prev, t], axis=3)

    kw, vw = with_prev(kb), with_prev(vb)
    s = jnp.einsum('brnqhd,brnkhd->brnhqk', qb, kw) * (Dh ** -0.5)
    qi = jnp.arange(BAND_BLOCK)[:, None]
    ki = jnp.arange(2 * BAND_BLOCK)[None, :]
    dist = qi + BAND_BLOCK - ki
    band = (dist >= 0) & (dist <= span)
    not_first = jnp.arange(nb)[:, None, None] > 0
    valid = band[None] & (not_first | (ki >= BAND_BLOCK)[None])
    s = jnp.where(valid[None, None, :, None], s, NEG_INF)
    m = jnp.max(s, axis=-1, keepdims=True)
    p = jnp.exp(s - m)
    l = jnp.sum(p, axis=-1, keepdims=True)
    o = jnp.einsum('brnhqk,brnkhd->brnqhd', p, vw)
    l_q = jnp.transpose(l[..., 0], (0, 1, 2, 4, 3))
    lse = jnp.transpose((m + jnp.log(l))[..., 0], (0, 1, 2, 4, 3))
    o = o / l_q[..., None]

    def from_blocks(t):
        tail = t.shape[5:]
        t = t.reshape((B, dilation, M, H) + tail)
        t = jnp.moveaxis(t, 1, 2).reshape((B, Sp, H) + tail)
        return t[:, :S]

    return from_blocks(o), from_blocks(lse)


def causal_depthwise_conv(x, w, b):
    S = x.shape[1]
    xp = jnp.pad(x, ((0, 0), (CONV_WIDTH - 1, 0), (0, 0)))
    y = b
    for j in range(CONV_WIDTH):
        y = y + xp[:, j:j + S] * w[j]
    return y


def rg_lru(x, w_r, b_r, w_i, b_i, lam):
    B, S, C = x.shape
    xb = x.reshape(B, S, LRU_BLOCKS, LRU_BLOCK_WIDTH)
    r = jax.nn.sigmoid((jnp.einsum('bsgi,gij->bsgj', xb, w_r).reshape(B, S, C) + b_r).astype(jnp.float32))
    i = jax.nn.sigmoid((jnp.einsum('bsgi,gij->bsgj', xb, w_i).reshape(B, S, C) + b_i).astype(jnp.float32))
    log_a = -LRU_C * r * jax.nn.softplus(-lam.astype(jnp.float32))
    a = jnp.exp(log_a)
    mult = jnp.sqrt(jnp.maximum(-jnp.expm1(2.0 * log_a), 0.0))
    u = mult * (i * x.astype(jnp.float32))

    def combine(e1, e2):
        a1, b1 = e1
        a2, b2 = e2
        return a1 * a2, a2 * b1 + b2

    _, h = lax.associative_scan(combine, (a, u), axis=1)
    return h


def setup_inputs(seed: int = 0) -> dict:
    key = jax.random.key(seed)
    ks = jax.random.split(key, 16)
    L, D = DEPTH, D_MODEL
    x = jax.random.normal(ks[0], (BATCH, SEQ, D), jnp.float32)
    w_in = jax.random.normal(ks[1], (L, D, IN_WIDTH), jnp.float32) * D ** -0.5
    v_lo, v_hi = 2 * ATTN_QKV_WIDTH, 3 * ATTN_QKV_WIDTH
    col_scale = jnp.ones((IN_WIDTH,), jnp.float32).at[v_lo:v_hi].set(DEEPNORM_BETA)
    w_in = w_in * col_scale
    b_in = 0.02 * jax.random.normal(ks[2], (L, IN_WIDTH), jnp.float32)
    conv_w = jax.random.normal(ks[3], (L, CONV_WIDTH, LRU_WIDTH), jnp.float32) * CONV_WIDTH ** -0.5
    conv_b = 0.02 * jax.random.normal(ks[4], (L, LRU_WIDTH), jnp.float32)
    lru_wr = jax.random.normal(ks[5], (L, LRU_BLOCKS, LRU_BLOCK_WIDTH, LRU_BLOCK_WIDTH), jnp.float32) * LRU_BLOCK_WIDTH ** -0.5
    lru_br = 0.02 * jax.random.normal(ks[6], (L, LRU_WIDTH), jnp.float32)
    lru_wi = jax.random.normal(ks[7], (L, LRU_BLOCKS, LRU_BLOCK_WIDTH, LRU_BLOCK_WIDTH), jnp.float32) * LRU_BLOCK_WIDTH ** -0.5
    lru_bi = 0.02 * jax.random.normal(ks[8], (L, LRU_WIDTH), jnp.float32)
    a_c = jax.random.uniform(ks[9], (L, LRU_WIDTH), jnp.float32, 0.9, 0.999)
    a0 = a_c ** (1.0 / LRU_C)
    lru_lambda = jnp.log(a0) - jnp.log1p(-a0)
    w_attn_proj = jax.random.normal(ks[10], (L, ATTN_OUT_WIDTH, D), jnp.float32) * ATTN_OUT_WIDTH ** -0.5 * DEEPNORM_BETA
    w_lru_proj = jax.random.normal(ks[11], (L, LRU_WIDTH, D), jnp.float32) * LRU_WIDTH ** -0.5 * DEEPNORM_BETA
    w_out = jax.random.normal(ks[12], (L, D, D), jnp.float32) * D ** -0.5 * DEEPNORM_BETA
    b_out = 0.02 * jax.random.normal(ks[13], (L, D), jnp.float32)
    ln_gain = 1.0 + 0.02 * jax.random.normal(ks[14], (L, D), jnp.float32)
    ln_bias = 0.02 * jax.random.normal(ks[15], (L, D), jnp.float32)
    return {"x": x, "w_in": w_in, "b_in": b_in, "conv_w": conv_w, "conv_b": conv_b,
            "lru_wr": lru_wr, "lru_br": lru_br, "lru_wi": lru_wi, "lru_bi": lru_bi,
            "lru_lambda": lru_lambda, "w_attn_proj": w_attn_proj, "w_lru_proj": w_lru_proj,
            "w_out": w_out, "b_out": b_out, "ln_gain": ln_gain, "ln_bias": ln_bias}


def reference(x, w_in, b_in, conv_w, conv_b, lru_wr, lru_br, lru_wi, lru_bi, lru_lambda,
              w_attn_proj, w_lru_proj, w_out, b_out, ln_gain, ln_bias):
    B, S, D = x.shape
    dt = x.dtype
    for l in range(DEPTH):
        proj = x @ w_in[l] + b_in[l]
        q, k, v, g_attn, u_lru, g_lru, gate_logits = jnp.split(proj, SPLIT_POINTS, axis=-1)

        q = rope(q.reshape(B, S, N_GROUPS * HEADS_PER_GROUP, HEAD_DIM)).reshape(B, S, N_GROUPS, HEADS_PER_GROUP, HEAD_DIM)
        k = rope(k.reshape(B, S, N_GROUPS * HEADS_PER_GROUP, HEAD_DIM)).reshape(B, S, N_GROUPS, HEADS_PER_GROUP, HEAD_DIM)
        v = v.astype(jnp.float32).reshape(B, S, N_GROUPS, HEADS_PER_GROUP, HEAD_DIM)
        outs, lses = [], []
        for g, (window, dilation) in enumerate(DILATION_GROUPS):
            o_g, lse_g = dilated_band_attention(q[:, :, g], k[:, :, g], v[:, :, g], window, dilation)
            outs.append(o_g)
            lses.append(lse_g)
        wts = jax.nn.softmax(jnp.stack(lses, axis=0), axis=0)
        attn = jnp.sum(wts[..., None] * jnp.stack(outs, axis=0), axis=0)
        attn = attn.reshape(B, S, ATTN_OUT_WIDTH).astype(dt) * jax.nn.silu(g_attn)
        y_a = attn @ w_attn_proj[l]

        u = causal_depthwise_conv(u_lru, conv_w[l], conv_b[l])
        h = rg_lru(u, lru_wr[l], lru_br[l], lru_wi[l], lru_bi[l], lru_lambda[l]).astype(dt)
        y_b = (h * jax.nn.silu(g_lru)) @ w_lru_proj[l]

        gates = jax.nn.sigmoid(gate_logits).reshape(B, S, N_BRANCHES, D)
        merged = gates[:, :, 0] * y_a + gates[:, :, 1] * y_b
        out = merged @ w_out[l] + b_out[l]
        x = layer_norm(DEEPNORM_ALPHA * x + out, ln_gain[l], ln_bias[l])
    return x
```

```python
import functools
import math

import jax
import jax.numpy as jnp
import numpy as np
from jax import lax
from jax.experimental import pallas as pl
from jax.experimental.pallas import tpu as pltpu

LANES = 128
SUBLANES = 8
VMEM_LIMIT_BYTES = 56 * 1024 * 1024

HEAD_DIM = 64
HALF = HEAD_DIM // 2
HEADS_PER_GROUP = 8
DILATIONS = (1, 4, 16)
N_GROUPS = len(DILATIONS)
BAND = 128
CLASSES = 16
QKV_W = N_GROUPS * HEADS_PER_GROUP * HEAD_DIM
ATTN_W = HEADS_PER_GROUP * HEAD_DIM
N_PAIRS = ATTN_W // LANES
LRU_W = 1024
LRU_BLOCK = 64
LRU_SLAB = 256
CONV_W = 4
LRU_C = 8.0
ROPE_THETA = 10000.0
NEG_INF = -1e30
LN_EPS = 1e-5
DEPTH = 1
ALPHA = (2.0 * DEPTH) ** 0.25
ATTN_COLS = 3 * N_GROUPS * LANES + LANES

BF16 = jnp.bfloat16
F32 = jnp.float32


def _cparams(n_axes):
    return pltpu.CompilerParams(dimension_semantics=("arbitrary",) * n_axes,
                                vmem_limit_bytes=VMEM_LIMIT_BYTES)


def _permute_cast_kernel(x_ref, xp_ref, xn_ref, *, rows_per_class):
    for c in range(CLASSES):
        xp_ref[c * rows_per_class:(c + 1) * rows_per_class, :] = (
            x_ref[pl.ds(c, rows_per_class, stride=CLASSES), :].astype(BF16))
    xn_ref[...] = x_ref[...].astype(BF16)


def _permute_cast(x):
    B, S, D = x.shape
    kern = functools.partial(_permute_cast_kernel, rows_per_class=S // CLASSES)
    spec = pl.BlockSpec((None, S, LANES), lambda b, j: (b, 0, j))
    return pl.pallas_call(
        kern,
        grid=(B, D // LANES),
        in_specs=[spec],
        out_specs=[spec, spec],
        out_shape=[jax.ShapeDtypeStruct((B, S, D), BF16)] * 2,
        compiler_params=_cparams(2),
        name="permute_cast",
    )(x)


def _band_bias(S):
    out = np.zeros((N_GROUPS, 2, 2 * BAND, 2 * BAND), np.float32)
    i = np.arange(BAND)
    for g, d in enumerate(DILATIONS):
        R = CLASSES // d
        L = BAND // R
        m = R * (i % L) + i // L
        m_k = np.concatenate([m - BAND, m])
        dist = m[:, None] - m_k[None, :]
        band = (dist >= 0) & (dist <= BAND)
        for first in range(2):
            valid = band & ((np.arange(2 * BAND) >= BAND)[None, :] | (first == 0))
            bias = np.where(valid, 0.0, NEG_INF).astype(np.float32)
            out[g, first] = np.concatenate([bias, bias], axis=0)
    return out


def _attention_kernel(x_ref, w_ref, b_ref, cos_ref, sin_ref, bias_ref, o_ref,
                      q_s, k_s, v_s, acc_s, m_s, l_s, *, S, row_chunk):
    A = S // CLASSES
    n_tiles = S // BAND
    lane = lax.broadcasted_iota(jnp.int32, (BAND, LANES), 1)
    head0_q = ((lane // HALF) % 2) == 0
    head0_v = lane < HEAD_DIM

    for g, d in enumerate(DILATIONS):
        R = CLASSES // d
        L = BAND // R
        nb = S // (d * BAND)
        col0 = g * 3 * LANES

        def proj_chunk(rc, carry, col0=col0):
            r0 = pl.multiple_of(rc * row_chunk, row_chunk)
            rows = pl.ds(r0, row_chunk)
            res = jnp.dot(x_ref[rows, :], w_ref[:, col0:col0 + 3 * LANES],
                          preferred_element_type=F32) + b_ref[:, col0:col0 + 3 * LANES]
            cos = cos_ref[rows, :]
            sin = sin_ref[rows, :]
            q = res[:, :LANES]
            k = res[:, LANES:2 * LANES]
            q = (q * cos + pltpu.roll(q, LANES // 2, axis=1) * sin) * (HEAD_DIM ** -0.5)
            k = k * cos + pltpu.roll(k, LANES // 2, axis=1) * sin
            q_s[rows, :] = q
            k_s[rows, :] = k
            v_s[rows, :] = res[:, 2 * LANES:]
            return carry

        lax.fori_loop(0, S // row_chunk, proj_chunk, 0)

        def tile_body(e, carry, d=d, R=R, L=L, nb=nb, g=g):
            rho = lax.shift_right_logical(e, int(math.log2(nb)))
            n = lax.bitwise_and(e, nb - 1)
            n_prev = jnp.maximum(n - 1, 0)

            def run(j, nn):
                return pl.ds(pl.multiple_of((rho + d * j) * A + L * nn, SUBLANES), L)

            def gather(ref, nn):
                return jnp.concatenate([ref[run(j, nn), :] for j in range(R)], axis=0)

            q_t = gather(q_s, n)
            zero = jnp.zeros_like(q_t)
            qm = jnp.concatenate([jnp.where(head0_q, q_t, zero),
                                  jnp.where(head0_q, zero, q_t)], axis=0).astype(BF16)
            kwin = jnp.concatenate([gather(k_s, n_prev), gather(k_s, n)], axis=0).astype(BF16)
            vwin = jnp.concatenate([gather(v_s, n_prev), gather(v_s, n)], axis=0).astype(BF16)

            s = lax.dot_general(qm, kwin, (((1,), (1,)), ((), ())),
                                preferred_element_type=F32)
            s = s + bias_ref[g, 1 - jnp.minimum(n, 1)]
            m_blk = jnp.max(s, axis=1, keepdims=True)
            p = jnp.exp(s - m_blk)
            l_blk = jnp.sum(p, axis=1, keepdims=True)
            pv = jnp.dot(p.astype(BF16), vwin, preferred_element_type=F32)
            o_t = jnp.where(head0_v, pv[:BAND], pv[BAND:])
            m_t = jnp.where(head0_v, m_blk[:BAND], m_blk[BAND:])
            l_t = jnp.where(head0_v, l_blk[:BAND], l_blk[BAND:])

            if g > 0:
                m_old = gather(m_s, n)
                diff = m_old - m_t
                decay = jnp.exp(-jnp.abs(diff))
                keep_old = diff >= 0
                w_old = jnp.where(keep_old, 1.0, decay)
                w_new = jnp.where(keep_old, decay, 1.0)
                m_t = jnp.maximum(m_old, m_t)
                l_t = gather(l_s, n) * w_old + l_t * w_new
                o_t = gather(acc_s, n) * w_old + o_t * w_new
            for j in range(R):
                m_s[run(j, n), :] = m_t[j * L:(j + 1) * L]
                l_s[run(j, n), :] = l_t[j * L:(j + 1) * L]
                acc_s[run(j, n), :] = o_t[j * L:(j + 1) * L]
            return carry

        lax.fori_loop(0, n_tiles, tile_body, 0)

    gate_col = 3 * N_GROUPS * LANES
    for c in range(CLASSES):
        rows = slice(c * A, (c + 1) * A)
        gate = jnp.dot(x_ref[rows, :], w_ref[:, gate_col:gate_col + LANES],
                       preferred_element_type=F32) + b_ref[:, gate_col:gate_col + LANES]
        res = (acc_s[rows, :] / l_s[rows, :]) * (gate * jax.nn.sigmoid(gate))
        o_ref[pl.ds(c, A, stride=CLASSES), :] = res


def _attention(xp, w_attn, b_attn, cos_t, sin_t, bias):
    B, S, D = xp.shape
    kern = functools.partial(_attention_kernel, S=S, row_chunk=512)
    scratch = [pltpu.VMEM((S, LANES), F32) for _ in range(6)]
    return pl.pallas_call(
        kern,
        grid=(B, N_PAIRS),
        in_specs=[
            pl.BlockSpec((None, S, D), lambda b, p: (b, 0, 0)),
            pl.BlockSpec((None, D, ATTN_COLS), lambda b, p: (p, 0, 0)),
            pl.BlockSpec((None, 1, ATTN_COLS), lambda b, p: (p, 0, 0)),
            pl.BlockSpec((S, LANES), lambda b, p: (0, 0)),
            pl.BlockSpec((S, LANES), lambda b, p: (0, 0)),
            pl.BlockSpec((N_GROUPS, 2, 2 * BAND, 2 * BAND), lambda b, p: (0, 0, 0, 0)),
        ],
        out_specs=pl.BlockSpec((None, S, LANES), lambda b, p: (b, 0, p)),
        out_shape=jax.ShapeDtypeStruct((B, S, ATTN_W), F32),
        scratch_shapes=scratch,
        compiler_params=_cparams(2),
        name="attention",
    )(xp, w_attn, b_attn, cos_t, sin_t, bias)


def _attention_params(w_in, b_in, S):
    l = np.arange(LANES)
    rope_perm = ((l // HALF) % 2) * HEAD_DIM + (l // HEAD_DIM) * HALF + l % HALF
    cols = []
    for p in range(N_PAIRS):
        c = []
        for g in range(N_GROUPS):
            base = g * ATTN_W + p * LANES
            c += [base + rope_perm, QKV_W + base + rope_perm, 2 * QKV_W + base + l]
        c.append(3 * QKV_W + p * LANES + l)
        cols.append(np.concatenate(c))
    cols = np.stack(cols)
    w = jnp.transpose(w_in[:, cols], (1, 0, 2)).astype(BF16)
    b = b_in[cols][:, None, :]

    A = S // CLASSES
    r = np.arange(S)
    pos = jnp.asarray(CLASSES * (r % A) + r // A, F32)
    inv_freq = ROPE_THETA ** (-jnp.arange(HALF, dtype=F32) / HALF)
    ang = pos[:, None] * inv_freq[None, :]
    cos_t = jnp.tile(jnp.cos(ang), (1, LANES // HALF))
    sin = jnp.sin(ang)
    sin_t = jnp.concatenate([-sin, -sin, sin, sin], axis=1)
    return w, b, cos_t, sin_t


def _rglru_kernel(x_ref, w_ref, b_ref, cw_ref, cb_ref, wg_ref, bg_ref, lam_ref, o_ref,
                  u_pad, gate_s, a_s, h_s, *, S, row_chunk):
    C = LRU_SLAB
    n_chunks = S // row_chunk
    u_pad[0:SUBLANES, :] = jnp.zeros((SUBLANES, C), F32)

    def proj_chunk(rc, carry):
        r0 = pl.multiple_of(rc * row_chunk, row_chunk)
        res = jnp.dot(x_ref[pl.ds(r0, row_chunk), :], w_ref[...],
                      preferred_element_type=F32) + b_ref[...]
        u_pad[pl.ds(r0 + SUBLANES, row_chunk), :] = res[:, :C]
        gl = res[:, C:]
        gate_s[pl.ds(r0, row_chunk), :] = gl * jax.nn.sigmoid(gl)
        return carry

    lax.fori_loop(0, n_chunks, proj_chunk, 0)

    neg_lam = -lam_ref[...]
    softplus = jnp.maximum(neg_lam, 0.0) + jnp.log1p(jnp.exp(-jnp.abs(neg_lam)))
    sub = lax.broadcasted_iota(jnp.int32, (row_chunk // SUBLANES, SUBLANES, C), 1)

    def gate_chunk(rc, carry):
        r0 = pl.multiple_of(rc * row_chunk, row_chunk)
        win = u_pad[pl.ds(r0, row_chunk + SUBLANES), :]
        uc = cb_ref[...]
        for j in range(CONV_W):
            off = SUBLANES - (CONV_W - 1) + j
            uc = uc + win[off:off + row_chunk, :] * cw_ref[j:j + 1, :]
        gates = jnp.dot(uc.astype(BF16), wg_ref[...], preferred_element_type=F32) + bg_ref[...]
        r = jax.nn.sigmoid(gates[:, :C])
        i = jax.nn.sigmoid(gates[:, C:])
        log_a = -LRU_C * r * softplus
        a = jnp.exp(log_a)
        mult = jnp.sqrt(jnp.maximum(-jnp.tanh(log_a) * (a * a + 1.0), 0.0))
        u = mult * (i * uc)
        a3 = a.reshape(row_chunk // SUBLANES, SUBLANES, C)
        u3 = u.reshape(row_chunk // SUBLANES, SUBLANES, C)
        for k in (1, 2, 4):
            keep = sub >= k
            a_sh = jnp.where(keep, pltpu.roll(a3, k, axis=1), 1.0)
            u_sh = jnp.where(keep, pltpu.roll(u3, k, axis=1), 0.0)
            u3 = a3 * u_sh + u3
            a3 = a3 * a_sh
        a_s[pl.ds(r0, row_chunk), :] = a3.reshape(row_chunk, C)
        h_s[pl.ds(r0, row_chunk), :] = u3.reshape(row_chunk, C)
        return carry

    lax.fori_loop(0, n_chunks, gate_chunk, 0)

    def carry_step(gi, h_prev):
        rows = pl.ds(pl.multiple_of(gi * SUBLANES, SUBLANES), SUBLANES)
        h = a_s[rows, :] * h_prev + h_s[rows, :]
        h_s[rows, :] = h
        return jnp.broadcast_to(h[SUBLANES - 1:SUBLANES, :], (SUBLANES, C))

    lax.fori_loop(0, S // SUBLANES, carry_step, jnp.zeros((SUBLANES, C), F32), unroll=4)

    def out_chunk(rc, carry):
        rows = pl.ds(pl.multiple_of(rc * row_chunk, row_chunk), row_chunk)
        o_ref[rows, :] = (h_s[rows, :] * gate_s[rows, :]).astype(BF16)
        return carry

    lax.fori_loop(0, n_chunks, out_chunk, 0)


def _rglru(xn, w_lru, b_lru, conv_w, conv_b, w_gate, b_gate, lam):
    B, S, D = xn.shape
    n_slabs = LRU_W // LRU_SLAB
    kern = functools.partial(_rglru_kernel, S=S, row_chunk=512)
    return pl.pallas_call(
        kern,
        grid=(B, n_slabs),
        in_specs=[
            pl.BlockSpec((None, S, D), lambda b, c: (b, 0, 0)),
            pl.BlockSpec((None, D, 2 * LRU_SLAB), lambda b, c: (c, 0, 0)),
            pl.BlockSpec((None, 1, 2 * LRU_SLAB), lambda b, c: (c, 0, 0)),
            pl.BlockSpec((CONV_W, LRU_SLAB), lambda b, c: (0, c)),
            pl.BlockSpec((1, LRU_SLAB), lambda b, c: (0, c)),
            pl.BlockSpec((None, LRU_SLAB, 2 * LRU_SLAB), lambda b, c: (c, 0, 0)),
            pl.BlockSpec((None, 1, 2 * LRU_SLAB), lambda b, c: (c, 0, 0)),
            pl.BlockSpec((1, LRU_SLAB), lambda b, c: (0, c)),
        ],
        out_specs=pl.BlockSpec((None, S, LRU_SLAB), lambda b, c: (b, 0, c)),
        out_shape=jax.ShapeDtypeStruct((B, S, LRU_W), BF16),
        scratch_shapes=[
            pltpu.VMEM((S + SUBLANES, LRU_SLAB), F32),
            pltpu.VMEM((S, LRU_SLAB), F32),
            pltpu.VMEM((S, LRU_SLAB), F32),
            pltpu.VMEM((S, LRU_SLAB), F32),
        ],
        compiler_params=_cparams(2),
        name="rglru",
    )(xn, w_lru, b_lru, conv_w, conv_b, w_gate, b_gate, lam)


def _rglru_params(w_in, b_in, lru_wr, lru_br, lru_wi, lru_bi):
    n_slabs = LRU_W // LRU_SLAB
    u0 = 3 * QKV_W + ATTN_W
    g0 = u0 + LRU_W
    sl = np.arange(LRU_SLAB)
    cols = np.stack([np.concatenate([u0 + c * LRU_SLAB + sl, g0 + c * LRU_SLAB + sl])
                     for c in range(n_slabs)])
    w = jnp.transpose(w_in[:, cols], (1, 0, 2)).astype(BF16)
    b = b_in[cols][:, None, :]
    per = LRU_SLAB // LRU_BLOCK

    def block_diag(wb):
        wb = wb.reshape(n_slabs, per, LRU_BLOCK, LRU_BLOCK)
        eye = jnp.eye(per, dtype=wb.dtype)
        full = wb[:, :, :, None, :] * eye[None, :, None, :, None]
        return full.reshape(n_slabs, LRU_SLAB, LRU_SLAB)

    w_gate = jnp.concatenate([block_diag(lru_wr), block_diag(lru_wi)], axis=2).astype(BF16)
    b_gate = jnp.concatenate([lru_br.reshape(n_slabs, 1, LRU_SLAB),
                              lru_bi.reshape(n_slabs, 1, LRU_SLAB)], axis=2)
    return w, b, w_gate, b_gate


def _merge_out_kernel(x_ref, attn_ref, hg_ref, wg_ref, bg_ref, wa_ref, wl_ref, wo_ref, bo_ref,
                      lng_ref, lnb_ref, o_ref):
    D = x_ref.shape[1]
    x = x_ref[...]
    gates = jax.nn.sigmoid(
        jnp.dot(x.astype(BF16), wg_ref[...], preferred_element_type=F32) + bg_ref[...])
    y_a = jnp.dot(attn_ref[...].astype(BF16), wa_ref[...], preferred_element_type=F32)
    y_b = jnp.dot(hg_ref[...], wl_ref[...], preferred_element_type=F32)
    merged = gates[:, :D] * y_a + gates[:, D:] * y_b
    out = jnp.dot(merged.astype(BF16), wo_ref[...], preferred_element_type=F32) + bo_ref[...]
    y = ALPHA * x + out
    mu = jnp.mean(y, axis=1, keepdims=True)
    yc = y - mu
    var = jnp.mean(yc * yc, axis=1, keepdims=True)
    o_ref[...] = yc * lax.rsqrt(var + LN_EPS) * lng_ref[...] + lnb_ref[...]


def _merge_out(x2, attn2, hg2, w_g, b_g, w_a, w_l, w_o, b_o, ln_g, ln_b, tm=512):
    T, D = x2.shape
    const = lambda shape: pl.BlockSpec(shape, lambda i: (0,) * len(shape))
    return pl.pallas_call(
        _merge_out_kernel,
        grid=(T // tm,),
        in_specs=[
            pl.BlockSpec((tm, D), lambda i: (i, 0)),
            pl.BlockSpec((tm, ATTN_W), lambda i: (i, 0)),
            pl.BlockSpec((tm, LRU_W), lambda i: (i, 0)),
            const((D, 2 * D)), const((1, 2 * D)),
            const((ATTN_W, D)), const((LRU_W, D)), const((D, D)), const((1, D)),
            const((1, D)), const((1, D)),
        ],
        out_specs=pl.BlockSpec((tm, D), lambda i: (i, 0)),
        out_shape=jax.ShapeDtypeStruct((T, D), F32),
        compiler_params=_cparams(1),
        name="merge_out",
    )(x2, attn2, hg2, w_g, b_g, w_a, w_l, w_o, b_o, ln_g, ln_b)


def kernel(x, w_in, b_in, conv_w, conv_b, lru_wr, lru_br, lru_wi, lru_bi, lru_lambda,
           w_attn_proj, w_lru_proj, w_out, b_out, ln_gain, ln_bias):
    B, S, D = x.shape
    assert w_in.shape[0] == DEPTH and S % (CLASSES * BAND) == 0 and D == 1024
    for l in range(DEPTH):
        wl, bl = w_in[l], b_in[l]
        xp, xn = _permute_cast(x)

        w_attn, b_attn, cos_t, sin_t = _attention_params(wl, bl, S)
        attn = _attention(xp, w_attn, b_attn, cos_t, sin_t, jnp.asarray(_band_bias(S)))

        w_lru, b_lru, w_gate, b_gate = _rglru_params(wl, bl, lru_wr[l], lru_br[l],
                                                    lru_wi[l], lru_bi[l])
        hg = _rglru(xn, w_lru, b_lru, conv_w[l], conv_b[l][None, :], w_gate, b_gate,
                    lru_lambda[l][None, :])

        g0 = 3 * QKV_W + ATTN_W + 2 * LRU_W
        out = _merge_out(
            x.reshape(B * S, D), attn.reshape(B * S, ATTN_W), hg.reshape(B * S, LRU_W),
            wl[:, g0:].astype(BF16), bl[None, g0:],
            w_attn_proj[l].astype(BF16), w_lru_proj[l].astype(BF16), w_out[l].astype(BF16),
            b_out[l][None, :], ln_gain[l][None, :], ln_bias[l][None, :])
        x = out.reshape(B, S, D)
    return x
```

```python
import functools
import math

import jax
import jax.numpy as jnp
import numpy as np
from jax import lax
from jax.experimental import pallas as pl
from jax.experimental.pallas import tpu as pltpu

LANES = 128
SUBLANES = 8
BF16_ROWS = 16
VMEM_LIMIT_BYTES = 56 * 1024 * 1024

HEAD_DIM = 64
HALF = HEAD_DIM // 2
HEADS_PER_GROUP = 8
DILATIONS = (1, 4, 16)
N_GROUPS = len(DILATIONS)
BAND = 128
CLASSES = 16
QKV_W = N_GROUPS * HEADS_PER_GROUP * HEAD_DIM
ATTN_W = HEADS_PER_GROUP * HEAD_DIM
N_PAIRS = ATTN_W // LANES
LRU_W = 1024
LRU_BLOCK = 64
LRU_SLAB = 256
CONV_W = 4
LRU_C = 8.0
ROPE_THETA = 10000.0
LOG2_E = math.log2(math.e)
NEG_INF = -1e30
LN_EPS = 1e-5
DEPTH = 1
ALPHA = (2.0 * DEPTH) ** 0.25
ATTN_COLS = 3 * N_GROUPS * LANES + LANES

BF16 = jnp.bfloat16
F32 = jnp.float32


def _cparams(n_axes):
    return pltpu.CompilerParams(dimension_semantics=("arbitrary",) * n_axes,
                                vmem_limit_bytes=VMEM_LIMIT_BYTES)


def _permute_cast_kernel(x_ref, xp_ref, xn_ref, *, rows_per_class):
    for c in range(CLASSES):
        xp_ref[c * rows_per_class:(c + 1) * rows_per_class, :] = (
            x_ref[pl.ds(c, rows_per_class, stride=CLASSES), :].astype(BF16))
    xn_ref[...] = x_ref[...].astype(BF16)


def _permute_cast(x):
    B, S, D = x.shape
    kern = functools.partial(_permute_cast_kernel, rows_per_class=S // CLASSES)
    spec = pl.BlockSpec((None, S, LANES), lambda b, j: (b, 0, j))
    return pl.pallas_call(
        kern,
        grid=(B, D // LANES),
        in_specs=[spec],
        out_specs=[spec, spec],
        out_shape=[jax.ShapeDtypeStruct((B, S, D), BF16)] * 2,
        compiler_params=_cparams(2),
        name="permute_cast",
    )(x)


def _band_bias(S):
    out = np.zeros((N_GROUPS, 2, 2 * BAND, 2 * BAND), np.float32)
    i = np.arange(BAND)
    for g, d in enumerate(DILATIONS):
        R = CLASSES // d
        L = BAND // R
        m = R * (i % L) + i // L
        m_k = np.concatenate([m - BAND, m])
        dist = m[:, None] - m_k[None, :]
        band = (dist >= 0) & (dist <= BAND)
        for first in range(2):
            valid = band & ((np.arange(2 * BAND) >= BAND)[None, :] | (first == 0))
            bias = np.where(valid, 0.0, NEG_INF).astype(np.float32)
            out[g, first] = np.concatenate([bias, bias], axis=0)
    return out


def _attention_kernel(x_ref, w_ref, b_ref, cos_ref, sin_ref, bias_ref, o_ref,
                      qkv0_s, qkv_s, gate_s, acc_s, m_s, l_s, *, S, row_chunk, tile_unroll):
    A = S // CLASSES

    def qkv_ref(g, i):
        return qkv0_s.at[i] if g == 0 else qkv_s.at[3 * (g - 1) + i]

    n_tiles = S // BAND
    lane = lax.broadcasted_iota(jnp.int32, (BAND, LANES), 1)
    head0_q = ((lane // HALF) % 2) == 0
    head0_v = lane < HEAD_DIM
    q_scale = (HEAD_DIM ** -0.5) * LOG2_E

    def proj_chunk(rc, carry):
        rows = pl.ds(pl.multiple_of(rc * row_chunk, row_chunk), row_chunk)
        res = jnp.dot(x_ref[rows, :], w_ref[...], preferred_element_type=F32) + b_ref[...]
        cos = cos_ref[rows, :]
        sin = sin_ref[rows, :]
        for g in range(N_GROUPS):
            col = 3 * g * LANES
            q = res[:, col:col + LANES]
            k = res[:, col + LANES:col + 2 * LANES]
            q = (q * cos + pltpu.roll(q, LANES // 2, axis=1) * sin) * q_scale
            k = k * cos + pltpu.roll(k, LANES // 2, axis=1) * sin
            for i, val in enumerate((q, k, res[:, col + 2 * LANES:col + 3 * LANES])):
                ref = qkv_ref(g, i)
                ref[rows, :] = val.astype(ref.dtype)
        gate = res[:, 3 * N_GROUPS * LANES:]
        gate_s[rows, :] = (gate * jax.nn.sigmoid(gate)).astype(BF16)
        return carry

    lax.fori_loop(0, S // row_chunk, proj_chunk, 0)

    for g, d in enumerate(DILATIONS):
        R = CLASSES // d
        L = BAND // R
        nb = S // (d * BAND)

        def tile_body(e, carry, d=d, R=R, L=L, nb=nb, g=g):
            rho = lax.shift_right_logical(e, int(math.log2(nb)))
            n = lax.bitwise_and(e, nb - 1)
            n_prev = jnp.maximum(n - 1, 0)

            def run(j, nn):
                return pl.ds(pl.multiple_of((rho + d * j) * A + L * nn, min(L, BF16_ROWS)), L)

            def gather(ref, nn):
                return jnp.concatenate([ref[run(j, nn), :] for j in range(R)], axis=0)

            q_t = gather(qkv_ref(g, 0), n).astype(BF16)
            zero = jnp.zeros_like(q_t)
            qm = jnp.concatenate([jnp.where(head0_q, q_t, zero),
                                  jnp.where(head0_q, zero, q_t)], axis=0)
            k_ref = qkv_ref(g, 1)
            v_ref = qkv_ref(g, 2)
            kwin = jnp.concatenate([gather(k_ref, n_prev), gather(k_ref, n)],
                                   axis=0).astype(BF16)
            vwin = jnp.concatenate([gather(v_ref, n_prev), gather(v_ref, n)],
                                   axis=0).astype(BF16)

            s = lax.dot_general(qm, kwin, (((1,), (1,)), ((), ())),
                                preferred_element_type=F32)
            s = s + bias_ref[g, 1 - jnp.minimum(n, 1)]
            m_blk = jnp.max(s, axis=1, keepdims=True)
            p = jnp.exp2(s - m_blk)
            l_blk = jnp.sum(p, axis=1, keepdims=True)
            pv = jnp.dot(p.astype(BF16), vwin, preferred_element_type=F32)
            o_t = jnp.where(head0_v, pv[:BAND], pv[BAND:])
            m_t = jnp.where(head0_v, m_blk[:BAND], m_blk[BAND:])
            l_t = jnp.where(head0_v, l_blk[:BAND], l_blk[BAND:])

            if g > 0:
                m_old = gather(m_s, n)
                m_new = jnp.maximum(m_old, m_t)
                w_old = jnp.exp2(m_old - m_new)
                w_new = jnp.exp2(m_t - m_new)
                m_t = m_new
                l_t = gather(l_s, n) * w_old + l_t * w_new
                o_t = gather(acc_s, n) * w_old + o_t * w_new
            for j in range(R):
                m_s[run(j, n), :] = m_t[j * L:(j + 1) * L]
                l_s[run(j, n), :] = l_t[j * L:(j + 1) * L]
                acc_s[run(j, n), :] = o_t[j * L:(j + 1) * L]
            return carry

        lax.fori_loop(0, n_tiles, tile_body, 0, unroll=tile_unroll)

    for c in range(CLASSES):
        rows = slice(c * A, (c + 1) * A)
        res = (acc_s[rows, :] / l_s[rows, :]) * gate_s[rows, :].astype(F32)
        o_ref[pl.ds(c, A, stride=CLASSES), :] = res


def _attention(xp, w_attn, b_attn, cos_t, sin_t, bias):
    B, S, D = xp.shape
    kern = functools.partial(_attention_kernel, S=S, row_chunk=512, tile_unroll=8)
    once = pl.Buffered(1)
    scratch = [pltpu.VMEM((3, S, LANES), F32), pltpu.VMEM((3 * (N_GROUPS - 1), S, LANES), BF16),
               pltpu.VMEM((S, LANES), BF16)]
    scratch += [pltpu.VMEM((S, LANES), F32) for _ in range(3)]
    return pl.pallas_call(
        kern,
        grid=(B, N_PAIRS),
        in_specs=[
            pl.BlockSpec((None, S, D), lambda b, p: (b, 0, 0), pipeline_mode=once),
            pl.BlockSpec((None, D, ATTN_COLS), lambda b, p: (p, 0, 0)),
            pl.BlockSpec((None, 1, ATTN_COLS), lambda b, p: (p, 0, 0)),
            pl.BlockSpec((S, LANES), lambda b, p: (0, 0), pipeline_mode=once),
            pl.BlockSpec((S, LANES), lambda b, p: (0, 0), pipeline_mode=once),
            pl.BlockSpec((N_GROUPS, 2, 2 * BAND, 2 * BAND), lambda b, p: (0, 0, 0, 0),
                         pipeline_mode=once),
        ],
        out_specs=pl.BlockSpec((None, S, LANES), lambda b, p: (b, 0, p)),
        out_shape=jax.ShapeDtypeStruct((B, S, ATTN_W), F32),
        scratch_shapes=scratch,
        compiler_params=_cparams(2),
        name="attention",
    )(xp, w_attn, b_attn, cos_t, sin_t, bias)


def _attention_params(w_in, b_in, S):
    def regroup(w):
        rows = w.shape[0]
        qk = w[:, :2 * QKV_W].reshape(rows, 2, N_GROUPS, N_PAIRS, 2, 2, HALF)
        qk = jnp.transpose(qk, (0, 1, 2, 3, 5, 4, 6)).reshape(rows, 2, N_GROUPS, N_PAIRS, LANES)
        v = w[:, 2 * QKV_W:3 * QKV_W].reshape(rows, 1, N_GROUPS, N_PAIRS, LANES)
        qkv = jnp.concatenate([qk, v], axis=1)
        qkv = jnp.transpose(qkv, (3, 0, 2, 1, 4)).reshape(N_PAIRS, rows, 3 * N_GROUPS * LANES)
        gate = w[:, 3 * QKV_W:3 * QKV_W + ATTN_W].reshape(rows, N_PAIRS, LANES)
        return jnp.concatenate([qkv, jnp.transpose(gate, (1, 0, 2))], axis=2)

    w = regroup(w_in.astype(BF16))
    b = regroup(b_in[None, :])

    A = S // CLASSES
    r = np.arange(S)
    pos = jnp.asarray(CLASSES * (r % A) + r // A, F32)
    inv_freq = ROPE_THETA ** (-jnp.arange(HALF, dtype=F32) / HALF)
    ang = pos[:, None] * inv_freq[None, :]
    cos_t = jnp.tile(jnp.cos(ang), (1, LANES // HALF))
    sin = jnp.sin(ang)
    sin_t = jnp.concatenate([-sin, -sin, sin, sin], axis=1)
    return w, b, cos_t, sin_t


def _rglru_kernel(x_ref, w_ref, b_ref, cw_ref, cb_ref, wg_ref, bg_ref, lam_ref, o_ref,
                  u_pad, gate_s, a_s, h_s, *, S, row_chunk):
    C = LRU_SLAB
    n_chunks = S // row_chunk
    u_pad[0:SUBLANES, :] = jnp.zeros((SUBLANES, C), F32)

    def proj_chunk(rc, carry):
        r0 = pl.multiple_of(rc * row_chunk, row_chunk)
        res = jnp.dot(x_ref[pl.ds(r0, row_chunk), :], w_ref[...],
                      preferred_element_type=F32) + b_ref[...]
        u_pad[pl.ds(r0 + SUBLANES, row_chunk), :] = res[:, :C]
        gl = res[:, C:]
        gate_s[pl.ds(r0, row_chunk), :] = gl * jax.nn.sigmoid(gl)
        return carry

    lax.fori_loop(0, n_chunks, proj_chunk, 0)

    neg_lam = -lam_ref[...]
    softplus = jnp.maximum(neg_lam, 0.0) + jnp.log1p(jnp.exp(-jnp.abs(neg_lam)))
    sub = lax.broadcasted_iota(jnp.int32, (row_chunk // SUBLANES, SUBLANES, C), 1)

    def gate_chunk(rc, carry):
        r0 = pl.multiple_of(rc * row_chunk, row_chunk)
        win = u_pad[pl.ds(r0, row_chunk + SUBLANES), :]
        uc = cb_ref[...]
        for j in range(CONV_W):
            off = SUBLANES - (CONV_W - 1) + j
            uc = uc + win[off:off + row_chunk, :] * cw_ref[j:j + 1, :]
        gates = jnp.dot(uc.astype(BF16), wg_ref[...], preferred_element_type=F32) + bg_ref[...]
        r = jax.nn.sigmoid(gates[:, :C])
        i = jax.nn.sigmoid(gates[:, C:])
        log_a = -LRU_C * r * softplus
        a = jnp.exp(log_a)
        mult = jnp.sqrt(jnp.maximum(-jnp.tanh(log_a) * (a * a + 1.0), 0.0))
        u = mult * (i * uc)
        a3 = a.reshape(row_chunk // SUBLANES, SUBLANES, C)
        u3 = u.reshape(row_chunk // SUBLANES, SUBLANES, C)
        for k in (1, 2, 4):
            keep = sub >= k
            a_sh = jnp.where(keep, pltpu.roll(a3, k, axis=1), 1.0)
            u_sh = jnp.where(keep, pltpu.roll(u3, k, axis=1), 0.0)
            u3 = a3 * u_sh + u3
            a3 = a3 * a_sh
        a_s[pl.ds(r0, row_chunk), :] = a3.reshape(row_chunk, C)
        h_s[pl.ds(r0, row_chunk), :] = u3.reshape(row_chunk, C)
        return carry

    lax.fori_loop(0, n_chunks, gate_chunk, 0)

    def carry_step(gi, h_prev):
        rows = pl.ds(pl.multiple_of(gi * SUBLANES, SUBLANES), SUBLANES)
        h = a_s[rows, :] * h_prev + h_s[rows, :]
        h_s[rows, :] = h
        return jnp.broadcast_to(h[SUBLANES - 1:SUBLANES, :], (SUBLANES, C))

    lax.fori_loop(0, S // SUBLANES, carry_step, jnp.zeros((SUBLANES, C), F32), unroll=4)

    def out_chunk(rc, carry):
        rows = pl.ds(pl.multiple_of(rc * row_chunk, row_chunk), row_chunk)
        o_ref[rows, :] = (h_s[rows, :] * gate_s[rows, :]).astype(BF16)
        return carry

    lax.fori_loop(0, n_chunks, out_chunk, 0)


def _rglru(xn, w_lru, b_lru, conv_w, conv_b, w_gate, b_gate, lam):
    B, S, D = xn.shape
    n_slabs = LRU_W // LRU_SLAB
    kern = functools.partial(_rglru_kernel, S=S, row_chunk=512)
    return pl.pallas_call(
        kern,
        grid=(B, n_slabs),
        in_specs=[
            pl.BlockSpec((None, S, D), lambda b, c: (b, 0, 0)),
            pl.BlockSpec((None, D, 2 * LRU_SLAB), lambda b, c: (c, 0, 0)),
            pl.BlockSpec((None, 1, 2 * LRU_SLAB), lambda b, c: (c, 0, 0)),
            pl.BlockSpec((CONV_W, LRU_SLAB), lambda b, c: (0, c)),
            pl.BlockSpec((1, LRU_SLAB), lambda b, c: (0, c)),
            pl.BlockSpec((None, LRU_SLAB, 2 * LRU_SLAB), lambda b, c: (c, 0, 0)),
            pl.BlockSpec((None, 1, 2 * LRU_SLAB), lambda b, c: (c, 0, 0)),
            pl.BlockSpec((1, LRU_SLAB), lambda b, c: (0, c)),
        ],
        out_specs=pl.BlockSpec((None, S, LRU_SLAB), lambda b, c: (b, 0, c)),
        out_shape=jax.ShapeDtypeStruct((B, S, LRU_W), BF16),
        scratch_shapes=[
            pltpu.VMEM((S + SUBLANES, LRU_SLAB), F32),
            pltpu.VMEM((S, LRU_SLAB), F32),
            pltpu.VMEM((S, LRU_SLAB), F32),
            pltpu.VMEM((S, LRU_SLAB), F32),
        ],
        compiler_params=_cparams(2),
        name="rglru",
    )(xn, w_lru, b_lru, conv_w, conv_b, w_gate, b_gate, lam)


def _rglru_params(w_in, b_in, lru_wr, lru_br, lru_wi, lru_bi):
    n_slabs = LRU_W // LRU_SLAB
    u0 = 3 * QKV_W + ATTN_W
    g0 = u0 + LRU_W

    def regroup(w):
        rows = w.shape[0]
        ug = jnp.stack([w[:, u0:g0].reshape(rows, n_slabs, LRU_SLAB),
                        w[:, g0:g0 + LRU_W].reshape(rows, n_slabs, LRU_SLAB)], axis=2)
        return jnp.transpose(ug, (1, 0, 2, 3)).reshape(n_slabs, rows, 2 * LRU_SLAB)

    w = regroup(w_in.astype(BF16))
    b = regroup(b_in[None, :])
    per = LRU_SLAB // LRU_BLOCK

    def block_diag(wb):
        wb = wb.reshape(n_slabs, per, LRU_BLOCK, LRU_BLOCK)
        eye = jnp.eye(per, dtype=wb.dtype)
        full = wb[:, :, :, None, :] * eye[None, :, None, :, None]
        return full.reshape(n_slabs, LRU_SLAB, LRU_SLAB)

    w_gate = jnp.concatenate([block_diag(lru_wr), block_diag(lru_wi)], axis=2).astype(BF16)
    b_gate = jnp.concatenate([lru_br.reshape(n_slabs, 1, LRU_SLAB),
                              lru_bi.reshape(n_slabs, 1, LRU_SLAB)], axis=2)
    return w, b, w_gate, b_gate


def _merge_out_kernel(x_ref, attn_ref, hg_ref, wg_ref, bg_ref, wa_ref, wl_ref, wo_ref, bo_ref,
                      lng_ref, lnb_ref, o_ref):
    D = x_ref.shape[1]
    x = x_ref[...]
    gates = jax.nn.sigmoid(
        jnp.dot(x.astype(BF16), wg_ref[...], preferred_element_type=F32) + bg_ref[...])
    y_a = jnp.dot(attn_ref[...].astype(BF16), wa_ref[...], preferred_element_type=F32)
    y_b = jnp.dot(hg_ref[...], wl_ref[...], preferred_element_type=F32)
    merged = gates[:, :D] * y_a + gates[:, D:] * y_b
    out = jnp.dot(merged.astype(BF16), wo_ref[...], preferred_element_type=F32) + bo_ref[...]
    y = ALPHA * x + out
    mu = jnp.mean(y, axis=1, keepdims=True)
    yc = y - mu
    var = jnp.mean(yc * yc, axis=1, keepdims=True)
    o_ref[...] = yc * lax.rsqrt(var + LN_EPS) * lng_ref[...] + lnb_ref[...]


def _merge_out(x2, attn2, hg2, w_g, b_g, w_a, w_l, w_o, b_o, ln_g, ln_b, tm=512):
    T, D = x2.shape
    const = lambda shape: pl.BlockSpec(shape, lambda i: (0,) * len(shape))
    return pl.pallas_call(
        _merge_out_kernel,
        grid=(T // tm,),
        in_specs=[
            pl.BlockSpec((tm, D), lambda i: (i, 0)),
            pl.BlockSpec((tm, ATTN_W), lambda i: (i, 0)),
            pl.BlockSpec((tm, LRU_W), lambda i: (i, 0)),
            const((D, 2 * D)), const((1, 2 * D)),
            const((ATTN_W, D)), const((LRU_W, D)), const((D, D)), const((1, D)),
            const((1, D)), const((1, D)),
        ],
        out_specs=pl.BlockSpec((tm, D), lambda i: (i, 0)),
        out_shape=jax.ShapeDtypeStruct((T, D), F32),
        compiler_params=_cparams(1),
        name="merge_out",
    )(x2, attn2, hg2, w_g, b_g, w_a, w_l, w_o, b_o, ln_g, ln_b)


def kernel(x, w_in, b_in, conv_w, conv_b, lru_wr, lru_br, lru_wi, lru_bi, lru_lambda,
           w_attn_proj, w_lru_proj, w_out, b_out, ln_gain, ln_bias):
    B, S, D = x.shape
    assert w_in.shape[0] == DEPTH and S % (CLASSES * BAND) == 0 and D == 1024
    for l in range(DEPTH):
        wl, bl = w_in[l], b_in[l]
        xp, xn = _permute_cast(x)

        w_attn, b_attn, cos_t, sin_t = _attention_params(wl, bl, S)
        attn = _attention(xp, w_attn, b_attn, cos_t, sin_t, jnp.asarray(_band_bias(S)))

        w_lru, b_lru, w_gate, b_gate = _rglru_params(wl, bl, lru_wr[l], lru_br[l],
                                                    lru_wi[l], lru_bi[l])
        hg = _rglru(xn, w_lru, b_lru, conv_w[l], conv_b[l][None, :], w_gate, b_gate,
                    lru_lambda[l][None, :])

        g0 = 3 * QKV_W + ATTN_W + 2 * LRU_W
        out = _merge_out(
            x.reshape(B * S, D), attn.reshape(B * S, ATTN_W), hg.reshape(B * S, LRU_W),
            wl[:, g0:].astype(BF16), bl[None, g0:],
            w_attn_proj[l].astype(BF16), w_lru_proj[l].astype(BF16), w_out[l].astype(BF16),
            b_out[l][None, :], ln_gain[l][None, :], ln_bias[l][None, :])
        x = out.reshape(B, S, D)
    return x
```

```python
import functools
import math

import jax
import jax.numpy as jnp
import numpy as np
from jax import lax
from jax.experimental import pallas as pl
from jax.experimental.pallas import tpu as pltpu

LANES = 128
SUBLANES = 8
BF16_ROWS = 16
VMEM_LIMIT_BYTES = 56 * 1024 * 1024

D_MODEL = 1024
HEAD_DIM = 64
HALF = HEAD_DIM // 2
HEADS_PER_GROUP = 8
DILATIONS = (1, 4, 16)
N_GROUPS = len(DILATIONS)
BAND = 128
CLASSES = 16
QKV_W = N_GROUPS * HEADS_PER_GROUP * HEAD_DIM
ATTN_W = HEADS_PER_GROUP * HEAD_DIM
N_PAIRS = ATTN_W // LANES
LRU_W = 1024
LRU_BLOCK = 64
LRU_SLAB = 256
CONV_W = 4
LRU_C = 8.0
ROPE_THETA = 10000.0
LOG2_E = math.log2(math.e)
NEG_INF = -1e30
LN_EPS = 1e-5
DEPTH = 1
ALPHA = (2.0 * DEPTH) ** 0.25
N_ATTN_SLABS = 3 * N_GROUPS + 1
ATTN_COLS = N_ATTN_SLABS * LANES
COL_K = QKV_W
COL_V = 2 * QKV_W
COL_GATE_A = 3 * QKV_W
COL_U = COL_GATE_A + ATTN_W
COL_GATE_L = COL_U + LRU_W
COL_MERGE = COL_GATE_L + LRU_W

BF16 = jnp.bfloat16
F32 = jnp.float32
ONCE = pl.Buffered(1)


def _cparams(n_axes):
    return pltpu.CompilerParams(dimension_semantics=("arbitrary",) * n_axes,
                                vmem_limit_bytes=VMEM_LIMIT_BYTES)


def _permute_cast_kernel(x_ref, xp_ref, *, rows_per_class):
    for c in range(CLASSES):
        xp_ref[c * rows_per_class:(c + 1) * rows_per_class, :] = (
            x_ref[pl.ds(c, rows_per_class, stride=CLASSES), :].astype(BF16))


def _permute_cast(x):
    B, S, D = x.shape
    kern = functools.partial(_permute_cast_kernel, rows_per_class=S // CLASSES)
    spec = pl.BlockSpec((None, S, LANES), lambda b, j: (b, 0, j))
    return pl.pallas_call(
        kern,
        grid=(B, D // LANES),
        in_specs=[spec],
        out_specs=spec,
        out_shape=jax.ShapeDtypeStruct((B, S, D), BF16),
        compiler_params=_cparams(2),
        name="permute_cast",
    )(x)


def _band_bias():
    out = np.zeros((N_GROUPS, 2, 2 * BAND, 2 * BAND), np.float32)
    i = np.arange(BAND)
    for g, d in enumerate(DILATIONS):
        R = CLASSES // d
        L = BAND // R
        m = R * (i % L) + i // L
        m_k = np.concatenate([m - BAND, m])
        dist = m[:, None] - m_k[None, :]
        band = (dist >= 0) & (dist <= BAND)
        for first in range(2):
            valid = band & ((np.arange(2 * BAND) >= BAND)[None, :] | (first == 0))
            bias = np.where(valid, 0.0, NEG_INF).astype(np.float32)
            out[g, first] = np.concatenate([bias, bias], axis=0)
    return out


def _rope_tables(S):
    A = S // CLASSES
    r = np.arange(S)
    pos = (CLASSES * (r % A) + r // A).astype(np.float64)
    inv_freq = ROPE_THETA ** (-np.arange(HALF, dtype=np.float64) / HALF)
    ang = pos[:, None] * inv_freq[None, :]
    cos_t = np.tile(np.cos(ang), (1, LANES // HALF))
    sin = np.sin(ang)
    sin_t = np.concatenate([-sin, -sin, sin, sin], axis=1)
    return cos_t.astype(np.float32), sin_t.astype(np.float32)


def _pair_interleave(w):
    lane = lax.broadcasted_iota(jnp.int32, w.shape, 1)
    from_right = pltpu.roll(w, LANES - HALF, axis=1)
    from_left = pltpu.roll(w, HALF, axis=1)
    quarter = lane // HALF
    return jnp.where(quarter == 1, from_right, jnp.where(quarter == 2, from_left, w))


def _attention_kernel(x_ref, *refs, S, row_chunk, tile_unroll):
    w_refs = refs[:N_ATTN_SLABS]
    b_refs = refs[N_ATTN_SLABS:2 * N_ATTN_SLABS]
    cos_ref, sin_ref, bias_ref, o_ref = refs[2 * N_ATTN_SLABS:2 * N_ATTN_SLABS + 4]
    w_s, b_s, qkv0_s, qkv_s, gate_s, acc_s, m_s, l_s = refs[2 * N_ATTN_SLABS + 4:]
    D = x_ref.shape[1]
    A = S // CLASSES

    def qkv_ref(g, i):
        return qkv0_s.at[i] if g == 0 else qkv_s.at[3 * (g - 1) + i]

    n_tiles = S // BAND
    lane = lax.broadcasted_iota(jnp.int32, (BAND, LANES), 1)
    head0_q = ((lane // HALF) % 2) == 0
    head0_v = lane < HEAD_DIM
    q_scale = (HEAD_DIM ** -0.5) * LOG2_E

    w_rows = 256
    for i in range(N_ATTN_SLABS):
        is_qk = i < 3 * N_GROUPS and i % 3 != 2
        cols = slice(i * LANES, (i + 1) * LANES)
        for r0 in range(0, D, w_rows):
            w = w_refs[i][r0:r0 + w_rows, :]
            w_s[r0:r0 + w_rows, cols] = (_pair_interleave(w) if is_qk else w).astype(BF16)
        b = jnp.broadcast_to(b_refs[i][...], (SUBLANES, LANES))
        b_s[:, cols] = _pair_interleave(b) if is_qk else b

    def proj_chunk(rc, carry):
        rows = pl.ds(pl.multiple_of(rc * row_chunk, row_chunk), row_chunk)
        res = jnp.dot(x_ref[rows, :], w_s[...], preferred_element_type=F32) + b_s[0:1, :]
        cos = cos_ref[rows, :]
        sin = sin_ref[rows, :]
        for g in range(N_GROUPS):
            col = 3 * g * LANES
            q = res[:, col:col + LANES]
            k = res[:, col + LANES:col + 2 * LANES]
            q = (q * cos + pltpu.roll(q, LANES // 2, axis=1) * sin) * q_scale
            k = k * cos + pltpu.roll(k, LANES // 2, axis=1) * sin
            for i, val in enumerate((q, k, res[:, col + 2 * LANES:col + 3 * LANES])):
                ref = qkv_ref(g, i)
                ref[rows, :] = val.astype(ref.dtype)
        gate = res[:, 3 * N_GROUPS * LANES:]
        gate_s[rows, :] = (gate * jax.nn.sigmoid(gate)).astype(BF16)
        return carry

    lax.fori_loop(0, S // row_chunk, proj_chunk, 0)

    for g, d in enumerate(DILATIONS):
        R = CLASSES // d
        L = BAND // R
        nb = S // (d * BAND)

        def tile_body(e, carry, d=d, R=R, L=L, nb=nb, g=g):
            rho = lax.shift_right_logical(e, int(math.log2(nb)))
            n = lax.bitwise_and(e, nb - 1)
            n_prev = jnp.maximum(n - 1, 0)

            def run(j, nn):
                return pl.ds(pl.multiple_of((rho + d * j) * A + L * nn, min(L, BF16_ROWS)), L)

            def gather(ref, nn):
                return jnp.concatenate([ref[run(j, nn), :] for j in range(R)], axis=0)

            q_t = gather(qkv_ref(g, 0), n).astype(BF16)
            zero = jnp.zeros_like(q_t)
            qm = jnp.concatenate([jnp.where(head0_q, q_t, zero),
                                  jnp.where(head0_q, zero, q_t)], axis=0)
            k_ref = qkv_ref(g, 1)
            v_ref = qkv_ref(g, 2)
            kwin = jnp.concatenate([gather(k_ref, n_prev), gather(k_ref, n)],
                                   axis=0).astype(BF16)
            vwin = jnp.concatenate([gather(v_ref, n_prev), gather(v_ref, n)],
                                   axis=0).astype(BF16)

            s = lax.dot_general(qm, kwin, (((1,), (1,)), ((), ())),
                                preferred_element_type=F32)
            s = s + bias_ref[g, 1 - jnp.minimum(n, 1)]
            m_blk = jnp.max(s, axis=1, keepdims=True)
            p = jnp.exp2(s - m_blk)
            l_blk = jnp.sum(p, axis=1, keepdims=True)
            pv = jnp.dot(p.astype(BF16), vwin, preferred_element_type=F32)
            o_t = jnp.where(head0_v, pv[:BAND], pv[BAND:])
            m_t = jnp.where(head0_v, m_blk[:BAND], m_blk[BAND:])
            l_t = jnp.where(head0_v, l_blk[:BAND], l_blk[BAND:])

            if g > 0:
                m_old = gather(m_s, n)
                m_new = jnp.maximum(m_old, m_t)
                w_old = jnp.exp2(m_old - m_new)
                w_new = jnp.exp2(m_t - m_new)
                m_t = m_new
                l_t = gather(l_s, n) * w_old + l_t * w_new
                o_t = gather(acc_s, n) * w_old + o_t * w_new
            for j in range(R):
                m_s[run(j, n), :] = m_t[j * L:(j + 1) * L]
                l_s[run(j, n), :] = l_t[j * L:(j + 1) * L]
                acc_s[run(j, n), :] = o_t[j * L:(j + 1) * L]
            return carry

        lax.fori_loop(0, n_tiles, tile_body, 0, unroll=tile_unroll)

    for c in range(CLASSES):
        rows = slice(c * A, (c + 1) * A)
        res = (acc_s[rows, :] / l_s[rows, :]) * gate_s[rows, :].astype(F32)
        o_ref[pl.ds(c, A, stride=CLASSES), :] = res


def _attention(xp, w_in, b_in):
    B, S, D = xp.shape
    kern = functools.partial(_attention_kernel, S=S, row_chunk=512, tile_unroll=8)
    slab0 = []
    for g in range(N_GROUPS):
        slab0 += [(c0 + g * ATTN_W) // LANES for c0 in (0, COL_K, COL_V)]
    slab0.append(COL_GATE_A // LANES)
    w_specs = [pl.BlockSpec((D, LANES), functools.partial(lambda b, p, s: (0, s + p), s=s))
               for s in slab0]
    b_specs = [pl.BlockSpec((1, LANES), functools.partial(lambda b, p, s: (0, s + p), s=s))
               for s in slab0]
    cos_t, sin_t = _rope_tables(S)
    const2 = pl.BlockSpec((S, LANES), lambda b, p: (0, 0), pipeline_mode=ONCE)
    scratch = [
        pltpu.VMEM((D, ATTN_COLS), BF16),
        pltpu.VMEM((SUBLANES, ATTN_COLS), F32),
        pltpu.VMEM((3, S, LANES), F32),
        pltpu.VMEM((3 * (N_GROUPS - 1), S, LANES), BF16),
        pltpu.VMEM((S, LANES), BF16),
        pltpu.VMEM((S, LANES), F32),
        pltpu.VMEM((S, LANES), F32),
        pltpu.VMEM((S, LANES), F32),
    ]
    return pl.pallas_call(
        kern,
        grid=(B, N_PAIRS),
        in_specs=[pl.BlockSpec((None, S, D), lambda b, p: (b, 0, 0), pipeline_mode=ONCE)]
        + w_specs + b_specs + [
            const2, const2,
            pl.BlockSpec((N_GROUPS, 2, 2 * BAND, 2 * BAND), lambda b, p: (0, 0, 0, 0),
                         pipeline_mode=ONCE),
        ],
        out_specs=pl.BlockSpec((None, S, LANES), lambda b, p: (b, 0, p)),
        out_shape=jax.ShapeDtypeStruct((B, S, ATTN_W), F32),
        scratch_shapes=scratch,
        compiler_params=_cparams(2),
        name="attention",
    )(xp, *([w_in] * N_ATTN_SLABS), *([b_in] * N_ATTN_SLABS), cos_t, sin_t, _band_bias())


def _rglru_kernel(x_ref, wu_ref, wl_ref, bu_ref, bl_ref, cw_ref, cb_ref, wr_ref, wi_ref,
                  br_ref, bi_ref, lam_ref, o_ref,
                  w_s, wg_s, u_pad, gate_s, a_s, h_s, *, S, row_chunk):
    C = LRU_SLAB
    n_chunks = S // row_chunk
    u_pad[0:SUBLANES, :] = jnp.zeros((SUBLANES, C), F32)

    w_s[:, :C] = wu_ref[...].astype(BF16)
    w_s[:, C:] = wl_ref[...].astype(BF16)
    wg_s[...] = jnp.zeros(wg_s.shape, BF16)
    for blk in range(C // LRU_BLOCK):
        rows = slice(blk * LRU_BLOCK, (blk + 1) * LRU_BLOCK)
        wg_s[rows, blk * LRU_BLOCK:(blk + 1) * LRU_BLOCK] = wr_ref[blk].astype(BF16)
        wg_s[rows, C + blk * LRU_BLOCK:C + (blk + 1) * LRU_BLOCK] = wi_ref[blk].astype(BF16)

    def proj_chunk(rc, carry):
        r0 = pl.multiple_of(rc * row_chunk, row_chunk)
        xb = x_ref[pl.ds(r0, row_chunk), :].astype(BF16)
        res = jnp.dot(xb, w_s[...], preferred_element_type=F32)
        u_pad[pl.ds(r0 + SUBLANES, row_chunk), :] = res[:, :C] + bu_ref[...]
        gl = res[:, C:] + bl_ref[...]
        gate_s[pl.ds(r0, row_chunk), :] = gl * jax.nn.sigmoid(gl)
        return carry

    lax.fori_loop(0, n_chunks, proj_chunk, 0)

    neg_lam = -lam_ref[...]
    softplus = jnp.maximum(neg_lam, 0.0) + jnp.log1p(jnp.exp(-jnp.abs(neg_lam)))
    sub = lax.broadcasted_iota(jnp.int32, (row_chunk // SUBLANES, SUBLANES, C), 1)

    def gate_chunk(rc, carry):
        r0 = pl.multiple_of(rc * row_chunk, row_chunk)
        win = u_pad[pl.ds(r0, row_chunk + SUBLANES), :]
        uc = cb_ref[...]
        for j in range(CONV_W):
            off = SUBLANES - (CONV_W - 1) + j
            uc = uc + win[off:off + row_chunk, :] * cw_ref[j:j + 1, :]
        gates = jnp.dot(uc.astype(BF16), wg_s[...], preferred_element_type=F32)
        r = jax.nn.sigmoid(gates[:, :C] + br_ref[...])
        i = jax.nn.sigmoid(gates[:, C:] + bi_ref[...])
        log_a = -LRU_C * r * softplus
        a = jnp.exp(log_a)
        mult = jnp.sqrt(jnp.maximum(-jnp.tanh(log_a) * (a * a + 1.0), 0.0))
        u = mult * (i * uc)
        a3 = a.reshape(row_chunk // SUBLANES, SUBLANES, C)
        u3 = u.reshape(row_chunk // SUBLANES, SUBLANES, C)
        for k in (1, 2, 4):
            keep = sub >= k
            a_sh = jnp.where(keep, pltpu.roll(a3, k, axis=1), 1.0)
            u_sh = jnp.where(keep, pltpu.roll(u3, k, axis=1), 0.0)
            u3 = a3 * u_sh + u3
            a3 = a3 * a_sh
        a_s[pl.ds(r0, row_chunk), :] = a3.reshape(row_chunk, C)
        h_s[pl.ds(r0, row_chunk), :] = u3.reshape(row_chunk, C)
        return carry

    lax.fori_loop(0, n_chunks, gate_chunk, 0)

    def carry_step(gi, h_prev):
        rows = pl.ds(pl.multiple_of(gi * SUBLANES, SUBLANES), SUBLANES)
        h = a_s[rows, :] * h_prev + h_s[rows, :]
        h_s[rows, :] = h
        return jnp.broadcast_to(h[SUBLANES - 1:SUBLANES, :], (SUBLANES, C))

    lax.fori_loop(0, S // SUBLANES, carry_step, jnp.zeros((SUBLANES, C), F32), unroll=4)

    def out_chunk(rc, carry):
        rows = pl.ds(pl.multiple_of(rc * row_chunk, row_chunk), row_chunk)
        o_ref[rows, :] = (h_s[rows, :] * gate_s[rows, :]).astype(BF16)
        return carry

    lax.fori_loop(0, n_chunks, out_chunk, 0)


def _rglru(x, w_in, b_in, conv_w, conv_b, lru_wr, lru_br, lru_wi, lru_bi, lam):
    B, S, D = x.shape
    C = LRU_SLAB
    n_slabs = LRU_W // C
    per = C // LRU_BLOCK
    kern = functools.partial(_rglru_kernel, S=S, row_chunk=512)
    col = lambda c0: (lambda b, c: (0, c0 // C + c))
    row1 = lambda: pl.BlockSpec((1, C), lambda b, c: (0, c))
    blocks = lambda: pl.BlockSpec((per, LRU_BLOCK, LRU_BLOCK), lambda b, c: (c, 0, 0))
    return pl.pallas_call(
        kern,
        grid=(B, n_slabs),
        in_specs=[
            pl.BlockSpec((None, S, D), lambda b, c: (b, 0, 0), pipeline_mode=ONCE),
            pl.BlockSpec((D, C), col(COL_U)), pl.BlockSpec((D, C), col(COL_GATE_L)),
            pl.BlockSpec((1, C), col(COL_U)), pl.BlockSpec((1, C), col(COL_GATE_L)),
            pl.BlockSpec((CONV_W, C), lambda b, c: (0, c)), row1(),
            blocks(), blocks(), row1(), row1(), row1(),
        ],
        out_specs=pl.BlockSpec((None, S, C), lambda b, c: (b, 0, c)),
        out_shape=jax.ShapeDtypeStruct((B, S, LRU_W), BF16),
        scratch_shapes=[
            pltpu.VMEM((D, 2 * C), BF16),
            pltpu.VMEM((C, 2 * C), BF16),
            pltpu.VMEM((S + SUBLANES, C), F32),
            pltpu.VMEM((S, C), F32),
            pltpu.VMEM((S, C), F32),
            pltpu.VMEM((S, C), F32),
        ],
        compiler_params=_cparams(2),
        name="rglru",
    )(x, w_in, w_in, b_in, b_in, conv_w, conv_b, lru_wr, lru_wi, lru_br, lru_bi, lam)


def _merge_out_kernel(x_ref, attn_ref, hg_ref, wga_ref, wgb_ref, bga_ref, bgb_ref,
                      wa_ref, wl_ref, wo_ref, bo_ref, lng_ref, lnb_ref, o_ref,
                      wg_s, wa_s, wl_s, wo_s):
    D = x_ref.shape[1]

    @pl.when(pl.program_id(0) == 0)
    def _():
        rows = 256
        for r0 in range(0, D, rows):
            sl = slice(r0, r0 + rows)
            wg_s[sl, :D] = wga_ref[sl, :].astype(BF16)
            wg_s[sl, D:] = wgb_ref[sl, :].astype(BF16)
            wl_s[sl, :] = wl_ref[sl, :].astype(BF16)
            wo_s[sl, :] = wo_ref[sl, :].astype(BF16)
        for r0 in range(0, ATTN_W, rows):
            sl = slice(r0, r0 + rows)
            wa_s[sl, :] = wa_ref[sl, :].astype(BF16)

    x = x_ref[...]
    xb = x.astype(BF16)
    gate_a = jax.nn.sigmoid(
        jnp.dot(xb, wg_s[:, :D], preferred_element_type=F32) + bga_ref[...])
    gate_b = jax.nn.sigmoid(
        jnp.dot(xb, wg_s[:, D:], preferred_element_type=F32) + bgb_ref[...])
    y_a = jnp.dot(attn_ref[...].astype(BF16), wa_s[...], preferred_element_type=F32)
    y_b = jnp.dot(hg_ref[...], wl_s[...], preferred_element_type=F32)
    merged = gate_a * y_a + gate_b * y_b
    out = jnp.dot(merged.astype(BF16), wo_s[...], preferred_element_type=F32) + bo_ref[...]
    y = ALPHA * x + out
    mu = jnp.mean(y, axis=1, keepdims=True)
    yc = y - mu
    var = jnp.mean(yc * yc, axis=1, keepdims=True)
    o_ref[...] = yc * lax.rsqrt(var + LN_EPS) * lng_ref[...] + lnb_ref[...]


def _merge_out(x2, attn2, hg2, w_in, b_in, w_a, w_l, w_o, b_o, ln_g, ln_b, tm=512):
    T, D = x2.shape
    once = lambda shape, idx: pl.BlockSpec(shape, lambda i: idx, pipeline_mode=ONCE)
    merge_blk = COL_MERGE // D
    return pl.pallas_call(
        _merge_out_kernel,
        grid=(T // tm,),
        in_specs=[
            pl.BlockSpec((tm, D), lambda i: (i, 0)),
            pl.BlockSpec((tm, ATTN_W), lambda i: (i, 0)),
            pl.BlockSpec((tm, LRU_W), lambda i: (i, 0)),
            once((D, D), (0, merge_blk)), once((D, D), (0, merge_blk + 1)),
            once((1, D), (0, merge_blk)), once((1, D), (0, merge_blk + 1)),
            once((ATTN_W, D), (0, 0)), once((LRU_W, D), (0, 0)), once((D, D), (0, 0)),
            once((1, D), (0, 0)), once((1, D), (0, 0)), once((1, D), (0, 0)),
        ],
        out_specs=pl.BlockSpec((tm, D), lambda i: (i, 0)),
        out_shape=jax.ShapeDtypeStruct((T, D), F32),
        scratch_shapes=[
            pltpu.VMEM((D, 2 * D), BF16), pltpu.VMEM((ATTN_W, D), BF16),
            pltpu.VMEM((LRU_W, D), BF16), pltpu.VMEM((D, D), BF16),
        ],
        compiler_params=_cparams(1),
        name="merge_out",
    )(x2, attn2, hg2, w_in, w_in, b_in, b_in, w_a, w_l, w_o, b_o, ln_g, ln_b)


def kernel(x, w_in, b_in, conv_w, conv_b, lru_wr, lru_br, lru_wi, lru_bi, lru_lambda,
           w_attn_proj, w_lru_proj, w_out, b_out, ln_gain, ln_bias):
    B, S, D = x.shape
    assert w_in.shape[0] == DEPTH and S % (CLASSES * BAND) == 0 and D == D_MODEL
    assert COL_MERGE % D == 0 and w_in.shape[2] == COL_MERGE + 2 * D
    row = lambda v: v.reshape(1, -1)
    for l in range(DEPTH):
        wl, bl = w_in[l], row(b_in[l])
        attn = _attention(_permute_cast(x), wl, bl)
        hg = _rglru(x, wl, bl, conv_w[l], row(conv_b[l]), lru_wr[l], row(lru_br[l]),
                    lru_wi[l], row(lru_bi[l]), row(lru_lambda[l]))
        out = _merge_out(
            x.reshape(B * S, D), attn.reshape(B * S, ATTN_W), hg.reshape(B * S, LRU_W),
            wl, bl, w_attn_proj[l], w_lru_proj[l], w_out[l],
            row(b_out[l]), row(ln_gain[l]), row(ln_bias[l]))
        x = out.reshape(B, S, D)
    return x
```

```python
import functools
import math

import jax
import jax.numpy as jnp
import numpy as np
from jax import lax
from jax.experimental import pallas as pl
from jax.experimental.pallas import tpu as pltpu

LANES = 128
SUBLANES = 8
BF16_ROWS = 16
VMEM_LIMIT_BYTES = 56 * 1024 * 1024

D_MODEL = 1024
HEAD_DIM = 64
HALF = HEAD_DIM // 2
HEADS_PER_GROUP = 8
DILATIONS = (1, 4, 16)
N_GROUPS = len(DILATIONS)
BAND = 128
CLASSES = 16
QKV_W = N_GROUPS * HEADS_PER_GROUP * HEAD_DIM
ATTN_W = HEADS_PER_GROUP * HEAD_DIM
N_PAIRS = ATTN_W // LANES
LRU_W = 1024
LRU_BLOCK = 64
LRU_SLAB = 256
CONV_W = 4
LRU_C = 8.0
ROPE_THETA = 10000.0
LOG2_E = math.log2(math.e)
NEG_INF = -1e30
LN_EPS = 1e-5
DEPTH = 1
ALPHA = (2.0 * DEPTH) ** 0.25
N_ATTN_SLABS = 3 * N_GROUPS + 1
ATTN_COLS = N_ATTN_SLABS * LANES
COL_K = QKV_W
COL_V = 2 * QKV_W
COL_GATE_A = 3 * QKV_W
COL_U = COL_GATE_A + ATTN_W
COL_GATE_L = COL_U + LRU_W
COL_MERGE = COL_GATE_L + LRU_W

BF16 = jnp.bfloat16
F32 = jnp.float32
ONCE = pl.Buffered(1)


def _cparams(n_axes):
    return pltpu.CompilerParams(dimension_semantics=("arbitrary",) * n_axes,
                                vmem_limit_bytes=VMEM_LIMIT_BYTES)


def _permute_cast_kernel(x_ref, xp_ref, *, rows_per_class):
    for c in range(CLASSES):
        xp_ref[c * rows_per_class:(c + 1) * rows_per_class, :] = (
            x_ref[pl.ds(c, rows_per_class, stride=CLASSES), :].astype(BF16))


def _permute_cast(x):
    B, S, D = x.shape
    kern = functools.partial(_permute_cast_kernel, rows_per_class=S // CLASSES)
    spec = pl.BlockSpec((None, S, LANES), lambda b, j: (b, 0, j))
    return pl.pallas_call(
        kern,
        grid=(B, D // LANES),
        in_specs=[spec],
        out_specs=spec,
        out_shape=jax.ShapeDtypeStruct((B, S, D), BF16),
        compiler_params=_cparams(2),
        name="permute_cast",
    )(x)


def _band_bias():
    out = np.zeros((N_GROUPS, 2, 2 * BAND, 2 * BAND), np.float32)
    i = np.arange(BAND)
    for g, d in enumerate(DILATIONS):
        R = CLASSES // d
        L = BAND // R
        m = R * (i % L) + i // L
        m_k = np.concatenate([m - BAND, m])
        dist = m[:, None] - m_k[None, :]
        band = (dist >= 0) & (dist <= BAND)
        for first in range(2):
            valid = band & ((np.arange(2 * BAND) >= BAND)[None, :] | (first == 0))
            bias = np.where(valid, 0.0, NEG_INF).astype(np.float32)
            out[g, first] = np.concatenate([bias, bias], axis=0)
    return out


def _rope_tables(S):
    A = S // CLASSES
    r = np.arange(S)
    pos = (CLASSES * (r % A) + r // A).astype(np.float64)
    inv_freq = ROPE_THETA ** (-np.arange(HALF, dtype=np.float64) / HALF)
    ang = pos[:, None] * inv_freq[None, :]
    cos_t = np.tile(np.cos(ang), (1, LANES // HALF))
    sin = np.sin(ang)
    sin_t = np.concatenate([-sin, -sin, sin, sin], axis=1)
    return cos_t.astype(np.float32), sin_t.astype(np.float32)


def _pair_interleave(w):
    lane = lax.broadcasted_iota(jnp.int32, w.shape, 1)
    from_right = pltpu.roll(w, LANES - HALF, axis=1)
    from_left = pltpu.roll(w, HALF, axis=1)
    quarter = lane // HALF
    return jnp.where(quarter == 1, from_right, jnp.where(quarter == 2, from_left, w))


def _attention_kernel(x_ref, *refs, S, row_chunk, tile_unroll):
    w_refs = refs[:N_ATTN_SLABS]
    b_refs = refs[N_ATTN_SLABS:2 * N_ATTN_SLABS]
    cos_ref, sin_ref, bias_ref, o_ref = refs[2 * N_ATTN_SLABS:2 * N_ATTN_SLABS + 4]
    w_s, b_s, qkv0_s, qkv_s, gate_s, acc_s, m_s, l_s = refs[2 * N_ATTN_SLABS + 4:]
    D = x_ref.shape[1]
    A = S // CLASSES

    def qkv_ref(g, i):
        return qkv0_s.at[i] if g == 0 else qkv_s.at[3 * (g - 1) + i]

    n_tiles = S // BAND
    lane = lax.broadcasted_iota(jnp.int32, (BAND, LANES), 1)
    head0_q = ((lane // HALF) % 2) == 0
    head0_v = lane < HEAD_DIM
    q_scale = (HEAD_DIM ** -0.5) * LOG2_E

    w_rows = 256
    for i in range(N_ATTN_SLABS):
        is_qk = i < 3 * N_GROUPS and i % 3 != 2
        cols = slice(i * LANES, (i + 1) * LANES)
        for r0 in range(0, D, w_rows):
            w = w_refs[i][r0:r0 + w_rows, :]
            w_s[r0:r0 + w_rows, cols] = (_pair_interleave(w) if is_qk else w).astype(BF16)
        b = jnp.broadcast_to(b_refs[i][...], (SUBLANES, LANES))
        b_s[:, cols] = _pair_interleave(b) if is_qk else b

    def proj_chunk(rc, carry):
        rows = pl.ds(pl.multiple_of(rc * row_chunk, row_chunk), row_chunk)
        res = jnp.dot(x_ref[rows, :], w_s[...], preferred_element_type=F32) + b_s[0:1, :]
        cos = cos_ref[rows, :]
        sin = sin_ref[rows, :]
        for g in range(N_GROUPS):
            col = 3 * g * LANES
            q = res[:, col:col + LANES]
            k = res[:, col + LANES:col + 2 * LANES]
            q = (q * cos + pltpu.roll(q, LANES // 2, axis=1) * sin) * q_scale
            k = k * cos + pltpu.roll(k, LANES // 2, axis=1) * sin
            for i, val in enumerate((q, k, res[:, col + 2 * LANES:col + 3 * LANES])):
                ref = qkv_ref(g, i)
                ref[rows, :] = val.astype(ref.dtype)
        gate = res[:, 3 * N_GROUPS * LANES:]
        gate_s[rows, :] = (gate * jax.nn.sigmoid(gate)).astype(BF16)
        return carry

    lax.fori_loop(0, S // row_chunk, proj_chunk, 0)

    for g, d in enumerate(DILATIONS):
        R = CLASSES // d
        L = BAND // R
        nb = S // (d * BAND)

        def tile_body(e, carry, d=d, R=R, L=L, nb=nb, g=g):
            rho = lax.shift_right_logical(e, int(math.log2(nb)))
            n = lax.bitwise_and(e, nb - 1)
            n_prev = jnp.maximum(n - 1, 0)

            def run(j, nn):
                return pl.ds(pl.multiple_of((rho + d * j) * A + L * nn, min(L, BF16_ROWS)), L)

            def gather(ref, nn):
                return jnp.concatenate([ref[run(j, nn), :] for j in range(R)], axis=0)

            q_t = gather(qkv_ref(g, 0), n).astype(BF16)
            zero = jnp.zeros_like(q_t)
            qm = jnp.concatenate([jnp.where(head0_q, q_t, zero),
                                  jnp.where(head0_q, zero, q_t)], axis=0)
            k_ref = qkv_ref(g, 1)
            v_ref = qkv_ref(g, 2)
            kwin = jnp.concatenate([gather(k_ref, n_prev), gather(k_ref, n)],
                                   axis=0).astype(BF16)
            vwin = jnp.concatenate([gather(v_ref, n_prev), gather(v_ref, n)],
                                   axis=0).astype(BF16)

            s = lax.dot_general(qm, kwin, (((1,), (1,)), ((), ())),
                                preferred_element_type=F32)
            s = s + bias_ref[g, 1 - jnp.minimum(n, 1)]
            m_blk = jnp.max(s, axis=1, keepdims=True)
            p = jnp.exp2(s - m_blk)
            l_blk = jnp.sum(p, axis=1, keepdims=True)
            pv = jnp.dot(p.astype(BF16), vwin, preferred_element_type=F32)
            o_t = jnp.where(head0_v, pv[:BAND], pv[BAND:])
            m_t = jnp.where(head0_v, m_blk[:BAND], m_blk[BAND:])
            l_t = jnp.where(head0_v, l_blk[:BAND], l_blk[BAND:])

            if g > 0:
                m_old = gather(m_s, n)
                m_new = jnp.maximum(m_old, m_t)
                w_old = jnp.exp2(m_old - m_new)
                w_new = jnp.exp2(m_t - m_new)
                m_t = m_new
                l_t = gather(l_s, n) * w_old + l_t * w_new
                o_t = gather(acc_s, n) * w_old + o_t * w_new
            for j in range(R):
                m_s[run(j, n), :] = m_t[j * L:(j + 1) * L]
                l_s[run(j, n), :] = l_t[j * L:(j + 1) * L]
                acc_s[run(j, n), :] = o_t[j * L:(j + 1) * L]
            return carry

        lax.fori_loop(0, n_tiles, tile_body, 0, unroll=tile_unroll)

    for c in range(CLASSES):
        rows = slice(c * A, (c + 1) * A)
        res = (acc_s[rows, :] / l_s[rows, :]) * gate_s[rows, :].astype(F32)
        o_ref[pl.ds(c, A, stride=CLASSES), :] = res


def _attention(xp, w_in, b_in):
    B, S, D = xp.shape
    kern = functools.partial(_attention_kernel, S=S, row_chunk=512, tile_unroll=8)
    slab0 = []
    for g in range(N_GROUPS):
        slab0 += [(c0 + g * ATTN_W) // LANES for c0 in (0, COL_K, COL_V)]
    slab0.append(COL_GATE_A // LANES)
    w_specs = [pl.BlockSpec((D, LANES), functools.partial(lambda b, p, s: (0, s + p), s=s))
               for s in slab0]
    b_specs = [pl.BlockSpec((1, LANES), functools.partial(lambda b, p, s: (0, s + p), s=s))
               for s in slab0]
    cos_t, sin_t = _rope_tables(S)
    const2 = pl.BlockSpec((S, LANES), lambda b, p: (0, 0), pipeline_mode=ONCE)
    scratch = [
        pltpu.VMEM((D, ATTN_COLS), BF16),
        pltpu.VMEM((SUBLANES, ATTN_COLS), F32),
        pltpu.VMEM((3, S, LANES), F32),
        pltpu.VMEM((3 * (N_GROUPS - 1), S, LANES), BF16),
        pltpu.VMEM((S, LANES), BF16),
        pltpu.VMEM((S, LANES), F32),
        pltpu.VMEM((S, LANES), F32),
        pltpu.VMEM((S, LANES), F32),
    ]
    return pl.pallas_call(
        kern,
        grid=(B, N_PAIRS),
        in_specs=[pl.BlockSpec((None, S, D), lambda b, p: (b, 0, 0), pipeline_mode=ONCE)]
        + w_specs + b_specs + [
            const2, const2,
            pl.BlockSpec((N_GROUPS, 2, 2 * BAND, 2 * BAND), lambda b, p: (0, 0, 0, 0),
                         pipeline_mode=ONCE),
        ],
        out_specs=pl.BlockSpec((None, S, LANES), lambda b, p: (b, 0, p)),
        out_shape=jax.ShapeDtypeStruct((B, S, ATTN_W), F32),
        scratch_shapes=scratch,
        compiler_params=_cparams(2),
        name="attention",
    )(xp, *([w_in] * N_ATTN_SLABS), *([b_in] * N_ATTN_SLABS), cos_t, sin_t, _band_bias())


def _rglru_kernel(x_ref, wu_ref, wl_ref, bu_ref, bl_ref, cw_ref, cb_ref, wr_ref, wi_ref,
                  br_ref, bi_ref, lam_ref, o_ref,
                  w_s, wg_s, u_pad, gate_s, a_p, h_p, *, S, row_chunk, pitch):
    C = LRU_SLAB
    n_slabs = C // LANES
    n_chunks = S // row_chunk
    u_pad[:, 0:SUBLANES, :] = jnp.zeros((n_slabs, SUBLANES, LANES), F32)

    w_s[:, :C] = wu_ref[...].astype(BF16)
    w_s[:, C:] = wl_ref[...].astype(BF16)
    wg_s[...] = jnp.zeros(wg_s.shape, BF16)
    for blk in range(C // LRU_BLOCK):
        rows = slice(blk * LRU_BLOCK, (blk + 1) * LRU_BLOCK)
        wg_s[rows, blk * LRU_BLOCK:(blk + 1) * LRU_BLOCK] = wr_ref[blk].astype(BF16)
        wg_s[rows, C + blk * LRU_BLOCK:C + (blk + 1) * LRU_BLOCK] = wi_ref[blk].astype(BF16)

    def proj_chunk(rc, carry):
        r0 = pl.multiple_of(rc * row_chunk, row_chunk)
        xb = x_ref[pl.ds(r0, row_chunk), :].astype(BF16)
        res = jnp.dot(xb, w_s[...], preferred_element_type=F32)
        u = res[:, :C] + bu_ref[...]
        for s in range(C // LANES):
            u_pad[s, pl.ds(r0 + SUBLANES, row_chunk), :] = u[:, s * LANES:(s + 1) * LANES]
        gl = res[:, C:] + bl_ref[...]
        gate_s[pl.ds(r0, row_chunk), :] = gl * jax.nn.sigmoid(gl)
        return carry

    lax.fori_loop(0, n_chunks, proj_chunk, 0)

    neg_lam = -lam_ref[...]
    softplus = jnp.maximum(neg_lam, 0.0) + jnp.log1p(jnp.exp(-jnp.abs(neg_lam)))
    decay_rate = LRU_C * softplus

    def gate_chunk(rc, carry):
        r0 = pl.multiple_of(rc * row_chunk, row_chunk)
        parts = []
        for s in range(C // LANES):
            lanes = slice(s * LANES, (s + 1) * LANES)
            acc = cb_ref[:, lanes]
            for j in range(CONV_W):
                off = SUBLANES - (CONV_W - 1) + j
                acc = acc + u_pad[s, pl.ds(r0 + off, row_chunk), :] * cw_ref[j:j + 1, lanes]
            parts.append(acc)
        uc = jnp.concatenate(parts, axis=1)
        gates = jnp.dot(uc.astype(BF16), wg_s[...], preferred_element_type=F32)
        r = jax.nn.sigmoid(gates[:, :C] + br_ref[...])
        i = jax.nn.sigmoid(gates[:, C:] + bi_ref[...])
        neg_log_a = r * decay_rate
        a = jnp.exp(-neg_log_a)
        one_minus_a2 = jnp.maximum(jnp.tanh(neg_log_a) * (a * a + 1.0), 0.0)
        mult = jnp.exp2(jnp.log(one_minus_a2) * (0.5 * LOG2_E))
        u = mult * (i * uc)
        for s in range(n_slabs):
            lanes = slice(s * LANES, (s + 1) * LANES)
            a_p[s, pl.ds(rc * pitch, row_chunk), :] = a[:, lanes]
            h_p[s, pl.ds(rc * pitch, row_chunk), :] = u[:, lanes]
        return carry

    lax.fori_loop(0, n_chunks, gate_chunk, 0)

    def scan_step(j, carry):
        rows = pl.ds(j, n_chunks, stride=pitch)
        out = []
        for s in range(n_slabs):
            h_prev, p_prev = carry[2 * s], carry[2 * s + 1]
            a = a_p[s, rows, :]
            h = a * h_prev + h_p[s, rows, :]
            p = a * p_prev
            h_p[s, rows, :] = h
            a_p[s, rows, :] = p
            out += [h, p]
        return tuple(out)

    init = (jnp.zeros((n_chunks, LANES), F32), jnp.ones((n_chunks, LANES), F32)) * n_slabs
    ends = lax.fori_loop(0, row_chunk, scan_step, init, unroll=8)

    for s in range(n_slabs):
        lanes = slice(s * LANES, (s + 1) * LANES)
        h_end, p_end = ends[2 * s], ends[2 * s + 1]
        h_in = jnp.zeros((1, LANES), F32)
        for seg in range(n_chunks):
            rows = slice(seg * pitch, seg * pitch + row_chunk)
            h = h_p[s, rows, :] + a_p[s, rows, :] * h_in
            nat = slice(seg * row_chunk, (seg + 1) * row_chunk)
            o_ref[nat, lanes] = (h * gate_s[nat, lanes]).astype(BF16)
            h_in = h_end[seg:seg + 1] + p_end[seg:seg + 1] * h_in


def _rglru(x, w_in, b_in, conv_w, conv_b, lru_wr, lru_br, lru_wi, lru_bi, lam):
    B, S, D = x.shape
    C = LRU_SLAB
    n_slabs = LRU_W // C
    per = C // LRU_BLOCK
    row_chunk = S // SUBLANES
    pitch = row_chunk + SUBLANES // 2
    kern = functools.partial(_rglru_kernel, S=S, row_chunk=row_chunk, pitch=pitch)
    col = lambda c0: (lambda b, c: (0, c0 // C + c))
    row1 = lambda: pl.BlockSpec((1, C), lambda b, c: (0, c))
    blocks = lambda: pl.BlockSpec((per, LRU_BLOCK, LRU_BLOCK), lambda b, c: (c, 0, 0))
    return pl.pallas_call(
        kern,
        grid=(B, n_slabs),
        in_specs=[
            pl.BlockSpec((None, S, D), lambda b, c: (b, 0, 0), pipeline_mode=ONCE),
            pl.BlockSpec((D, C), col(COL_U)), pl.BlockSpec((D, C), col(COL_GATE_L)),
            pl.BlockSpec((1, C), col(COL_U)), pl.BlockSpec((1, C), col(COL_GATE_L)),
            pl.BlockSpec((CONV_W, C), lambda b, c: (0, c)), row1(),
            blocks(), blocks(), row1(), row1(), row1(),
        ],
        out_specs=pl.BlockSpec((None, S, C), lambda b, c: (b, 0, c)),
        out_shape=jax.ShapeDtypeStruct((B, S, LRU_W), BF16),
        scratch_shapes=[
            pltpu.VMEM((D, 2 * C), BF16),
            pltpu.VMEM((C, 2 * C), BF16),
            pltpu.VMEM((C // LANES, S + SUBLANES, LANES), F32),
            pltpu.VMEM((S, C), F32),
            pltpu.VMEM((C // LANES, SUBLANES * pitch, LANES), F32),
            pltpu.VMEM((C // LANES, SUBLANES * pitch, LANES), F32),
        ],
        compiler_params=_cparams(2),
        name="rglru",
    )(x, w_in, w_in, b_in, b_in, conv_w, conv_b, lru_wr, lru_wi, lru_br, lru_bi, lam)


def _merge_out_kernel(x_ref, attn_ref, hg_ref, wga_ref, wgb_ref, bga_ref, bgb_ref,
                      wa_ref, wl_ref, wo_ref, bo_ref, lng_ref, lnb_ref, o_ref,
                      wg_s, wa_s, wl_s, wo_s):
    D = x_ref.shape[1]

    @pl.when(pl.program_id(0) == 0)
    def _():
        rows = 256
        for r0 in range(0, D, rows):
            sl = slice(r0, r0 + rows)
            wg_s[sl, :D] = wga_ref[sl, :].astype(BF16)
            wg_s[sl, D:] = wgb_ref[sl, :].astype(BF16)
            wl_s[sl, :] = wl_ref[sl, :].astype(BF16)
            wo_s[sl, :] = wo_ref[sl, :].astype(BF16)
        for r0 in range(0, ATTN_W, rows):
            sl = slice(r0, r0 + rows)
            wa_s[sl, :] = wa_ref[sl, :].astype(BF16)

    x = x_ref[...]
    xb = x.astype(BF16)
    gate_a = jax.nn.sigmoid(
        jnp.dot(xb, wg_s[:, :D], preferred_element_type=F32) + bga_ref[...])
    gate_b = jax.nn.sigmoid(
        jnp.dot(xb, wg_s[:, D:], preferred_element_type=F32) + bgb_ref[...])
    y_a = jnp.dot(attn_ref[...].astype(BF16), wa_s[...], preferred_element_type=F32)
    y_b = jnp.dot(hg_ref[...], wl_s[...], preferred_element_type=F32)
    merged = gate_a * y_a + gate_b * y_b
    out = jnp.dot(merged.astype(BF16), wo_s[...], preferred_element_type=F32) + bo_ref[...]
    y = ALPHA * x + out
    mu = jnp.mean(y, axis=1, keepdims=True)
    yc = y - mu
    var = jnp.mean(yc * yc, axis=1, keepdims=True)
    o_ref[...] = yc * lax.rsqrt(var + LN_EPS) * lng_ref[...] + lnb_ref[...]


def _merge_out(x2, attn2, hg2, w_in, b_in, w_a, w_l, w_o, b_o, ln_g, ln_b, tm=512):
    T, D = x2.shape
    once = lambda shape, idx: pl.BlockSpec(shape, lambda i: idx, pipeline_mode=ONCE)
    merge_blk = COL_MERGE // D
    return pl.pallas_call(
        _merge_out_kernel,
        grid=(T // tm,),
        in_specs=[
            pl.BlockSpec((tm, D), lambda i: (i, 0)),
            pl.BlockSpec((tm, ATTN_W), lambda i: (i, 0)),
            pl.BlockSpec((tm, LRU_W), lambda i: (i, 0)),
            once((D, D), (0, merge_blk)), once((D, D), (0, merge_blk + 1)),
            once((1, D), (0, merge_blk)), once((1, D), (0, merge_blk + 1)),
            once((ATTN_W, D), (0, 0)), once((LRU_W, D), (0, 0)), once((D, D), (0, 0)),
            once((1, D), (0, 0)), once((1, D), (0, 0)), once((1, D), (0, 0)),
        ],
        out_specs=pl.BlockSpec((tm, D), lambda i: (i, 0)),
        out_shape=jax.ShapeDtypeStruct((T, D), F32),
        scratch_shapes=[
            pltpu.VMEM((D, 2 * D), BF16), pltpu.VMEM((ATTN_W, D), BF16),
            pltpu.VMEM((LRU_W, D), BF16), pltpu.VMEM((D, D), BF16),
        ],
        compiler_params=_cparams(1),
        name="merge_out",
    )(x2, attn2, hg2, w_in, w_in, b_in, b_in, w_a, w_l, w_o, b_o, ln_g, ln_b)


def kernel(x, w_in, b_in, conv_w, conv_b, lru_wr, lru_br, lru_wi, lru_bi, lru_lambda,
           w_attn_proj, w_lru_proj, w_out, b_out, ln_gain, ln_bias):
    B, S, D = x.shape
    assert w_in.shape[0] == DEPTH and S % (CLASSES * BAND) == 0 and D == D_MODEL
    assert COL_MERGE % D == 0 and w_in.shape[2] == COL_MERGE + 2 * D
    row = lambda v: v.reshape(1, -1)
    for l in range(DEPTH):
        wl, bl = w_in[l], row(b_in[l])
        attn = _attention(_permute_cast(x), wl, bl)
        hg = _rglru(x, wl, bl, conv_w[l], row(conv_b[l]), lru_wr[l], row(lru_br[l]),
                    lru_wi[l], row(lru_bi[l]), row(lru_lambda[l]))
        out = _merge_out(
            x.reshape(B * S, D), attn.reshape(B * S, ATTN_W), hg.reshape(B * S, LRU_W),
            wl, bl, w_attn_proj[l], w_lru_proj[l], w_out[l],
            row(b_out[l]), row(ln_gain[l]), row(ln_bias[l]))
        x = out.reshape(B, S, D)
    return x
```

```python
import functools
import math

import jax
import jax.numpy as jnp
import numpy as np
from jax import lax
from jax.experimental import pallas as pl
from jax.experimental.pallas import tpu as pltpu

LANES = 128
SUBLANES = 8
BF16_ROWS = 16
VMEM_LIMIT_BYTES = 56 * 1024 * 1024

D_MODEL = 1024
HEAD_DIM = 64
HALF = HEAD_DIM // 2
HEADS_PER_GROUP = 8
DILATIONS = (1, 4, 16)
N_GROUPS = len(DILATIONS)
BAND = 128
CLASSES = 16
QKV_W = N_GROUPS * HEADS_PER_GROUP * HEAD_DIM
ATTN_W = HEADS_PER_GROUP * HEAD_DIM
N_PAIRS = ATTN_W // LANES
LRU_W = 1024
LRU_BLOCK = 64
LRU_SLAB = 256
CONV_W = 4
LRU_C = 8.0
ROPE_THETA = 10000.0
LOG2_E = math.log2(math.e)
NEG_INF = -1e30
LN_EPS = 1e-5
DEPTH = 1
ALPHA = (2.0 * DEPTH) ** 0.25
N_ATTN_SLABS = 3 * N_GROUPS + 1
ATTN_COLS = N_ATTN_SLABS * LANES
COL_K = QKV_W
COL_V = 2 * QKV_W
COL_GATE_A = 3 * QKV_W
COL_U = COL_GATE_A + ATTN_W
COL_GATE_L = COL_U + LRU_W
COL_MERGE = COL_GATE_L + LRU_W

BF16 = jnp.bfloat16
F32 = jnp.float32
ONCE = pl.Buffered(1)


def _cparams(n_axes):
    return pltpu.CompilerParams(dimension_semantics=("arbitrary",) * n_axes,
                                vmem_limit_bytes=VMEM_LIMIT_BYTES)


def _permute_cast_kernel(x_ref, xp_ref, *, rows_per_class):
    for c in range(CLASSES):
        xp_ref[c * rows_per_class:(c + 1) * rows_per_class, :] = (
            x_ref[pl.ds(c, rows_per_class, stride=CLASSES), :].astype(BF16))


def _permute_cast(x):
    B, S, D = x.shape
    kern = functools.partial(_permute_cast_kernel, rows_per_class=S // CLASSES)
    spec = pl.BlockSpec((None, S, LANES), lambda b, j: (b, 0, j))
    return pl.pallas_call(
        kern,
        grid=(B, D // LANES),
        in_specs=[spec],
        out_specs=spec,
        out_shape=jax.ShapeDtypeStruct((B, S, D), BF16),
        compiler_params=_cparams(2),
        name="permute_cast",
    )(x)


def _band_bias():
    out = np.zeros((N_GROUPS, 2, 2 * BAND, 2 * BAND), np.float32)
    i = np.arange(BAND)
    for g, d in enumerate(DILATIONS):
        R = CLASSES // d
        L = BAND // R
        m = R * (i % L) + i // L
        m_k = np.concatenate([m - BAND, m])
        dist = m[:, None] - m_k[None, :]
        band = (dist >= 0) & (dist <= BAND)
        for first in range(2):
            valid = band & ((np.arange(2 * BAND) >= BAND)[None, :] | (first == 0))
            bias = np.where(valid, 0.0, NEG_INF).astype(np.float32)
            out[g, first] = np.concatenate([bias, bias], axis=0)
    return out


def _rope_tables(S):
    A = S // CLASSES
    r = np.arange(S)
    pos = (CLASSES * (r % A) + r // A).astype(np.float64)
    inv_freq = ROPE_THETA ** (-np.arange(HALF, dtype=np.float64) / HALF)
    ang = pos[:, None] * inv_freq[None, :]
    cos_t = np.tile(np.cos(ang), (1, LANES // HALF))
    sin = np.sin(ang)
    sin_t = np.concatenate([-sin, -sin, sin, sin], axis=1)
    return cos_t.astype(np.float32), sin_t.astype(np.float32)


def _pair_interleave(w):
    lane = lax.broadcasted_iota(jnp.int32, w.shape, 1)
    from_right = pltpu.roll(w, LANES - HALF, axis=1)
    from_left = pltpu.roll(w, HALF, axis=1)
    quarter = lane // HALF
    return jnp.where(quarter == 1, from_right, jnp.where(quarter == 2, from_left, w))


def _attention_kernel(x_ref, *refs, S, row_chunk, tile_unroll):
    w_refs = refs[:N_ATTN_SLABS]
    b_refs = refs[N_ATTN_SLABS:2 * N_ATTN_SLABS]
    cos_ref, sin_ref, bias_ref, o_ref = refs[2 * N_ATTN_SLABS:2 * N_ATTN_SLABS + 4]
    w_s, b_s, qkv0_s, qkv_s, gate_s, acc_s, m_s, l_s = refs[2 * N_ATTN_SLABS + 4:]
    D = x_ref.shape[1]
    A = S // CLASSES

    def qkv_ref(g, i):
        return qkv0_s.at[i] if g == 0 else qkv_s.at[3 * (g - 1) + i]

    n_tiles = S // BAND
    lane = lax.broadcasted_iota(jnp.int32, (BAND, LANES), 1)
    head0_q = ((lane // HALF) % 2) == 0
    head0_v = lane < HEAD_DIM
    q_scale = (HEAD_DIM ** -0.5) * LOG2_E

    w_rows = 256
    for i in range(N_ATTN_SLABS):
        is_qk = i < 3 * N_GROUPS and i % 3 != 2
        cols = slice(i * LANES, (i + 1) * LANES)
        for r0 in range(0, D, w_rows):
            w = w_refs[i][r0:r0 + w_rows, :]
            w_s[r0:r0 + w_rows, cols] = (_pair_interleave(w) if is_qk else w).astype(BF16)
        b = jnp.broadcast_to(b_refs[i][...], (SUBLANES, LANES))
        b_s[:, cols] = _pair_interleave(b) if is_qk else b

    def proj_chunk(rc, carry):
        rows = pl.ds(pl.multiple_of(rc * row_chunk, row_chunk), row_chunk)
        res = jnp.dot(x_ref[rows, :], w_s[...], preferred_element_type=F32) + b_s[0:1, :]
        cos = cos_ref[rows, :]
        sin = sin_ref[rows, :]
        for g in range(N_GROUPS):
            col = 3 * g * LANES
            q = res[:, col:col + LANES]
            k = res[:, col + LANES:col + 2 * LANES]
            q = (q * cos + pltpu.roll(q, LANES // 2, axis=1) * sin) * q_scale
            k = k * cos + pltpu.roll(k, LANES // 2, axis=1) * sin
            for i, val in enumerate((q, k, res[:, col + 2 * LANES:col + 3 * LANES])):
                ref = qkv_ref(g, i)
                ref[rows, :] = val.astype(ref.dtype)
        gate = res[:, 3 * N_GROUPS * LANES:]
        gate_s[rows, :] = (gate * jax.nn.sigmoid(gate)).astype(BF16)
        return carry

    lax.fori_loop(0, S // row_chunk, proj_chunk, 0)

    for g, d in enumerate(DILATIONS):
        R = CLASSES // d
        L = BAND // R
        nb = S // (d * BAND)

        def tile_group(it, carry, d=d, R=R, L=L, nb=nb, g=g):
            def make_run(e):
                rho = lax.shift_right_logical(e, int(math.log2(nb)))
                n = lax.bitwise_and(e, nb - 1)

                def run(j, nn):
                    start = (rho + d * j) * A + L * nn
                    return pl.ds(pl.multiple_of(start, min(L, BF16_ROWS)), L)
                return n, run

            def gather(ref, run, nn):
                return jnp.concatenate([ref[run(j, nn), :] for j in range(R)], axis=0)

            def scores(n, run):
                q_t = gather(qkv_ref(g, 0), run, n).astype(BF16)
                zero = jnp.zeros_like(q_t)
                qm = jnp.concatenate([jnp.where(head0_q, q_t, zero),
                                      jnp.where(head0_q, zero, q_t)], axis=0)
                n_prev = jnp.maximum(n - 1, 0)
                k_ref = qkv_ref(g, 1)
                kwin = jnp.concatenate([gather(k_ref, run, n_prev), gather(k_ref, run, n)],
                                       axis=0).astype(BF16)
                s = lax.dot_general(qm, kwin, (((1,), (1,)), ((), ())),
                                    preferred_element_type=F32)
                return s + bias_ref[g, 1 - jnp.minimum(n, 1)]

            def softmax(s):
                m_blk = jnp.max(s, axis=1, keepdims=True)
                p = jnp.exp2(s - m_blk)
                return p.astype(BF16), m_blk, jnp.sum(p, axis=1, keepdims=True)

            def finish(n, run, p, m_blk, l_blk):
                n_prev = jnp.maximum(n - 1, 0)
                v_ref = qkv_ref(g, 2)
                vwin = jnp.concatenate([gather(v_ref, run, n_prev), gather(v_ref, run, n)],
                                       axis=0).astype(BF16)
                pv = jnp.dot(p, vwin, preferred_element_type=F32)
                o_t = jnp.where(head0_v, pv[:BAND], pv[BAND:])
                m_t = jnp.where(head0_v, m_blk[:BAND], m_blk[BAND:])
                l_t = jnp.where(head0_v, l_blk[:BAND], l_blk[BAND:])
                if g > 0:
                    m_old = gather(m_s, run, n)
                    m_new = jnp.maximum(m_old, m_t)
                    w_old = jnp.exp2(m_old - m_new)
                    w_new = jnp.exp2(m_t - m_new)
                    m_t = m_new
                    l_t = gather(l_s, run, n) * w_old + l_t * w_new
                    o_t = gather(acc_s, run, n) * w_old + o_t * w_new
                for j in range(R):
                    m_s[run(j, n), :] = m_t[j * L:(j + 1) * L]
                    l_s[run(j, n), :] = l_t[j * L:(j + 1) * L]
                    acc_s[run(j, n), :] = o_t[j * L:(j + 1) * L]

            ctx = [make_run(it * tile_unroll + u) for u in range(tile_unroll)]
            sc, sm = {}, {}
            for t in range(tile_unroll + 2):
                if t < tile_unroll:
                    sc[t] = scores(*ctx[t])
                if 0 <= t - 1 < tile_unroll:
                    sm[t - 1] = softmax(sc.pop(t - 1))
                if 0 <= t - 2 < tile_unroll:
                    finish(*ctx[t - 2], *sm.pop(t - 2))
            return carry

        lax.fori_loop(0, n_tiles // tile_unroll, tile_group, 0)

    for c in range(CLASSES):
        rows = slice(c * A, (c + 1) * A)
        res = (acc_s[rows, :] / l_s[rows, :]) * gate_s[rows, :].astype(F32)
        o_ref[pl.ds(c, A, stride=CLASSES), :] = res


def _attention(xp, w_in, b_in):
    B, S, D = xp.shape
    kern = functools.partial(_attention_kernel, S=S, row_chunk=512, tile_unroll=16)
    slab0 = []
    for g in range(N_GROUPS):
        slab0 += [(c0 + g * ATTN_W) // LANES for c0 in (0, COL_K, COL_V)]
    slab0.append(COL_GATE_A // LANES)
    w_specs = [pl.BlockSpec((D, LANES), functools.partial(lambda b, p, s: (0, s + p), s=s))
               for s in slab0]
    b_specs = [pl.BlockSpec((1, LANES), functools.partial(lambda b, p, s: (0, s + p), s=s))
               for s in slab0]
    cos_t, sin_t = _rope_tables(S)
    const2 = pl.BlockSpec((S, LANES), lambda b, p: (0, 0), pipeline_mode=ONCE)
    scratch = [
        pltpu.VMEM((D, ATTN_COLS), BF16),
        pltpu.VMEM((SUBLANES, ATTN_COLS), F32),
        pltpu.VMEM((3, S, LANES), F32),
        pltpu.VMEM((3 * (N_GROUPS - 1), S, LANES), BF16),
        pltpu.VMEM((S, LANES), BF16),
        pltpu.VMEM((S, LANES), F32),
        pltpu.VMEM((S, LANES), F32),
        pltpu.VMEM((S, LANES), F32),
    ]
    return pl.pallas_call(
        kern,
        grid=(B, N_PAIRS),
        in_specs=[pl.BlockSpec((None, S, D), lambda b, p: (b, 0, 0), pipeline_mode=ONCE)]
        + w_specs + b_specs + [
            const2, const2,
            pl.BlockSpec((N_GROUPS, 2, 2 * BAND, 2 * BAND), lambda b, p: (0, 0, 0, 0),
                         pipeline_mode=ONCE),
        ],
        out_specs=pl.BlockSpec((None, S, LANES), lambda b, p: (b, 0, p)),
        out_shape=jax.ShapeDtypeStruct((B, S, ATTN_W), F32),
        scratch_shapes=scratch,
        compiler_params=_cparams(2),
        name="attention",
    )(xp, *([w_in] * N_ATTN_SLABS), *([b_in] * N_ATTN_SLABS), cos_t, sin_t, _band_bias())


def _rglru_kernel(x_ref, wu_ref, wl_ref, bu_ref, bl_ref, cw_ref, cb_ref, wr_ref, wi_ref,
                  br_ref, bi_ref, lam_ref, o_ref,
                  w_s, wg_s, u_pad, gate_s, a_p, h_p, *, S, row_chunk, pitch):
    C = LRU_SLAB
    n_slabs = C // LANES
    n_chunks = S // row_chunk
    u_pad[:, 0:SUBLANES, :] = jnp.zeros((n_slabs, SUBLANES, LANES), F32)

    w_s[:, :C] = wu_ref[...].astype(BF16)
    w_s[:, C:] = wl_ref[...].astype(BF16)
    wg_s[...] = jnp.zeros(wg_s.shape, BF16)
    for blk in range(C // LRU_BLOCK):
        rows = slice(blk * LRU_BLOCK, (blk + 1) * LRU_BLOCK)
        wg_s[rows, blk * LRU_BLOCK:(blk + 1) * LRU_BLOCK] = wr_ref[blk].astype(BF16)
        wg_s[rows, C + blk * LRU_BLOCK:C + (blk + 1) * LRU_BLOCK] = wi_ref[blk].astype(BF16)

    def proj_chunk(rc, carry):
        r0 = pl.multiple_of(rc * row_chunk, row_chunk)
        xb = x_ref[pl.ds(r0, row_chunk), :].astype(BF16)
        res = jnp.dot(xb, w_s[...], preferred_element_type=F32)
        u = res[:, :C] + bu_ref[...]
        for s in range(C // LANES):
            u_pad[s, pl.ds(r0 + SUBLANES, row_chunk), :] = u[:, s * LANES:(s + 1) * LANES]
        gl = res[:, C:] + bl_ref[...]
        gate_s[pl.ds(r0, row_chunk), :] = gl * jax.nn.sigmoid(gl)
        return carry

    lax.fori_loop(0, n_chunks, proj_chunk, 0)

    neg_lam = -lam_ref[...]
    softplus = jnp.maximum(neg_lam, 0.0) + jnp.log1p(jnp.exp(-jnp.abs(neg_lam)))
    decay_rate = LRU_C * softplus

    def gate_chunk(rc, carry):
        r0 = pl.multiple_of(rc * row_chunk, row_chunk)
        parts = []
        for s in range(C // LANES):
            lanes = slice(s * LANES, (s + 1) * LANES)
            acc = cb_ref[:, lanes]
            for j in range(CONV_W):
                off = SUBLANES - (CONV_W - 1) + j
                acc = acc + u_pad[s, pl.ds(r0 + off, row_chunk), :] * cw_ref[j:j + 1, lanes]
            parts.append(acc)
        uc = jnp.concatenate(parts, axis=1)
        gates = jnp.dot(uc.astype(BF16), wg_s[...], preferred_element_type=F32)
        r = jax.nn.sigmoid(gates[:, :C] + br_ref[...])
        i = jax.nn.sigmoid(gates[:, C:] + bi_ref[...])
        neg_log_a = r * decay_rate
        a = jnp.exp(-neg_log_a)
        one_minus_a2 = jnp.maximum(jnp.tanh(neg_log_a) * (a * a + 1.0), 0.0)
        mult = jnp.exp2(jnp.log(one_minus_a2) * (0.5 * LOG2_E))
        u = mult * (i * uc)
        for s in range(n_slabs):
            lanes = slice(s * LANES, (s + 1) * LANES)
            a_p[s, pl.ds(rc * pitch, row_chunk), :] = a[:, lanes]
            h_p[s, pl.ds(rc * pitch, row_chunk), :] = u[:, lanes]
        return carry

    lax.fori_loop(0, n_chunks, gate_chunk, 0)

    def scan_step(j, carry):
        rows = pl.ds(j, n_chunks, stride=pitch)
        out = []
        for s in range(n_slabs):
            h_prev, p_prev = carry[2 * s], carry[2 * s + 1]
            a = a_p[s, rows, :]
            h = a * h_prev + h_p[s, rows, :]
            p = a * p_prev
            h_p[s, rows, :] = h
            a_p[s, rows, :] = p
            out += [h, p]
        return tuple(out)

    init = (jnp.zeros((n_chunks, LANES), F32), jnp.ones((n_chunks, LANES), F32)) * n_slabs
    ends = lax.fori_loop(0, row_chunk, scan_step, init, unroll=8)

    for s in range(n_slabs):
        lanes = slice(s * LANES, (s + 1) * LANES)
        h_end, p_end = ends[2 * s], ends[2 * s + 1]
        h_in = jnp.zeros((1, LANES), F32)
        for seg in range(n_chunks):
            rows = slice(seg * pitch, seg * pitch + row_chunk)
            h = h_p[s, rows, :] + a_p[s, rows, :] * h_in
            nat = slice(seg * row_chunk, (seg + 1) * row_chunk)
            o_ref[nat, lanes] = (h * gate_s[nat, lanes]).astype(BF16)
            h_in = h_end[seg:seg + 1] + p_end[seg:seg + 1] * h_in


def _rglru(x, w_in, b_in, conv_w, conv_b, lru_wr, lru_br, lru_wi, lru_bi, lam):
    B, S, D = x.shape
    C = LRU_SLAB
    n_slabs = LRU_W // C
    per = C // LRU_BLOCK
    row_chunk = S // SUBLANES
    pitch = row_chunk + SUBLANES // 2
    kern = functools.partial(_rglru_kernel, S=S, row_chunk=row_chunk, pitch=pitch)
    col = lambda c0: (lambda b, c: (0, c0 // C + c))
    row1 = lambda: pl.BlockSpec((1, C), lambda b, c: (0, c))
    blocks = lambda: pl.BlockSpec((per, LRU_BLOCK, LRU_BLOCK), lambda b, c: (c, 0, 0))
    return pl.pallas_call(
        kern,
        grid=(B, n_slabs),
        in_specs=[
            pl.BlockSpec((None, S, D), lambda b, c: (b, 0, 0), pipeline_mode=ONCE),
            pl.BlockSpec((D, C), col(COL_U)), pl.BlockSpec((D, C), col(COL_GATE_L)),
            pl.BlockSpec((1, C), col(COL_U)), pl.BlockSpec((1, C), col(COL_GATE_L)),
            pl.BlockSpec((CONV_W, C), lambda b, c: (0, c)), row1(),
            blocks(), blocks(), row1(), row1(), row1(),
        ],
        out_specs=pl.BlockSpec((None, S, C), lambda b, c: (b, 0, c)),
        out_shape=jax.ShapeDtypeStruct((B, S, LRU_W), BF16),
        scratch_shapes=[
            pltpu.VMEM((D, 2 * C), BF16),
            pltpu.VMEM((C, 2 * C), BF16),
            pltpu.VMEM((C // LANES, S + SUBLANES, LANES), F32),
            pltpu.VMEM((S, C), F32),
            pltpu.VMEM((C // LANES, SUBLANES * pitch, LANES), F32),
            pltpu.VMEM((C // LANES, SUBLANES * pitch, LANES), F32),
        ],
        compiler_params=_cparams(2),
        name="rglru",
    )(x, w_in, w_in, b_in, b_in, conv_w, conv_b, lru_wr, lru_wi, lru_br, lru_bi, lam)


def _merge_out_kernel(x_ref, attn_ref, hg_ref, wga_ref, wgb_ref, bga_ref, bgb_ref,
                      wa_ref, wl_ref, wo_ref, bo_ref, lng_ref, lnb_ref, o_ref,
                      wg_s, wa_s, wl_s, wo_s):
    D = x_ref.shape[1]

    @pl.when(pl.program_id(0) == 0)
    def _():
        rows = 256
        for r0 in range(0, D, rows):
            sl = slice(r0, r0 + rows)
            wg_s[sl, :D] = wga_ref[sl, :].astype(BF16)
            wg_s[sl, D:] = wgb_ref[sl, :].astype(BF16)
            wl_s[sl, :] = wl_ref[sl, :].astype(BF16)
            wo_s[sl, :] = wo_ref[sl, :].astype(BF16)
        for r0 in range(0, ATTN_W, rows):
            sl = slice(r0, r0 + rows)
            wa_s[sl, :] = wa_ref[sl, :].astype(BF16)

    x = x_ref[...]
    xb = x.astype(BF16)
    gate_a = jax.nn.sigmoid(
        jnp.dot(xb, wg_s[:, :D], preferred_element_type=F32) + bga_ref[...])
    gate_b = jax.nn.sigmoid(
        jnp.dot(xb, wg_s[:, D:], preferred_element_type=F32) + bgb_ref[...])
    y_a = jnp.dot(attn_ref[...].astype(BF16), wa_s[...], preferred_element_type=F32)
    y_b = jnp.dot(hg_ref[...], wl_s[...], preferred_element_type=F32)
    merged = gate_a * y_a + gate_b * y_b
    out = jnp.dot(merged.astype(BF16), wo_s[...], preferred_element_type=F32) + bo_ref[...]
    y = ALPHA * x + out
    mu = jnp.mean(y, axis=1, keepdims=True)
    yc = y - mu
    var = jnp.mean(yc * yc, axis=1, keepdims=True)
    o_ref[...] = yc * lax.rsqrt(var + LN_EPS) * lng_ref[...] + lnb_ref[...]


def _merge_out(x2, attn2, hg2, w_in, b_in, w_a, w_l, w_o, b_o, ln_g, ln_b, tm=512):
    T, D = x2.shape
    once = lambda shape, idx: pl.BlockSpec(shape, lambda i: idx, pipeline_mode=ONCE)
    merge_blk = COL_MERGE // D
    return pl.pallas_call(
        _merge_out_kernel,
        grid=(T // tm,),
        in_specs=[
            pl.BlockSpec((tm, D), lambda i: (i, 0)),
            pl.BlockSpec((tm, ATTN_W), lambda i: (i, 0)),
            pl.BlockSpec((tm, LRU_W), lambda i: (i, 0)),
            once((D, D), (0, merge_blk)), once((D, D), (0, merge_blk + 1)),
            once((1, D), (0, merge_blk)), once((1, D), (0, merge_blk + 1)),
            once((ATTN_W, D), (0, 0)), once((LRU_W, D), (0, 0)), once((D, D), (0, 0)),
            once((1, D), (0, 0)), once((1, D), (0, 0)), once((1, D), (0, 0)),
        ],
        out_specs=pl.BlockSpec((tm, D), lambda i: (i, 0)),
        out_shape=jax.ShapeDtypeStruct((T, D), F32),
        scratch_shapes=[
            pltpu.VMEM((D, 2 * D), BF16), pltpu.VMEM((ATTN_W, D), BF16),
            pltpu.VMEM((LRU_W, D), BF16), pltpu.VMEM((D, D), BF16),
        ],
        compiler_params=_cparams(1),
        name="merge_out",
    )(x2, attn2, hg2, w_in, w_in, b_in, b_in, w_a, w_l, w_o, b_o, ln_g, ln_b)


def kernel(x, w_in, b_in, conv_w, conv_b, lru_wr, lru_br, lru_wi, lru_bi, lru_lambda,
           w_attn_proj, w_lru_proj, w_out, b_out, ln_gain, ln_bias):
    B, S, D = x.shape
    assert w_in.shape[0] == DEPTH and S % (CLASSES * BAND) == 0 and D == D_MODEL
    assert COL_MERGE % D == 0 and w_in.shape[2] == COL_MERGE + 2 * D
    row = lambda v: v.reshape(1, -1)
    for l in range(DEPTH):
        wl, bl = w_in[l], row(b_in[l])
        attn = _attention(_permute_cast(x), wl, bl)
        hg = _rglru(x, wl, bl, conv_w[l], row(conv_b[l]), lru_wr[l], row(lru_br[l]),
                    lru_wi[l], row(lru_bi[l]), row(lru_lambda[l]))
        out = _merge_out(
            x.reshape(B * S, D), attn.reshape(B * S, ATTN_W), hg.reshape(B * S, LRU_W),
            wl, bl, w_attn_proj[l], w_lru_proj[l], w_out[l],
            row(b_out[l]), row(ln_gain[l]), row(ln_bias[l]))
        x = out.reshape(B, S, D)
    return x
```

```python
import functools
import math

import jax
import jax.numpy as jnp
import numpy as np
from jax import lax
from jax.experimental import pallas as pl
from jax.experimental.pallas import tpu as pltpu

LANES = 128
SUBLANES = 8
BF16_ROWS = 16
VMEM_LIMIT_BYTES = 56 * 1024 * 1024

D_MODEL = 1024
HEAD_DIM = 64
HALF = HEAD_DIM // 2
HEADS_PER_GROUP = 8
DILATIONS = (1, 4, 16)
N_GROUPS = len(DILATIONS)
BAND = 128
CLASSES = 16
QKV_W = N_GROUPS * HEADS_PER_GROUP * HEAD_DIM
ATTN_W = HEADS_PER_GROUP * HEAD_DIM
N_PAIRS = ATTN_W // LANES
LRU_W = 1024
LRU_BLOCK = 64
LRU_SLAB = 256
CONV_W = 4
LRU_C = 8.0
ROPE_THETA = 10000.0
LOG2_E = math.log2(math.e)
NEG_INF = -1e30
LN_EPS = 1e-5
DEPTH = 1
ALPHA = (2.0 * DEPTH) ** 0.25
N_ATTN_SLABS = 3 * N_GROUPS + 1
ATTN_COLS = N_ATTN_SLABS * LANES
COL_K = QKV_W
COL_V = 2 * QKV_W
COL_GATE_A = 3 * QKV_W
COL_U = COL_GATE_A + ATTN_W
COL_GATE_L = COL_U + LRU_W
COL_MERGE = COL_GATE_L + LRU_W

BF16 = jnp.bfloat16
F32 = jnp.float32
ONCE = pl.Buffered(1)


def _cparams(n_axes):
    return pltpu.CompilerParams(dimension_semantics=("arbitrary",) * n_axes,
                                vmem_limit_bytes=VMEM_LIMIT_BYTES)


def _permute_cast_kernel(x_ref, xp_ref, *, rows_per_class):
    for c in range(CLASSES):
        xp_ref[c * rows_per_class:(c + 1) * rows_per_class, :] = (
            x_ref[pl.ds(c, rows_per_class, stride=CLASSES), :].astype(BF16))


def _permute_cast(x):
    B, S, D = x.shape
    kern = functools.partial(_permute_cast_kernel, rows_per_class=S // CLASSES)
    spec = pl.BlockSpec((None, S, LANES), lambda b, j: (b, 0, j))
    return pl.pallas_call(
        kern,
        grid=(B, D // LANES),
        in_specs=[spec],
        out_specs=spec,
        out_shape=jax.ShapeDtypeStruct((B, S, D), BF16),
        compiler_params=_cparams(2),
        name="permute_cast",
    )(x)


def _band_bias():
    out = np.zeros((N_GROUPS, 2, 2 * BAND, 2 * BAND), np.float32)
    i = np.arange(BAND)
    for g, d in enumerate(DILATIONS):
        R = CLASSES // d
        L = BAND // R
        m = R * (i % L) + i // L
        m_k = np.concatenate([m - BAND, m])
        dist = m[:, None] - m_k[None, :]
        band = (dist >= 0) & (dist <= BAND)
        for first in range(2):
            valid = band & ((np.arange(2 * BAND) >= BAND)[None, :] | (first == 0))
            bias = np.where(valid, 0.0, NEG_INF).astype(np.float32)
            out[g, first] = np.concatenate([bias, bias], axis=0)
    return out


def _rope_tables(S):
    A = S // CLASSES
    r = np.arange(S)
    pos = (CLASSES * (r % A) + r // A).astype(np.float64)
    inv_freq = ROPE_THETA ** (-np.arange(HALF, dtype=np.float64) / HALF)
    ang = pos[:, None] * inv_freq[None, :]
    cos_t = np.tile(np.cos(ang), (1, LANES // HALF))
    sin = np.sin(ang)
    sin_t = np.concatenate([-sin, -sin, sin, sin], axis=1)
    return cos_t.astype(np.float32), sin_t.astype(np.float32)


def _pair_interleave(w):
    lane = lax.broadcasted_iota(jnp.int32, w.shape, 1)
    from_right = pltpu.roll(w, LANES - HALF, axis=1)
    from_left = pltpu.roll(w, HALF, axis=1)
    quarter = lane // HALF
    return jnp.where(quarter == 1, from_right, jnp.where(quarter == 2, from_left, w))


def _attention_kernel(x_ref, *refs, S, row_chunk, tile_unroll):
    w_refs = refs[:N_ATTN_SLABS]
    b_refs = refs[N_ATTN_SLABS:2 * N_ATTN_SLABS]
    cos_ref, sin_ref, bias_ref, o_ref = refs[2 * N_ATTN_SLABS:2 * N_ATTN_SLABS + 4]
    w_s, b_s, qkv0_s, qkv_s, gate_s, acc_s, m_s, l_s = refs[2 * N_ATTN_SLABS + 4:]
    D = x_ref.shape[1]
    A = S // CLASSES

    def qkv_ref(g, i):
        return qkv0_s.at[i] if g == 0 else qkv_s.at[3 * (g - 1) + i]

    n_tiles = S // BAND
    lane = lax.broadcasted_iota(jnp.int32, (BAND, LANES), 1)
    head0_q = ((lane // HALF) % 2) == 0
    head0_v = lane < HEAD_DIM
    q_scale = (HEAD_DIM ** -0.5) * LOG2_E

    w_rows = 256
    for i in range(N_ATTN_SLABS):
        is_qk = i < 3 * N_GROUPS and i % 3 != 2
        cols = slice(i * LANES, (i + 1) * LANES)
        for r0 in range(0, D, w_rows):
            w = w_refs[i][r0:r0 + w_rows, :]
            w_s[r0:r0 + w_rows, cols] = (_pair_interleave(w) if is_qk else w).astype(BF16)
        b = jnp.broadcast_to(b_refs[i][...], (SUBLANES, LANES))
        b_s[:, cols] = _pair_interleave(b) if is_qk else b

    def proj_chunk(rc, carry):
        rows = pl.ds(pl.multiple_of(rc * row_chunk, row_chunk), row_chunk)
        res = jnp.dot(x_ref[rows, :], w_s[...], preferred_element_type=F32) + b_s[0:1, :]
        cos = cos_ref[rows, :]
        sin = sin_ref[rows, :]
        for g in range(N_GROUPS):
            col = 3 * g * LANES
            q = res[:, col:col + LANES]
            k = res[:, col + LANES:col + 2 * LANES]
            q = (q * cos + pltpu.roll(q, LANES // 2, axis=1) * sin) * q_scale
            k = k * cos + pltpu.roll(k, LANES // 2, axis=1) * sin
            for i, val in enumerate((q, k, res[:, col + 2 * LANES:col + 3 * LANES])):
                ref = qkv_ref(g, i)
                ref[rows, :] = val.astype(ref.dtype)
        gate = res[:, 3 * N_GROUPS * LANES:]
        gate_s[rows, :] = (gate * jax.nn.sigmoid(gate)).astype(BF16)
        return carry

    lax.fori_loop(0, S // row_chunk, proj_chunk, 0)

    for g, d in enumerate(DILATIONS):
        R = CLASSES // d
        L = BAND // R
        nb = S // (d * BAND)

        def tile_group(it, carry, d=d, R=R, L=L, nb=nb, g=g):
            def make_run(e):
                rho = lax.shift_right_logical(e, int(math.log2(nb)))
                n = lax.bitwise_and(e, nb - 1)

                def run(j, nn):
                    start = (rho + d * j) * A + L * nn
                    return pl.ds(pl.multiple_of(start, min(L, BF16_ROWS)), L)
                return n, run

            def gather(ref, run, nn):
                return jnp.concatenate([ref[run(j, nn), :] for j in range(R)], axis=0)

            def scores(n, run):
                q_t = gather(qkv_ref(g, 0), run, n).astype(BF16)
                zero = jnp.zeros_like(q_t)
                qm = jnp.concatenate([jnp.where(head0_q, q_t, zero),
                                      jnp.where(head0_q, zero, q_t)], axis=0)
                n_prev = jnp.maximum(n - 1, 0)
                k_ref = qkv_ref(g, 1)
                kwin = jnp.concatenate([gather(k_ref, run, n_prev), gather(k_ref, run, n)],
                                       axis=0).astype(BF16)
                s = lax.dot_general(qm, kwin, (((1,), (1,)), ((), ())),
                                    preferred_element_type=F32)
                return s + bias_ref[g, 1 - jnp.minimum(n, 1)]

            def softmax(s):
                m_blk = jnp.max(s, axis=1, keepdims=True)
                p = jnp.exp2(s - m_blk)
                return p.astype(BF16), m_blk, jnp.sum(p, axis=1, keepdims=True)

            def finish(n, run, p, m_blk, l_blk):
                n_prev = jnp.maximum(n - 1, 0)
                v_ref = qkv_ref(g, 2)
                vwin = jnp.concatenate([gather(v_ref, run, n_prev), gather(v_ref, run, n)],
                                       axis=0).astype(BF16)
                pv = jnp.dot(p, vwin, preferred_element_type=F32)
                o_t = jnp.where(head0_v, pv[:BAND], pv[BAND:])
                m_t = jnp.where(head0_v, m_blk[:BAND], m_blk[BAND:])
                l_t = jnp.where(head0_v, l_blk[:BAND], l_blk[BAND:])
                if g > 0:
                    m_old = gather(m_s, run, n)
                    m_new = jnp.maximum(m_old, m_t)
                    w_old = jnp.exp2(m_old - m_new)
                    w_new = jnp.exp2(m_t - m_new)
                    m_t = m_new
                    l_t = gather(l_s, run, n) * w_old + l_t * w_new
                    o_t = gather(acc_s, run, n) * w_old + o_t * w_new
                for j in range(R):
                    m_s[run(j, n), :] = m_t[j * L:(j + 1) * L]
                    l_s[run(j, n), :] = l_t[j * L:(j + 1) * L]
                    acc_s[run(j, n), :] = o_t[j * L:(j + 1) * L]

            ctx = [make_run(it * tile_unroll + u) for u in range(tile_unroll)]
            sc, sm = {}, {}
            for t in range(tile_unroll + 2):
                if t < tile_unroll:
                    sc[t] = scores(*ctx[t])
                if 0 <= t - 1 < tile_unroll:
                    sm[t - 1] = softmax(sc.pop(t - 1))
                if 0 <= t - 2 < tile_unroll:
                    finish(*ctx[t - 2], *sm.pop(t - 2))
            return carry

        lax.fori_loop(0, n_tiles // tile_unroll, tile_group, 0)

    for c in range(CLASSES):
        rows = slice(c * A, (c + 1) * A)
        res = (acc_s[rows, :] / l_s[rows, :]) * gate_s[rows, :].astype(F32)
        o_ref[pl.ds(c, A, stride=CLASSES), :] = res


def _attention(xp, w_in, b_in):
    B, S, D = xp.shape
    kern = functools.partial(_attention_kernel, S=S, row_chunk=512, tile_unroll=16)
    slab0 = []
    for g in range(N_GROUPS):
        slab0 += [(c0 + g * ATTN_W) // LANES for c0 in (0, COL_K, COL_V)]
    slab0.append(COL_GATE_A // LANES)
    w_specs = [pl.BlockSpec((D, LANES), functools.partial(lambda b, p, s: (0, s + p), s=s))
               for s in slab0]
    b_specs = [pl.BlockSpec((1, LANES), functools.partial(lambda b, p, s: (0, s + p), s=s))
               for s in slab0]
    cos_t, sin_t = _rope_tables(S)
    const2 = pl.BlockSpec((S, LANES), lambda b, p: (0, 0), pipeline_mode=ONCE)
    scratch = [
        pltpu.VMEM((D, ATTN_COLS), BF16),
        pltpu.VMEM((SUBLANES, ATTN_COLS), F32),
        pltpu.VMEM((3, S, LANES), F32),
        pltpu.VMEM((3 * (N_GROUPS - 1), S, LANES), BF16),
        pltpu.VMEM((S, LANES), BF16),
        pltpu.VMEM((S, LANES), F32),
        pltpu.VMEM((S, LANES), F32),
        pltpu.VMEM((S, LANES), F32),
    ]
    return pl.pallas_call(
        kern,
        grid=(B, N_PAIRS),
        in_specs=[pl.BlockSpec((None, S, D), lambda b, p: (b, 0, 0), pipeline_mode=ONCE)]
        + w_specs + b_specs + [
            const2, const2,
            pl.BlockSpec((N_GROUPS, 2, 2 * BAND, 2 * BAND), lambda b, p: (0, 0, 0, 0),
                         pipeline_mode=ONCE),
        ],
        out_specs=pl.BlockSpec((None, S, LANES), lambda b, p: (b, 0, p)),
        out_shape=jax.ShapeDtypeStruct((B, S, ATTN_W), F32),
        scratch_shapes=scratch,
        compiler_params=_cparams(2),
        name="attention",
    )(xp, *([w_in] * N_ATTN_SLABS), *([b_in] * N_ATTN_SLABS), cos_t, sin_t, _band_bias())


def _rglru_kernel(x_ref, wu_ref, wl_ref, bu_ref, bl_ref, cw_ref, cb_ref, wr_ref, wi_ref,
                  br_ref, bi_ref, lam_ref, o_ref,
                  w_s, wg_s, u_pad, gate_s, a_p, h_p, *, S, row_chunk, pitch):
    C = LRU_SLAB
    n_slabs = C // LANES
    n_chunks = S // row_chunk
    u_pad[:, 0:SUBLANES, :] = jnp.zeros((n_slabs, SUBLANES, LANES), F32)

    w_s[:, :C] = wu_ref[...].astype(BF16)
    w_s[:, C:] = wl_ref[...].astype(BF16)
    wg_s[...] = jnp.zeros(wg_s.shape, BF16)
    for blk in range(C // LRU_BLOCK):
        rows = slice(blk * LRU_BLOCK, (blk + 1) * LRU_BLOCK)
        wg_s[rows, blk * LRU_BLOCK:(blk + 1) * LRU_BLOCK] = wr_ref[blk].astype(BF16)
        wg_s[rows, C + blk * LRU_BLOCK:C + (blk + 1) * LRU_BLOCK] = wi_ref[blk].astype(BF16)

    def proj_chunk(rc, carry):
        r0 = pl.multiple_of(rc * row_chunk, row_chunk)
        xb = x_ref[pl.ds(r0, row_chunk), :].astype(BF16)
        res = jnp.dot(xb, w_s[...], preferred_element_type=F32)
        u = res[:, :C] + bu_ref[...]
        for s in range(C // LANES):
            u_pad[s, pl.ds(r0 + SUBLANES, row_chunk), :] = u[:, s * LANES:(s + 1) * LANES]
        gl = res[:, C:] + bl_ref[...]
        gate_s[pl.ds(r0, row_chunk), :] = gl * jax.nn.sigmoid(gl)
        return carry

    neg_lam = -lam_ref[...]
    softplus = jnp.maximum(neg_lam, 0.0) + jnp.log1p(jnp.exp(-jnp.abs(neg_lam)))
    decay_rate = LRU_C * softplus

    def gate_chunk(rc, carry):
        r0 = pl.multiple_of(rc * row_chunk, row_chunk)
        parts = []
        for s in range(C // LANES):
            lanes = slice(s * LANES, (s + 1) * LANES)
            acc = cb_ref[:, lanes]
            for j in range(CONV_W):
                off = SUBLANES - (CONV_W - 1) + j
                acc = acc + u_pad[s, pl.ds(r0 + off, row_chunk), :] * cw_ref[j:j + 1, lanes]
            parts.append(acc)
        uc = jnp.concatenate(parts, axis=1)
        gates = jnp.dot(uc.astype(BF16), wg_s[...], preferred_element_type=F32)
        r = jax.nn.sigmoid(gates[:, :C] + br_ref[...])
        i = jax.nn.sigmoid(gates[:, C:] + bi_ref[...])
        neg_log_a = r * decay_rate
        a = jnp.exp(-neg_log_a)
        one_minus_a2 = jnp.maximum(jnp.tanh(neg_log_a) * (a * a + 1.0), 0.0)
        mult = jnp.exp2(jnp.log(one_minus_a2) * (0.5 * LOG2_E))
        u = mult * (i * uc)
        for s in range(n_slabs):
            lanes = slice(s * LANES, (s + 1) * LANES)
            a_p[s, pl.ds(rc * pitch, row_chunk), :] = a[:, lanes]
            h_p[s, pl.ds(rc * pitch, row_chunk), :] = u[:, lanes]
        return carry

    def gate_then_proj(rc, carry):
        gate_chunk(rc, carry)
        return proj_chunk(rc + 1, carry)

    proj_chunk(0, 0)
    lax.fori_loop(0, n_chunks - 1, gate_then_proj, 0)
    gate_chunk(n_chunks - 1, 0)

    def scan_step(j, carry):
        rows = pl.ds(j, n_chunks, stride=pitch)
        out = []
        for s in range(n_slabs):
            h_prev, p_prev = carry[2 * s], carry[2 * s + 1]
            a = a_p[s, rows, :]
            h = a * h_prev + h_p[s, rows, :]
            p = a * p_prev
            h_p[s, rows, :] = h
            a_p[s, rows, :] = p
            out += [h, p]
        return tuple(out)

    init = (jnp.zeros((n_chunks, LANES), F32), jnp.ones((n_chunks, LANES), F32)) * n_slabs
    ends = lax.fori_loop(0, row_chunk, scan_step, init, unroll=8)

    for s in range(n_slabs):
        lanes = slice(s * LANES, (s + 1) * LANES)
        h_end, p_end = ends[2 * s], ends[2 * s + 1]
        h_in = jnp.zeros((1, LANES), F32)
        for seg in range(n_chunks):
            rows = slice(seg * pitch, seg * pitch + row_chunk)
            h = h_p[s, rows, :] + a_p[s, rows, :] * h_in
            nat = slice(seg * row_chunk, (seg + 1) * row_chunk)
            o_ref[nat, lanes] = (h * gate_s[nat, lanes]).astype(BF16)
            h_in = h_end[seg:seg + 1] + p_end[seg:seg + 1] * h_in


def _rglru(x, w_in, b_in, conv_w, conv_b, lru_wr, lru_br, lru_wi, lru_bi, lam):
    B, S, D = x.shape
    C = LRU_SLAB
    n_slabs = LRU_W // C
    per = C // LRU_BLOCK
    row_chunk = S // SUBLANES
    pitch = row_chunk + SUBLANES // 2
    kern = functools.partial(_rglru_kernel, S=S, row_chunk=row_chunk, pitch=pitch)
    col = lambda c0: (lambda b, c: (0, c0 // C + c))
    row1 = lambda: pl.BlockSpec((1, C), lambda b, c: (0, c))
    blocks = lambda: pl.BlockSpec((per, LRU_BLOCK, LRU_BLOCK), lambda b, c: (c, 0, 0))
    return pl.pallas_call(
        kern,
        grid=(B, n_slabs),
        in_specs=[
            pl.BlockSpec((None, S, D), lambda b, c: (b, 0, 0), pipeline_mode=ONCE),
            pl.BlockSpec((D, C), col(COL_U)), pl.BlockSpec((D, C), col(COL_GATE_L)),
            pl.BlockSpec((1, C), col(COL_U)), pl.BlockSpec((1, C), col(COL_GATE_L)),
            pl.BlockSpec((CONV_W, C), lambda b, c: (0, c)), row1(),
            blocks(), blocks(), row1(), row1(), row1(),
        ],
        out_specs=pl.BlockSpec((None, S, C), lambda b, c: (b, 0, c)),
        out_shape=jax.ShapeDtypeStruct((B, S, LRU_W), BF16),
        scratch_shapes=[
            pltpu.VMEM((D, 2 * C), BF16),
            pltpu.VMEM((C, 2 * C), BF16),
            pltpu.VMEM((C // LANES, S + SUBLANES, LANES), F32),
            pltpu.VMEM((S, C), F32),
            pltpu.VMEM((C // LANES, SUBLANES * pitch, LANES), F32),
            pltpu.VMEM((C // LANES, SUBLANES * pitch, LANES), F32),
        ],
        compiler_params=_cparams(2),
        name="rglru",
    )(x, w_in, w_in, b_in, b_in, conv_w, conv_b, lru_wr, lru_wi, lru_br, lru_bi, lam)


def _merge_out_kernel(x_ref, attn_ref, hg_ref, wga_ref, wgb_ref, bga_ref, bgb_ref,
                      wa_ref, wl_ref, wo_ref, bo_ref, lng_ref, lnb_ref, o_ref,
                      wg_s, wa_s, wl_s, wo_s):
    D = x_ref.shape[1]

    @pl.when(pl.program_id(0) == 0)
    def _():
        rows = 256
        for r0 in range(0, D, rows):
            sl = slice(r0, r0 + rows)
            wg_s[sl, :D] = wga_ref[sl, :].astype(BF16)
            wg_s[sl, D:] = wgb_ref[sl, :].astype(BF16)
            wl_s[sl, :] = wl_ref[sl, :].astype(BF16)
            wo_s[sl, :] = wo_ref[sl, :].astype(BF16)
        for r0 in range(0, ATTN_W, rows):
            sl = slice(r0, r0 + rows)
            wa_s[sl, :] = wa_ref[sl, :].astype(BF16)

    x = x_ref[...]
    xb = x.astype(BF16)
    gate_a = jax.nn.sigmoid(
        jnp.dot(xb, wg_s[:, :D], preferred_element_type=F32) + bga_ref[...])
    gate_b = jax.nn.sigmoid(
        jnp.dot(xb, wg_s[:, D:], preferred_element_type=F32) + bgb_ref[...])
    y_a = jnp.dot(attn_ref[...].astype(BF16), wa_s[...], preferred_element_type=F32)
    y_b = jnp.dot(hg_ref[...], wl_s[...], preferred_element_type=F32)
    merged = gate_a * y_a + gate_b * y_b
    out = jnp.dot(merged.astype(BF16), wo_s[...], preferred_element_type=F32) + bo_ref[...]
    y = ALPHA * x + out
    mu = jnp.mean(y, axis=1, keepdims=True)
    yc = y - mu
    var = jnp.mean(yc * yc, axis=1, keepdims=True)
    o_ref[...] = yc * lax.rsqrt(var + LN_EPS) * lng_ref[...] + lnb_ref[...]


def _merge_out(x2, attn2, hg2, w_in, b_in, w_a, w_l, w_o, b_o, ln_g, ln_b, tm=512):
    T, D = x2.shape
    once = lambda shape, idx: pl.BlockSpec(shape, lambda i: idx, pipeline_mode=ONCE)
    merge_blk = COL_MERGE // D
    return pl.pallas_call(
        _merge_out_kernel,
        grid=(T // tm,),
        in_specs=[
            pl.BlockSpec((tm, D), lambda i: (i, 0)),
            pl.BlockSpec((tm, ATTN_W), lambda i: (i, 0)),
            pl.BlockSpec((tm, LRU_W), lambda i: (i, 0)),
            once((D, D), (0, merge_blk)), once((D, D), (0, merge_blk + 1)),
            once((1, D), (0, merge_blk)), once((1, D), (0, merge_blk + 1)),
            once((ATTN_W, D), (0, 0)), once((LRU_W, D), (0, 0)), once((D, D), (0, 0)),
            once((1, D), (0, 0)), once((1, D), (0, 0)), once((1, D), (0, 0)),
        ],
        out_specs=pl.BlockSpec((tm, D), lambda i: (i, 0)),
        out_shape=jax.ShapeDtypeStruct((T, D), F32),
        scratch_shapes=[
            pltpu.VMEM((D, 2 * D), BF16), pltpu.VMEM((ATTN_W, D), BF16),
            pltpu.VMEM((LRU_W, D), BF16), pltpu.VMEM((D, D), BF16),
        ],
        compiler_params=_cparams(1),
        name="merge_out",
    )(x2, attn2, hg2, w_in, w_in, b_in, b_in, w_a, w_l, w_o, b_o, ln_g, ln_b)


def kernel(x, w_in, b_in, conv_w, conv_b, lru_wr, lru_br, lru_wi, lru_bi, lru_lambda,
           w_attn_proj, w_lru_proj, w_out, b_out, ln_gain, ln_bias):
    B, S, D = x.shape
    assert w_in.shape[0] == DEPTH and S % (CLASSES * BAND) == 0 and D == D_MODEL
    assert COL_MERGE % D == 0 and w_in.shape[2] == COL_MERGE + 2 * D
    row = lambda v: v.reshape(1, -1)
    for l in range(DEPTH):
        wl, bl = w_in[l], row(b_in[l])
        attn = _attention(_permute_cast(x), wl, bl)
        hg = _rglru(x, wl, bl, conv_w[l], row(conv_b[l]), lru_wr[l], row(lru_br[l]),
                    lru_wi[l], row(lru_bi[l]), row(lru_lambda[l]))
        out = _merge_out(
            x.reshape(B * S, D), attn.reshape(B * S, ATTN_W), hg.reshape(B * S, LRU_W),
            wl, bl, w_attn_proj[l], w_lru_proj[l], w_out[l],
            row(b_out[l]), row(ln_gain[l]), row(ln_bias[l]))
        x = out.reshape(B, S, D)
    return x
```

```python
import functools
import math

import jax
import jax.numpy as jnp
import numpy as np
from jax import lax
from jax.experimental import pallas as pl
from jax.experimental.pallas import tpu as pltpu

LANES = 128
SUBLANES = 8
BF16_ROWS = 16
VMEM_LIMIT_BYTES = 56 * 1024 * 1024

D_MODEL = 1024
HEAD_DIM = 64
HALF = HEAD_DIM // 2
HEADS_PER_GROUP = 8
DILATIONS = (1, 4, 16)
N_GROUPS = len(DILATIONS)
BAND = 128
CLASSES = 16
QKV_W = N_GROUPS * HEADS_PER_GROUP * HEAD_DIM
ATTN_W = HEADS_PER_GROUP * HEAD_DIM
N_PAIRS = ATTN_W // LANES
LRU_W = 1024
LRU_BLOCK = 64
LRU_SLAB = 256
CONV_W = 4
LRU_C = 8.0
ROPE_THETA = 10000.0
LOG2_E = math.log2(math.e)
NEG_INF = -1e30
LN_EPS = 1e-5
DEPTH = 1
ALPHA = (2.0 * DEPTH) ** 0.25
GATE_SLAB = 3
ATTN_SLAB_LAYOUT = ([(0, 0), (0, 1), (0, 2), (None, GATE_SLAB)]
                    + [(g, kind) for g in range(1, N_GROUPS) for kind in range(3)])
N_ATTN_SLABS = len(ATTN_SLAB_LAYOUT)
ATTN_COLS = N_ATTN_SLABS * LANES
COL_K = QKV_W
COL_V = 2 * QKV_W
COL_GATE_A = 3 * QKV_W
COL_U = COL_GATE_A + ATTN_W
COL_GATE_L = COL_U + LRU_W
COL_MERGE = COL_GATE_L + LRU_W

BF16 = jnp.bfloat16
F32 = jnp.float32
ONCE = pl.Buffered(1)


def _cparams(n_axes):
    return pltpu.CompilerParams(dimension_semantics=("arbitrary",) * n_axes,
                                vmem_limit_bytes=VMEM_LIMIT_BYTES)


PERMUTE_CLASSES_PER_STEP = 4


def _permute_cast_kernel(x_hbm, xp_ref, buf, sem, *, rows_per_class):
    per = PERMUTE_CLASSES_PER_STEP
    steps_per_batch = CLASSES // per
    step = pl.program_id(0) * steps_per_batch + pl.program_id(1)
    n_steps = pl.num_programs(0) * steps_per_batch

    def copies(st, slot):
        b = lax.div(st, steps_per_batch)
        q = lax.rem(st, steps_per_batch)
        return [pltpu.make_async_copy(x_hbm.at[b, :, q * per + i, :], buf.at[slot, i],
                                      sem.at[slot, i]) for i in range(per)]

    @pl.when(step == 0)
    def _():
        for cp in copies(step, 0):
            cp.start()

    slot = lax.rem(step, 2)

    @pl.when(step + 1 < n_steps)
    def _():
        for cp in copies(step + 1, 1 - slot):
            cp.start()

    for i, cp in enumerate(copies(step, slot)):
        cp.wait()
        xp_ref[i * rows_per_class:(i + 1) * rows_per_class, :] = buf[slot, i].astype(BF16)


def _permute_cast(x):
    B, S, D = x.shape
    A = S // CLASSES
    per = PERMUTE_CLASSES_PER_STEP
    kern = functools.partial(_permute_cast_kernel, rows_per_class=A)
    return pl.pallas_call(
        kern,
        grid=(B, CLASSES // per),
        in_specs=[pl.BlockSpec(memory_space=pl.ANY)],
        out_specs=pl.BlockSpec((None, per * A, D), lambda b, q: (b, q, 0)),
        out_shape=jax.ShapeDtypeStruct((B, S, D), BF16),
        scratch_shapes=[pltpu.VMEM((2, per, A, D), F32), pltpu.SemaphoreType.DMA((2, per))],
        compiler_params=_cparams(2),
        name="permute_cast",
    )(x.reshape(B, A, CLASSES, D))


def _band_bias():
    out = np.zeros((N_GROUPS, 2, 2 * BAND, 2 * BAND), np.float32)
    i = np.arange(BAND)
    for g, d in enumerate(DILATIONS):
        R = CLASSES // d
        L = BAND // R
        m = R * (i % L) + i // L
        m_k = np.concatenate([m - BAND, m])
        dist = m[:, None] - m_k[None, :]
        band = (dist >= 0) & (dist <= BAND)
        for first in range(2):
            valid = band & ((np.arange(2 * BAND) >= BAND)[None, :] | (first == 0))
            bias = np.where(valid, 0.0, NEG_INF).astype(np.float32)
            out[g, first] = np.concatenate([bias, bias], axis=0)
    return out


def _rope_tables(S):
    A = S // CLASSES
    r = np.arange(S)
    pos = (CLASSES * (r % A) + r // A).astype(np.float64)
    inv_freq = ROPE_THETA ** (-np.arange(HALF, dtype=np.float64) / HALF)
    ang = pos[:, None] * inv_freq[None, :]
    cos_t = np.tile(np.cos(ang), (1, LANES // HALF))
    sin = np.sin(ang)
    sin_t = np.concatenate([-sin, -sin, sin, sin], axis=1)
    return cos_t.astype(np.float32), sin_t.astype(np.float32)


def _pair_interleave(w):
    lane = lax.broadcasted_iota(jnp.int32, w.shape, 1)
    from_right = pltpu.roll(w, LANES - HALF, axis=1)
    from_left = pltpu.roll(w, HALF, axis=1)
    quarter = lane // HALF
    return jnp.where(quarter == 1, from_right, jnp.where(quarter == 2, from_left, w))


def _attn_weights_kernel(*refs):
    w_refs = refs[:N_ATTN_SLABS]
    b_refs = refs[N_ATTN_SLABS:2 * N_ATTN_SLABS]
    w_out, b_out = refs[2 * N_ATTN_SLABS:]
    D = w_out.shape[0]
    w_rows = 256
    for i, (g, kind) in enumerate(ATTN_SLAB_LAYOUT):
        is_qk = kind in (0, 1)
        cols = slice(i * LANES, (i + 1) * LANES)
        for r0 in range(0, D, w_rows):
            w = w_refs[i][r0:r0 + w_rows, :]
            w_out[r0:r0 + w_rows, cols] = (_pair_interleave(w) if is_qk else w).astype(BF16)
        b = jnp.broadcast_to(b_refs[i][...], (SUBLANES, LANES))
        b_out[:, cols] = _pair_interleave(b) if is_qk else b


def _attn_weights(w_in, b_in):
    D = w_in.shape[0]
    kind_col = (0, COL_K, COL_V)
    slab0 = [COL_GATE_A // LANES if kind == GATE_SLAB else (kind_col[kind] + g * ATTN_W) // LANES
             for g, kind in ATTN_SLAB_LAYOUT]
    w_specs = [pl.BlockSpec((D, LANES), functools.partial(lambda p, s: (0, s + p), s=s))
               for s in slab0]
    b_specs = [pl.BlockSpec((1, LANES), functools.partial(lambda p, s: (0, s + p), s=s))
               for s in slab0]
    return pl.pallas_call(
        _attn_weights_kernel,
        grid=(N_PAIRS,),
        in_specs=w_specs + b_specs,
        out_specs=[pl.BlockSpec((None, D, ATTN_COLS), lambda p: (p, 0, 0)),
                   pl.BlockSpec((None, SUBLANES, ATTN_COLS), lambda p: (p, 0, 0))],
        out_shape=[jax.ShapeDtypeStruct((N_PAIRS, D, ATTN_COLS), BF16),
                   jax.ShapeDtypeStruct((N_PAIRS, SUBLANES, ATTN_COLS), F32)],
        compiler_params=_cparams(1),
        name="attn_weights",
    )(*([w_in] * N_ATTN_SLABS), *([b_in] * N_ATTN_SLABS))


def _attention_kernel(x_ref, w_s, b_s, cos_ref, sin_ref, bias_ref, o_ref,
                      qkv0_s, qkv_s, gate_s, acc_s, m_s, l_s, *, S, row_chunk, tile_unroll):
    A = S // CLASSES

    def qkv_ref(g, i):
        return qkv0_s.at[i] if g == 0 else qkv_s.at[3 * (g - 1) + i]

    n_tiles = S // BAND
    lane = lax.broadcasted_iota(jnp.int32, (BAND, LANES), 1)
    head0_q = ((lane // HALF) % 2) == 0
    head0_v = lane < HEAD_DIM
    q_scale = (HEAD_DIM ** -0.5) * LOG2_E

    def proj_chunk(rc, slab_lo, slab_hi):
        rows = pl.ds(pl.multiple_of(rc * row_chunk, row_chunk), row_chunk)
        cols = slice(slab_lo * LANES, slab_hi * LANES)
        res = jnp.dot(x_ref[rows, :], w_s[:, cols], preferred_element_type=F32) + b_s[0:1, cols]
        cos = cos_ref[rows, :]
        sin = sin_ref[rows, :]
        for i, (g, kind) in enumerate(ATTN_SLAB_LAYOUT[slab_lo:slab_hi]):
            val = res[:, i * LANES:(i + 1) * LANES]
            if kind == GATE_SLAB:
                gate_s[rows, :] = (val * jax.nn.sigmoid(val)).astype(BF16)
                continue
            if kind in (0, 1):
                val = val * cos + pltpu.roll(val, LANES // 2, axis=1) * sin
            if kind == 0:
                val = val * q_scale
            ref = qkv_ref(g, kind)
            ref[rows, :] = val.astype(ref.dtype)

    def tiles(g, first, count):
        d = DILATIONS[g]
        R = CLASSES // d
        L = BAND // R
        nb = S // (d * BAND)

        def make_run(e):
            rho = lax.shift_right_logical(e, int(math.log2(nb)))
            n = lax.bitwise_and(e, nb - 1)

            def run(j, nn):
                start = (rho + d * j) * A + L * nn
                return pl.ds(pl.multiple_of(start, min(L, BF16_ROWS)), L)
            return n, run

        def gather(ref, run, nn):
            return jnp.concatenate([ref[run(j, nn), :] for j in range(R)], axis=0)

        def scores(n, run):
            q_t = gather(qkv_ref(g, 0), run, n).astype(BF16)
            zero = jnp.zeros_like(q_t)
            qm = jnp.concatenate([jnp.where(head0_q, q_t, zero),
                                  jnp.where(head0_q, zero, q_t)], axis=0)
            n_prev = jnp.maximum(n - 1, 0)
            k_ref = qkv_ref(g, 1)
            kwin = jnp.concatenate([gather(k_ref, run, n_prev), gather(k_ref, run, n)],
                                   axis=0).astype(BF16)
            s = lax.dot_general(qm, kwin, (((1,), (1,)), ((), ())),
                                preferred_element_type=F32)
            return s + bias_ref[g, 1 - jnp.minimum(n, 1)]

        def softmax(s):
            m_blk = jnp.max(s, axis=1, keepdims=True)
            p = jnp.exp2(s - m_blk)
            return p.astype(BF16), m_blk, jnp.sum(p, axis=1, keepdims=True)

        def finish(n, run, p, m_blk, l_blk):
            n_prev = jnp.maximum(n - 1, 0)
            v_ref = qkv_ref(g, 2)
            vwin = jnp.concatenate([gather(v_ref, run, n_prev), gather(v_ref, run, n)],
                                   axis=0).astype(BF16)
            pv = jnp.dot(p, vwin, preferred_element_type=F32)
            o_t = jnp.where(head0_v, pv[:BAND], pv[BAND:])
            m_t = jnp.where(head0_v, m_blk[:BAND], m_blk[BAND:])
            l_t = jnp.where(head0_v, l_blk[:BAND], l_blk[BAND:])
            if g > 0:
                m_old = gather(m_s, run, n)
                m_new = jnp.maximum(m_old, m_t)
                w_old = jnp.exp2(m_old - m_new)
                w_new = jnp.exp2(m_t - m_new)
                m_t = m_new
                l_t = gather(l_s, run, n) * w_old + l_t * w_new
                o_t = gather(acc_s, run, n) * w_old + o_t * w_new
            for j in range(R):
                m_s[run(j, n), :] = m_t[j * L:(j + 1) * L]
                l_s[run(j, n), :] = l_t[j * L:(j + 1) * L]
                acc_s[run(j, n), :] = o_t[j * L:(j + 1) * L]

        ctx = [make_run(first + u) for u in range(count)]
        sc, sm = {}, {}
        for t in range(count + 2):
            if t < count:
                sc[t] = scores(*ctx[t])
            if 0 <= t - 1 < count:
                sm[t - 1] = softmax(sc.pop(t - 1))
            if 0 <= t - 2 < count:
                finish(*ctx[t - 2], *sm.pop(t - 2))

    def proj_all(rc, carry):
        proj_chunk(rc, 0, N_ATTN_SLABS)
        return carry

    lax.fori_loop(0, S // row_chunk, proj_all, 0)
    for g in range(N_GROUPS):
        def tile_group(it, carry, g=g):
            tiles(g, it * tile_unroll, tile_unroll)
            return carry

        lax.fori_loop(0, n_tiles // tile_unroll, tile_group, 0)

    for c in range(CLASSES):
        rows = slice(c * A, (c + 1) * A)
        res = (acc_s[rows, :] / l_s[rows, :]) * gate_s[rows, :].astype(F32)
        o_ref[pl.ds(c, A, stride=CLASSES), :] = res


def _attention(xp, w_attn, b_attn):
    B, S, D = xp.shape
    kern = functools.partial(_attention_kernel, S=S, row_chunk=512, tile_unroll=16)
    cos_t, sin_t = _rope_tables(S)
    const2 = pl.BlockSpec((S, LANES), lambda b, p: (0, 0), pipeline_mode=ONCE)
    scratch = [
        pltpu.VMEM((3, S, LANES), F32),
        pltpu.VMEM((3 * (N_GROUPS - 1), S, LANES), BF16),
        pltpu.VMEM((S, LANES), BF16),
        pltpu.VMEM((S, LANES), F32),
        pltpu.VMEM((S, LANES), F32),
        pltpu.VMEM((S, LANES), F32),
    ]
    return pl.pallas_call(
        kern,
        grid=(B, N_PAIRS),
        in_specs=[
            pl.BlockSpec((None, S, D), lambda b, p: (b, 0, 0), pipeline_mode=ONCE),
            pl.BlockSpec((None, D, ATTN_COLS), lambda b, p: (p, 0, 0)),
            pl.BlockSpec((None, SUBLANES, ATTN_COLS), lambda b, p: (p, 0, 0)),
            const2, const2,
            pl.BlockSpec((N_GROUPS, 2, 2 * BAND, 2 * BAND), lambda b, p: (0, 0, 0, 0),
                         pipeline_mode=ONCE),
        ],
        out_specs=pl.BlockSpec((None, S, LANES), lambda b, p: (b, 0, p)),
        out_shape=jax.ShapeDtypeStruct((B, S, ATTN_W), F32),
        scratch_shapes=scratch,
        compiler_params=_cparams(2),
        name="attention",
    )(xp, w_attn, b_attn, cos_t, sin_t, _band_bias())


def _rglru_kernel(x_ref, wu_ref, wl_ref, bu_ref, bl_ref, cw_ref, cb_ref, wr_ref, wi_ref,
                  br_ref, bi_ref, lam_ref, o_ref,
                  w_s, wg_s, u_pad, gate_s, a_p, h_p, *, S, row_chunk, pitch):
    C = LRU_SLAB
    n_slabs = C // LANES
    n_chunks = S // row_chunk
    n_seg = SUBLANES
    seg_len = S // n_seg
    per_seg = seg_len // row_chunk
    u_pad[:, 0:SUBLANES, :] = jnp.zeros((n_slabs, SUBLANES, LANES), F32)

    w_s[:, :C] = wu_ref[...].astype(BF16)
    w_s[:, C:] = wl_ref[...].astype(BF16)
    wg_s[...] = jnp.zeros(wg_s.shape, BF16)
    for blk in range(C // LRU_BLOCK):
        rows = slice(blk * LRU_BLOCK, (blk + 1) * LRU_BLOCK)
        wg_s[rows, blk * LRU_BLOCK:(blk + 1) * LRU_BLOCK] = wr_ref[blk].astype(BF16)
        wg_s[rows, C + blk * LRU_BLOCK:C + (blk + 1) * LRU_BLOCK] = wi_ref[blk].astype(BF16)

    def proj_chunk(rc, carry):
        r0 = pl.multiple_of(rc * row_chunk, row_chunk)
        xb = x_ref[pl.ds(r0, row_chunk), :].astype(BF16)
        res = jnp.dot(xb, w_s[...], preferred_element_type=F32)
        u = res[:, :C] + bu_ref[...]
        for s in range(C // LANES):
            u_pad[s, pl.ds(r0 + SUBLANES, row_chunk), :] = u[:, s * LANES:(s + 1) * LANES]
        gl = res[:, C:] + bl_ref[...]
        gate_s[pl.ds(r0, row_chunk), :] = gl * jax.nn.sigmoid(gl)
        return carry

    neg_lam = -lam_ref[...]
    softplus = jnp.maximum(neg_lam, 0.0) + jnp.log1p(jnp.exp(-jnp.abs(neg_lam)))
    decay_rate = LRU_C * softplus

    def gate_chunk(rc, carry):
        r0 = pl.multiple_of(rc * row_chunk, row_chunk)
        parts = []
        for s in range(C // LANES):
            lanes = slice(s * LANES, (s + 1) * LANES)
            acc = cb_ref[:, lanes]
            for j in range(CONV_W):
                off = SUBLANES - (CONV_W - 1) + j
                acc = acc + u_pad[s, pl.ds(r0 + off, row_chunk), :] * cw_ref[j:j + 1, lanes]
            parts.append(acc)
        uc = jnp.concatenate(parts, axis=1)
        gates = jnp.dot(uc.astype(BF16), wg_s[...], preferred_element_type=F32)
        r = jax.nn.sigmoid(gates[:, :C] + br_ref[...])
        i = jax.nn.sigmoid(gates[:, C:] + bi_ref[...])
        neg_log_a = r * decay_rate
        a = jnp.exp(-neg_log_a)
        one_minus_a2 = jnp.maximum(jnp.tanh(neg_log_a) * (a * a + 1.0), 0.0)
        mult = jnp.exp2(jnp.log(one_minus_a2) * (0.5 * LOG2_E))
        u = mult * (i * uc)
        start = (rc // per_seg) * pitch + (rc % per_seg) * row_chunk
        for s in range(n_slabs):
            lanes = slice(s * LANES, (s + 1) * LANES)
            a_p[s, pl.ds(start, row_chunk), :] = a[:, lanes]
            h_p[s, pl.ds(start, row_chunk), :] = u[:, lanes]
        return carry

    def gate_then_proj(rc, carry):
        gate_chunk(rc, carry)
        return proj_chunk(rc + 1, carry)

    proj_chunk(0, 0)
    lax.fori_loop(0, n_chunks - 1, gate_then_proj, 0)
    gate_chunk(n_chunks - 1, 0)

    def scan_step(j, carry):
        rows = pl.ds(j, n_seg, stride=pitch)
        out = []
        for s in range(n_slabs):
            h_prev, p_prev = carry[2 * s], carry[2 * s + 1]
            a = a_p[s, rows, :]
            h = a * h_prev + h_p[s, rows, :]
            p = a * p_prev
            h_p[s, rows, :] = h
            a_p[s, rows, :] = p
            out += [h, p]
        return tuple(out)

    init = (jnp.zeros((n_seg, LANES), F32), jnp.ones((n_seg, LANES), F32)) * n_slabs
    ends = lax.fori_loop(0, seg_len, scan_step, init, unroll=8)

    for s in range(n_slabs):
        lanes = slice(s * LANES, (s + 1) * LANES)
        h_end, p_end = ends[2 * s], ends[2 * s + 1]
        h_in = jnp.zeros((1, LANES), F32)
        for seg in range(n_seg):
            rows = slice(seg * pitch, seg * pitch + seg_len)
            h = h_p[s, rows, :] + a_p[s, rows, :] * h_in
            nat = slice(seg * seg_len, (seg + 1) * seg_len)
            o_ref[nat, lanes] = (h * gate_s[nat, lanes]).astype(BF16)
            h_in = h_end[seg:seg + 1] + p_end[seg:seg + 1] * h_in


def _rglru(x, w_in, b_in, conv_w, conv_b, lru_wr, lru_br, lru_wi, lru_bi, lam):
    B, S, D = x.shape
    C = LRU_SLAB
    n_slabs = LRU_W // C
    per = C // LRU_BLOCK
    pitch = S // SUBLANES + SUBLANES // 2
    kern = functools.partial(_rglru_kernel, S=S, row_chunk=512, pitch=pitch)
    col = lambda c0: (lambda b, c: (0, c0 // C + c))
    row1 = lambda: pl.BlockSpec((1, C), lambda b, c: (0, c))
    blocks = lambda: pl.BlockSpec((per, LRU_BLOCK, LRU_BLOCK), lambda b, c: (c, 0, 0))
    return pl.pallas_call(
        kern,
        grid=(B, n_slabs),
        in_specs=[
            pl.BlockSpec((None, S, D), lambda b, c: (b, 0, 0), pipeline_mode=ONCE),
            pl.BlockSpec((D, C), col(COL_U)), pl.BlockSpec((D, C), col(COL_GATE_L)),
            pl.BlockSpec((1, C), col(COL_U)), pl.BlockSpec((1, C), col(COL_GATE_L)),
            pl.BlockSpec((CONV_W, C), lambda b, c: (0, c)), row1(),
            blocks(), blocks(), row1(), row1(), row1(),
        ],
        out_specs=pl.BlockSpec((None, S, C), lambda b, c: (b, 0, c)),
        out_shape=jax.ShapeDtypeStruct((B, S, LRU_W), BF16),
        scratch_shapes=[
            pltpu.VMEM((D, 2 * C), BF16),
            pltpu.VMEM((C, 2 * C), BF16),
            pltpu.VMEM((C // LANES, S + SUBLANES, LANES), F32),
            pltpu.VMEM((S, C), F32),
            pltpu.VMEM((C // LANES, SUBLANES * pitch, LANES), F32),
            pltpu.VMEM((C // LANES, SUBLANES * pitch, LANES), F32),
        ],
        compiler_params=_cparams(2),
        name="rglru",
    )(x, w_in, w_in, b_in, b_in, conv_w, conv_b, lru_wr, lru_wi, lru_br, lru_bi, lam)


def _merge_out_kernel(x_ref, attn_ref, hg_ref, wga_ref, wgb_ref, bga_ref, bgb_ref,
                      wa_ref, wl_ref, wo_ref, bo_ref, lng_ref, lnb_ref, o_ref,
                      wg_s, wa_s, wl_s, wo_s):
    D = x_ref.shape[1]

    @pl.when(pl.program_id(0) == 0)
    def _():
        rows = 256
        for r0 in range(0, D, rows):
            sl = slice(r0, r0 + rows)
            wg_s[sl, :D] = wga_ref[sl, :].astype(BF16)
            wg_s[sl, D:] = wgb_ref[sl, :].astype(BF16)
            wl_s[sl, :] = wl_ref[sl, :].astype(BF16)
            wo_s[sl, :] = wo_ref[sl, :].astype(BF16)
        for r0 in range(0, ATTN_W, rows):
            sl = slice(r0, r0 + rows)
            wa_s[sl, :] = wa_ref[sl, :].astype(BF16)

    x = x_ref[...]
    xb = x.astype(BF16)
    gate_a = jax.nn.sigmoid(
        jnp.dot(xb, wg_s[:, :D], preferred_element_type=F32) + bga_ref[...])
    gate_b = jax.nn.sigmoid(
        jnp.dot(xb, wg_s[:, D:], preferred_element_type=F32) + bgb_ref[...])
    y_a = jnp.dot(attn_ref[...].astype(BF16), wa_s[...], preferred_element_type=F32)
    y_b = jnp.dot(hg_ref[...], wl_s[...], preferred_element_type=F32)
    merged = gate_a * y_a + gate_b * y_b
    out = jnp.dot(merged.astype(BF16), wo_s[...], preferred_element_type=F32) + bo_ref[...]
    y = ALPHA * x + out
    mu = jnp.mean(y, axis=1, keepdims=True)
    yc = y - mu
    var = jnp.mean(yc * yc, axis=1, keepdims=True)
    o_ref[...] = yc * lax.rsqrt(var + LN_EPS) * lng_ref[...] + lnb_ref[...]


def _merge_out(x2, attn2, hg2, w_in, b_in, w_a, w_l, w_o, b_o, ln_g, ln_b, tm=512):
    T, D = x2.shape
    once = lambda shape, idx: pl.BlockSpec(shape, lambda i: idx, pipeline_mode=ONCE)
    merge_blk = COL_MERGE // D
    return pl.pallas_call(
        _merge_out_kernel,
        grid=(T // tm,),
        in_specs=[
            pl.BlockSpec((tm, D), lambda i: (i, 0)),
            pl.BlockSpec((tm, ATTN_W), lambda i: (i, 0)),
            pl.BlockSpec((tm, LRU_W), lambda i: (i, 0)),
            once((D, D), (0, merge_blk)), once((D, D), (0, merge_blk + 1)),
            once((1, D), (0, merge_blk)), once((1, D), (0, merge_blk + 1)),
            once((ATTN_W, D), (0, 0)), once((LRU_W, D), (0, 0)), once((D, D), (0, 0)),
            once((1, D), (0, 0)), once((1, D), (0, 0)), once((1, D), (0, 0)),
        ],
        out_specs=pl.BlockSpec((tm, D), lambda i: (i, 0)),
        out_shape=jax.ShapeDtypeStruct((T, D), F32),
        scratch_shapes=[
            pltpu.VMEM((D, 2 * D), BF16), pltpu.VMEM((ATTN_W, D), BF16),
            pltpu.VMEM((LRU_W, D), BF16), pltpu.VMEM((D, D), BF16),
        ],
        compiler_params=_cparams(1),
        name="merge_out",
    )(x2, attn2, hg2, w_in, w_in, b_in, b_in, w_a, w_l, w_o, b_o, ln_g, ln_b)


def kernel(x, w_in, b_in, conv_w, conv_b, lru_wr, lru_br, lru_wi, lru_bi, lru_lambda,
           w_attn_proj, w_lru_proj, w_out, b_out, ln_gain, ln_bias):
    B, S, D = x.shape
    assert w_in.shape[0] == DEPTH and S % (CLASSES * BAND) == 0 and D == D_MODEL
    assert COL_MERGE % D == 0 and w_in.shape[2] == COL_MERGE + 2 * D
    row = lambda v: v.reshape(1, -1)
    for l in range(DEPTH):
        wl, bl = w_in[l], row(b_in[l])
        attn = _attention(_permute_cast(x), *_attn_weights(wl, bl))
        hg = _rglru(x, wl, bl, conv_w[l], row(conv_b[l]), lru_wr[l], row(lru_br[l]),
                    lru_wi[l], row(lru_bi[l]), row(lru_lambda[l]))
        out = _merge_out(
            x.reshape(B * S, D), attn.reshape(B * S, ATTN_W), hg.reshape(B * S, LRU_W),
            wl, bl, w_attn_proj[l], w_lru_proj[l], w_out[l],
            row(b_out[l]), row(ln_gain[l]), row(ln_bias[l]))
        x = out.reshape(B, S, D)
    return x
```

```python
import functools
import math

import jax
import jax.numpy as jnp
import numpy as np
from jax import lax
from jax.experimental import pallas as pl
from jax.experimental.pallas import tpu as pltpu

LANES = 128
SUBLANES = 8
BF16_ROWS = 16
VMEM_LIMIT_BYTES = 56 * 1024 * 1024

D_MODEL = 1024
HEAD_DIM = 64
HALF = HEAD_DIM // 2
HEADS_PER_GROUP = 8
DILATIONS = (1, 4, 16)
N_GROUPS = len(DILATIONS)
BAND = 128
CLASSES = 16
QKV_W = N_GROUPS * HEADS_PER_GROUP * HEAD_DIM
ATTN_W = HEADS_PER_GROUP * HEAD_DIM
N_PAIRS = ATTN_W // LANES
LRU_W = 1024
LRU_BLOCK = 64
LRU_SLAB = 256
CONV_W = 4
LRU_C = 8.0
ROPE_THETA = 10000.0
LOG2_E = math.log2(math.e)
NEG_INF = -1e30
LN_EPS = 1e-5
DEPTH = 1
ALPHA = (2.0 * DEPTH) ** 0.25
GATE_SLAB = 3
ATTN_SLAB_LAYOUT = ([(0, 0), (0, 1), (0, 2), (None, GATE_SLAB)]
                    + [(g, kind) for g in range(1, N_GROUPS) for kind in range(3)])
N_ATTN_SLABS = len(ATTN_SLAB_LAYOUT)
ATTN_COLS = N_ATTN_SLABS * LANES
COL_K = QKV_W
COL_V = 2 * QKV_W
COL_GATE_A = 3 * QKV_W
COL_U = COL_GATE_A + ATTN_W
COL_GATE_L = COL_U + LRU_W
COL_MERGE = COL_GATE_L + LRU_W

BF16 = jnp.bfloat16
F32 = jnp.float32
ONCE = pl.Buffered(1)


def _cparams(n_axes, flags=None):
    return pltpu.CompilerParams(dimension_semantics=("arbitrary",) * n_axes,
                                vmem_limit_bytes=VMEM_LIMIT_BYTES, flags=flags)


PERMUTE_CLASSES_PER_STEP = 4


def _permute_cast_kernel(x_hbm, xp_ref, buf, sem, *, rows_per_class):
    per = PERMUTE_CLASSES_PER_STEP
    steps_per_batch = CLASSES // per
    step = pl.program_id(0) * steps_per_batch + pl.program_id(1)
    n_steps = pl.num_programs(0) * steps_per_batch

    def copies(st, slot):
        b = lax.div(st, steps_per_batch)
        q = lax.rem(st, steps_per_batch)
        return [pltpu.make_async_copy(x_hbm.at[b, :, q * per + i, :], buf.at[slot, i],
                                      sem.at[slot, i]) for i in range(per)]

    @pl.when(step == 0)
    def _():
        for cp in copies(step, 0):
            cp.start()

    slot = lax.rem(step, 2)

    @pl.when(step + 1 < n_steps)
    def _():
        for cp in copies(step + 1, 1 - slot):
            cp.start()

    for i, cp in enumerate(copies(step, slot)):
        cp.wait()
        xp_ref[i * rows_per_class:(i + 1) * rows_per_class, :] = buf[slot, i].astype(BF16)


def _permute_cast(x):
    B, S, D = x.shape
    A = S // CLASSES
    per = PERMUTE_CLASSES_PER_STEP
    kern = functools.partial(_permute_cast_kernel, rows_per_class=A)
    return pl.pallas_call(
        kern,
        grid=(B, CLASSES // per),
        in_specs=[pl.BlockSpec(memory_space=pl.ANY)],
        out_specs=pl.BlockSpec((None, per * A, D), lambda b, q: (b, q, 0)),
        out_shape=jax.ShapeDtypeStruct((B, S, D), BF16),
        scratch_shapes=[pltpu.VMEM((2, per, A, D), F32), pltpu.SemaphoreType.DMA((2, per))],
        compiler_params=_cparams(2),
        name="permute_cast",
    )(x.reshape(B, A, CLASSES, D))


def _band_bias():
    out = np.zeros((N_GROUPS, 2, 2 * BAND, 2 * BAND), np.float32)
    i = np.arange(BAND)
    for g, d in enumerate(DILATIONS):
        R = CLASSES // d
        L = BAND // R
        m = R * (i % L) + i // L
        m_k = np.concatenate([m - BAND, m])
        dist = m[:, None] - m_k[None, :]
        band = (dist >= 0) & (dist <= BAND)
        for first in range(2):
            valid = band & ((np.arange(2 * BAND) >= BAND)[None, :] | (first == 0))
            bias = np.where(valid, 0.0, NEG_INF).astype(np.float32)
            out[g, first] = np.concatenate([bias, bias], axis=0)
    return out


def _rope_tables(S):
    A = S // CLASSES
    r = np.arange(S)
    pos = (CLASSES * (r % A) + r // A).astype(np.float64)
    inv_freq = ROPE_THETA ** (-np.arange(HALF, dtype=np.float64) / HALF)
    ang = pos[:, None] * inv_freq[None, :]
    cos_t = np.tile(np.cos(ang), (1, LANES // HALF))
    sin = np.sin(ang)
    sin_t = np.concatenate([-sin, -sin, sin, sin], axis=1)
    return cos_t.astype(np.float32), sin_t.astype(np.float32)


def _pair_interleave(w):
    lane = lax.broadcasted_iota(jnp.int32, w.shape, 1)
    from_right = pltpu.roll(w, LANES - HALF, axis=1)
    from_left = pltpu.roll(w, HALF, axis=1)
    quarter = lane // HALF
    return jnp.where(quarter == 1, from_right, jnp.where(quarter == 2, from_left, w))


def _attn_weights_kernel(*refs):
    w_refs = refs[:N_ATTN_SLABS]
    b_refs = refs[N_ATTN_SLABS:2 * N_ATTN_SLABS]
    w_out, b_out = refs[2 * N_ATTN_SLABS:]
    D = w_out.shape[0]
    w_rows = 256
    for i, (g, kind) in enumerate(ATTN_SLAB_LAYOUT):
        is_qk = kind in (0, 1)
        cols = slice(i * LANES, (i + 1) * LANES)
        for r0 in range(0, D, w_rows):
            w = w_refs[i][r0:r0 + w_rows, :]
            w_out[r0:r0 + w_rows, cols] = (_pair_interleave(w) if is_qk else w).astype(BF16)
        b = jnp.broadcast_to(b_refs[i][...], (SUBLANES, LANES))
        b_out[:, cols] = _pair_interleave(b) if is_qk else b


def _attn_weights(w_in, b_in):
    D = w_in.shape[0]
    kind_col = (0, COL_K, COL_V)
    slab0 = [COL_GATE_A // LANES if kind == GATE_SLAB else (kind_col[kind] + g * ATTN_W) // LANES
             for g, kind in ATTN_SLAB_LAYOUT]
    w_specs = [pl.BlockSpec((D, LANES), functools.partial(lambda p, s: (0, s + p), s=s))
               for s in slab0]
    b_specs = [pl.BlockSpec((1, LANES), functools.partial(lambda p, s: (0, s + p), s=s))
               for s in slab0]
    return pl.pallas_call(
        _attn_weights_kernel,
        grid=(N_PAIRS,),
        in_specs=w_specs + b_specs,
        out_specs=[pl.BlockSpec((None, D, ATTN_COLS), lambda p: (p, 0, 0)),
                   pl.BlockSpec((None, SUBLANES, ATTN_COLS), lambda p: (p, 0, 0))],
        out_shape=[jax.ShapeDtypeStruct((N_PAIRS, D, ATTN_COLS), BF16),
                   jax.ShapeDtypeStruct((N_PAIRS, SUBLANES, ATTN_COLS), F32)],
        compiler_params=_cparams(1),
        name="attn_weights",
    )(*([w_in] * N_ATTN_SLABS), *([b_in] * N_ATTN_SLABS))


def _attention_kernel(x_ref, w_s, b_s, cos_ref, sin_ref, bias_ref, o_ref,
                      qkv0_s, qkv_s, gate_s, acc_s, m_s, l_s, *, S, row_chunk, tile_unroll):
    A = S // CLASSES

    def qkv_ref(g, i):
        return qkv0_s.at[i] if g == 0 else qkv_s.at[3 * (g - 1) + i]

    n_tiles = S // BAND
    lane = lax.broadcasted_iota(jnp.int32, (BAND, LANES), 1)
    head0_q = ((lane // HALF) % 2) == 0
    head0_v = lane < HEAD_DIM
    q_scale = (HEAD_DIM ** -0.5) * LOG2_E

    def proj_chunk(rc, slab_lo, slab_hi):
        rows = pl.ds(pl.multiple_of(rc * row_chunk, row_chunk), row_chunk)
        cols = slice(slab_lo * LANES, slab_hi * LANES)
        res = jnp.dot(x_ref[rows, :], w_s[:, cols], preferred_element_type=F32) + b_s[0:1, cols]
        cos = cos_ref[rows, :]
        sin = sin_ref[rows, :]
        for i, (g, kind) in enumerate(ATTN_SLAB_LAYOUT[slab_lo:slab_hi]):
            val = res[:, i * LANES:(i + 1) * LANES]
            if kind == GATE_SLAB:
                gate_s[rows, :] = (val * jax.nn.sigmoid(val)).astype(BF16)
                continue
            if kind in (0, 1):
                val = val * cos + pltpu.roll(val, LANES // 2, axis=1) * sin
            if kind == 0:
                val = val * q_scale
            ref = qkv_ref(g, kind)
            ref[rows, :] = val.astype(ref.dtype)

    def tiles(g, first, count):
        d = DILATIONS[g]
        R = CLASSES // d
        L = BAND // R
        nb = S // (d * BAND)

        def make_run(e):
            rho = lax.shift_right_logical(e, int(math.log2(nb)))
            n = lax.bitwise_and(e, nb - 1)

            def run(j, nn):
                start = (rho + d * j) * A + L * nn
                return pl.ds(pl.multiple_of(start, min(L, BF16_ROWS)), L)
            return n, run

        def gather(ref, run, nn):
            return jnp.concatenate([ref[run(j, nn), :] for j in range(R)], axis=0)

        def scores(n, run):
            q_t = gather(qkv_ref(g, 0), run, n).astype(BF16)
            zero = jnp.zeros_like(q_t)
            qm = jnp.concatenate([jnp.where(head0_q, q_t, zero),
                                  jnp.where(head0_q, zero, q_t)], axis=0)
            n_prev = jnp.maximum(n - 1, 0)
            k_ref = qkv_ref(g, 1)
            kwin = jnp.concatenate([gather(k_ref, run, n_prev), gather(k_ref, run, n)],
                                   axis=0).astype(BF16)
            s = lax.dot_general(qm, kwin, (((1,), (1,)), ((), ())),
                                preferred_element_type=F32)
            return s + bias_ref[g, 1 - jnp.minimum(n, 1)]

        def softmax(s):
            m_blk = jnp.max(s, axis=1, keepdims=True)
            p = jnp.exp2(s - m_blk)
            return p.astype(BF16), m_blk

        def finish(n, run, p, m_blk):
            n_prev = jnp.maximum(n - 1, 0)
            v_ref = qkv_ref(g, 2)
            vwin = jnp.concatenate([gather(v_ref, run, n_prev), gather(v_ref, run, n)],
                                   axis=0).astype(BF16)
            v_ext = jnp.concatenate([vwin, jnp.ones_like(vwin)], axis=1)
            pv = jnp.dot(p, v_ext, preferred_element_type=F32)
            o_t = jnp.where(head0_v, pv[:BAND, :LANES], pv[BAND:, :LANES])
            m_t = jnp.where(head0_v, m_blk[:BAND], m_blk[BAND:])
            l_t = jnp.where(head0_v, pv[:BAND, LANES:], pv[BAND:, LANES:])
            if g > 0:
                m_old = gather(m_s, run, n)
                m_new = jnp.maximum(m_old, m_t)
                w_old = jnp.exp2(m_old - m_new)
                w_new = jnp.exp2(m_t - m_new)
                m_t = m_new
                l_t = gather(l_s, run, n) * w_old + l_t * w_new
                o_t = gather(acc_s, run, n) * w_old + o_t * w_new
            for j in range(R):
                m_s[run(j, n), :] = m_t[j * L:(j + 1) * L]
                l_s[run(j, n), :] = l_t[j * L:(j + 1) * L]
                acc_s[run(j, n), :] = o_t[j * L:(j + 1) * L]

        ctx = [make_run(first + u) for u in range(count)]
        sc, sm = {}, {}
        for t in range(count + 2):
            if t < count:
                sc[t] = scores(*ctx[t])
            if 0 <= t - 1 < count:
                sm[t - 1] = softmax(sc.pop(t - 1))
            if 0 <= t - 2 < count:
                finish(*ctx[t - 2], *sm.pop(t - 2))

    def proj_all(rc, carry):
        proj_chunk(rc, 0, N_ATTN_SLABS)
        return carry

    lax.fori_loop(0, S // row_chunk, proj_all, 0)
    for g in range(N_GROUPS):
        def tile_group(it, carry, g=g):
            tiles(g, it * tile_unroll, tile_unroll)
            return carry

        lax.fori_loop(0, n_tiles // tile_unroll, tile_group, 0)

    for c in range(CLASSES):
        rows = slice(c * A, (c + 1) * A)
        res = (acc_s[rows, :] / l_s[rows, :]) * gate_s[rows, :].astype(F32)
        o_ref[pl.ds(c, A, stride=CLASSES), :] = res


def _attention(xp, w_attn, b_attn):
    B, S, D = xp.shape
    kern = functools.partial(_attention_kernel, S=S, row_chunk=1024, tile_unroll=16)
    cos_t, sin_t = _rope_tables(S)
    const2 = pl.BlockSpec((S, LANES), lambda b, p: (0, 0), pipeline_mode=ONCE)
    scratch = [
        pltpu.VMEM((3, S, LANES), F32),
        pltpu.VMEM((3 * (N_GROUPS - 1), S, LANES), BF16),
        pltpu.VMEM((S, LANES), BF16),
        pltpu.VMEM((S, LANES), F32),
        pltpu.VMEM((S, LANES), F32),
        pltpu.VMEM((S, LANES), F32),
    ]
    return pl.pallas_call(
        kern,
        grid=(B, N_PAIRS),
        in_specs=[
            pl.BlockSpec((None, S, D), lambda b, p: (b, 0, 0), pipeline_mode=ONCE),
            pl.BlockSpec((None, D, ATTN_COLS), lambda b, p: (p, 0, 0)),
            pl.BlockSpec((None, SUBLANES, ATTN_COLS), lambda b, p: (p, 0, 0)),
            const2, const2,
            pl.BlockSpec((N_GROUPS, 2, 2 * BAND, 2 * BAND), lambda b, p: (0, 0, 0, 0),
                         pipeline_mode=ONCE),
        ],
        out_specs=pl.BlockSpec((None, S, LANES), lambda b, p: (b, 0, p)),
        out_shape=jax.ShapeDtypeStruct((B, S, ATTN_W), F32),
        scratch_shapes=scratch,
        compiler_params=_cparams(2),
        name="attention",
    )(xp, w_attn, b_attn, cos_t, sin_t, _band_bias())


def _rglru_kernel(x_ref, wu_ref, wl_ref, bu_ref, bl_ref, cw_ref, cb_ref, wr_ref, wi_ref,
                  br_ref, bi_ref, lam_ref, o_ref,
                  w_s, wg_s, u_pad, gate_s, a_p, h_p, *, S, row_chunk, pitch):
    C = LRU_SLAB
    n_slabs = C // LANES
    n_chunks = S // row_chunk
    n_seg = SUBLANES
    seg_len = S // n_seg
    per_seg = seg_len // row_chunk
    gate_rows = row_chunk
    u_pad[:, 0:SUBLANES, :] = jnp.zeros((n_slabs, SUBLANES, LANES), F32)

    w_s[:, :C] = wu_ref[...].astype(BF16)
    w_s[:, C:] = wl_ref[...].astype(BF16)
    wg_s[...] = jnp.zeros(wg_s.shape, BF16)
    for blk in range(C // LRU_BLOCK):
        rows = slice(blk * LRU_BLOCK, (blk + 1) * LRU_BLOCK)
        wg_s[rows, blk * LRU_BLOCK:(blk + 1) * LRU_BLOCK] = wr_ref[blk].astype(BF16)
        wg_s[rows, C + blk * LRU_BLOCK:C + (blk + 1) * LRU_BLOCK] = wi_ref[blk].astype(BF16)

    def proj_chunk(rc, carry):
        r0 = pl.multiple_of(rc * row_chunk, row_chunk)
        xb = x_ref[pl.ds(r0, row_chunk), :].astype(BF16)
        res = jnp.dot(xb, w_s[...], preferred_element_type=F32)
        u = res[:, :C] + bu_ref[...]
        for s in range(C // LANES):
            u_pad[s, pl.ds(r0 + SUBLANES, row_chunk), :] = u[:, s * LANES:(s + 1) * LANES]
        gl = res[:, C:] + bl_ref[...]
        gate_s[pl.ds(r0, row_chunk), :] = gl * jax.nn.sigmoid(gl)
        return carry

    neg_lam = -lam_ref[...]
    softplus = jnp.maximum(neg_lam, 0.0) + jnp.log1p(jnp.exp(-jnp.abs(neg_lam)))
    decay_rate = LRU_C * softplus

    def gate_chunk(rc, carry):
        seg_start = (rc // per_seg) * pitch + (rc % per_seg) * row_chunk
        for sb in range(row_chunk // gate_rows):
            r0 = pl.multiple_of(rc * row_chunk, row_chunk) + sb * gate_rows
            parts = []
            for s in range(C // LANES):
                lanes = slice(s * LANES, (s + 1) * LANES)
                acc = cb_ref[:, lanes]
                for j in range(CONV_W):
                    off = SUBLANES - (CONV_W - 1) + j
                    acc = acc + u_pad[s, pl.ds(r0 + off, gate_rows), :] * cw_ref[j:j + 1, lanes]
                parts.append(acc)
            uc = jnp.concatenate(parts, axis=1)
            gates = jnp.dot(uc.astype(BF16), wg_s[...], preferred_element_type=F32)
            r = jax.nn.sigmoid(gates[:, :C] + br_ref[...])
            i = jax.nn.sigmoid(gates[:, C:] + bi_ref[...])
            neg_log_a = r * decay_rate
            a = jnp.exp(-neg_log_a)
            one_minus_a2 = jnp.maximum(jnp.tanh(neg_log_a) * (a * a + 1.0), 0.0)
            mult = jnp.exp2(jnp.log(one_minus_a2) * (0.5 * LOG2_E))
            u = mult * (i * uc)
            start = seg_start + sb * gate_rows
            for s in range(n_slabs):
                lanes = slice(s * LANES, (s + 1) * LANES)
                a_p[s, pl.ds(start, gate_rows), :] = a[:, lanes]
                h_p[s, pl.ds(start, gate_rows), :] = u[:, lanes]
        return carry

    def gate_then_proj(rc, carry):
        gate_chunk(rc, carry)
        return proj_chunk(rc + 1, carry)

    proj_chunk(0, 0)
    lax.fori_loop(0, n_chunks - 1, gate_then_proj, 0)
    gate_chunk(n_chunks - 1, 0)

    def scan_step(j, carry):
        rows = pl.ds(j, n_seg, stride=pitch)
        out = []
        for s in range(n_slabs):
            h_prev, p_prev = carry[2 * s], carry[2 * s + 1]
            a = a_p[s, rows, :]
            h = a * h_prev + h_p[s, rows, :]
            p = a * p_prev
            h_p[s, rows, :] = h
            a_p[s, rows, :] = p
            out += [h, p]
        return tuple(out)

    init = (jnp.zeros((n_seg, LANES), F32), jnp.ones((n_seg, LANES), F32)) * n_slabs
    ends = lax.fori_loop(0, seg_len, scan_step, init, unroll=8)

    for s in range(n_slabs):
        lanes = slice(s * LANES, (s + 1) * LANES)
        h_end, p_end = ends[2 * s], ends[2 * s + 1]
        h_in = jnp.zeros((1, LANES), F32)
        for seg in range(n_seg):
            rows = slice(seg * pitch, seg * pitch + seg_len)
            h = h_p[s, rows, :] + a_p[s, rows, :] * h_in
            nat = slice(seg * seg_len, (seg + 1) * seg_len)
            o_ref[nat, lanes] = (h * gate_s[nat, lanes]).astype(BF16)
            h_in = h_end[seg:seg + 1] + p_end[seg:seg + 1] * h_in


def _rglru(x, w_in, b_in, conv_w, conv_b, lru_wr, lru_br, lru_wi, lru_bi, lam):
    B, S, D = x.shape
    C = LRU_SLAB
    n_slabs = LRU_W // C
    per = C // LRU_BLOCK
    pitch = S // SUBLANES + SUBLANES // 2
    kern = functools.partial(_rglru_kernel, S=S, row_chunk=512, pitch=pitch)
    col = lambda c0: (lambda b, c: (0, c0 // C + c))
    row1 = lambda: pl.BlockSpec((1, C), lambda b, c: (0, c))
    blocks = lambda: pl.BlockSpec((per, LRU_BLOCK, LRU_BLOCK), lambda b, c: (c, 0, 0))
    return pl.pallas_call(
        kern,
        grid=(B, n_slabs),
        in_specs=[
            pl.BlockSpec((None, S, D), lambda b, c: (b, 0, 0), pipeline_mode=ONCE),
            pl.BlockSpec((D, C), col(COL_U)), pl.BlockSpec((D, C), col(COL_GATE_L)),
            pl.BlockSpec((1, C), col(COL_U)), pl.BlockSpec((1, C), col(COL_GATE_L)),
            pl.BlockSpec((CONV_W, C), lambda b, c: (0, c)), row1(),
            blocks(), blocks(), row1(), row1(), row1(),
        ],
        out_specs=pl.BlockSpec((None, S, C), lambda b, c: (b, 0, c)),
        out_shape=jax.ShapeDtypeStruct((B, S, LRU_W), BF16),
        scratch_shapes=[
            pltpu.VMEM((D, 2 * C), BF16),
            pltpu.VMEM((C, 2 * C), BF16),
            pltpu.VMEM((C // LANES, S + SUBLANES, LANES), F32),
            pltpu.VMEM((S, C), F32),
            pltpu.VMEM((C // LANES, SUBLANES * pitch, LANES), F32),
            pltpu.VMEM((C // LANES, SUBLANES * pitch, LANES), F32),
        ],
        compiler_params=_cparams(2),
        name="rglru",
    )(x, w_in, w_in, b_in, b_in, conv_w, conv_b, lru_wr, lru_wi, lru_br, lru_bi, lam)


def _merge_out_kernel(x_ref, attn_ref, hg_ref, wga_ref, wgb_ref, bga_ref, bgb_ref,
                      wa_ref, wl_ref, wo_ref, bo_ref, lng_ref, lnb_ref, o_ref,
                      wg_s, wa_s, wl_s, wo_s):
    D = x_ref.shape[1]

    @pl.when(pl.program_id(0) == 0)
    def _():
        rows = 256
        for r0 in range(0, D, rows):
            sl = slice(r0, r0 + rows)
            wg_s[sl, :D] = wga_ref[sl, :].astype(BF16)
            wg_s[sl, D:] = wgb_ref[sl, :].astype(BF16)
            wl_s[sl, :] = wl_ref[sl, :].astype(BF16)
            wo_s[sl, :] = wo_ref[sl, :].astype(BF16)
        for r0 in range(0, ATTN_W, rows):
            sl = slice(r0, r0 + rows)
            wa_s[sl, :] = wa_ref[sl, :].astype(BF16)

    x = x_ref[...]
    xb = x.astype(BF16)
    gate_a = jax.nn.sigmoid(
        jnp.dot(xb, wg_s[:, :D], preferred_element_type=F32) + bga_ref[...])
    gate_b = jax.nn.sigmoid(
        jnp.dot(xb, wg_s[:, D:], preferred_element_type=F32) + bgb_ref[...])
    y_a = jnp.dot(attn_ref[...].astype(BF16), wa_s[...], preferred_element_type=F32)
    y_b = jnp.dot(hg_ref[...], wl_s[...], preferred_element_type=F32)
    merged = gate_a * y_a + gate_b * y_b
    out = jnp.dot(merged.astype(BF16), wo_s[...], preferred_element_type=F32) + bo_ref[...]
    y = ALPHA * x + out
    mu = jnp.mean(y, axis=1, keepdims=True)
    yc = y - mu
    var = jnp.mean(yc * yc, axis=1, keepdims=True)
    o_ref[...] = yc * lax.rsqrt(var + LN_EPS) * lng_ref[...] + lnb_ref[...]


def _merge_out(x2, attn2, hg2, w_in, b_in, w_a, w_l, w_o, b_o, ln_g, ln_b, tm=512):
    T, D = x2.shape
    once = lambda shape, idx: pl.BlockSpec(shape, lambda i: idx, pipeline_mode=ONCE)
    merge_blk = COL_MERGE // D
    return pl.pallas_call(
        _merge_out_kernel,
        grid=(T // tm,),
        in_specs=[
            pl.BlockSpec((tm, D), lambda i: (i, 0)),
            pl.BlockSpec((tm, ATTN_W), lambda i: (i, 0)),
            pl.BlockSpec((tm, LRU_W), lambda i: (i, 0)),
            once((D, D), (0, merge_blk)), once((D, D), (0, merge_blk + 1)),
            once((1, D), (0, merge_blk)), once((1, D), (0, merge_blk + 1)),
            once((ATTN_W, D), (0, 0)), once((LRU_W, D), (0, 0)), once((D, D), (0, 0)),
            once((1, D), (0, 0)), once((1, D), (0, 0)), once((1, D), (0, 0)),
        ],
        out_specs=pl.BlockSpec((tm, D), lambda i: (i, 0)),
        out_shape=jax.ShapeDtypeStruct((T, D), F32),
        scratch_shapes=[
            pltpu.VMEM((D, 2 * D), BF16), pltpu.VMEM((ATTN_W, D), BF16),
            pltpu.VMEM((LRU_W, D), BF16), pltpu.VMEM((D, D), BF16),
        ],
        compiler_params=_cparams(1),
        name="merge_out",
    )(x2, attn2, hg2, w_in, w_in, b_in, b_in, w_a, w_l, w_o, b_o, ln_g, ln_b)


def kernel(x, w_in, b_in, conv_w, conv_b, lru_wr, lru_br, lru_wi, lru_bi, lru_lambda,
           w_attn_proj, w_lru_proj, w_out, b_out, ln_gain, ln_bias):
    B, S, D = x.shape
    assert w_in.shape[0] == DEPTH and S % (CLASSES * BAND) == 0 and D == D_MODEL
    assert COL_MERGE % D == 0 and w_in.shape[2] == COL_MERGE + 2 * D
    row = lambda v: v.reshape(1, -1)
    for l in range(DEPTH):
        wl, bl = w_in[l], row(b_in[l])
        attn = _attention(_permute_cast(x), *_attn_weights(wl, bl))
        hg = _rglru(x, wl, bl, conv_w[l], row(conv_b[l]), lru_wr[l], row(lru_br[l]),
                    lru_wi[l], row(lru_bi[l]), row(lru_lambda[l]))
        out = _merge_out(
            x.reshape(B * S, D), attn.reshape(B * S, ATTN_W), hg.reshape(B * S, LRU_W),
            wl, bl, w_attn_proj[l], w_lru_proj[l], w_out[l],
            row(b_out[l]), row(ln_gain[l]), row(ln_bias[l]))
        x = out.reshape(B, S, D)
    return x
```

```python
import functools
import math

import jax
import jax.numpy as jnp
import numpy as np
from jax import lax
from jax.experimental import pallas as pl
from jax.experimental.pallas import tpu as pltpu

LANES = 128
SUBLANES = 8
BF16_ROWS = 16
VMEM_LIMIT_BYTES = 56 * 1024 * 1024

D_MODEL = 1024
HEAD_DIM = 64
HALF = HEAD_DIM // 2
HEADS_PER_GROUP = 8
DILATIONS = (1, 4, 16)
N_GROUPS = len(DILATIONS)
BAND = 128
CLASSES = 16
QKV_W = N_GROUPS * HEADS_PER_GROUP * HEAD_DIM
ATTN_W = HEADS_PER_GROUP * HEAD_DIM
N_PAIRS = ATTN_W // LANES
LRU_W = 1024
LRU_BLOCK = 64
LRU_SLAB = 256
CONV_W = 4
LRU_C = 8.0
ROPE_THETA = 10000.0
LOG2_E = math.log2(math.e)
NEG_INF = -1e30
LN_EPS = 1e-5
DEPTH = 1
ALPHA = (2.0 * DEPTH) ** 0.25
GATE_SLAB = 3
ATTN_SLAB_LAYOUT = ([(0, 0), (0, 1), (0, 2), (None, GATE_SLAB)]
                    + [(g, kind) for g in range(1, N_GROUPS) for kind in range(3)])
N_ATTN_SLABS = len(ATTN_SLAB_LAYOUT)
ATTN_COLS = N_ATTN_SLABS * LANES
COL_K = QKV_W
COL_V = 2 * QKV_W
COL_GATE_A = 3 * QKV_W
COL_U = COL_GATE_A + ATTN_W
COL_GATE_L = COL_U + LRU_W
COL_MERGE = COL_GATE_L + LRU_W

BF16 = jnp.bfloat16
F32 = jnp.float32
ONCE = pl.Buffered(1)


def _cparams(n_axes, flags=None):
    return pltpu.CompilerParams(dimension_semantics=("arbitrary",) * n_axes,
                                vmem_limit_bytes=VMEM_LIMIT_BYTES, flags=flags)


PERMUTE_CLASSES_PER_STEP = 4


def _permute_cast_kernel(x_hbm, xp_ref, buf, sem, *, rows_per_class):
    per = PERMUTE_CLASSES_PER_STEP
    steps_per_batch = CLASSES // per
    step = pl.program_id(0) * steps_per_batch + pl.program_id(1)
    n_steps = pl.num_programs(0) * steps_per_batch

    def copies(st, slot):
        b = lax.div(st, steps_per_batch)
        q = lax.rem(st, steps_per_batch)
        return [pltpu.make_async_copy(x_hbm.at[b, :, q * per + i, :], buf.at[slot, i],
                                      sem.at[slot, i]) for i in range(per)]

    @pl.when(step == 0)
    def _():
        for cp in copies(step, 0):
            cp.start()

    slot = lax.rem(step, 2)

    @pl.when(step + 1 < n_steps)
    def _():
        for cp in copies(step + 1, 1 - slot):
            cp.start()

    for i, cp in enumerate(copies(step, slot)):
        cp.wait()
        xp_ref[i * rows_per_class:(i + 1) * rows_per_class, :] = buf[slot, i].astype(BF16)


def _permute_cast(x):
    B, S, D = x.shape
    A = S // CLASSES
    per = PERMUTE_CLASSES_PER_STEP
    kern = functools.partial(_permute_cast_kernel, rows_per_class=A)
    return pl.pallas_call(
        kern,
        grid=(B, CLASSES // per),
        in_specs=[pl.BlockSpec(memory_space=pl.ANY)],
        out_specs=pl.BlockSpec((None, per * A, D), lambda b, q: (b, q, 0)),
        out_shape=jax.ShapeDtypeStruct((B, S, D), BF16),
        scratch_shapes=[pltpu.VMEM((2, per, A, D), F32), pltpu.SemaphoreType.DMA((2, per))],
        compiler_params=_cparams(2),
        name="permute_cast",
    )(x.reshape(B, A, CLASSES, D))


def _band_bias():
    out = np.zeros((N_GROUPS, 2, 2 * BAND, 2 * BAND), np.float32)
    i = np.arange(BAND)
    for g, d in enumerate(DILATIONS):
        R = CLASSES // d
        L = BAND // R
        m = R * (i % L) + i // L
        m_k = np.concatenate([m - BAND, m])
        dist = m[:, None] - m_k[None, :]
        band = (dist >= 0) & (dist <= BAND)
        for first in range(2):
            valid = band & ((np.arange(2 * BAND) >= BAND)[None, :] | (first == 0))
            bias = np.where(valid, 0.0, NEG_INF).astype(np.float32)
            out[g, first] = np.concatenate([bias, bias], axis=0)
    return out


def _rope_tables(S):
    A = S // CLASSES
    r = np.arange(S)
    pos = (CLASSES * (r % A) + r // A).astype(np.float64)
    inv_freq = ROPE_THETA ** (-np.arange(HALF, dtype=np.float64) / HALF)
    ang = pos[:, None] * inv_freq[None, :]
    cos_t = np.tile(np.cos(ang), (1, LANES // HALF))
    sin = np.sin(ang)
    sin_t = np.concatenate([-sin, -sin, sin, sin], axis=1)
    return cos_t.astype(np.float32), sin_t.astype(np.float32)


def _pair_interleave(w):
    lane = lax.broadcasted_iota(jnp.int32, w.shape, 1)
    from_right = pltpu.roll(w, LANES - HALF, axis=1)
    from_left = pltpu.roll(w, HALF, axis=1)
    quarter = lane // HALF
    return jnp.where(quarter == 1, from_right, jnp.where(quarter == 2, from_left, w))


def _attn_weights_kernel(*refs):
    w_refs = refs[:N_ATTN_SLABS]
    b_refs = refs[N_ATTN_SLABS:2 * N_ATTN_SLABS]
    w_out, b_out = refs[2 * N_ATTN_SLABS:]
    D = w_out.shape[0]
    w_rows = 256
    for i, (g, kind) in enumerate(ATTN_SLAB_LAYOUT):
        is_qk = kind in (0, 1)
        cols = slice(i * LANES, (i + 1) * LANES)
        for r0 in range(0, D, w_rows):
            w = w_refs[i][r0:r0 + w_rows, :]
            w_out[r0:r0 + w_rows, cols] = (_pair_interleave(w) if is_qk else w).astype(BF16)
        b = jnp.broadcast_to(b_refs[i][...], (SUBLANES, LANES))
        b_out[:, cols] = _pair_interleave(b) if is_qk else b


def _attn_weights(w_in, b_in):
    D = w_in.shape[0]
    kind_col = (0, COL_K, COL_V)
    slab0 = [COL_GATE_A // LANES if kind == GATE_SLAB else (kind_col[kind] + g * ATTN_W) // LANES
             for g, kind in ATTN_SLAB_LAYOUT]
    w_specs = [pl.BlockSpec((D, LANES), functools.partial(lambda p, s: (0, s + p), s=s))
               for s in slab0]
    b_specs = [pl.BlockSpec((1, LANES), functools.partial(lambda p, s: (0, s + p), s=s))
               for s in slab0]
    return pl.pallas_call(
        _attn_weights_kernel,
        grid=(N_PAIRS,),
        in_specs=w_specs + b_specs,
        out_specs=[pl.BlockSpec((None, D, ATTN_COLS), lambda p: (p, 0, 0)),
                   pl.BlockSpec((None, SUBLANES, ATTN_COLS), lambda p: (p, 0, 0))],
        out_shape=[jax.ShapeDtypeStruct((N_PAIRS, D, ATTN_COLS), BF16),
                   jax.ShapeDtypeStruct((N_PAIRS, SUBLANES, ATTN_COLS), F32)],
        compiler_params=_cparams(1),
        name="attn_weights",
    )(*([w_in] * N_ATTN_SLABS), *([b_in] * N_ATTN_SLABS))


def _attention_kernel(x_ref, w_s, b_s, cos_ref, sin_ref, bias_ref, o_ref,
                      qkv0_s, qkv_s, gate_s, acc_s, m_s, l_s, *, S, row_chunk, tile_unroll):
    A = S // CLASSES

    def qkv_ref(g, i):
        return qkv0_s.at[i] if g == 0 else qkv_s.at[3 * (g - 1) + i]

    n_tiles = S // BAND
    lane = lax.broadcasted_iota(jnp.int32, (BAND, LANES), 1)
    head0_q = ((lane // HALF) % 2) == 0
    head0_v = lane < HEAD_DIM
    q_scale = (HEAD_DIM ** -0.5) * LOG2_E

    def proj_chunk(rc, slab_lo, slab_hi):
        rows = pl.ds(pl.multiple_of(rc * row_chunk, row_chunk), row_chunk)
        cols = slice(slab_lo * LANES, slab_hi * LANES)
        res = jnp.dot(x_ref[rows, :], w_s[:, cols], preferred_element_type=F32) + b_s[0:1, cols]
        cos = cos_ref[rows, :]
        sin = sin_ref[rows, :]
        for i, (g, kind) in enumerate(ATTN_SLAB_LAYOUT[slab_lo:slab_hi]):
            val = res[:, i * LANES:(i + 1) * LANES]
            if kind == GATE_SLAB:
                gate_s[rows, :] = (val * jax.nn.sigmoid(val)).astype(BF16)
                continue
            if kind in (0, 1):
                val = val * cos + pltpu.roll(val, LANES // 2, axis=1) * sin
            if kind == 0:
                val = val * q_scale
            ref = qkv_ref(g, kind)
            ref[rows, :] = val.astype(ref.dtype)

    def tiles(g, first, count):
        d = DILATIONS[g]
        R = CLASSES // d
        L = BAND // R
        nb = S // (d * BAND)

        def make_run(e):
            rho = lax.shift_right_logical(e, int(math.log2(nb)))
            n = lax.bitwise_and(e, nb - 1)

            def run(j, nn):
                start = (rho + d * j) * A + L * nn
                return pl.ds(pl.multiple_of(start, min(L, BF16_ROWS)), L)
            return n, run

        def gather(ref, run, nn):
            return jnp.concatenate([ref[run(j, nn), :] for j in range(R)], axis=0)

        def scores(n, run):
            q_t = gather(qkv_ref(g, 0), run, n).astype(BF16)
            zero = jnp.zeros_like(q_t)
            qm = jnp.concatenate([jnp.where(head0_q, q_t, zero),
                                  jnp.where(head0_q, zero, q_t)], axis=0)
            n_prev = jnp.maximum(n - 1, 0)
            k_ref = qkv_ref(g, 1)
            kwin = jnp.concatenate([gather(k_ref, run, n_prev), gather(k_ref, run, n)],
                                   axis=0).astype(BF16)
            s = lax.dot_general(qm, kwin, (((1,), (1,)), ((), ())),
                                preferred_element_type=F32)
            return s + bias_ref[g, 1 - jnp.minimum(n, 1)]

        def softmax(s):
            m_blk = jnp.max(s, axis=1, keepdims=True)
            p = jnp.exp2(s - m_blk)
            return p.astype(BF16), m_blk

        def finish(n, run, p, m_blk):
            n_prev = jnp.maximum(n - 1, 0)
            v_ref = qkv_ref(g, 2)
            vwin = jnp.concatenate([gather(v_ref, run, n_prev), gather(v_ref, run, n)],
                                   axis=0).astype(BF16)
            v_ext = jnp.concatenate([vwin, jnp.ones_like(vwin)], axis=1)
            pv = jnp.dot(p, v_ext, preferred_element_type=F32)
            o_t = jnp.where(head0_v, pv[:BAND, :LANES], pv[BAND:, :LANES])
            m_t = jnp.where(head0_v, m_blk[:BAND], m_blk[BAND:])
            l_t = jnp.where(head0_v, pv[:BAND, LANES:], pv[BAND:, LANES:])
            if g > 0:
                m_old = gather(m_s, run, n)
                m_new = jnp.maximum(m_old, m_t)
                w_old = jnp.exp2(m_old - m_new)
                w_new = jnp.exp2(m_t - m_new)
                m_t = m_new
                l_t = gather(l_s, run, n) * w_old + l_t * w_new
                o_t = gather(acc_s, run, n) * w_old + o_t * w_new
            for j in range(R):
                m_s[run(j, n), :] = m_t[j * L:(j + 1) * L]
                l_s[run(j, n), :] = l_t[j * L:(j + 1) * L]
                acc_s[run(j, n), :] = o_t[j * L:(j + 1) * L]

        ctx = [make_run(first + u) for u in range(count)]
        sc, sm = {}, {}
        for t in range(count + 2):
            if t < count:
                sc[t] = scores(*ctx[t])
            if 0 <= t - 1 < count:
                sm[t - 1] = softmax(sc.pop(t - 1))
            if 0 <= t - 2 < count:
                finish(*ctx[t - 2], *sm.pop(t - 2))

    def proj_all(rc, carry):
        proj_chunk(rc, 0, N_ATTN_SLABS)
        return carry

    lax.fori_loop(0, S // row_chunk, proj_all, 0)
    for g in range(N_GROUPS):
        def tile_group(it, carry, g=g):
            tiles(g, it * tile_unroll, tile_unroll)
            return carry

        lax.fori_loop(0, n_tiles // tile_unroll, tile_group, 0)

    for c in range(CLASSES):
        rows = slice(c * A, (c + 1) * A)
        res = (acc_s[rows, :] / l_s[rows, :]) * gate_s[rows, :].astype(F32)
        o_ref[pl.ds(c, A, stride=CLASSES), :] = res


def _attention(xp, w_attn, b_attn):
    B, S, D = xp.shape
    kern = functools.partial(_attention_kernel, S=S, row_chunk=1024, tile_unroll=32)
    cos_t, sin_t = _rope_tables(S)
    const2 = pl.BlockSpec((S, LANES), lambda b, p: (0, 0), pipeline_mode=ONCE)
    scratch = [
        pltpu.VMEM((3, S, LANES), F32),
        pltpu.VMEM((3 * (N_GROUPS - 1), S, LANES), BF16),
        pltpu.VMEM((S, LANES), BF16),
        pltpu.VMEM((S, LANES), F32),
        pltpu.VMEM((S, LANES), F32),
        pltpu.VMEM((S, LANES), F32),
    ]
    return pl.pallas_call(
        kern,
        grid=(B, N_PAIRS),
        in_specs=[
            pl.BlockSpec((None, S, D), lambda b, p: (b, 0, 0), pipeline_mode=ONCE),
            pl.BlockSpec((None, D, ATTN_COLS), lambda b, p: (p, 0, 0)),
            pl.BlockSpec((None, SUBLANES, ATTN_COLS), lambda b, p: (p, 0, 0)),
            const2, const2,
            pl.BlockSpec((N_GROUPS, 2, 2 * BAND, 2 * BAND), lambda b, p: (0, 0, 0, 0),
                         pipeline_mode=ONCE),
        ],
        out_specs=pl.BlockSpec((None, S, LANES), lambda b, p: (b, 0, p)),
        out_shape=jax.ShapeDtypeStruct((B, S, ATTN_W), F32),
        scratch_shapes=scratch,
        compiler_params=_cparams(2),
        name="attention",
    )(xp, w_attn, b_attn, cos_t, sin_t, _band_bias())


def _rglru_kernel(x_ref, wu_ref, wl_ref, bu_ref, bl_ref, cw_ref, cb_ref, wr_ref, wi_ref,
                  br_ref, bi_ref, lam_ref, o_ref,
                  w_s, wg_s, u_pad, gate_s, a_p, h_p, *, S, row_chunk, pitch):
    C = LRU_SLAB
    n_slabs = C // LANES
    n_chunks = S // row_chunk
    n_seg = SUBLANES
    seg_len = S // n_seg
    per_seg = seg_len // row_chunk
    gate_rows = row_chunk
    u_pad[:, 0:SUBLANES, :] = jnp.zeros((n_slabs, SUBLANES, LANES), F32)

    w_s[:, :C] = wu_ref[...].astype(BF16)
    w_s[:, C:] = wl_ref[...].astype(BF16)
    wg_s[...] = jnp.zeros(wg_s.shape, BF16)
    for blk in range(C // LRU_BLOCK):
        rows = slice(blk * LRU_BLOCK, (blk + 1) * LRU_BLOCK)
        wg_s[rows, blk * LRU_BLOCK:(blk + 1) * LRU_BLOCK] = wr_ref[blk].astype(BF16)
        wg_s[rows, C + blk * LRU_BLOCK:C + (blk + 1) * LRU_BLOCK] = wi_ref[blk].astype(BF16)

    def proj_chunk(rc, carry):
        r0 = pl.multiple_of(rc * row_chunk, row_chunk)
        xb = x_ref[pl.ds(r0, row_chunk), :].astype(BF16)
        res = jnp.dot(xb, w_s[...], preferred_element_type=F32)
        u = res[:, :C] + bu_ref[...]
        for s in range(C // LANES):
            u_pad[s, pl.ds(r0 + SUBLANES, row_chunk), :] = u[:, s * LANES:(s + 1) * LANES]
        gl = res[:, C:] + bl_ref[...]
        gate_s[pl.ds(r0, row_chunk), :] = gl * jax.nn.sigmoid(gl)
        return carry

    neg_lam = -lam_ref[...]
    softplus = jnp.maximum(neg_lam, 0.0) + jnp.log1p(jnp.exp(-jnp.abs(neg_lam)))
    decay_rate = LRU_C * softplus

    def gate_chunk(rc, carry):
        seg_start = (rc // per_seg) * pitch + (rc % per_seg) * row_chunk
        for sb in range(row_chunk // gate_rows):
            r0 = pl.multiple_of(rc * row_chunk, row_chunk) + sb * gate_rows
            parts = []
            for s in range(C // LANES):
                lanes = slice(s * LANES, (s + 1) * LANES)
                acc = cb_ref[:, lanes]
                for j in range(CONV_W):
                    off = SUBLANES - (CONV_W - 1) + j
                    acc = acc + u_pad[s, pl.ds(r0 + off, gate_rows), :] * cw_ref[j:j + 1, lanes]
                parts.append(acc)
            uc = jnp.concatenate(parts, axis=1)
            gates = jnp.dot(uc.astype(BF16), wg_s[...], preferred_element_type=F32)
            r = jax.nn.sigmoid(gates[:, :C] + br_ref[...])
            i = jax.nn.sigmoid(gates[:, C:] + bi_ref[...])
            neg_log_a = r * decay_rate
            a = jnp.exp(-neg_log_a)
            one_minus_a2 = jnp.maximum(jnp.tanh(neg_log_a) * (a * a + 1.0), 0.0)
            mult = jnp.exp2(jnp.log(one_minus_a2) * (0.5 * LOG2_E))
            u = mult * (i * uc)
            start = seg_start + sb * gate_rows
            for s in range(n_slabs):
                lanes = slice(s * LANES, (s + 1) * LANES)
                a_p[s, pl.ds(start, gate_rows), :] = a[:, lanes]
                h_p[s, pl.ds(start, gate_rows), :] = u[:, lanes]
        return carry

    def gate_then_proj(rc, carry):
        gate_chunk(rc, carry)
        return proj_chunk(rc + 1, carry)

    proj_chunk(0, 0)
    lax.fori_loop(0, n_chunks - 1, gate_then_proj, 0)
    gate_chunk(n_chunks - 1, 0)

    def scan_step(j, carry):
        rows = pl.ds(j, n_seg, stride=pitch)
        out = []
        for s in range(n_slabs):
            h_prev, p_prev = carry[2 * s], carry[2 * s + 1]
            a = a_p[s, rows, :]
            h = a * h_prev + h_p[s, rows, :]
            p = a * p_prev
            h_p[s, rows, :] = h
            a_p[s, rows, :] = p
            out += [h, p]
        return tuple(out)

    init = (jnp.zeros((n_seg, LANES), F32), jnp.ones((n_seg, LANES), F32)) * n_slabs
    ends = lax.fori_loop(0, seg_len, scan_step, init, unroll=8)

    for s in range(n_slabs):
        lanes = slice(s * LANES, (s + 1) * LANES)
        h_end, p_end = ends[2 * s], ends[2 * s + 1]
        h_in = jnp.zeros((1, LANES), F32)
        for seg in range(n_seg):
            rows = slice(seg * pitch, seg * pitch + seg_len)
            h = h_p[s, rows, :] + a_p[s, rows, :] * h_in
            nat = slice(seg * seg_len, (seg + 1) * seg_len)
            o_ref[nat, lanes] = (h * gate_s[nat, lanes]).astype(BF16)
            h_in = h_end[seg:seg + 1] + p_end[seg:seg + 1] * h_in


def _rglru(x, w_in, b_in, conv_w, conv_b, lru_wr, lru_br, lru_wi, lru_bi, lam):
    B, S, D = x.shape
    C = LRU_SLAB
    n_slabs = LRU_W // C
    per = C // LRU_BLOCK
    pitch = S // SUBLANES + SUBLANES // 2
    kern = functools.partial(_rglru_kernel, S=S, row_chunk=512, pitch=pitch)
    col = lambda c0: (lambda b, c: (0, c0 // C + c))
    row1 = lambda: pl.BlockSpec((1, C), lambda b, c: (0, c))
    blocks = lambda: pl.BlockSpec((per, LRU_BLOCK, LRU_BLOCK), lambda b, c: (c, 0, 0))
    return pl.pallas_call(
        kern,
        grid=(B, n_slabs),
        in_specs=[
            pl.BlockSpec((None, S, D), lambda b, c: (b, 0, 0), pipeline_mode=ONCE),
            pl.BlockSpec((D, C), col(COL_U)), pl.BlockSpec((D, C), col(COL_GATE_L)),
            pl.BlockSpec((1, C), col(COL_U)), pl.BlockSpec((1, C), col(COL_GATE_L)),
            pl.BlockSpec((CONV_W, C), lambda b, c: (0, c)), row1(),
            blocks(), blocks(), row1(), row1(), row1(),
        ],
        out_specs=pl.BlockSpec((None, S, C), lambda b, c: (b, 0, c)),
        out_shape=jax.ShapeDtypeStruct((B, S, LRU_W), BF16),
        scratch_shapes=[
            pltpu.VMEM((D, 2 * C), BF16),
            pltpu.VMEM((C, 2 * C), BF16),
            pltpu.VMEM((C // LANES, S + SUBLANES, LANES), F32),
            pltpu.VMEM((S, C), F32),
            pltpu.VMEM((C // LANES, SUBLANES * pitch, LANES), F32),
            pltpu.VMEM((C // LANES, SUBLANES * pitch, LANES), F32),
        ],
        compiler_params=_cparams(2),
        name="rglru",
    )(x, w_in, w_in, b_in, b_in, conv_w, conv_b, lru_wr, lru_wi, lru_br, lru_bi, lam)


def _merge_out_kernel(x_ref, attn_ref, hg_ref, wga_ref, wgb_ref, bga_ref, bgb_ref,
                      wa_ref, wl_ref, wo_ref, bo_ref, lng_ref, lnb_ref, o_ref,
                      wg_s, wa_s, wl_s, wo_s):
    D = x_ref.shape[1]

    @pl.when(pl.program_id(0) == 0)
    def _():
        rows = 256
        for r0 in range(0, D, rows):
            sl = slice(r0, r0 + rows)
            wg_s[sl, :D] = wga_ref[sl, :].astype(BF16)
            wg_s[sl, D:] = wgb_ref[sl, :].astype(BF16)
            wl_s[sl, :] = wl_ref[sl, :].astype(BF16)
            wo_s[sl, :] = wo_ref[sl, :].astype(BF16)
        for r0 in range(0, ATTN_W, rows):
            sl = slice(r0, r0 + rows)
            wa_s[sl, :] = wa_ref[sl, :].astype(BF16)

    x = x_ref[...]
    xb = x.astype(BF16)
    gate_a = jax.nn.sigmoid(
        jnp.dot(xb, wg_s[:, :D], preferred_element_type=F32) + bga_ref[...])
    gate_b = jax.nn.sigmoid(
        jnp.dot(xb, wg_s[:, D:], preferred_element_type=F32) + bgb_ref[...])
    y_a = jnp.dot(attn_ref[...].astype(BF16), wa_s[...], preferred_element_type=F32)
    y_b = jnp.dot(hg_ref[...], wl_s[...], preferred_element_type=F32)
    merged = gate_a * y_a + gate_b * y_b
    out = jnp.dot(merged.astype(BF16), wo_s[...], preferred_element_type=F32) + bo_ref[...]
    y = ALPHA * x + out
    mu = jnp.mean(y, axis=1, keepdims=True)
    yc = y - mu
    var = jnp.mean(yc * yc, axis=1, keepdims=True)
    o_ref[...] = yc * lax.rsqrt(var + LN_EPS) * lng_ref[...] + lnb_ref[...]


def _merge_out(x2, attn2, hg2, w_in, b_in, w_a, w_l, w_o, b_o, ln_g, ln_b, tm=512):
    T, D = x2.shape
    once = lambda shape, idx: pl.BlockSpec(shape, lambda i: idx, pipeline_mode=ONCE)
    merge_blk = COL_MERGE // D
    return pl.pallas_call(
        _merge_out_kernel,
        grid=(T // tm,),
        in_specs=[
            pl.BlockSpec((tm, D), lambda i: (i, 0)),
            pl.BlockSpec((tm, ATTN_W), lambda i: (i, 0)),
            pl.BlockSpec((tm, LRU_W), lambda i: (i, 0)),
            once((D, D), (0, merge_blk)), once((D, D), (0, merge_blk + 1)),
            once((1, D), (0, merge_blk)), once((1, D), (0, merge_blk + 1)),
            once((ATTN_W, D), (0, 0)), once((LRU_W, D), (0, 0)), once((D, D), (0, 0)),
            once((1, D), (0, 0)), once((1, D), (0, 0)), once((1, D), (0, 0)),
        ],
        out_specs=pl.BlockSpec((tm, D), lambda i: (i, 0)),
        out_shape=jax.ShapeDtypeStruct((T, D), F32),
        scratch_shapes=[
            pltpu.VMEM((D, 2 * D), BF16), pltpu.VMEM((ATTN_W, D), BF16),
            pltpu.VMEM((LRU_W, D), BF16), pltpu.VMEM((D, D), BF16),
        ],
        compiler_params=_cparams(1),
        name="merge_out",
    )(x2, attn2, hg2, w_in, w_in, b_in, b_in, w_a, w_l, w_o, b_o, ln_g, ln_b)


def kernel(x, w_in, b_in, conv_w, conv_b, lru_wr, lru_br, lru_wi, lru_bi, lru_lambda,
           w_attn_proj, w_lru_proj, w_out, b_out, ln_gain, ln_bias):
    B, S, D = x.shape
    assert w_in.shape[0] == DEPTH and S % (CLASSES * BAND) == 0 and D == D_MODEL
    assert COL_MERGE % D == 0 and w_in.shape[2] == COL_MERGE + 2 * D
    row = lambda v: v.reshape(1, -1)
    for l in range(DEPTH):
        wl, bl = w_in[l], row(b_in[l])
        attn = _attention(_permute_cast(x), *_attn_weights(wl, bl))
        hg = _rglru(x, wl, bl, conv_w[l], row(conv_b[l]), lru_wr[l], row(lru_br[l]),
                    lru_wi[l], row(lru_bi[l]), row(lru_lambda[l]))
        out = _merge_out(
            x.reshape(B * S, D), attn.reshape(B * S, ATTN_W), hg.reshape(B * S, LRU_W),
            wl, bl, w_attn_proj[l], w_lru_proj[l], w_out[l],
            row(b_out[l]), row(ln_gain[l]), row(ln_bias[l]))
        x = out.reshape(B, S, D)
    return x
```

```python
import functools
import math

import jax
import jax.numpy as jnp
import numpy as np
from jax import lax
from jax.experimental import pallas as pl
from jax.experimental.pallas import tpu as pltpu

LANES = 128
SUBLANES = 8
BF16_ROWS = 16
VMEM_LIMIT_BYTES = 56 * 1024 * 1024

D_MODEL = 1024
HEAD_DIM = 64
HALF = HEAD_DIM // 2
HEADS_PER_GROUP = 8
DILATIONS = (1, 4, 16)
N_GROUPS = len(DILATIONS)
BAND = 128
CLASSES = 16
QKV_W = N_GROUPS * HEADS_PER_GROUP * HEAD_DIM
ATTN_W = HEADS_PER_GROUP * HEAD_DIM
N_PAIRS = ATTN_W // LANES
LRU_W = 1024
LRU_BLOCK = 64
LRU_SLAB = 256
CONV_W = 4
LRU_C = 8.0
ROPE_THETA = 10000.0
LOG2_E = math.log2(math.e)
NEG_INF = -1e30
LN_EPS = 1e-5
DEPTH = 1
ALPHA = (2.0 * DEPTH) ** 0.25
GATE_SLAB = 3
ATTN_SLAB_LAYOUT = ([(0, 0), (0, 1), (0, 2), (None, GATE_SLAB)]
                    + [(g, kind) for g in range(1, N_GROUPS) for kind in range(3)])
N_ATTN_SLABS = len(ATTN_SLAB_LAYOUT)
ATTN_COLS = N_ATTN_SLABS * LANES
COL_K = QKV_W
COL_V = 2 * QKV_W
COL_GATE_A = 3 * QKV_W
COL_U = COL_GATE_A + ATTN_W
COL_GATE_L = COL_U + LRU_W
COL_MERGE = COL_GATE_L + LRU_W

BF16 = jnp.bfloat16
F32 = jnp.float32
ONCE = pl.Buffered(1)


def _cparams(n_axes, flags=None):
    return pltpu.CompilerParams(dimension_semantics=("arbitrary",) * n_axes,
                                vmem_limit_bytes=VMEM_LIMIT_BYTES, flags=flags)


PERMUTE_CLASSES_PER_STEP = 4


def _permute_cast_kernel(x_hbm, xp_ref, buf, sem, *, rows_per_class):
    per = PERMUTE_CLASSES_PER_STEP
    steps_per_batch = CLASSES // per
    step = pl.program_id(0) * steps_per_batch + pl.program_id(1)
    n_steps = pl.num_programs(0) * steps_per_batch

    def copies(st, slot):
        b = lax.div(st, steps_per_batch)
        q = lax.rem(st, steps_per_batch)
        return [pltpu.make_async_copy(x_hbm.at[b, :, q * per + i, :], buf.at[slot, i],
                                      sem.at[slot, i]) for i in range(per)]

    @pl.when(step == 0)
    def _():
        for cp in copies(step, 0):
            cp.start()

    slot = lax.rem(step, 2)

    @pl.when(step + 1 < n_steps)
    def _():
        for cp in copies(step + 1, 1 - slot):
            cp.start()

    for i, cp in enumerate(copies(step, slot)):
        cp.wait()
        xp_ref[i * rows_per_class:(i + 1) * rows_per_class, :] = buf[slot, i].astype(BF16)


def _permute_cast(x):
    B, S, D = x.shape
    A = S // CLASSES
    per = PERMUTE_CLASSES_PER_STEP
    kern = functools.partial(_permute_cast_kernel, rows_per_class=A)
    return pl.pallas_call(
        kern,
        grid=(B, CLASSES // per),
        in_specs=[pl.BlockSpec(memory_space=pl.ANY)],
        out_specs=pl.BlockSpec((None, per * A, D), lambda b, q: (b, q, 0)),
        out_shape=jax.ShapeDtypeStruct((B, S, D), BF16),
        scratch_shapes=[pltpu.VMEM((2, per, A, D), F32), pltpu.SemaphoreType.DMA((2, per))],
        compiler_params=_cparams(2),
        name="permute_cast",
    )(x.reshape(B, A, CLASSES, D))


def _band_bias():
    out = np.zeros((N_GROUPS, 2, 2 * BAND, 2 * BAND), np.float32)
    i = np.arange(BAND)
    for g, d in enumerate(DILATIONS):
        R = CLASSES // d
        L = BAND // R
        m = R * (i % L) + i // L
        m_k = np.concatenate([m - BAND, m])
        dist = m[:, None] - m_k[None, :]
        band = (dist >= 0) & (dist <= BAND)
        for first in range(2):
            valid = band & ((np.arange(2 * BAND) >= BAND)[None, :] | (first == 0))
            bias = np.where(valid, 0.0, NEG_INF).astype(np.float32)
            out[g, first] = np.concatenate([bias, bias], axis=0)
    return out


def _rope_tables(S):
    A = S // CLASSES
    r = np.arange(S)
    pos = (CLASSES * (r % A) + r // A).astype(np.float64)
    inv_freq = ROPE_THETA ** (-np.arange(HALF, dtype=np.float64) / HALF)
    ang = pos[:, None] * inv_freq[None, :]
    cos_t = np.tile(np.cos(ang), (1, LANES // HALF))
    sin = np.sin(ang)
    sin_t = np.concatenate([-sin, -sin, sin, sin], axis=1)
    return cos_t.astype(np.float32), sin_t.astype(np.float32)


def _pair_interleave(w):
    lane = lax.broadcasted_iota(jnp.int32, w.shape, 1)
    from_right = pltpu.roll(w, LANES - HALF, axis=1)
    from_left = pltpu.roll(w, HALF, axis=1)
    quarter = lane // HALF
    return jnp.where(quarter == 1, from_right, jnp.where(quarter == 2, from_left, w))


def _attn_weights_kernel(*refs):
    w_refs = refs[:N_ATTN_SLABS]
    b_refs = refs[N_ATTN_SLABS:2 * N_ATTN_SLABS]
    w_out, b_out = refs[2 * N_ATTN_SLABS:]
    D = w_out.shape[0]
    w_rows = 256
    for i, (g, kind) in enumerate(ATTN_SLAB_LAYOUT):
        is_qk = kind in (0, 1)
        cols = slice(i * LANES, (i + 1) * LANES)
        for r0 in range(0, D, w_rows):
            w = w_refs[i][r0:r0 + w_rows, :]
            w_out[r0:r0 + w_rows, cols] = (_pair_interleave(w) if is_qk else w).astype(BF16)
        b = jnp.broadcast_to(b_refs[i][...], (SUBLANES, LANES))
        b_out[:, cols] = _pair_interleave(b) if is_qk else b


def _attn_weights(w_in, b_in):
    D = w_in.shape[0]
    kind_col = (0, COL_K, COL_V)
    slab0 = [COL_GATE_A // LANES if kind == GATE_SLAB else (kind_col[kind] + g * ATTN_W) // LANES
             for g, kind in ATTN_SLAB_LAYOUT]
    w_specs = [pl.BlockSpec((D, LANES), functools.partial(lambda p, s: (0, s + p), s=s))
               for s in slab0]
    b_specs = [pl.BlockSpec((1, LANES), functools.partial(lambda p, s: (0, s + p), s=s))
               for s in slab0]
    return pl.pallas_call(
        _attn_weights_kernel,
        grid=(N_PAIRS,),
        in_specs=w_specs + b_specs,
        out_specs=[pl.BlockSpec((None, D, ATTN_COLS), lambda p: (p, 0, 0)),
                   pl.BlockSpec((None, SUBLANES, ATTN_COLS), lambda p: (p, 0, 0))],
        out_shape=[jax.ShapeDtypeStruct((N_PAIRS, D, ATTN_COLS), BF16),
                   jax.ShapeDtypeStruct((N_PAIRS, SUBLANES, ATTN_COLS), F32)],
        compiler_params=_cparams(1),
        name="attn_weights",
    )(*([w_in] * N_ATTN_SLABS), *([b_in] * N_ATTN_SLABS))


def _attention_kernel(x_ref, w_s, b_s, cos_ref, sin_ref, bias_ref, o_ref,
                      qkv0_s, qkv_s, gate_s, acc_s, m_s, l_s, *, S, row_chunk):
    A = S // CLASSES

    def qkv_ref(g, i):
        return qkv0_s.at[i] if g == 0 else qkv_s.at[3 * (g - 1) + i]

    n_tiles = S // BAND
    lane = lax.broadcasted_iota(jnp.int32, (BAND, LANES), 1)
    head0_q = ((lane // HALF) % 2) == 0
    head0_v = lane < HEAD_DIM
    q_scale = (HEAD_DIM ** -0.5) * LOG2_E

    def proj_chunk(rc, slab_lo, slab_hi):
        rows = pl.ds(pl.multiple_of(rc * row_chunk, row_chunk), row_chunk)
        cols = slice(slab_lo * LANES, slab_hi * LANES)
        res = jnp.dot(x_ref[rows, :], w_s[:, cols], preferred_element_type=F32) + b_s[0:1, cols]
        cos = cos_ref[rows, :]
        sin = sin_ref[rows, :]
        for i, (g, kind) in enumerate(ATTN_SLAB_LAYOUT[slab_lo:slab_hi]):
            val = res[:, i * LANES:(i + 1) * LANES]
            if kind == GATE_SLAB:
                gate_s[rows, :] = (val * jax.nn.sigmoid(val)).astype(BF16)
                continue
            if kind in (0, 1):
                val = val * cos + pltpu.roll(val, LANES // 2, axis=1) * sin
            if kind == 0:
                val = val * q_scale
            ref = qkv_ref(g, kind)
            ref[rows, :] = val.astype(ref.dtype)

    def tiles(g, is_first_group):
        count = n_tiles
        d = DILATIONS[g]
        R = CLASSES // d
        L = BAND // R
        nb = S // (d * BAND)

        def make_run(e):
            rho, n = divmod(e, nb)

            def run(j, nn):
                start = (rho + d * j) * A + L * nn
                return slice(start, start + L)
            return n, run

        def gather(ref, run, nn):
            return jnp.concatenate([ref[run(j, nn), :] for j in range(R)], axis=0)

        def scores(n, run):
            q_t = gather(qkv_ref(g, 0), run, n).astype(BF16)
            zero = jnp.zeros_like(q_t)
            qm = jnp.concatenate([jnp.where(head0_q, q_t, zero),
                                  jnp.where(head0_q, zero, q_t)], axis=0)
            n_prev = max(n - 1, 0)
            k_ref = qkv_ref(g, 1)
            kwin = jnp.concatenate([gather(k_ref, run, n_prev), gather(k_ref, run, n)],
                                   axis=0).astype(BF16)
            s = lax.dot_general(qm, kwin, (((1,), (1,)), ((), ())),
                                preferred_element_type=F32)
            return s + bias_ref[g, int(n == 0)]

        def softmax(s):
            m_blk = jnp.max(s, axis=1, keepdims=True)
            p = jnp.exp2(s - m_blk)
            return p.astype(BF16), m_blk

        def finish(n, run, p, m_blk):
            n_prev = max(n - 1, 0)
            v_ref = qkv_ref(g, 2)
            vwin = jnp.concatenate([gather(v_ref, run, n_prev), gather(v_ref, run, n)],
                                   axis=0).astype(BF16)
            v_ext = jnp.concatenate([vwin, jnp.ones_like(vwin)], axis=1)
            pv = jnp.dot(p, v_ext, preferred_element_type=F32)
            o_t = jnp.where(head0_v, pv[:BAND, :LANES], pv[BAND:, :LANES])
            m_t = jnp.where(head0_v, m_blk[:BAND], m_blk[BAND:])
            l_t = jnp.where(head0_v, pv[:BAND, LANES:], pv[BAND:, LANES:])
            if not is_first_group:
                m_old = gather(m_s, run, n)
                m_new = jnp.maximum(m_old, m_t)
                w_old = jnp.exp2(m_old - m_new)
                w_new = jnp.exp2(m_t - m_new)
                m_t = m_new
                l_t = gather(l_s, run, n) * w_old + l_t * w_new
                o_t = gather(acc_s, run, n) * w_old + o_t * w_new
            for j in range(R):
                m_s[run(j, n), :] = m_t[j * L:(j + 1) * L]
                l_s[run(j, n), :] = l_t[j * L:(j + 1) * L]
                acc_s[run(j, n), :] = o_t[j * L:(j + 1) * L]

        ctx = [make_run(e) for e in range(count)]
        sc, sm = {}, {}
        for t in range(count + 2):
            if t < count:
                sc[t] = scores(*ctx[t])
            if 0 <= t - 1 < count:
                sm[t - 1] = softmax(sc.pop(t - 1))
            if 0 <= t - 2 < count:
                finish(*ctx[t - 2], *sm.pop(t - 2))

    def proj_all(rc, carry):
        proj_chunk(rc, 0, N_ATTN_SLABS)
        return carry

    lax.fori_loop(0, S // row_chunk, proj_all, 0)
    for g in range(N_GROUPS):
        tiles(g, g == 0)

    for c in range(CLASSES):
        rows = slice(c * A, (c + 1) * A)
        res = (acc_s[rows, :] / l_s[rows, :]) * gate_s[rows, :].astype(F32)
        o_ref[pl.ds(c, A, stride=CLASSES), :] = res


def _attention(xp, w_attn, b_attn):
    B, S, D = xp.shape
    kern = functools.partial(_attention_kernel, S=S, row_chunk=1024)
    cos_t, sin_t = _rope_tables(S)
    const2 = pl.BlockSpec((S, LANES), lambda b, p: (0, 0), pipeline_mode=ONCE)
    scratch = [
        pltpu.VMEM((3, S, LANES), F32),
        pltpu.VMEM((3 * (N_GROUPS - 1), S, LANES), BF16),
        pltpu.VMEM((S, LANES), BF16),
        pltpu.VMEM((S, LANES), F32),
        pltpu.VMEM((S, LANES), F32),
        pltpu.VMEM((S, LANES), F32),
    ]
    return pl.pallas_call(
        kern,
        grid=(B, N_PAIRS),
        in_specs=[
            pl.BlockSpec((None, S, D), lambda b, p: (b, 0, 0), pipeline_mode=ONCE),
            pl.BlockSpec((None, D, ATTN_COLS), lambda b, p: (p, 0, 0)),
            pl.BlockSpec((None, SUBLANES, ATTN_COLS), lambda b, p: (p, 0, 0)),
            const2, const2,
            pl.BlockSpec((N_GROUPS, 2, 2 * BAND, 2 * BAND), lambda b, p: (0, 0, 0, 0),
                         pipeline_mode=ONCE),
        ],
        out_specs=pl.BlockSpec((None, S, LANES), lambda b, p: (b, 0, p)),
        out_shape=jax.ShapeDtypeStruct((B, S, ATTN_W), F32),
        scratch_shapes=scratch,
        compiler_params=_cparams(2),
        name="attention",
    )(xp, w_attn, b_attn, cos_t, sin_t, _band_bias())


X_AHEAD = 2
X_SLOTS = X_AHEAD + 1


def _rglru_kernel(x_hbm, wu_ref, wl_ref, bu_ref, bl_ref, cw_ref, cb_ref, wr_ref, wi_ref,
                  br_ref, bi_ref, lam_ref, o_ref,
                  w_s, wg_s, u_pad, gate_s, a_p, h_p, x_buf, x_sem, *, S, row_chunk, pitch):
    C = LRU_SLAB
    n_slabs = C // LANES
    n_chunks = S // row_chunk
    n_seg = SUBLANES
    seg_len = S // n_seg
    per_seg = seg_len // row_chunk
    gate_rows = row_chunk
    u_pad[:, 0:SUBLANES, :] = jnp.zeros((n_slabs, SUBLANES, LANES), F32)

    w_s[:, :C] = wu_ref[...].astype(BF16)
    w_s[:, C:] = wl_ref[...].astype(BF16)
    wg_s[...] = jnp.zeros(wg_s.shape, BF16)
    for blk in range(C // LRU_BLOCK):
        rows = slice(blk * LRU_BLOCK, (blk + 1) * LRU_BLOCK)
        wg_s[rows, blk * LRU_BLOCK:(blk + 1) * LRU_BLOCK] = wr_ref[blk].astype(BF16)
        wg_s[rows, C + blk * LRU_BLOCK:C + (blk + 1) * LRU_BLOCK] = wi_ref[blk].astype(BF16)

    step = pl.program_id(0) * pl.num_programs(1) + pl.program_id(1)
    total_chunks = pl.num_programs(0) * pl.num_programs(1) * n_chunks
    chunks_per_batch = pl.num_programs(1) * n_chunks

    def x_copy(q):
        b = lax.div(q, chunks_per_batch)
        rows = pl.ds(pl.multiple_of(lax.rem(q, n_chunks) * row_chunk, row_chunk), row_chunk)
        slot = lax.rem(q, X_SLOTS)
        return pltpu.make_async_copy(x_hbm.at[b, rows, :], x_buf.at[slot], x_sem.at[slot])

    @pl.when(step == 0)
    def _():
        for q in range(X_AHEAD):
            x_copy(q).start()

    def x_arrive(rc):
        q = step * n_chunks + rc
        x_copy(q).wait()

        @pl.when(q + X_AHEAD < total_chunks)
        def _():
            x_copy(q + X_AHEAD).start()

    def proj_chunk(rc, carry):
        r0 = pl.multiple_of(rc * row_chunk, row_chunk)
        q = step * n_chunks + rc
        xb = x_buf[lax.rem(q, X_SLOTS)].astype(BF16)
        res = jnp.dot(xb, w_s[...], preferred_element_type=F32)
        u = res[:, :C] + bu_ref[...]
        for s in range(C // LANES):
            u_pad[s, pl.ds(r0 + SUBLANES, row_chunk), :] = u[:, s * LANES:(s + 1) * LANES]
        gl = res[:, C:] + bl_ref[...]
        gate_s[pl.ds(r0, row_chunk), :] = gl * jax.nn.sigmoid(gl)
        return carry

    neg_lam = -lam_ref[...]
    softplus = jnp.maximum(neg_lam, 0.0) + jnp.log1p(jnp.exp(-jnp.abs(neg_lam)))
    decay_rate = LRU_C * softplus

    def gate_chunk(rc, carry):
        seg_start = (rc // per_seg) * pitch + (rc % per_seg) * row_chunk
        for sb in range(row_chunk // gate_rows):
            r0 = pl.multiple_of(rc * row_chunk, row_chunk) + sb * gate_rows
            parts = []
            for s in range(C // LANES):
                lanes = slice(s * LANES, (s + 1) * LANES)
                acc = cb_ref[:, lanes]
                for j in range(CONV_W):
                    off = SUBLANES - (CONV_W - 1) + j
                    acc = acc + u_pad[s, pl.ds(r0 + off, gate_rows), :] * cw_ref[j:j + 1, lanes]
                parts.append(acc)
            uc = jnp.concatenate(parts, axis=1)
            gates = jnp.dot(uc.astype(BF16), wg_s[...], preferred_element_type=F32)
            r = jax.nn.sigmoid(gates[:, :C] + br_ref[...])
            i = jax.nn.sigmoid(gates[:, C:] + bi_ref[...])
            neg_log_a = r * decay_rate
            a = jnp.exp(-neg_log_a)
            one_minus_a2 = jnp.maximum(jnp.tanh(neg_log_a) * (a * a + 1.0), 0.0)
            mult = jnp.exp2(jnp.log(one_minus_a2) * (0.5 * LOG2_E))
            u = mult * (i * uc)
            start = seg_start + sb * gate_rows
            for s in range(n_slabs):
                lanes = slice(s * LANES, (s + 1) * LANES)
                a_p[s, pl.ds(start, gate_rows), :] = a[:, lanes]
                h_p[s, pl.ds(start, gate_rows), :] = u[:, lanes]
        return carry

    def gate_then_proj(rc, carry):
        x_arrive(rc + 1)
        gate_chunk(rc, carry)
        return proj_chunk(rc + 1, carry)

    x_arrive(0)
    proj_chunk(0, 0)
    lax.fori_loop(0, n_chunks - 1, gate_then_proj, 0)
    gate_chunk(n_chunks - 1, 0)

    def scan_step(j, carry):
        rows = pl.ds(j, n_seg, stride=pitch)
        out = []
        for s in range(n_slabs):
            h_prev, p_prev = carry[2 * s], carry[2 * s + 1]
            a = a_p[s, rows, :]
            h = a * h_prev + h_p[s, rows, :]
            p = a * p_prev
            h_p[s, rows, :] = h
            a_p[s, rows, :] = p
            out += [h, p]
        return tuple(out)

    init = (jnp.zeros((n_seg, LANES), F32), jnp.ones((n_seg, LANES), F32)) * n_slabs
    ends = lax.fori_loop(0, seg_len, scan_step, init, unroll=8)

    for s in range(n_slabs):
        lanes = slice(s * LANES, (s + 1) * LANES)
        h_end, p_end = ends[2 * s], ends[2 * s + 1]
        h_in = jnp.zeros((1, LANES), F32)
        for seg in range(n_seg):
            rows = slice(seg * pitch, seg * pitch + seg_len)
            h = h_p[s, rows, :] + a_p[s, rows, :] * h_in
            nat = slice(seg * seg_len, (seg + 1) * seg_len)
            o_ref[nat, lanes] = (h * gate_s[nat, lanes]).astype(BF16)
            h_in = h_end[seg:seg + 1] + p_end[seg:seg + 1] * h_in


def _rglru(x, w_in, b_in, conv_w, conv_b, lru_wr, lru_br, lru_wi, lru_bi, lam):
    B, S, D = x.shape
    C = LRU_SLAB
    n_slabs = LRU_W // C
    per = C // LRU_BLOCK
    pitch = S // SUBLANES + SUBLANES // 2
    row_chunk = 512
    kern = functools.partial(_rglru_kernel, S=S, row_chunk=row_chunk, pitch=pitch)
    col = lambda c0: (lambda b, c: (0, c0 // C + c))
    row1 = lambda: pl.BlockSpec((1, C), lambda b, c: (0, c))
    blocks = lambda: pl.BlockSpec((per, LRU_BLOCK, LRU_BLOCK), lambda b, c: (c, 0, 0))
    return pl.pallas_call(
        kern,
        grid=(B, n_slabs),
        in_specs=[
            pl.BlockSpec(memory_space=pl.ANY),
            pl.BlockSpec((D, C), col(COL_U)), pl.BlockSpec((D, C), col(COL_GATE_L)),
            pl.BlockSpec((1, C), col(COL_U)), pl.BlockSpec((1, C), col(COL_GATE_L)),
            pl.BlockSpec((CONV_W, C), lambda b, c: (0, c)), row1(),
            blocks(), blocks(), row1(), row1(), row1(),
        ],
        out_specs=pl.BlockSpec((None, S, C), lambda b, c: (b, 0, c)),
        out_shape=jax.ShapeDtypeStruct((B, S, LRU_W), BF16),
        scratch_shapes=[
            pltpu.VMEM((D, 2 * C), BF16),
            pltpu.VMEM((C, 2 * C), BF16),
            pltpu.VMEM((C // LANES, S + SUBLANES, LANES), F32),
            pltpu.VMEM((S, C), F32),
            pltpu.VMEM((C // LANES, SUBLANES * pitch, LANES), F32),
            pltpu.VMEM((C // LANES, SUBLANES * pitch, LANES), F32),
            pltpu.VMEM((X_SLOTS, row_chunk, D), F32),
            pltpu.SemaphoreType.DMA((X_SLOTS,)),
        ],
        compiler_params=_cparams(2),
        name="rglru",
    )(x, w_in, w_in, b_in, b_in, conv_w, conv_b, lru_wr, lru_wi, lru_br, lru_bi, lam)


def _merge_out_kernel(x_ref, attn_ref, hg_ref, wga_ref, wgb_ref, bga_ref, bgb_ref,
                      wa_ref, wl_ref, wo_ref, bo_ref, lng_ref, lnb_ref, o_ref,
                      wg_s, wa_s, wl_s, wo_s):
    D = x_ref.shape[1]

    @pl.when(pl.program_id(0) == 0)
    def _():
        rows = 256
        for r0 in range(0, D, rows):
            sl = slice(r0, r0 + rows)
            wg_s[sl, :D] = wga_ref[sl, :].astype(BF16)
            wg_s[sl, D:] = wgb_ref[sl, :].astype(BF16)
            wl_s[sl, :] = wl_ref[sl, :].astype(BF16)
            wo_s[sl, :] = wo_ref[sl, :].astype(BF16)
        for r0 in range(0, ATTN_W, rows):
            sl = slice(r0, r0 + rows)
            wa_s[sl, :] = wa_ref[sl, :].astype(BF16)

    x = x_ref[...]
    xb = x.astype(BF16)
    gate_a = jax.nn.sigmoid(
        jnp.dot(xb, wg_s[:, :D], preferred_element_type=F32) + bga_ref[...])
    gate_b = jax.nn.sigmoid(
        jnp.dot(xb, wg_s[:, D:], preferred_element_type=F32) + bgb_ref[...])
    y_a = jnp.dot(attn_ref[...].astype(BF16), wa_s[...], preferred_element_type=F32)
    y_b = jnp.dot(hg_ref[...], wl_s[...], preferred_element_type=F32)
    merged = gate_a * y_a + gate_b * y_b
    out = jnp.dot(merged.astype(BF16), wo_s[...], preferred_element_type=F32) + bo_ref[...]
    y = ALPHA * x + out
    mu = jnp.mean(y, axis=1, keepdims=True)
    yc = y - mu
    var = jnp.mean(yc * yc, axis=1, keepdims=True)
    o_ref[...] = yc * lax.rsqrt(var + LN_EPS) * lng_ref[...] + lnb_ref[...]


def _merge_out(x2, attn2, hg2, w_in, b_in, w_a, w_l, w_o, b_o, ln_g, ln_b, tm=512):
    T, D = x2.shape
    once = lambda shape, idx: pl.BlockSpec(shape, lambda i: idx, pipeline_mode=ONCE)
    merge_blk = COL_MERGE // D
    return pl.pallas_call(
        _merge_out_kernel,
        grid=(T // tm,),
        in_specs=[
            pl.BlockSpec((tm, D), lambda i: (i, 0)),
            pl.BlockSpec((tm, ATTN_W), lambda i: (i, 0)),
            pl.BlockSpec((tm, LRU_W), lambda i: (i, 0)),
            once((D, D), (0, merge_blk)), once((D, D), (0, merge_blk + 1)),
            once((1, D), (0, merge_blk)), once((1, D), (0, merge_blk + 1)),
            once((ATTN_W, D), (0, 0)), once((LRU_W, D), (0, 0)), once((D, D), (0, 0)),
            once((1, D), (0, 0)), once((1, D), (0, 0)), once((1, D), (0, 0)),
        ],
        out_specs=pl.BlockSpec((tm, D), lambda i: (i, 0)),
        out_shape=jax.ShapeDtypeStruct((T, D), F32),
        scratch_shapes=[
            pltpu.VMEM((D, 2 * D), BF16), pltpu.VMEM((ATTN_W, D), BF16),
            pltpu.VMEM((LRU_W, D), BF16), pltpu.VMEM((D, D), BF16),
        ],
        compiler_params=_cparams(1),
        name="merge_out",
    )(x2, attn2, hg2, w_in, w_in, b_in, b_in, w_a, w_l, w_o, b_o, ln_g, ln_b)


def kernel(x, w_in, b_in, conv_w, conv_b, lru_wr, lru_br, lru_wi, lru_bi, lru_lambda,
           w_attn_proj, w_lru_proj, w_out, b_out, ln_gain, ln_bias):
    B, S, D = x.shape
    assert w_in.shape[0] == DEPTH and S % (CLASSES * BAND) == 0 and D == D_MODEL
    assert COL_MERGE % D == 0 and w_in.shape[2] == COL_MERGE + 2 * D
    row = lambda v: v.reshape(1, -1)
    for l in range(DEPTH):
        wl, bl = w_in[l], row(b_in[l])
        attn = _attention(_permute_cast(x), *_attn_weights(wl, bl))
        hg = _rglru(x, wl, bl, conv_w[l], row(conv_b[l]), lru_wr[l], row(lru_br[l]),
                    lru_wi[l], row(lru_bi[l]), row(lru_lambda[l]))
        out = _merge_out(
            x.reshape(B * S, D), attn.reshape(B * S, ATTN_W), hg.reshape(B * S, LRU_W),
            wl, bl, w_attn_proj[l], w_lru_proj[l], w_out[l],
            row(b_out[l]), row(ln_gain[l]), row(ln_bias[l]))
        x = out.reshape(B, S, D)
    return x
```

```python
import functools
import math

import jax
import jax.numpy as jnp
import numpy as np
from jax import lax
from jax.experimental import pallas as pl
from jax.experimental.pallas import tpu as pltpu

LANES = 128
SUBLANES = 8
BF16_ROWS = 16
VMEM_LIMIT_BYTES = 56 * 1024 * 1024

D_MODEL = 1024
HEAD_DIM = 64
HALF = HEAD_DIM // 2
HEADS_PER_GROUP = 8
DILATIONS = (1, 4, 16)
N_GROUPS = len(DILATIONS)
BAND = 128
CLASSES = 16
QKV_W = N_GROUPS * HEADS_PER_GROUP * HEAD_DIM
ATTN_W = HEADS_PER_GROUP * HEAD_DIM
N_PAIRS = ATTN_W // LANES
LRU_W = 1024
LRU_BLOCK = 64
LRU_SLAB = 256
CONV_W = 4
LRU_C = 8.0
ROPE_THETA = 10000.0
LOG2_E = math.log2(math.e)
NEG_INF = -1e30
LN_EPS = 1e-5
DEPTH = 1
ALPHA = (2.0 * DEPTH) ** 0.25
GATE_SLAB = 3
ATTN_SLAB_LAYOUT = ([(0, 0), (0, 1), (0, 2), (None, GATE_SLAB)]
                    + [(g, kind) for g in range(1, N_GROUPS) for kind in range(3)])
N_ATTN_SLABS = len(ATTN_SLAB_LAYOUT)
ATTN_COLS = N_ATTN_SLABS * LANES
COL_K = QKV_W
COL_V = 2 * QKV_W
COL_GATE_A = 3 * QKV_W
COL_U = COL_GATE_A + ATTN_W
COL_GATE_L = COL_U + LRU_W
COL_MERGE = COL_GATE_L + LRU_W

BF16 = jnp.bfloat16
F32 = jnp.float32
ONCE = pl.Buffered(1)


def _cparams(n_axes, flags=None):
    return pltpu.CompilerParams(dimension_semantics=("arbitrary",) * n_axes,
                                vmem_limit_bytes=VMEM_LIMIT_BYTES, flags=flags)


PERMUTE_CLASSES_PER_STEP = 4


def _permute_cast_kernel(x_hbm, xp_ref, buf, sem, *, rows_per_class):
    per = PERMUTE_CLASSES_PER_STEP
    steps_per_batch = CLASSES // per
    step = pl.program_id(0) * steps_per_batch + pl.program_id(1)
    n_steps = pl.num_programs(0) * steps_per_batch

    def copies(st, slot):
        b = lax.div(st, steps_per_batch)
        q = lax.rem(st, steps_per_batch)
        return [pltpu.make_async_copy(x_hbm.at[b, :, q * per + i, :], buf.at[slot, i],
                                      sem.at[slot, i]) for i in range(per)]

    @pl.when(step == 0)
    def _():
        for cp in copies(step, 0):
            cp.start()

    slot = lax.rem(step, 2)

    @pl.when(step + 1 < n_steps)
    def _():
        for cp in copies(step + 1, 1 - slot):
            cp.start()

    for i, cp in enumerate(copies(step, slot)):
        cp.wait()
        xp_ref[i * rows_per_class:(i + 1) * rows_per_class, :] = buf[slot, i].astype(BF16)


def _permute_cast(x):
    B, S, D = x.shape
    A = S // CLASSES
    per = PERMUTE_CLASSES_PER_STEP
    kern = functools.partial(_permute_cast_kernel, rows_per_class=A)
    return pl.pallas_call(
        kern,
        grid=(B, CLASSES // per),
        in_specs=[pl.BlockSpec(memory_space=pl.ANY)],
        out_specs=pl.BlockSpec((None, per * A, D), lambda b, q: (b, q, 0)),
        out_shape=jax.ShapeDtypeStruct((B, S, D), BF16),
        scratch_shapes=[pltpu.VMEM((2, per, A, D), F32), pltpu.SemaphoreType.DMA((2, per))],
        compiler_params=_cparams(2),
        name="permute_cast",
    )(x.reshape(B, A, CLASSES, D))


def _band_bias():
    out = np.zeros((N_GROUPS, 2, 2 * BAND, 2 * BAND), np.float32)
    i = np.arange(BAND)
    for g, d in enumerate(DILATIONS):
        R = CLASSES // d
        L = BAND // R
        m = R * (i % L) + i // L
        m_k = np.concatenate([m - BAND, m])
        dist = m[:, None] - m_k[None, :]
        band = (dist >= 0) & (dist <= BAND)
        for first in range(2):
            valid = band & ((np.arange(2 * BAND) >= BAND)[None, :] | (first == 0))
            bias = np.where(valid, 0.0, NEG_INF).astype(np.float32)
            out[g, first] = np.concatenate([bias, bias], axis=0)
    return out


def _rope_tables(S):
    A = S // CLASSES
    r = np.arange(S)
    pos = (CLASSES * (r % A) + r // A).astype(np.float64)
    inv_freq = ROPE_THETA ** (-np.arange(HALF, dtype=np.float64) / HALF)
    ang = pos[:, None] * inv_freq[None, :]
    cos_t = np.tile(np.cos(ang), (1, LANES // HALF))
    sin = np.sin(ang)
    sin_t = np.concatenate([-sin, -sin, sin, sin], axis=1)
    return cos_t.astype(np.float32), sin_t.astype(np.float32)


def _pair_interleave(w):
    lane = lax.broadcasted_iota(jnp.int32, w.shape, 1)
    from_right = pltpu.roll(w, LANES - HALF, axis=1)
    from_left = pltpu.roll(w, HALF, axis=1)
    quarter = lane // HALF
    return jnp.where(quarter == 1, from_right, jnp.where(quarter == 2, from_left, w))


def _attn_weights_kernel(*refs):
    w_refs = refs[:N_ATTN_SLABS]
    b_refs = refs[N_ATTN_SLABS:2 * N_ATTN_SLABS]
    w_out, b_out = refs[2 * N_ATTN_SLABS:]
    D = w_out.shape[0]
    w_rows = 256
    for i, (g, kind) in enumerate(ATTN_SLAB_LAYOUT):
        is_qk = kind in (0, 1)
        cols = slice(i * LANES, (i + 1) * LANES)
        for r0 in range(0, D, w_rows):
            w = w_refs[i][r0:r0 + w_rows, :]
            w_out[r0:r0 + w_rows, cols] = (_pair_interleave(w) if is_qk else w).astype(BF16)
        b = jnp.broadcast_to(b_refs[i][...], (SUBLANES, LANES))
        b_out[:, cols] = _pair_interleave(b) if is_qk else b


def _attn_weights(w_in, b_in):
    D = w_in.shape[0]
    kind_col = (0, COL_K, COL_V)
    slab0 = [COL_GATE_A // LANES if kind == GATE_SLAB else (kind_col[kind] + g * ATTN_W) // LANES
             for g, kind in ATTN_SLAB_LAYOUT]
    w_specs = [pl.BlockSpec((D, LANES), functools.partial(lambda p, s: (0, s + p), s=s))
               for s in slab0]
    b_specs = [pl.BlockSpec((1, LANES), functools.partial(lambda p, s: (0, s + p), s=s))
               for s in slab0]
    return pl.pallas_call(
        _attn_weights_kernel,
        grid=(N_PAIRS,),
        in_specs=w_specs + b_specs,
        out_specs=[pl.BlockSpec((None, D, ATTN_COLS), lambda p: (p, 0, 0)),
                   pl.BlockSpec((None, SUBLANES, ATTN_COLS), lambda p: (p, 0, 0))],
        out_shape=[jax.ShapeDtypeStruct((N_PAIRS, D, ATTN_COLS), BF16),
                   jax.ShapeDtypeStruct((N_PAIRS, SUBLANES, ATTN_COLS), F32)],
        compiler_params=_cparams(1),
        name="attn_weights",
    )(*([w_in] * N_ATTN_SLABS), *([b_in] * N_ATTN_SLABS))


def _attention_kernel(x_ref, w_s, b_s, cos_ref, sin_ref, bias_ref, o_ref,
                      qkv0_s, qkv_s, gate_s, acc_s, m_s, l_s, *, S, row_chunk):
    A = S // CLASSES

    def qkv_ref(g, i):
        return qkv0_s.at[i] if g == 0 else qkv_s.at[3 * (g - 1) + i]

    n_tiles = S // BAND
    lane = lax.broadcasted_iota(jnp.int32, (BAND, LANES), 1)
    head0_q = ((lane // HALF) % 2) == 0
    head0_v = lane < HEAD_DIM
    head0_kv = lax.broadcasted_iota(jnp.int32, (2 * BAND, LANES), 1) < HEAD_DIM
    ones_h0 = head0_kv.astype(BF16)
    ones_h1 = 1 - ones_h0
    q_scale = (HEAD_DIM ** -0.5) * LOG2_E

    def proj_chunk(rc, slab_lo, slab_hi):
        rows = pl.ds(pl.multiple_of(rc * row_chunk, row_chunk), row_chunk)
        cols = slice(slab_lo * LANES, slab_hi * LANES)
        res = jnp.dot(x_ref[rows, :], w_s[:, cols], preferred_element_type=F32) + b_s[0:1, cols]
        cos = cos_ref[rows, :]
        sin = sin_ref[rows, :]
        for i, (g, kind) in enumerate(ATTN_SLAB_LAYOUT[slab_lo:slab_hi]):
            val = res[:, i * LANES:(i + 1) * LANES]
            if kind == GATE_SLAB:
                gate_s[rows, :] = (val * jax.nn.sigmoid(val)).astype(BF16)
                continue
            if kind in (0, 1):
                val = val * cos + pltpu.roll(val, LANES // 2, axis=1) * sin
            if kind == 0:
                val = val * q_scale
            ref = qkv_ref(g, kind)
            ref[rows, :] = val.astype(ref.dtype)

    def tiles(g, is_first_group):
        count = n_tiles
        d = DILATIONS[g]
        R = CLASSES // d
        L = BAND // R
        nb = S // (d * BAND)

        def make_run(e):
            rho, n = divmod(e, nb)

            def run(j, nn):
                start = (rho + d * j) * A + L * nn
                return slice(start, start + L)
            return n, run

        def gather(ref, run, nn):
            return jnp.concatenate([ref[run(j, nn), :] for j in range(R)], axis=0)

        def scores(n, run):
            q_t = gather(qkv_ref(g, 0), run, n).astype(BF16)
            zero = jnp.zeros_like(q_t)
            qm = jnp.concatenate([jnp.where(head0_q, q_t, zero),
                                  jnp.where(head0_q, zero, q_t)], axis=0)
            n_prev = max(n - 1, 0)
            k_ref = qkv_ref(g, 1)
            kwin = jnp.concatenate([gather(k_ref, run, n_prev), gather(k_ref, run, n)],
                                   axis=0).astype(BF16)
            s = lax.dot_general(qm, kwin, (((1,), (1,)), ((), ())),
                                preferred_element_type=F32)
            return s + bias_ref[g, int(n == 0)]

        def softmax(s):
            m_blk = jnp.max(s, axis=1, keepdims=True)
            p = jnp.exp2(s - m_blk)
            return p.astype(BF16), m_blk

        def finish(n, run, p, m_blk):
            n_prev = max(n - 1, 0)
            v_ref = qkv_ref(g, 2)
            vwin = jnp.concatenate([gather(v_ref, run, n_prev), gather(v_ref, run, n)],
                                   axis=0).astype(BF16)
            zero_v = jnp.zeros_like(vwin)
            v_bd = jnp.concatenate([
                jnp.concatenate([jnp.where(head0_kv, vwin, zero_v), ones_h0], axis=1),
                jnp.concatenate([jnp.where(head0_kv, zero_v, vwin), ones_h1], axis=1)], axis=0)
            p_cat = jnp.concatenate([p[:BAND], p[BAND:]], axis=1)
            pv = jnp.dot(p_cat, v_bd, preferred_element_type=F32)
            o_t = pv[:, :LANES]
            m_t = jnp.where(head0_v, m_blk[:BAND], m_blk[BAND:])
            l_t = pv[:, LANES:]
            if not is_first_group:
                m_old = gather(m_s, run, n)
                m_new = jnp.maximum(m_old, m_t)
                w_old = jnp.exp2(m_old - m_new)
                w_new = jnp.exp2(m_t - m_new)
                m_t = m_new
                l_t = gather(l_s, run, n) * w_old + l_t * w_new
                o_t = gather(acc_s, run, n) * w_old + o_t * w_new
            for j in range(R):
                m_s[run(j, n), :] = m_t[j * L:(j + 1) * L]
                l_s[run(j, n), :] = l_t[j * L:(j + 1) * L]
                acc_s[run(j, n), :] = o_t[j * L:(j + 1) * L]

        ctx = [make_run(e) for e in range(count)]
        sc, sm = {}, {}
        for t in range(count + 2):
            if t < count:
                sc[t] = scores(*ctx[t])
            if 0 <= t - 1 < count:
                sm[t - 1] = softmax(sc.pop(t - 1))
            if 0 <= t - 2 < count:
                finish(*ctx[t - 2], *sm.pop(t - 2))

    def proj_all(rc, carry):
        proj_chunk(rc, 0, N_ATTN_SLABS)
        return carry

    lax.fori_loop(0, S // row_chunk, proj_all, 0)
    for g in range(N_GROUPS):
        tiles(g, g == 0)

    for c in range(CLASSES):
        rows = slice(c * A, (c + 1) * A)
        res = (acc_s[rows, :] / l_s[rows, :]) * gate_s[rows, :].astype(F32)
        o_ref[pl.ds(c, A, stride=CLASSES), :] = res


def _attention(xp, w_attn, b_attn):
    B, S, D = xp.shape
    kern = functools.partial(_attention_kernel, S=S, row_chunk=1024)
    cos_t, sin_t = _rope_tables(S)
    const2 = pl.BlockSpec((S, LANES), lambda b, p: (0, 0), pipeline_mode=ONCE)
    scratch = [
        pltpu.VMEM((3, S, LANES), F32),
        pltpu.VMEM((3 * (N_GROUPS - 1), S, LANES), BF16),
        pltpu.VMEM((S, LANES), BF16),
        pltpu.VMEM((S, LANES), F32),
        pltpu.VMEM((S, LANES), F32),
        pltpu.VMEM((S, LANES), F32),
    ]
    return pl.pallas_call(
        kern,
        grid=(B, N_PAIRS),
        in_specs=[
            pl.BlockSpec((None, S, D), lambda b, p: (b, 0, 0), pipeline_mode=ONCE),
            pl.BlockSpec((None, D, ATTN_COLS), lambda b, p: (p, 0, 0)),
            pl.BlockSpec((None, SUBLANES, ATTN_COLS), lambda b, p: (p, 0, 0)),
            const2, const2,
            pl.BlockSpec((N_GROUPS, 2, 2 * BAND, 2 * BAND), lambda b, p: (0, 0, 0, 0),
                         pipeline_mode=ONCE),
        ],
        out_specs=pl.BlockSpec((None, S, LANES), lambda b, p: (b, 0, p)),
        out_shape=jax.ShapeDtypeStruct((B, S, ATTN_W), F32),
        scratch_shapes=scratch,
        compiler_params=_cparams(2),
        name="attention",
    )(xp, w_attn, b_attn, cos_t, sin_t, _band_bias())


X_AHEAD = 2
X_SLOTS = X_AHEAD + 1


def _rglru_kernel(x_hbm, wu_ref, wl_ref, bu_ref, bl_ref, cw_ref, cb_ref, wr_ref, wi_ref,
                  br_ref, bi_ref, lam_ref, o_ref,
                  w_s, wg_s, u_pad, gate_s, a_p, h_p, x_buf, x_sem, *, S, row_chunk, pitch):
    C = LRU_SLAB
    n_slabs = C // LANES
    n_chunks = S // row_chunk
    n_seg = SUBLANES
    seg_len = S // n_seg
    per_seg = seg_len // row_chunk
    u_pad[:, 0:SUBLANES, :] = jnp.zeros((n_slabs, SUBLANES, LANES), F32)

    w_s[:, :C] = wu_ref[...].astype(BF16)
    w_s[:, C:] = wl_ref[...].astype(BF16)
    wg_s[...] = jnp.zeros(wg_s.shape, BF16)
    for blk in range(C // LRU_BLOCK):
        rows = slice(blk * LRU_BLOCK, (blk + 1) * LRU_BLOCK)
        wg_s[rows, blk * LRU_BLOCK:(blk + 1) * LRU_BLOCK] = wr_ref[blk].astype(BF16)
        wg_s[rows, C + blk * LRU_BLOCK:C + (blk + 1) * LRU_BLOCK] = wi_ref[blk].astype(BF16)

    step = pl.program_id(0) * pl.num_programs(1) + pl.program_id(1)
    total_chunks = pl.num_programs(0) * pl.num_programs(1) * n_chunks
    chunks_per_batch = pl.num_programs(1) * n_chunks

    def x_copy(q):
        b = lax.div(q, chunks_per_batch)
        rows = pl.ds(pl.multiple_of(lax.rem(q, n_chunks) * row_chunk, row_chunk), row_chunk)
        slot = lax.rem(q, X_SLOTS)
        return pltpu.make_async_copy(x_hbm.at[b, rows, :], x_buf.at[slot], x_sem.at[slot])

    @pl.when(step == 0)
    def _():
        for q in range(X_AHEAD):
            x_copy(q).start()

    def x_arrive(rc):
        q = step * n_chunks + rc
        x_copy(q).wait()

        @pl.when(q + X_AHEAD < total_chunks)
        def _():
            x_copy(q + X_AHEAD).start()

    def proj_chunk(rc, carry):
        r0 = pl.multiple_of(rc * row_chunk, row_chunk)
        q = step * n_chunks + rc
        xb = x_buf[lax.rem(q, X_SLOTS)].astype(BF16)
        res = jnp.dot(xb, w_s[...], preferred_element_type=F32)
        u = res[:, :C] + bu_ref[...]
        for s in range(n_slabs):
            u_pad[s, pl.ds(r0 + SUBLANES, row_chunk), :] = u[:, s * LANES:(s + 1) * LANES]
        gl = res[:, C:] + bl_ref[...]
        gate_s[pl.ds(r0, row_chunk), :] = gl * jax.nn.sigmoid(gl)
        return carry

    neg_lam = -lam_ref[...]
    softplus = jnp.maximum(neg_lam, 0.0) + jnp.log1p(jnp.exp(-jnp.abs(neg_lam)))
    decay_rate = LRU_C * softplus

    def gate_chunk(rc, carry):
        r0 = pl.multiple_of(rc * row_chunk, row_chunk)
        parts = []
        for s in range(n_slabs):
            lanes = slice(s * LANES, (s + 1) * LANES)
            acc = cb_ref[:, lanes]
            for j in range(CONV_W):
                off = SUBLANES - (CONV_W - 1) + j
                acc = acc + u_pad[s, pl.ds(r0 + off, row_chunk), :] * cw_ref[j:j + 1, lanes]
            parts.append(acc)
        uc = jnp.concatenate(parts, axis=1)
        gates = jnp.dot(uc.astype(BF16), wg_s[...], preferred_element_type=F32)
        r = jax.nn.sigmoid(gates[:, :C] + br_ref[...])
        i = jax.nn.sigmoid(gates[:, C:] + bi_ref[...])
        neg_log_a = r * decay_rate
        a = jnp.exp(-neg_log_a)
        one_minus_a2 = jnp.maximum(jnp.tanh(neg_log_a) * (a * a + 1.0), 0.0)
        mult = jnp.exp2(jnp.log(one_minus_a2) * (0.5 * LOG2_E))
        u = mult * (i * uc)
        start = (rc // per_seg) * pitch + (rc % per_seg) * row_chunk
        for s in range(n_slabs):
            lanes = slice(s * LANES, (s + 1) * LANES)
            a_p[s, pl.ds(start, row_chunk), :] = a[:, lanes]
            h_p[s, pl.ds(start, row_chunk), :] = u[:, lanes]
        return carry

    def gate_then_proj(rc, carry):
        x_arrive(rc + 1)
        gate_chunk(rc, carry)
        return proj_chunk(rc + 1, carry)

    x_arrive(0)
    proj_chunk(0, 0)
    lax.fori_loop(0, n_chunks - 1, gate_then_proj, 0)
    gate_chunk(n_chunks - 1, 0)

    def scan_step(j, carry):
        rows = pl.ds(j, n_seg, stride=pitch)
        out = []
        for s in range(n_slabs):
            h_prev, p_prev = carry[2 * s], carry[2 * s + 1]
            a = a_p[s, rows, :]
            h = a * h_prev + h_p[s, rows, :]
            p = a * p_prev
            h_p[s, rows, :] = h
            a_p[s, rows, :] = p
            out += [h, p]
        return tuple(out)

    init = (jnp.zeros((n_seg, LANES), F32), jnp.ones((n_seg, LANES), F32)) * n_slabs
    ends = lax.fori_loop(0, seg_len, scan_step, init, unroll=8)

    for s in range(n_slabs):
        lanes = slice(s * LANES, (s + 1) * LANES)
        h_end, p_end = ends[2 * s], ends[2 * s + 1]
        h_in = jnp.zeros((1, LANES), F32)
        for seg in range(n_seg):
            rows = slice(seg * pitch, seg * pitch + seg_len)
            h = h_p[s, rows, :] + a_p[s, rows, :] * h_in
            nat = slice(seg * seg_len, (seg + 1) * seg_len)
            o_ref[nat, lanes] = (h * gate_s[nat, lanes]).astype(BF16)
            h_in = h_end[seg:seg + 1] + p_end[seg:seg + 1] * h_in


def _rglru(x, w_in, b_in, conv_w, conv_b, lru_wr, lru_br, lru_wi, lru_bi, lam):
    B, S, D = x.shape
    C = LRU_SLAB
    n_slabs = LRU_W // C
    per = C // LRU_BLOCK
    pitch = S // SUBLANES + SUBLANES // 2
    row_chunk = 512
    kern = functools.partial(_rglru_kernel, S=S, row_chunk=row_chunk, pitch=pitch)
    col = lambda c0: (lambda b, c: (0, c0 // C + c))
    row1 = lambda: pl.BlockSpec((1, C), lambda b, c: (0, c))
    blocks = lambda: pl.BlockSpec((per, LRU_BLOCK, LRU_BLOCK), lambda b, c: (c, 0, 0))
    return pl.pallas_call(
        kern,
        grid=(B, n_slabs),
        in_specs=[
            pl.BlockSpec(memory_space=pl.ANY),
            pl.BlockSpec((D, C), col(COL_U)), pl.BlockSpec((D, C), col(COL_GATE_L)),
            pl.BlockSpec((1, C), col(COL_U)), pl.BlockSpec((1, C), col(COL_GATE_L)),
            pl.BlockSpec((CONV_W, C), lambda b, c: (0, c)), row1(),
            blocks(), blocks(), row1(), row1(), row1(),
        ],
        out_specs=pl.BlockSpec((None, S, C), lambda b, c: (b, 0, c)),
        out_shape=jax.ShapeDtypeStruct((B, S, LRU_W), BF16),
        scratch_shapes=[
            pltpu.VMEM((D, 2 * C), BF16),
            pltpu.VMEM((C, 2 * C), BF16),
            pltpu.VMEM((C // LANES, S + SUBLANES, LANES), F32),
            pltpu.VMEM((S, C), F32),
            pltpu.VMEM((C // LANES, SUBLANES * pitch, LANES), F32),
            pltpu.VMEM((C // LANES, SUBLANES * pitch, LANES), F32),
            pltpu.VMEM((X_SLOTS, row_chunk, D), F32),
            pltpu.SemaphoreType.DMA((X_SLOTS,)),
        ],
        compiler_params=_cparams(2),
        name="rglru",
    )(x, w_in, w_in, b_in, b_in, conv_w, conv_b, lru_wr, lru_wi, lru_br, lru_bi, lam)


def _merge_out_kernel(x_ref, attn_ref, hg_ref, wga_ref, wgb_ref, bga_ref, bgb_ref,
                      wa_ref, wl_ref, wo_ref, bo_ref, lng_ref, lnb_ref, o_ref,
                      wg_s, wa_s, wl_s, wo_s):
    D = x_ref.shape[1]

    @pl.when(pl.program_id(0) == 0)
    def _():
        rows = 256
        for r0 in range(0, D, rows):
            sl = slice(r0, r0 + rows)
            wg_s[sl, :D] = wga_ref[sl, :].astype(BF16)
            wg_s[sl, D:] = wgb_ref[sl, :].astype(BF16)
            wl_s[sl, :] = wl_ref[sl, :].astype(BF16)
            wo_s[sl, :] = wo_ref[sl, :].astype(BF16)
        for r0 in range(0, ATTN_W, rows):
            sl = slice(r0, r0 + rows)
            wa_s[sl, :] = wa_ref[sl, :].astype(BF16)

    x = x_ref[...]
    xb = x.astype(BF16)
    gate_a = jax.nn.sigmoid(
        jnp.dot(xb, wg_s[:, :D], preferred_element_type=F32) + bga_ref[...])
    gate_b = jax.nn.sigmoid(
        jnp.dot(xb, wg_s[:, D:], preferred_element_type=F32) + bgb_ref[...])
    y_a = jnp.dot(attn_ref[...].astype(BF16), wa_s[...], preferred_element_type=F32)
    y_b = jnp.dot(hg_ref[...], wl_s[...], preferred_element_type=F32)
    merged = gate_a * y_a + gate_b * y_b
    out = jnp.dot(merged.astype(BF16), wo_s[...], preferred_element_type=F32) + bo_ref[...]
    y = ALPHA * x + out
    mu = jnp.mean(y, axis=1, keepdims=True)
    yc = y - mu
    var = jnp.mean(yc * yc, axis=1, keepdims=True)
    o_ref[...] = yc * lax.rsqrt(var + LN_EPS) * lng_ref[...] + lnb_ref[...]


def _merge_out(x2, attn2, hg2, w_in, b_in, w_a, w_l, w_o, b_o, ln_g, ln_b, tm=512):
    T, D = x2.shape
    once = lambda shape, idx: pl.BlockSpec(shape, lambda i: idx, pipeline_mode=ONCE)
    merge_blk = COL_MERGE // D
    return pl.pallas_call(
        _merge_out_kernel,
        grid=(T // tm,),
        in_specs=[
            pl.BlockSpec((tm, D), lambda i: (i, 0)),
            pl.BlockSpec((tm, ATTN_W), lambda i: (i, 0)),
            pl.BlockSpec((tm, LRU_W), lambda i: (i, 0)),
            once((D, D), (0, merge_blk)), once((D, D), (0, merge_blk + 1)),
            once((1, D), (0, merge_blk)), once((1, D), (0, merge_blk + 1)),
            once((ATTN_W, D), (0, 0)), once((LRU_W, D), (0, 0)), once((D, D), (0, 0)),
            once((1, D), (0, 0)), once((1, D), (0, 0)), once((1, D), (0, 0)),
        ],
        out_specs=pl.BlockSpec((tm, D), lambda i: (i, 0)),
        out_shape=jax.ShapeDtypeStruct((T, D), F32),
        scratch_shapes=[
            pltpu.VMEM((D, 2 * D), BF16), pltpu.VMEM((ATTN_W, D), BF16),
            pltpu.VMEM((LRU_W, D), BF16), pltpu.VMEM((D, D), BF16),
        ],
        compiler_params=_cparams(1),
        name="merge_out",
    )(x2, attn2, hg2, w_in, w_in, b_in, b_in, w_a, w_l, w_o, b_o, ln_g, ln_b)


def kernel(x, w_in, b_in, conv_w, conv_b, lru_wr, lru_br, lru_wi, lru_bi, lru_lambda,
           w_attn_proj, w_lru_proj, w_out, b_out, ln_gain, ln_bias):
    B, S, D = x.shape
    assert w_in.shape[0] == DEPTH and S % (CLASSES * BAND) == 0 and D == D_MODEL
    assert COL_MERGE % D == 0 and w_in.shape[2] == COL_MERGE + 2 * D
    row = lambda v: v.reshape(1, -1)
    for l in range(DEPTH):
        wl, bl = w_in[l], row(b_in[l])
        attn = _attention(_permute_cast(x), *_attn_weights(wl, bl))
        hg = _rglru(x, wl, bl, conv_w[l], row(conv_b[l]), lru_wr[l], row(lru_br[l]),
                    lru_wi[l], row(lru_bi[l]), row(lru_lambda[l]))
        out = _merge_out(
            x.reshape(B * S, D), attn.reshape(B * S, ATTN_W), hg.reshape(B * S, LRU_W),
            wl, bl, w_attn_proj[l], w_lru_proj[l], w_out[l],
            row(b_out[l]), row(ln_gain[l]), row(ln_bias[l]))
        x = out.reshape(B, S, D)
    return x
```

```python
import functools
import math

import jax
import jax.numpy as jnp
import numpy as np
from jax import lax
from jax.experimental import pallas as pl
from jax.experimental.pallas import tpu as pltpu

LANES = 128
SUBLANES = 8
BF16_ROWS = 16
VMEM_LIMIT_BYTES = 56 * 1024 * 1024

D_MODEL = 1024
HEAD_DIM = 64
HALF = HEAD_DIM // 2
HEADS_PER_GROUP = 8
DILATIONS = (1, 4, 16)
N_GROUPS = len(DILATIONS)
BAND = 128
CLASSES = 16
QKV_W = N_GROUPS * HEADS_PER_GROUP * HEAD_DIM
ATTN_W = HEADS_PER_GROUP * HEAD_DIM
N_PAIRS = ATTN_W // LANES
LRU_W = 1024
LRU_BLOCK = 64
LRU_SLAB = 256
CONV_W = 4
LRU_C = 8.0
ROPE_THETA = 10000.0
LOG2_E = math.log2(math.e)
NEG_INF = -1e30
LN_EPS = 1e-5
DEPTH = 1
ALPHA = (2.0 * DEPTH) ** 0.25
GATE_SLAB = 3
ATTN_SLAB_LAYOUT = ([(0, 0), (0, 1), (0, 2), (None, GATE_SLAB)]
                    + [(g, kind) for g in range(1, N_GROUPS) for kind in range(3)])
N_ATTN_SLABS = len(ATTN_SLAB_LAYOUT)
ATTN_COLS = N_ATTN_SLABS * LANES
COL_K = QKV_W
COL_V = 2 * QKV_W
COL_GATE_A = 3 * QKV_W
COL_U = COL_GATE_A + ATTN_W
COL_GATE_L = COL_U + LRU_W
COL_MERGE = COL_GATE_L + LRU_W

BF16 = jnp.bfloat16
F32 = jnp.float32
ONCE = pl.Buffered(1)


def _cparams(n_axes, flags=None):
    return pltpu.CompilerParams(dimension_semantics=("arbitrary",) * n_axes,
                                vmem_limit_bytes=VMEM_LIMIT_BYTES, flags=flags)


PERMUTE_CLASSES_PER_STEP = 4


def _permute_cast_kernel(x_hbm, xp_ref, buf, sem, *, rows_per_class):
    per = PERMUTE_CLASSES_PER_STEP
    steps_per_batch = CLASSES // per
    step = pl.program_id(0) * steps_per_batch + pl.program_id(1)
    n_steps = pl.num_programs(0) * steps_per_batch

    def copies(st, slot):
        b = lax.div(st, steps_per_batch)
        q = lax.rem(st, steps_per_batch)
        return [pltpu.make_async_copy(x_hbm.at[b, :, q * per + i, :], buf.at[slot, i],
                                      sem.at[slot, i]) for i in range(per)]

    @pl.when(step == 0)
    def _():
        for cp in copies(step, 0):
            cp.start()

    slot = lax.rem(step, 2)

    @pl.when(step + 1 < n_steps)
    def _():
        for cp in copies(step + 1, 1 - slot):
            cp.start()

    for i, cp in enumerate(copies(step, slot)):
        cp.wait()
        xp_ref[i * rows_per_class:(i + 1) * rows_per_class, :] = buf[slot, i].astype(BF16)


def _permute_cast(x):
    B, S, D = x.shape
    A = S // CLASSES
    per = PERMUTE_CLASSES_PER_STEP
    kern = functools.partial(_permute_cast_kernel, rows_per_class=A)
    return pl.pallas_call(
        kern,
        grid=(B, CLASSES // per),
        in_specs=[pl.BlockSpec(memory_space=pl.ANY)],
        out_specs=pl.BlockSpec((None, per * A, D), lambda b, q: (b, q, 0)),
        out_shape=jax.ShapeDtypeStruct((B, S, D), BF16),
        scratch_shapes=[pltpu.VMEM((2, per, A, D), F32), pltpu.SemaphoreType.DMA((2, per))],
        compiler_params=_cparams(2),
        name="permute_cast",
    )(x.reshape(B, A, CLASSES, D))


def _band_bias():
    out = np.zeros((N_GROUPS, 2, 2 * BAND, 2 * BAND), np.float32)
    i = np.arange(BAND)
    for g, d in enumerate(DILATIONS):
        R = CLASSES // d
        L = BAND // R
        m = R * (i % L) + i // L
        m_k = np.concatenate([m - BAND, m])
        dist = m[:, None] - m_k[None, :]
        band = (dist >= 0) & (dist <= BAND)
        for first in range(2):
            valid = band & ((np.arange(2 * BAND) >= BAND)[None, :] | (first == 0))
            bias = np.where(valid, 0.0, NEG_INF).astype(np.float32)
            out[g, first] = np.concatenate([bias, bias], axis=0)
    return out


def _rope_tables(S):
    A = S // CLASSES
    r = np.arange(S)
    pos = (CLASSES * (r % A) + r // A).astype(np.float64)
    inv_freq = ROPE_THETA ** (-np.arange(HALF, dtype=np.float64) / HALF)
    ang = pos[:, None] * inv_freq[None, :]
    cos_t = np.tile(np.cos(ang), (1, LANES // HALF))
    sin = np.sin(ang)
    sin_t = np.concatenate([-sin, -sin, sin, sin], axis=1)
    return cos_t.astype(np.float32), sin_t.astype(np.float32)


def _pair_interleave(w):
    lane = lax.broadcasted_iota(jnp.int32, w.shape, 1)
    from_right = pltpu.roll(w, LANES - HALF, axis=1)
    from_left = pltpu.roll(w, HALF, axis=1)
    quarter = lane // HALF
    return jnp.where(quarter == 1, from_right, jnp.where(quarter == 2, from_left, w))


def _attn_weights_kernel(*refs):
    w_refs = refs[:N_ATTN_SLABS]
    b_refs = refs[N_ATTN_SLABS:2 * N_ATTN_SLABS]
    w_out, b_out = refs[2 * N_ATTN_SLABS:]
    D = w_out.shape[0]
    w_rows = 256
    for i, (g, kind) in enumerate(ATTN_SLAB_LAYOUT):
        is_qk = kind in (0, 1)
        cols = slice(i * LANES, (i + 1) * LANES)
        for r0 in range(0, D, w_rows):
            w = w_refs[i][r0:r0 + w_rows, :]
            w_out[r0:r0 + w_rows, cols] = (_pair_interleave(w) if is_qk else w).astype(BF16)
        b = jnp.broadcast_to(b_refs[i][...], (SUBLANES, LANES))
        b_out[:, cols] = _pair_interleave(b) if is_qk else b


def _attn_weights(w_in, b_in):
    D = w_in.shape[0]
    kind_col = (0, COL_K, COL_V)
    slab0 = [COL_GATE_A // LANES if kind == GATE_SLAB else (kind_col[kind] + g * ATTN_W) // LANES
             for g, kind in ATTN_SLAB_LAYOUT]
    w_specs = [pl.BlockSpec((D, LANES), functools.partial(lambda p, s: (0, s + p), s=s))
               for s in slab0]
    b_specs = [pl.BlockSpec((1, LANES), functools.partial(lambda p, s: (0, s + p), s=s))
               for s in slab0]
    return pl.pallas_call(
        _attn_weights_kernel,
        grid=(N_PAIRS,),
        in_specs=w_specs + b_specs,
        out_specs=[pl.BlockSpec((None, D, ATTN_COLS), lambda p: (p, 0, 0)),
                   pl.BlockSpec((None, SUBLANES, ATTN_COLS), lambda p: (p, 0, 0))],
        out_shape=[jax.ShapeDtypeStruct((N_PAIRS, D, ATTN_COLS), BF16),
                   jax.ShapeDtypeStruct((N_PAIRS, SUBLANES, ATTN_COLS), F32)],
        compiler_params=_cparams(1),
        name="attn_weights",
    )(*([w_in] * N_ATTN_SLABS), *([b_in] * N_ATTN_SLABS))


def _attention_kernel(x_ref, w_s, b_s, cos_ref, sin_ref, bias_ref, o_ref,
                      qkv0_s, qkv_s, gate_s, acc_s, m_s, l_s, *, S, row_chunk):
    A = S // CLASSES

    def qkv_ref(g, i):
        return qkv0_s.at[i] if g == 0 else qkv_s.at[3 * (g - 1) + i]

    n_tiles = S // BAND
    lane = lax.broadcasted_iota(jnp.int32, (BAND, LANES), 1)
    head0_q = ((lane // HALF) % 2) == 0
    head0_v = lane < HEAD_DIM
    head0_kv = lax.broadcasted_iota(jnp.int32, (2 * BAND, LANES), 1) < HEAD_DIM
    ones_h0 = head0_kv.astype(BF16)
    ones_h1 = 1 - ones_h0
    q_scale = (HEAD_DIM ** -0.5) * LOG2_E

    def proj_chunk(rc, slab_lo, slab_hi):
        rows = pl.ds(pl.multiple_of(rc * row_chunk, row_chunk), row_chunk)
        cols = slice(slab_lo * LANES, slab_hi * LANES)
        res = jnp.dot(x_ref[rows, :], w_s[:, cols], preferred_element_type=F32) + b_s[0:1, cols]
        cos = cos_ref[rows, :]
        sin = sin_ref[rows, :]
        for i, (g, kind) in enumerate(ATTN_SLAB_LAYOUT[slab_lo:slab_hi]):
            val = res[:, i * LANES:(i + 1) * LANES]
            if kind == GATE_SLAB:
                gate_s[rows, :] = (val * jax.nn.sigmoid(val)).astype(BF16)
                continue
            if kind in (0, 1):
                val = val * cos + pltpu.roll(val, LANES // 2, axis=1) * sin
            if kind == 0:
                val = val * q_scale
            ref = qkv_ref(g, kind)
            ref[rows, :] = val.astype(ref.dtype)

    def tiles(g, is_first_group):
        count = n_tiles
        d = DILATIONS[g]
        R = CLASSES // d
        L = BAND // R
        nb = S // (d * BAND)

        def make_run(e):
            rho, n = divmod(e, nb)

            def run(j, nn):
                start = (rho + d * j) * A + L * nn
                return slice(start, start + L)
            return n, run

        def gather(ref, run, nn):
            return jnp.concatenate([ref[run(j, nn), :] for j in range(R)], axis=0)

        def scores(n, run):
            q_t = gather(qkv_ref(g, 0), run, n).astype(BF16)
            zero = jnp.zeros_like(q_t)
            qm = jnp.concatenate([jnp.where(head0_q, q_t, zero),
                                  jnp.where(head0_q, zero, q_t)], axis=0)
            n_prev = max(n - 1, 0)
            k_ref = qkv_ref(g, 1)
            kwin = jnp.concatenate([gather(k_ref, run, n_prev), gather(k_ref, run, n)],
                                   axis=0).astype(BF16)
            s = lax.dot_general(qm, kwin, (((1,), (1,)), ((), ())),
                                preferred_element_type=F32)
            return s + bias_ref[g, int(n == 0)]

        def softmax(s):
            m_blk = jnp.max(s, axis=1, keepdims=True)
            p = jnp.exp2(s - m_blk)
            return p.astype(BF16), m_blk

        def finish(n, run, p, m_blk):
            n_prev = max(n - 1, 0)
            v_ref = qkv_ref(g, 2)
            vwin = jnp.concatenate([gather(v_ref, run, n_prev), gather(v_ref, run, n)],
                                   axis=0).astype(BF16)
            zero_v = jnp.zeros_like(vwin)
            v_bd = jnp.concatenate([
                jnp.concatenate([jnp.where(head0_kv, vwin, zero_v), ones_h0], axis=1),
                jnp.concatenate([jnp.where(head0_kv, zero_v, vwin), ones_h1], axis=1)], axis=0)
            p_cat = jnp.concatenate([p[:BAND], p[BAND:]], axis=1)
            pv = jnp.dot(p_cat, v_bd, preferred_element_type=F32)
            o_t = pv[:, :LANES]
            m_t = jnp.where(head0_v, m_blk[:BAND], m_blk[BAND:])
            l_t = pv[:, LANES:]
            if not is_first_group:
                m_old = gather(m_s, run, n)
                m_new = jnp.maximum(m_old, m_t)
                w_old = jnp.exp2(m_old - m_new)
                w_new = jnp.exp2(m_t - m_new)
                m_t = m_new
                l_t = gather(l_s, run, n) * w_old + l_t * w_new
                o_t = gather(acc_s, run, n) * w_old + o_t * w_new
            for j in range(R):
                m_s[run(j, n), :] = m_t[j * L:(j + 1) * L]
                l_s[run(j, n), :] = l_t[j * L:(j + 1) * L]
                acc_s[run(j, n), :] = o_t[j * L:(j + 1) * L]

        ctx = [make_run(e) for e in range(count)]
        sc, sm = {}, {}
        for t in range(count + 2):
            if t < count:
                sc[t] = scores(*ctx[t])
            if 0 <= t - 1 < count:
                sm[t - 1] = softmax(sc.pop(t - 1))
            if 0 <= t - 2 < count:
                finish(*ctx[t - 2], *sm.pop(t - 2))

    def proj_all(rc, carry):
        proj_chunk(rc, 0, N_ATTN_SLABS)
        return carry

    lax.fori_loop(0, S // row_chunk, proj_all, 0)
    for g in range(N_GROUPS):
        tiles(g, g == 0)

    for c in range(CLASSES):
        rows = slice(c * A, (c + 1) * A)
        res = (acc_s[rows, :] / l_s[rows, :]) * gate_s[rows, :].astype(F32)
        o_ref[pl.ds(c, A, stride=CLASSES), :] = res


def _attention(xp, w_attn, b_attn):
    B, S, D = xp.shape
    kern = functools.partial(_attention_kernel, S=S, row_chunk=1024)
    cos_t, sin_t = _rope_tables(S)
    const2 = pl.BlockSpec((S, LANES), lambda b, p: (0, 0), pipeline_mode=ONCE)
    scratch = [
        pltpu.VMEM((3, S, LANES), F32),
        pltpu.VMEM((3 * (N_GROUPS - 1), S, LANES), BF16),
        pltpu.VMEM((S, LANES), BF16),
        pltpu.VMEM((S, LANES), F32),
        pltpu.VMEM((S, LANES), F32),
        pltpu.VMEM((S, LANES), F32),
    ]
    return pl.pallas_call(
        kern,
        grid=(B, N_PAIRS),
        in_specs=[
            pl.BlockSpec((None, S, D), lambda b, p: (b, 0, 0), pipeline_mode=ONCE),
            pl.BlockSpec((None, D, ATTN_COLS), lambda b, p: (p, 0, 0)),
            pl.BlockSpec((None, SUBLANES, ATTN_COLS), lambda b, p: (p, 0, 0)),
            const2, const2,
            pl.BlockSpec((N_GROUPS, 2, 2 * BAND, 2 * BAND), lambda b, p: (0, 0, 0, 0),
                         pipeline_mode=ONCE),
        ],
        out_specs=pl.BlockSpec((None, S, LANES), lambda b, p: (b, 0, p)),
        out_shape=jax.ShapeDtypeStruct((B, S, ATTN_W), F32),
        scratch_shapes=scratch,
        compiler_params=_cparams(2),
        name="attention",
    )(xp, w_attn, b_attn, cos_t, sin_t, _band_bias())


X_AHEAD = 2
X_SLOTS = X_AHEAD + 1
SCAN_VREGS = 2


def _rglru_kernel(x_hbm, wu_ref, wl_ref, bu_ref, bl_ref, cw_ref, cb_ref, wr_ref, wi_ref,
                  br_ref, bi_ref, lam_ref, o_ref,
                  w_s, wg_s, u_pad, gate_s, a_p, h_p, x_buf, x_sem, *, S, row_chunk, pitch):
    C = LRU_SLAB
    n_slabs = C // LANES
    n_chunks = S // row_chunk
    n_seg = SCAN_VREGS * SUBLANES
    seg_len = S // n_seg
    seg_per_chunk = row_chunk // seg_len
    u_pad[:, 0:SUBLANES, :] = jnp.zeros((n_slabs, SUBLANES, LANES), F32)

    w_s[:, :C] = wu_ref[...].astype(BF16)
    w_s[:, C:] = wl_ref[...].astype(BF16)
    wg_s[...] = jnp.zeros(wg_s.shape, BF16)
    for blk in range(C // LRU_BLOCK):
        rows = slice(blk * LRU_BLOCK, (blk + 1) * LRU_BLOCK)
        wg_s[rows, blk * LRU_BLOCK:(blk + 1) * LRU_BLOCK] = wr_ref[blk].astype(BF16)
        wg_s[rows, C + blk * LRU_BLOCK:C + (blk + 1) * LRU_BLOCK] = wi_ref[blk].astype(BF16)

    step = pl.program_id(0) * pl.num_programs(1) + pl.program_id(1)
    total_chunks = pl.num_programs(0) * pl.num_programs(1) * n_chunks
    chunks_per_batch = pl.num_programs(1) * n_chunks

    def x_copy(q):
        b = lax.div(q, chunks_per_batch)
        rows = pl.ds(pl.multiple_of(lax.rem(q, n_chunks) * row_chunk, row_chunk), row_chunk)
        slot = lax.rem(q, X_SLOTS)
        return pltpu.make_async_copy(x_hbm.at[b, rows, :], x_buf.at[slot], x_sem.at[slot])

    @pl.when(step == 0)
    def _():
        for q in range(X_AHEAD):
            x_copy(q).start()

    def x_arrive(rc):
        q = step * n_chunks + rc
        x_copy(q).wait()

        @pl.when(q + X_AHEAD < total_chunks)
        def _():
            x_copy(q + X_AHEAD).start()

    def proj_chunk(rc, carry):
        r0 = pl.multiple_of(rc * row_chunk, row_chunk)
        q = step * n_chunks + rc
        xb = x_buf[lax.rem(q, X_SLOTS)].astype(BF16)
        res = jnp.dot(xb, w_s[...], preferred_element_type=F32)
        u = res[:, :C] + bu_ref[...]
        for s in range(n_slabs):
            u_pad[s, pl.ds(r0 + SUBLANES, row_chunk), :] = u[:, s * LANES:(s + 1) * LANES]
        gl = res[:, C:] + bl_ref[...]
        gate_s[pl.ds(r0, row_chunk), :] = gl * jax.nn.sigmoid(gl)
        return carry

    neg_lam = -lam_ref[...]
    softplus = jnp.maximum(neg_lam, 0.0) + jnp.log1p(jnp.exp(-jnp.abs(neg_lam)))
    decay_rate = LRU_C * softplus

    def gate_chunk(rc, carry):
        r0 = pl.multiple_of(rc * row_chunk, row_chunk)
        parts = []
        for s in range(n_slabs):
            lanes = slice(s * LANES, (s + 1) * LANES)
            acc = cb_ref[:, lanes]
            for j in range(CONV_W):
                off = SUBLANES - (CONV_W - 1) + j
                acc = acc + u_pad[s, pl.ds(r0 + off, row_chunk), :] * cw_ref[j:j + 1, lanes]
            parts.append(acc)
        uc = jnp.concatenate(parts, axis=1)
        gates = jnp.dot(uc.astype(BF16), wg_s[...], preferred_element_type=F32)
        r = jax.nn.sigmoid(gates[:, :C] + br_ref[...])
        i = jax.nn.sigmoid(gates[:, C:] + bi_ref[...])
        neg_log_a = r * decay_rate
        a = jnp.exp(-neg_log_a)
        one_minus_a2 = jnp.maximum(jnp.tanh(neg_log_a) * (a * a + 1.0), 0.0)
        mult = jnp.exp2(jnp.log(one_minus_a2) * (0.5 * LOG2_E))
        u = mult * (i * uc)
        for k in range(seg_per_chunk):
            start = (rc * seg_per_chunk + k) * pitch
            rows = slice(k * seg_len, (k + 1) * seg_len)
            for s in range(n_slabs):
                lanes = slice(s * LANES, (s + 1) * LANES)
                a_p[s, pl.ds(start, seg_len), :] = a[rows, lanes]
                h_p[s, pl.ds(start, seg_len), :] = u[rows, lanes]
        return carry

    def gate_then_proj(rc, carry):
        x_arrive(rc + 1)
        gate_chunk(rc, carry)
        return proj_chunk(rc + 1, carry)

    x_arrive(0)
    proj_chunk(0, 0)
    lax.fori_loop(0, n_chunks - 1, gate_then_proj, 0)
    gate_chunk(n_chunks - 1, 0)

    chains = [(s, v) for s in range(n_slabs) for v in range(SCAN_VREGS)]

    def scan_step(j, carry):
        out = []
        for c, (s, v) in enumerate(chains):
            rows = pl.ds(v * SUBLANES * pitch + j, SUBLANES, stride=pitch)
            h_prev, p_prev = carry[2 * c], carry[2 * c + 1]
            a = a_p[s, rows, :]
            h = a * h_prev + h_p[s, rows, :]
            p = a * p_prev
            h_p[s, rows, :] = h
            a_p[s, rows, :] = p
            out += [h, p]
        return tuple(out)

    init = (jnp.zeros((SUBLANES, LANES), F32), jnp.ones((SUBLANES, LANES), F32)) * len(chains)
    ends = lax.fori_loop(0, seg_len, scan_step, init, unroll=8)

    for s in range(n_slabs):
        lanes = slice(s * LANES, (s + 1) * LANES)
        h_in = jnp.zeros((1, LANES), F32)
        for seg in range(n_seg):
            c, sub = chains.index((s, seg // SUBLANES)), seg % SUBLANES
            h_end, p_end = ends[2 * c], ends[2 * c + 1]
            rows = slice(seg * pitch, seg * pitch + seg_len)
            h = h_p[s, rows, :] + a_p[s, rows, :] * h_in
            nat = slice(seg * seg_len, (seg + 1) * seg_len)
            o_ref[nat, lanes] = (h * gate_s[nat, lanes]).astype(BF16)
            h_in = h_end[sub:sub + 1] + p_end[sub:sub + 1] * h_in


def _rglru(x, w_in, b_in, conv_w, conv_b, lru_wr, lru_br, lru_wi, lru_bi, lam):
    B, S, D = x.shape
    C = LRU_SLAB
    n_slabs = LRU_W // C
    per = C // LRU_BLOCK
    n_seg = SCAN_VREGS * SUBLANES
    pitch = S // n_seg + SUBLANES // 2
    row_chunk = 512
    kern = functools.partial(_rglru_kernel, S=S, row_chunk=row_chunk, pitch=pitch)
    col = lambda c0: (lambda b, c: (0, c0 // C + c))
    row1 = lambda: pl.BlockSpec((1, C), lambda b, c: (0, c))
    blocks = lambda: pl.BlockSpec((per, LRU_BLOCK, LRU_BLOCK), lambda b, c: (c, 0, 0))
    return pl.pallas_call(
        kern,
        grid=(B, n_slabs),
        in_specs=[
            pl.BlockSpec(memory_space=pl.ANY),
            pl.BlockSpec((D, C), col(COL_U)), pl.BlockSpec((D, C), col(COL_GATE_L)),
            pl.BlockSpec((1, C), col(COL_U)), pl.BlockSpec((1, C), col(COL_GATE_L)),
            pl.BlockSpec((CONV_W, C), lambda b, c: (0, c)), row1(),
            blocks(), blocks(), row1(), row1(), row1(),
        ],
        out_specs=pl.BlockSpec((None, S, C), lambda b, c: (b, 0, c)),
        out_shape=jax.ShapeDtypeStruct((B, S, LRU_W), BF16),
        scratch_shapes=[
            pltpu.VMEM((D, 2 * C), BF16),
            pltpu.VMEM((C, 2 * C), BF16),
            pltpu.VMEM((C // LANES, S + SUBLANES, LANES), F32),
            pltpu.VMEM((S, C), F32),
            pltpu.VMEM((C // LANES, n_seg * pitch, LANES), F32),
            pltpu.VMEM((C // LANES, n_seg * pitch, LANES), F32),
            pltpu.VMEM((X_SLOTS, row_chunk, D), F32),
            pltpu.SemaphoreType.DMA((X_SLOTS,)),
        ],
        compiler_params=_cparams(2),
        name="rglru",
    )(x, w_in, w_in, b_in, b_in, conv_w, conv_b, lru_wr, lru_wi, lru_br, lru_bi, lam)


def _merge_out_kernel(x_ref, attn_ref, hg_ref, wga_ref, wgb_ref, bga_ref, bgb_ref,
                      wa_ref, wl_ref, wo_ref, bo_ref, lng_ref, lnb_ref, o_ref,
                      wg_s, wa_s, wl_s, wo_s):
    D = x_ref.shape[1]

    @pl.when(pl.program_id(0) == 0)
    def _():
        rows = 256
        for r0 in range(0, D, rows):
            sl = slice(r0, r0 + rows)
            wg_s[sl, :D] = wga_ref[sl, :].astype(BF16)
            wg_s[sl, D:] = wgb_ref[sl, :].astype(BF16)
            wl_s[sl, :] = wl_ref[sl, :].astype(BF16)
            wo_s[sl, :] = wo_ref[sl, :].astype(BF16)
        for r0 in range(0, ATTN_W, rows):
            sl = slice(r0, r0 + rows)
            wa_s[sl, :] = wa_ref[sl, :].astype(BF16)

    x = x_ref[...]
    xb = x.astype(BF16)
    gate_a = jax.nn.sigmoid(
        jnp.dot(xb, wg_s[:, :D], preferred_element_type=F32) + bga_ref[...])
    gate_b = jax.nn.sigmoid(
        jnp.dot(xb, wg_s[:, D:], preferred_element_type=F32) + bgb_ref[...])
    y_a = jnp.dot(attn_ref[...].astype(BF16), wa_s[...], preferred_element_type=F32)
    y_b = jnp.dot(hg_ref[...], wl_s[...], preferred_element_type=F32)
    merged = gate_a * y_a + gate_b * y_b
    out = jnp.dot(merged.astype(BF16), wo_s[...], preferred_element_type=F32) + bo_ref[...]
    y = ALPHA * x + out
    mu = jnp.mean(y, axis=1, keepdims=True)
    yc = y - mu
    var = jnp.mean(yc * yc, axis=1, keepdims=True)
    o_ref[...] = yc * lax.rsqrt(var + LN_EPS) * lng_ref[...] + lnb_ref[...]


def _merge_out(x2, attn2, hg2, w_in, b_in, w_a, w_l, w_o, b_o, ln_g, ln_b, tm=512):
    T, D = x2.shape
    once = lambda shape, idx: pl.BlockSpec(shape, lambda i: idx, pipeline_mode=ONCE)
    merge_blk = COL_MERGE // D
    return pl.pallas_call(
        _merge_out_kernel,
        grid=(T // tm,),
        in_specs=[
            pl.BlockSpec((tm, D), lambda i: (i, 0)),
            pl.BlockSpec((tm, ATTN_W), lambda i: (i, 0)),
            pl.BlockSpec((tm, LRU_W), lambda i: (i, 0)),
            once((D, D), (0, merge_blk)), once((D, D), (0, merge_blk + 1)),
            once((1, D), (0, merge_blk)), once((1, D), (0, merge_blk + 1)),
            once((ATTN_W, D), (0, 0)), once((LRU_W, D), (0, 0)), once((D, D), (0, 0)),
            once((1, D), (0, 0)), once((1, D), (0, 0)), once((1, D), (0, 0)),
        ],
        out_specs=pl.BlockSpec((tm, D), lambda i: (i, 0)),
        out_shape=jax.ShapeDtypeStruct((T, D), F32),
        scratch_shapes=[
            pltpu.VMEM((D, 2 * D), BF16), pltpu.VMEM((ATTN_W, D), BF16),
            pltpu.VMEM((LRU_W, D), BF16), pltpu.VMEM((D, D), BF16),
        ],
        compiler_params=_cparams(1),
        name="merge_out",
    )(x2, attn2, hg2, w_in, w_in, b_in, b_in, w_a, w_l, w_o, b_o, ln_g, ln_b)


def kernel(x, w_in, b_in, conv_w, conv_b, lru_wr, lru_br, lru_wi, lru_bi, lru_lambda,
           w_attn_proj, w_lru_proj, w_out, b_out, ln_gain, ln_bias):
    B, S, D = x.shape
    assert w_in.shape[0] == DEPTH and S % (CLASSES * BAND) == 0 and D == D_MODEL
    assert COL_MERGE % D == 0 and w_in.shape[2] == COL_MERGE + 2 * D
    row = lambda v: v.reshape(1, -1)
    for l in range(DEPTH):
        wl, bl = w_in[l], row(b_in[l])
        attn = _attention(_permute_cast(x), *_attn_weights(wl, bl))
        hg = _rglru(x, wl, bl, conv_w[l], row(conv_b[l]), lru_wr[l], row(lru_br[l]),
                    lru_wi[l], row(lru_bi[l]), row(lru_lambda[l]))
        out = _merge_out(
            x.reshape(B * S, D), attn.reshape(B * S, ATTN_W), hg.reshape(B * S, LRU_W),
            wl, bl, w_attn_proj[l], w_lru_proj[l], w_out[l],
            row(b_out[l]), row(ln_gain[l]), row(ln_bias[l]))
        x = out.reshape(B, S, D)
    return x
```

```python
import functools
import math

import jax
import jax.numpy as jnp
import numpy as np
from jax import lax
from jax.experimental import pallas as pl
from jax.experimental.pallas import tpu as pltpu

LANES = 128
SUBLANES = 8
BF16_ROWS = 16
VMEM_LIMIT_BYTES = 56 * 1024 * 1024

D_MODEL = 1024
HEAD_DIM = 64
HALF = HEAD_DIM // 2
HEADS_PER_GROUP = 8
DILATIONS = (1, 4, 16)
N_GROUPS = len(DILATIONS)
BAND = 128
CLASSES = 16
QKV_W = N_GROUPS * HEADS_PER_GROUP * HEAD_DIM
ATTN_W = HEADS_PER_GROUP * HEAD_DIM
N_PAIRS = ATTN_W // LANES
LRU_W = 1024
LRU_BLOCK = 64
LRU_SLAB = 256
CONV_W = 4
LRU_C = 8.0
ROPE_THETA = 10000.0
LOG2_E = math.log2(math.e)
NEG_INF = -1e30
LN_EPS = 1e-5
DEPTH = 1
ALPHA = (2.0 * DEPTH) ** 0.25
GATE_SLAB = 3
ATTN_SLAB_LAYOUT = ([(0, 0), (0, 1), (0, 2), (None, GATE_SLAB)]
                    + [(g, kind) for g in range(1, N_GROUPS) for kind in range(3)])
N_ATTN_SLABS = len(ATTN_SLAB_LAYOUT)
ATTN_COLS = N_ATTN_SLABS * LANES
COL_K = QKV_W
COL_V = 2 * QKV_W
COL_GATE_A = 3 * QKV_W
COL_U = COL_GATE_A + ATTN_W
COL_GATE_L = COL_U + LRU_W
COL_MERGE = COL_GATE_L + LRU_W

BF16 = jnp.bfloat16
F32 = jnp.float32
ONCE = pl.Buffered(1)


def _cparams(n_axes, flags=None):
    return pltpu.CompilerParams(dimension_semantics=("arbitrary",) * n_axes,
                                vmem_limit_bytes=VMEM_LIMIT_BYTES, flags=flags)


PERMUTE_CLASSES_PER_STEP = 4


def _permute_cast_kernel(x_hbm, xp_ref, buf, sem, *, rows_per_class):
    per = PERMUTE_CLASSES_PER_STEP
    steps_per_batch = CLASSES // per
    step = pl.program_id(0) * steps_per_batch + pl.program_id(1)
    n_steps = pl.num_programs(0) * steps_per_batch

    def copies(st, slot):
        b = lax.div(st, steps_per_batch)
        q = lax.rem(st, steps_per_batch)
        return [pltpu.make_async_copy(x_hbm.at[b, :, q * per + i, :], buf.at[slot, i],
                                      sem.at[slot, i]) for i in range(per)]

    @pl.when(step == 0)
    def _():
        for cp in copies(step, 0):
            cp.start()

    slot = lax.rem(step, 2)

    @pl.when(step + 1 < n_steps)
    def _():
        for cp in copies(step + 1, 1 - slot):
            cp.start()

    for i, cp in enumerate(copies(step, slot)):
        cp.wait()
        xp_ref[i * rows_per_class:(i + 1) * rows_per_class, :] = buf[slot, i].astype(BF16)


def _permute_cast(x):
    B, S, D = x.shape
    A = S // CLASSES
    per = PERMUTE_CLASSES_PER_STEP
    kern = functools.partial(_permute_cast_kernel, rows_per_class=A)
    return pl.pallas_call(
        kern,
        grid=(B, CLASSES // per),
        in_specs=[pl.BlockSpec(memory_space=pl.ANY)],
        out_specs=pl.BlockSpec((None, per * A, D), lambda b, q: (b, q, 0)),
        out_shape=jax.ShapeDtypeStruct((B, S, D), BF16),
        scratch_shapes=[pltpu.VMEM((2, per, A, D), F32), pltpu.SemaphoreType.DMA((2, per))],
        compiler_params=_cparams(2),
        name="permute_cast",
    )(x.reshape(B, A, CLASSES, D))


def _band_bias():
    out = np.zeros((N_GROUPS, 2, 2 * BAND, 2 * BAND), np.float32)
    i = np.arange(BAND)
    for g, d in enumerate(DILATIONS):
        R = CLASSES // d
        L = BAND // R
        m = R * (i % L) + i // L
        m_k = np.concatenate([m - BAND, m])
        dist = m[:, None] - m_k[None, :]
        band = (dist >= 0) & (dist <= BAND)
        for first in range(2):
            valid = band & ((np.arange(2 * BAND) >= BAND)[None, :] | (first == 0))
            bias = np.where(valid, 0.0, NEG_INF).astype(np.float32)
            out[g, first] = np.concatenate([bias, bias], axis=0)
    return out


def _rope_tables(S):
    A = S // CLASSES
    r = np.arange(S)
    pos = (CLASSES * (r % A) + r // A).astype(np.float64)
    inv_freq = ROPE_THETA ** (-np.arange(HALF, dtype=np.float64) / HALF)
    ang = pos[:, None] * inv_freq[None, :]
    cos_t = np.tile(np.cos(ang), (1, LANES // HALF))
    sin = np.sin(ang)
    sin_t = np.concatenate([-sin, -sin, sin, sin], axis=1)
    return cos_t.astype(np.float32), sin_t.astype(np.float32)


def _pair_interleave(w):
    lane = lax.broadcasted_iota(jnp.int32, w.shape, 1)
    from_right = pltpu.roll(w, LANES - HALF, axis=1)
    from_left = pltpu.roll(w, HALF, axis=1)
    quarter = lane // HALF
    return jnp.where(quarter == 1, from_right, jnp.where(quarter == 2, from_left, w))


def _attn_weights_kernel(*refs):
    w_refs = refs[:N_ATTN_SLABS]
    b_refs = refs[N_ATTN_SLABS:2 * N_ATTN_SLABS]
    w_out, b_out = refs[2 * N_ATTN_SLABS:]
    D = w_out.shape[0]
    w_rows = 256
    for i, (g, kind) in enumerate(ATTN_SLAB_LAYOUT):
        is_qk = kind in (0, 1)
        cols = slice(i * LANES, (i + 1) * LANES)
        for r0 in range(0, D, w_rows):
            w = w_refs[i][r0:r0 + w_rows, :]
            w_out[r0:r0 + w_rows, cols] = (_pair_interleave(w) if is_qk else w).astype(BF16)
        b = jnp.broadcast_to(b_refs[i][...], (SUBLANES, LANES))
        b_out[:, cols] = _pair_interleave(b) if is_qk else b


def _attn_weights(w_in, b_in):
    D = w_in.shape[0]
    kind_col = (0, COL_K, COL_V)
    slab0 = [COL_GATE_A // LANES if kind == GATE_SLAB else (kind_col[kind] + g * ATTN_W) // LANES
             for g, kind in ATTN_SLAB_LAYOUT]
    w_specs = [pl.BlockSpec((D, LANES), functools.partial(lambda p, s: (0, s + p), s=s))
               for s in slab0]
    b_specs = [pl.BlockSpec((1, LANES), functools.partial(lambda p, s: (0, s + p), s=s))
               for s in slab0]
    return pl.pallas_call(
        _attn_weights_kernel,
        grid=(N_PAIRS,),
        in_specs=w_specs + b_specs,
        out_specs=[pl.BlockSpec((None, D, ATTN_COLS), lambda p: (p, 0, 0)),
                   pl.BlockSpec((None, SUBLANES, ATTN_COLS), lambda p: (p, 0, 0))],
        out_shape=[jax.ShapeDtypeStruct((N_PAIRS, D, ATTN_COLS), BF16),
                   jax.ShapeDtypeStruct((N_PAIRS, SUBLANES, ATTN_COLS), F32)],
        compiler_params=_cparams(1),
        name="attn_weights",
    )(*([w_in] * N_ATTN_SLABS), *([b_in] * N_ATTN_SLABS))


def _attention_kernel(x_ref, w_s, b_s, cos_ref, sin_ref, bias_ref, o_ref,
                      qkv0_s, qkv_s, gate_s, acc_s, m_s, l_s, *, S, row_chunk):
    A = S // CLASSES

    def qkv_ref(g, i):
        return qkv0_s.at[i] if g == 0 else qkv_s.at[3 * (g - 1) + i]

    n_tiles = S // BAND
    lane = lax.broadcasted_iota(jnp.int32, (BAND, LANES), 1)
    head0_q = ((lane // HALF) % 2) == 0
    head0_v = lane < HEAD_DIM
    head0_kv = lax.broadcasted_iota(jnp.int32, (2 * BAND, LANES), 1) < HEAD_DIM
    ones_h0 = head0_kv.astype(BF16)
    ones_h1 = 1 - ones_h0
    q_scale = (HEAD_DIM ** -0.5) * LOG2_E

    def proj_chunk(rc, slab_lo, slab_hi):
        rows = pl.ds(pl.multiple_of(rc * row_chunk, row_chunk), row_chunk)
        cols = slice(slab_lo * LANES, slab_hi * LANES)
        res = jnp.dot(x_ref[rows, :], w_s[:, cols], preferred_element_type=F32) + b_s[0:1, cols]
        cos = cos_ref[rows, :]
        sin = sin_ref[rows, :]
        for i, (g, kind) in enumerate(ATTN_SLAB_LAYOUT[slab_lo:slab_hi]):
            val = res[:, i * LANES:(i + 1) * LANES]
            if kind == GATE_SLAB:
                gate_s[rows, :] = (val * jax.nn.sigmoid(val)).astype(BF16)
                continue
            if kind in (0, 1):
                val = val * cos + pltpu.roll(val, LANES // 2, axis=1) * sin
            if kind == 0:
                val = val * q_scale
            ref = qkv_ref(g, kind)
            ref[rows, :] = val.astype(ref.dtype)

    def tiles(g, is_first_group):
        d = DILATIONS[g]
        R = CLASSES // d
        L = BAND // R
        nb = S // (d * BAND)

        def make_run(e):
            rho, n = divmod(e, nb)

            def run(j, nn):
                start = (rho + d * j) * A + L * nn
                return slice(start, start + L)
            return n, run

        def gather(ref, run, nn):
            return jnp.concatenate([ref[run(j, nn), :] for j in range(R)], axis=0)

        def scores(n, run):
            q_t = gather(qkv_ref(g, 0), run, n).astype(BF16)
            zero = jnp.zeros_like(q_t)
            qm = jnp.concatenate([jnp.where(head0_q, q_t, zero),
                                  jnp.where(head0_q, zero, q_t)], axis=0)
            n_prev = max(n - 1, 0)
            k_ref = qkv_ref(g, 1)
            kwin = jnp.concatenate([gather(k_ref, run, n_prev), gather(k_ref, run, n)],
                                   axis=0).astype(BF16)
            s = lax.dot_general(qm, kwin, (((1,), (1,)), ((), ())),
                                preferred_element_type=F32)
            return s + bias_ref[g, int(n == 0)]

        def softmax(s):
            m_blk = jnp.max(s, axis=1, keepdims=True)
            p = jnp.exp2(s - m_blk)
            return p.astype(BF16), m_blk

        def finish(n, run, p, m_blk):
            n_prev = max(n - 1, 0)
            v_ref = qkv_ref(g, 2)
            vwin = jnp.concatenate([gather(v_ref, run, n_prev), gather(v_ref, run, n)],
                                   axis=0).astype(BF16)
            zero_v = jnp.zeros_like(vwin)
            v_bd = jnp.concatenate([
                jnp.concatenate([jnp.where(head0_kv, vwin, zero_v), ones_h0], axis=1),
                jnp.concatenate([jnp.where(head0_kv, zero_v, vwin), ones_h1], axis=1)], axis=0)
            p_cat = jnp.concatenate([p[:BAND], p[BAND:]], axis=1)
            pv = jnp.dot(p_cat, v_bd, preferred_element_type=F32)
            o_t = pv[:, :LANES]
            m_t = jnp.where(head0_v, m_blk[:BAND], m_blk[BAND:])
            l_t = pv[:, LANES:]
            if not is_first_group:
                m_old = gather(m_s, run, n)
                m_new = jnp.maximum(m_old, m_t)
                w_old = jnp.exp2(m_old - m_new)
                w_new = jnp.exp2(m_t - m_new)
                m_t = m_new
                l_t = gather(l_s, run, n) * w_old + l_t * w_new
                o_t = gather(acc_s, run, n) * w_old + o_t * w_new
            for j in range(R):
                m_s[run(j, n), :] = m_t[j * L:(j + 1) * L]
                l_s[run(j, n), :] = l_t[j * L:(j + 1) * L]
                acc_s[run(j, n), :] = o_t[j * L:(j + 1) * L]

        for e in range(n_tiles):
            n, run = make_run(e)
            finish(n, run, *softmax(scores(n, run)))

    def proj_all(rc, carry):
        proj_chunk(rc, 0, N_ATTN_SLABS)
        return carry

    lax.fori_loop(0, S // row_chunk, proj_all, 0)
    for g in range(N_GROUPS):
        tiles(g, g == 0)

    for c in range(CLASSES):
        rows = slice(c * A, (c + 1) * A)
        res = (acc_s[rows, :] / l_s[rows, :]) * gate_s[rows, :].astype(F32)
        o_ref[pl.ds(c, A, stride=CLASSES), :] = res


def _attention(xp, w_attn, b_attn):
    B, S, D = xp.shape
    kern = functools.partial(_attention_kernel, S=S, row_chunk=1024)
    cos_t, sin_t = _rope_tables(S)
    const2 = pl.BlockSpec((S, LANES), lambda b, p: (0, 0), pipeline_mode=ONCE)
    scratch = [
        pltpu.VMEM((3, S, LANES), F32),
        pltpu.VMEM((3 * (N_GROUPS - 1), S, LANES), BF16),
        pltpu.VMEM((S, LANES), BF16),
        pltpu.VMEM((S, LANES), F32),
        pltpu.VMEM((S, LANES), F32),
        pltpu.VMEM((S, LANES), F32),
    ]
    return pl.pallas_call(
        kern,
        grid=(B, N_PAIRS),
        in_specs=[
            pl.BlockSpec((None, S, D), lambda b, p: (b, 0, 0), pipeline_mode=ONCE),
            pl.BlockSpec((None, D, ATTN_COLS), lambda b, p: (p, 0, 0)),
            pl.BlockSpec((None, SUBLANES, ATTN_COLS), lambda b, p: (p, 0, 0)),
            const2, const2,
            pl.BlockSpec((N_GROUPS, 2, 2 * BAND, 2 * BAND), lambda b, p: (0, 0, 0, 0),
                         pipeline_mode=ONCE),
        ],
        out_specs=pl.BlockSpec((None, S, LANES), lambda b, p: (b, 0, p)),
        out_shape=jax.ShapeDtypeStruct((B, S, ATTN_W), F32),
        scratch_shapes=scratch,
        compiler_params=_cparams(2),
        name="attention",
    )(xp, w_attn, b_attn, cos_t, sin_t, _band_bias())


X_AHEAD = 2
X_SLOTS = X_AHEAD + 1
SCAN_VREGS = 2


def _rglru_kernel(x_hbm, wu_ref, wl_ref, bu_ref, bl_ref, cw_ref, cb_ref, wr_ref, wi_ref,
                  br_ref, bi_ref, lam_ref, o_ref,
                  w_s, wg_s, u_pad, gate_s, a_p, h_p, x_buf, x_sem, *, S, row_chunk, pitch):
    C = LRU_SLAB
    n_slabs = C // LANES
    n_chunks = S // row_chunk
    n_seg = SCAN_VREGS * SUBLANES
    seg_len = S // n_seg
    seg_per_chunk = row_chunk // seg_len
    u_pad[:, 0:SUBLANES, :] = jnp.zeros((n_slabs, SUBLANES, LANES), F32)

    w_s[:, :C] = wu_ref[...].astype(BF16)
    w_s[:, C:] = wl_ref[...].astype(BF16)
    wg_s[...] = jnp.zeros(wg_s.shape, BF16)
    for blk in range(C // LRU_BLOCK):
        rows = slice(blk * LRU_BLOCK, (blk + 1) * LRU_BLOCK)
        wg_s[rows, blk * LRU_BLOCK:(blk + 1) * LRU_BLOCK] = wr_ref[blk].astype(BF16)
        wg_s[rows, C + blk * LRU_BLOCK:C + (blk + 1) * LRU_BLOCK] = wi_ref[blk].astype(BF16)

    step = pl.program_id(0) * pl.num_programs(1) + pl.program_id(1)
    total_chunks = pl.num_programs(0) * pl.num_programs(1) * n_chunks
    chunks_per_batch = pl.num_programs(1) * n_chunks

    def x_copy(q):
        b = lax.div(q, chunks_per_batch)
        rows = pl.ds(pl.multiple_of(lax.rem(q, n_chunks) * row_chunk, row_chunk), row_chunk)
        slot = lax.rem(q, X_SLOTS)
        return pltpu.make_async_copy(x_hbm.at[b, rows, :], x_buf.at[slot], x_sem.at[slot])

    @pl.when(step == 0)
    def _():
        for q in range(X_AHEAD):
            x_copy(q).start()

    def x_arrive(rc):
        q = step * n_chunks + rc
        x_copy(q).wait()

        @pl.when(q + X_AHEAD < total_chunks)
        def _():
            x_copy(q + X_AHEAD).start()

    def proj_chunk(rc, carry):
        r0 = pl.multiple_of(rc * row_chunk, row_chunk)
        q = step * n_chunks + rc
        xb = x_buf[lax.rem(q, X_SLOTS)].astype(BF16)
        res = jnp.dot(xb, w_s[...], preferred_element_type=F32)
        u = res[:, :C] + bu_ref[...]
        for s in range(n_slabs):
            u_pad[s, pl.ds(r0 + SUBLANES, row_chunk), :] = u[:, s * LANES:(s + 1) * LANES]
        gl = res[:, C:] + bl_ref[...]
        gate_s[pl.ds(r0, row_chunk), :] = gl * jax.nn.sigmoid(gl)
        return carry

    neg_lam = -lam_ref[...]
    softplus = jnp.maximum(neg_lam, 0.0) + jnp.log1p(jnp.exp(-jnp.abs(neg_lam)))
    decay_rate = LRU_C * softplus

    def gate_chunk(rc, carry):
        r0 = pl.multiple_of(rc * row_chunk, row_chunk)
        parts = []
        for s in range(n_slabs):
            lanes = slice(s * LANES, (s + 1) * LANES)
            acc = cb_ref[:, lanes]
            for j in range(CONV_W):
                off = SUBLANES - (CONV_W - 1) + j
                acc = acc + u_pad[s, pl.ds(r0 + off, row_chunk), :] * cw_ref[j:j + 1, lanes]
            parts.append(acc)
        uc = jnp.concatenate(parts, axis=1)
        gates = jnp.dot(uc.astype(BF16), wg_s[...], preferred_element_type=F32)
        r = jax.nn.sigmoid(gates[:, :C] + br_ref[...])
        i = jax.nn.sigmoid(gates[:, C:] + bi_ref[...])
        neg_log_a = r * decay_rate
        a = jnp.exp(-neg_log_a)
        one_minus_a2 = jnp.maximum(jnp.tanh(neg_log_a) * (a * a + 1.0), 0.0)
        mult = jnp.exp2(jnp.log(one_minus_a2) * (0.5 * LOG2_E))
        u = mult * (i * uc)
        for k in range(seg_per_chunk):
            start = (rc * seg_per_chunk + k) * pitch
            rows = slice(k * seg_len, (k + 1) * seg_len)
            for s in range(n_slabs):
                lanes = slice(s * LANES, (s + 1) * LANES)
                a_p[s, pl.ds(start, seg_len), :] = a[rows, lanes]
                h_p[s, pl.ds(start, seg_len), :] = u[rows, lanes]
        return carry

    def gate_then_proj(rc, carry):
        x_arrive(rc + 1)
        gate_chunk(rc, carry)
        return proj_chunk(rc + 1, carry)

    x_arrive(0)
    proj_chunk(0, 0)
    lax.fori_loop(0, n_chunks - 1, gate_then_proj, 0)
    gate_chunk(n_chunks - 1, 0)

    chains = [(s, v) for s in range(n_slabs) for v in range(SCAN_VREGS)]

    def scan_step(j, carry):
        out = []
        for c, (s, v) in enumerate(chains):
            rows = pl.ds(v * SUBLANES * pitch + j, SUBLANES, stride=pitch)
            h_prev, p_prev = carry[2 * c], carry[2 * c + 1]
            a = a_p[s, rows, :]
            h = a * h_prev + h_p[s, rows, :]
            p = a * p_prev
            h_p[s, rows, :] = h
            a_p[s, rows, :] = p
            out += [h, p]
        return tuple(out)

    init = (jnp.zeros((SUBLANES, LANES), F32), jnp.ones((SUBLANES, LANES), F32)) * len(chains)
    ends = lax.fori_loop(0, seg_len, scan_step, init, unroll=8)

    for s in range(n_slabs):
        lanes = slice(s * LANES, (s + 1) * LANES)
        h_in = jnp.zeros((1, LANES), F32)
        for seg in range(n_seg):
            c, sub = chains.index((s, seg // SUBLANES)), seg % SUBLANES
            h_end, p_end = ends[2 * c], ends[2 * c + 1]
            rows = slice(seg * pitch, seg * pitch + seg_len)
            h = h_p[s, rows, :] + a_p[s, rows, :] * h_in
            nat = slice(seg * seg_len, (seg + 1) * seg_len)
            o_ref[nat, lanes] = (h * gate_s[nat, lanes]).astype(BF16)
            h_in = h_end[sub:sub + 1] + p_end[sub:sub + 1] * h_in


def _rglru(x, w_in, b_in, conv_w, conv_b, lru_wr, lru_br, lru_wi, lru_bi, lam):
    B, S, D = x.shape
    C = LRU_SLAB
    n_slabs = LRU_W // C
    per = C // LRU_BLOCK
    n_seg = SCAN_VREGS * SUBLANES
    pitch = S // n_seg + SUBLANES // 2
    row_chunk = 512
    kern = functools.partial(_rglru_kernel, S=S, row_chunk=row_chunk, pitch=pitch)
    col = lambda c0: (lambda b, c: (0, c0 // C + c))
    row1 = lambda: pl.BlockSpec((1, C), lambda b, c: (0, c))
    blocks = lambda: pl.BlockSpec((per, LRU_BLOCK, LRU_BLOCK), lambda b, c: (c, 0, 0))
    return pl.pallas_call(
        kern,
        grid=(B, n_slabs),
        in_specs=[
            pl.BlockSpec(memory_space=pl.ANY),
            pl.BlockSpec((D, C), col(COL_U)), pl.BlockSpec((D, C), col(COL_GATE_L)),
            pl.BlockSpec((1, C), col(COL_U)), pl.BlockSpec((1, C), col(COL_GATE_L)),
            pl.BlockSpec((CONV_W, C), lambda b, c: (0, c)), row1(),
            blocks(), blocks(), row1(), row1(), row1(),
        ],
        out_specs=pl.BlockSpec((None, S, C), lambda b, c: (b, 0, c)),
        out_shape=jax.ShapeDtypeStruct((B, S, LRU_W), BF16),
        scratch_shapes=[
            pltpu.VMEM((D, 2 * C), BF16),
            pltpu.VMEM((C, 2 * C), BF16),
            pltpu.VMEM((C // LANES, S + SUBLANES, LANES), F32),
            pltpu.VMEM((S, C), F32),
            pltpu.VMEM((C // LANES, n_seg * pitch, LANES), F32),
            pltpu.VMEM((C // LANES, n_seg * pitch, LANES), F32),
            pltpu.VMEM((X_SLOTS, row_chunk, D), F32),
            pltpu.SemaphoreType.DMA((X_SLOTS,)),
        ],
        compiler_params=_cparams(2),
        name="rglru",
    )(x, w_in, w_in, b_in, b_in, conv_w, conv_b, lru_wr, lru_wi, lru_br, lru_bi, lam)


def _merge_out_kernel(x_ref, attn_ref, hg_ref, wga_ref, wgb_ref, bga_ref, bgb_ref,
                      wa_ref, wl_ref, wo_ref, bo_ref, lng_ref, lnb_ref, o_ref,
                      wg_s, wa_s, wl_s, wo_s):
    D = x_ref.shape[1]

    @pl.when(pl.program_id(0) == 0)
    def _():
        rows = 256
        for r0 in range(0, D, rows):
            sl = slice(r0, r0 + rows)
            wg_s[sl, :D] = wga_ref[sl, :].astype(BF16)
            wg_s[sl, D:] = wgb_ref[sl, :].astype(BF16)
            wl_s[sl, :] = wl_ref[sl, :].astype(BF16)
            wo_s[sl, :] = wo_ref[sl, :].astype(BF16)
        for r0 in range(0, ATTN_W, rows):
            sl = slice(r0, r0 + rows)
            wa_s[sl, :] = wa_ref[sl, :].astype(BF16)

    x = x_ref[...]
    xb = x.astype(BF16)
    gate_a = jax.nn.sigmoid(
        jnp.dot(xb, wg_s[:, :D], preferred_element_type=F32) + bga_ref[...])
    gate_b = jax.nn.sigmoid(
        jnp.dot(xb, wg_s[:, D:], preferred_element_type=F32) + bgb_ref[...])
    y_a = jnp.dot(attn_ref[...].astype(BF16), wa_s[...], preferred_element_type=F32)
    y_b = jnp.dot(hg_ref[...], wl_s[...], preferred_element_type=F32)
    merged = gate_a * y_a + gate_b * y_b
    out = jnp.dot(merged.astype(BF16), wo_s[...], preferred_element_type=F32) + bo_ref[...]
    y = ALPHA * x + out
    mu = jnp.mean(y, axis=1, keepdims=True)
    yc = y - mu
    var = jnp.mean(yc * yc, axis=1, keepdims=True)
    o_ref[...] = yc * lax.rsqrt(var + LN_EPS) * lng_ref[...] + lnb_ref[...]


def _merge_out(x2, attn2, hg2, w_in, b_in, w_a, w_l, w_o, b_o, ln_g, ln_b, tm=512):
    T, D = x2.shape
    once = lambda shape, idx: pl.BlockSpec(shape, lambda i: idx, pipeline_mode=ONCE)
    merge_blk = COL_MERGE // D
    return pl.pallas_call(
        _merge_out_kernel,
        grid=(T // tm,),
        in_specs=[
            pl.BlockSpec((tm, D), lambda i: (i, 0)),
            pl.BlockSpec((tm, ATTN_W), lambda i: (i, 0)),
            pl.BlockSpec((tm, LRU_W), lambda i: (i, 0)),
            once((D, D), (0, merge_blk)), once((D, D), (0, merge_blk + 1)),
            once((1, D), (0, merge_blk)), once((1, D), (0, merge_blk + 1)),
            once((ATTN_W, D), (0, 0)), once((LRU_W, D), (0, 0)), once((D, D), (0, 0)),
            once((1, D), (0, 0)), once((1, D), (0, 0)), once((1, D), (0, 0)),
        ],
        out_specs=pl.BlockSpec((tm, D), lambda i: (i, 0)),
        out_shape=jax.ShapeDtypeStruct((T, D), F32),
        scratch_shapes=[
            pltpu.VMEM((D, 2 * D), BF16), pltpu.VMEM((ATTN_W, D), BF16),
            pltpu.VMEM((LRU_W, D), BF16), pltpu.VMEM((D, D), BF16),
        ],
        compiler_params=_cparams(1),
        name="merge_out",
    )(x2, attn2, hg2, w_in, w_in, b_in, b_in, w_a, w_l, w_o, b_o, ln_g, ln_b)


def kernel(x, w_in, b_in, conv_w, conv_b, lru_wr, lru_br, lru_wi, lru_bi, lru_lambda,
           w_attn_proj, w_lru_proj, w_out, b_out, ln_gain, ln_bias):
    B, S, D = x.shape
    assert w_in.shape[0] == DEPTH and S % (CLASSES * BAND) == 0 and D == D_MODEL
    assert COL_MERGE % D == 0 and w_in.shape[2] == COL_MERGE + 2 * D
    row = lambda v: v.reshape(1, -1)
    for l in range(DEPTH):
        wl, bl = w_in[l], row(b_in[l])
        attn = _attention(_permute_cast(x), *_attn_weights(wl, bl))
        hg = _rglru(x, wl, bl, conv_w[l], row(conv_b[l]), lru_wr[l], row(lru_br[l]),
                    lru_wi[l], row(lru_bi[l]), row(lru_lambda[l]))
        out = _merge_out(
            x.reshape(B * S, D), attn.reshape(B * S, ATTN_W), hg.reshape(B * S, LRU_W),
            wl, bl, w_attn_proj[l], w_lru_proj[l], w_out[l],
            row(b_out[l]), row(ln_gain[l]), row(ln_bias[l]))
        x = out.reshape(B, S, D)
    return x
```

```python
import functools
import math

import jax
import jax.numpy as jnp
import numpy as np
from jax import lax
from jax.experimental import pallas as pl
from jax.experimental.pallas import tpu as pltpu

LANES = 128
SUBLANES = 8
BF16_ROWS = 16
VMEM_LIMIT_BYTES = 56 * 1024 * 1024

D_MODEL = 1024
HEAD_DIM = 64
HALF = HEAD_DIM // 2
HEADS_PER_GROUP = 8
DILATIONS = (1, 4, 16)
N_GROUPS = len(DILATIONS)
BAND = 128
CLASSES = 16
QKV_W = N_GROUPS * HEADS_PER_GROUP * HEAD_DIM
ATTN_W = HEADS_PER_GROUP * HEAD_DIM
N_PAIRS = ATTN_W // LANES
LRU_W = 1024
LRU_BLOCK = 64
LRU_SLAB = 256
CONV_W = 4
LRU_C = 8.0
ROPE_THETA = 10000.0
LOG2_E = math.log2(math.e)
NEG_INF = -1e30
LN_EPS = 1e-5
DEPTH = 1
ALPHA = (2.0 * DEPTH) ** 0.25
GATE_SLAB = 3
ATTN_SLAB_LAYOUT = ([(0, 0), (0, 1), (0, 2), (None, GATE_SLAB)]
                    + [(g, kind) for g in range(1, N_GROUPS) for kind in range(3)])
N_ATTN_SLABS = len(ATTN_SLAB_LAYOUT)
ATTN_COLS = N_ATTN_SLABS * LANES
COL_K = QKV_W
COL_V = 2 * QKV_W
COL_GATE_A = 3 * QKV_W
COL_U = COL_GATE_A + ATTN_W
COL_GATE_L = COL_U + LRU_W
COL_MERGE = COL_GATE_L + LRU_W

BF16 = jnp.bfloat16
F32 = jnp.float32
ONCE = pl.Buffered(1)


def _cparams(n_axes, flags=None):
    return pltpu.CompilerParams(dimension_semantics=("arbitrary",) * n_axes,
                                vmem_limit_bytes=VMEM_LIMIT_BYTES, flags=flags)


PERMUTE_CLASSES_PER_STEP = 4


def _permute_cast_kernel(x_hbm, xp_ref, buf, sem, *, rows_per_class):
    per = PERMUTE_CLASSES_PER_STEP
    steps_per_batch = CLASSES // per
    step = pl.program_id(0) * steps_per_batch + pl.program_id(1)
    n_steps = pl.num_programs(0) * steps_per_batch

    def copies(st, slot):
        b = lax.div(st, steps_per_batch)
        q = lax.rem(st, steps_per_batch)
        return [pltpu.make_async_copy(x_hbm.at[b, :, q * per + i, :], buf.at[slot, i],
                                      sem.at[slot, i]) for i in range(per)]

    @pl.when(step == 0)
    def _():
        for cp in copies(step, 0):
            cp.start()

    slot = lax.rem(step, 2)

    @pl.when(step + 1 < n_steps)
    def _():
        for cp in copies(step + 1, 1 - slot):
            cp.start()

    for i, cp in enumerate(copies(step, slot)):
        cp.wait()
        xp_ref[i * rows_per_class:(i + 1) * rows_per_class, :] = buf[slot, i].astype(BF16)


def _permute_cast(x):
    B, S, D = x.shape
    A = S // CLASSES
    per = PERMUTE_CLASSES_PER_STEP
    kern = functools.partial(_permute_cast_kernel, rows_per_class=A)
    return pl.pallas_call(
        kern,
        grid=(B, CLASSES // per),
        in_specs=[pl.BlockSpec(memory_space=pl.ANY)],
        out_specs=pl.BlockSpec((None, per * A, D), lambda b, q: (b, q, 0)),
        out_shape=jax.ShapeDtypeStruct((B, S, D), BF16),
        scratch_shapes=[pltpu.VMEM((2, per, A, D), F32), pltpu.SemaphoreType.DMA((2, per))],
        compiler_params=_cparams(2),
        name="permute_cast",
    )(x.reshape(B, A, CLASSES, D))


def _band_bias():
    out = np.zeros((N_GROUPS, 2, 2 * BAND, 2 * BAND), np.float32)
    i = np.arange(BAND)
    for g, d in enumerate(DILATIONS):
        R = CLASSES // d
        L = BAND // R
        m = R * (i % L) + i // L
        m_k = np.concatenate([m - BAND, m])
        dist = m[:, None] - m_k[None, :]
        band = (dist >= 0) & (dist <= BAND)
        for first in range(2):
            valid = band & ((np.arange(2 * BAND) >= BAND)[None, :] | (first == 0))
            bias = np.where(valid, 0.0, NEG_INF).astype(np.float32)
            out[g, first] = np.concatenate([bias, bias], axis=0)
    return out


def _rope_tables(S):
    A = S // CLASSES
    r = np.arange(S)
    pos = (CLASSES * (r % A) + r // A).astype(np.float64)
    inv_freq = ROPE_THETA ** (-np.arange(HALF, dtype=np.float64) / HALF)
    ang = pos[:, None] * inv_freq[None, :]
    cos_t = np.tile(np.cos(ang), (1, LANES // HALF))
    sin = np.sin(ang)
    sin_t = np.concatenate([-sin, -sin, sin, sin], axis=1)
    return cos_t.astype(np.float32), sin_t.astype(np.float32)


def _pair_interleave(w):
    lane = lax.broadcasted_iota(jnp.int32, w.shape, 1)
    from_right = pltpu.roll(w, LANES - HALF, axis=1)
    from_left = pltpu.roll(w, HALF, axis=1)
    quarter = lane // HALF
    return jnp.where(quarter == 1, from_right, jnp.where(quarter == 2, from_left, w))


def _attn_weights_kernel(*refs):
    w_refs = refs[:N_ATTN_SLABS]
    b_refs = refs[N_ATTN_SLABS:2 * N_ATTN_SLABS]
    w_out, b_out = refs[2 * N_ATTN_SLABS:]
    D = w_out.shape[0]
    w_rows = 256
    for i, (g, kind) in enumerate(ATTN_SLAB_LAYOUT):
        is_qk = kind in (0, 1)
        cols = slice(i * LANES, (i + 1) * LANES)
        for r0 in range(0, D, w_rows):
            w = w_refs[i][r0:r0 + w_rows, :]
            w_out[r0:r0 + w_rows, cols] = (_pair_interleave(w) if is_qk else w).astype(BF16)
        b = jnp.broadcast_to(b_refs[i][...], (SUBLANES, LANES))
        b_out[:, cols] = _pair_interleave(b) if is_qk else b


def _attn_weights(w_in, b_in):
    D = w_in.shape[0]
    kind_col = (0, COL_K, COL_V)
    slab0 = [COL_GATE_A // LANES if kind == GATE_SLAB else (kind_col[kind] + g * ATTN_W) // LANES
             for g, kind in ATTN_SLAB_LAYOUT]
    w_specs = [pl.BlockSpec((D, LANES), functools.partial(lambda p, s: (0, s + p), s=s))
               for s in slab0]
    b_specs = [pl.BlockSpec((1, LANES), functools.partial(lambda p, s: (0, s + p), s=s))
               for s in slab0]
    return pl.pallas_call(
        _attn_weights_kernel,
        grid=(N_PAIRS,),
        in_specs=w_specs + b_specs,
        out_specs=[pl.BlockSpec((None, D, ATTN_COLS), lambda p: (p, 0, 0)),
                   pl.BlockSpec((None, SUBLANES, ATTN_COLS), lambda p: (p, 0, 0))],
        out_shape=[jax.ShapeDtypeStruct((N_PAIRS, D, ATTN_COLS), BF16),
                   jax.ShapeDtypeStruct((N_PAIRS, SUBLANES, ATTN_COLS), F32)],
        compiler_params=_cparams(1),
        name="attn_weights",
    )(*([w_in] * N_ATTN_SLABS), *([b_in] * N_ATTN_SLABS))


def _attention_kernel(x_ref, w_s, b_s, cos_ref, sin_ref, bias_ref, o_ref,
                      qkv0_s, qkv_s, gate_s, acc_s, m_s, l_s, *, S, row_chunk):
    A = S // CLASSES

    def qkv_ref(g, i):
        return qkv0_s.at[i] if g == 0 else qkv_s.at[3 * (g - 1) + i]

    n_tiles = S // BAND
    lane = lax.broadcasted_iota(jnp.int32, (BAND, LANES), 1)
    head0_q = ((lane // HALF) % 2) == 0
    head0_v = lane < HEAD_DIM
    head0_kv = lax.broadcasted_iota(jnp.int32, (2 * BAND, LANES), 1) < HEAD_DIM
    ones_h0 = head0_kv.astype(BF16)
    ones_h1 = 1 - ones_h0
    q_scale = (HEAD_DIM ** -0.5) * LOG2_E

    def proj_chunk(rc, slab_lo, slab_hi):
        rows = pl.ds(pl.multiple_of(rc * row_chunk, row_chunk), row_chunk)
        cols = slice(slab_lo * LANES, slab_hi * LANES)
        res = jnp.dot(x_ref[rows, :], w_s[:, cols], preferred_element_type=F32) + b_s[0:1, cols]
        cos = cos_ref[rows, :]
        sin = sin_ref[rows, :]
        for i, (g, kind) in enumerate(ATTN_SLAB_LAYOUT[slab_lo:slab_hi]):
            val = res[:, i * LANES:(i + 1) * LANES]
            if kind == GATE_SLAB:
                gate_s[rows, :] = (val * jax.nn.sigmoid(val)).astype(BF16)
                continue
            if kind in (0, 1):
                val = val * cos + pltpu.roll(val, LANES // 2, axis=1) * sin
            if kind == 0:
                val = val * q_scale
            ref = qkv_ref(g, kind)
            ref[rows, :] = val.astype(ref.dtype)

    def tiles(g, is_first_group):
        d = DILATIONS[g]
        R = CLASSES // d
        L = BAND // R
        nb = S // (d * BAND)

        def make_run(e):
            rho, n = divmod(e, nb)

            def run(j, nn):
                start = (rho + d * j) * A + L * nn
                return slice(start, start + L)
            return n, run

        def gather(ref, run, nn):
            return jnp.concatenate([ref[run(j, nn), :] for j in range(R)], axis=0)

        def scores(n, run):
            q_t = gather(qkv_ref(g, 0), run, n).astype(BF16)
            zero = jnp.zeros_like(q_t)
            qm = jnp.concatenate([jnp.where(head0_q, q_t, zero),
                                  jnp.where(head0_q, zero, q_t)], axis=0)
            n_prev = max(n - 1, 0)
            k_ref = qkv_ref(g, 1)
            kwin = jnp.concatenate([gather(k_ref, run, n_prev), gather(k_ref, run, n)],
                                   axis=0).astype(BF16)
            s = lax.dot_general(qm, kwin, (((1,), (1,)), ((), ())),
                                preferred_element_type=F32)
            return s + bias_ref[g, int(n == 0)]

        def softmax(s):
            m_blk = jnp.max(s, axis=1, keepdims=True)
            p = jnp.exp2(s - m_blk)
            return p.astype(BF16), m_blk

        def finish(n, run, p, m_blk):
            n_prev = max(n - 1, 0)
            v_ref = qkv_ref(g, 2)
            vwin = jnp.concatenate([gather(v_ref, run, n_prev), gather(v_ref, run, n)],
                                   axis=0).astype(BF16)
            zero_v = jnp.zeros_like(vwin)
            v_bd = jnp.concatenate([
                jnp.concatenate([jnp.where(head0_kv, vwin, zero_v), ones_h0], axis=1),
                jnp.concatenate([jnp.where(head0_kv, zero_v, vwin), ones_h1], axis=1)], axis=0)
            p_cat = jnp.concatenate([p[:BAND], p[BAND:]], axis=1)
            pv = jnp.dot(p_cat, v_bd, preferred_element_type=F32)
            o_t = pv[:, :LANES]
            m_t = jnp.where(head0_v, m_blk[:BAND], m_blk[BAND:])
            l_t = pv[:, LANES:]
            if not is_first_group:
                m_old = gather(m_s, run, n)
                m_new = jnp.maximum(m_old, m_t)
                w_old = jnp.exp2(m_old - m_new)
                w_new = jnp.exp2(m_t - m_new)
                m_t = m_new
                l_t = gather(l_s, run, n) * w_old + l_t * w_new
                o_t = gather(acc_s, run, n) * w_old + o_t * w_new
            for j in range(R):
                m_s[run(j, n), :] = m_t[j * L:(j + 1) * L]
                l_s[run(j, n), :] = l_t[j * L:(j + 1) * L]
                acc_s[run(j, n), :] = o_t[j * L:(j + 1) * L]

        for e in range(n_tiles):
            n, run = make_run(e)
            finish(n, run, *softmax(scores(n, run)))

    def proj_all(rc, carry):
        proj_chunk(rc, 0, N_ATTN_SLABS)
        return carry

    lax.fori_loop(0, S // row_chunk, proj_all, 0)
    for g in range(N_GROUPS):
        tiles(g, g == 0)

    for c in range(CLASSES):
        rows = slice(c * A, (c + 1) * A)
        res = (acc_s[rows, :] / l_s[rows, :]) * gate_s[rows, :].astype(F32)
        o_ref[pl.ds(c, A, stride=CLASSES), :] = res


def _attention(xp, w_attn, b_attn):
    B, S, D = xp.shape
    kern = functools.partial(_attention_kernel, S=S, row_chunk=1024)
    cos_t, sin_t = _rope_tables(S)
    const2 = pl.BlockSpec((S, LANES), lambda b, p: (0, 0), pipeline_mode=ONCE)
    scratch = [
        pltpu.VMEM((3, S, LANES), F32),
        pltpu.VMEM((3 * (N_GROUPS - 1), S, LANES), BF16),
        pltpu.VMEM((S, LANES), BF16),
        pltpu.VMEM((S, LANES), F32),
        pltpu.VMEM((S, LANES), F32),
        pltpu.VMEM((S, LANES), F32),
    ]
    return pl.pallas_call(
        kern,
        grid=(B, N_PAIRS),
        in_specs=[
            pl.BlockSpec((None, S, D), lambda b, p: (b, 0, 0)),
            pl.BlockSpec((None, D, ATTN_COLS), lambda b, p: (p, 0, 0)),
            pl.BlockSpec((None, SUBLANES, ATTN_COLS), lambda b, p: (p, 0, 0)),
            const2, const2,
            pl.BlockSpec((N_GROUPS, 2, 2 * BAND, 2 * BAND), lambda b, p: (0, 0, 0, 0),
                         pipeline_mode=ONCE),
        ],
        out_specs=pl.BlockSpec((None, S, LANES), lambda b, p: (b, 0, p)),
        out_shape=jax.ShapeDtypeStruct((B, S, ATTN_W), F32),
        scratch_shapes=scratch,
        compiler_params=_cparams(2),
        name="attention",
    )(xp, w_attn, b_attn, cos_t, sin_t, _band_bias())


X_AHEAD = 2
X_SLOTS = X_AHEAD + 1
SCAN_VREGS = 2


def _rglru_kernel(x_hbm, wu_ref, wl_ref, bu_ref, bl_ref, cw_ref, cb_ref, wr_ref, wi_ref,
                  br_ref, bi_ref, lam_ref, o_ref,
                  w_s, wg_s, u_pad, gate_s, a_p, h_p, x_buf, x_sem, *, S, row_chunk, pitch):
    C = LRU_SLAB
    n_slabs = C // LANES
    n_chunks = S // row_chunk
    n_seg = SCAN_VREGS * SUBLANES
    seg_len = S // n_seg
    seg_per_chunk = row_chunk // seg_len
    u_pad[:, 0:SUBLANES, :] = jnp.zeros((n_slabs, SUBLANES, LANES), F32)

    w_s[:, :C] = wu_ref[...].astype(BF16)
    w_s[:, C:] = wl_ref[...].astype(BF16)
    wg_s[...] = jnp.zeros(wg_s.shape, BF16)
    for blk in range(C // LRU_BLOCK):
        rows = slice(blk * LRU_BLOCK, (blk + 1) * LRU_BLOCK)
        wg_s[rows, blk * LRU_BLOCK:(blk + 1) * LRU_BLOCK] = wr_ref[blk].astype(BF16)
        wg_s[rows, C + blk * LRU_BLOCK:C + (blk + 1) * LRU_BLOCK] = wi_ref[blk].astype(BF16)

    step = pl.program_id(0) * pl.num_programs(1) + pl.program_id(1)
    total_chunks = pl.num_programs(0) * pl.num_programs(1) * n_chunks
    chunks_per_batch = pl.num_programs(1) * n_chunks

    def x_copy(q):
        b = lax.div(q, chunks_per_batch)
        rows = pl.ds(pl.multiple_of(lax.rem(q, n_chunks) * row_chunk, row_chunk), row_chunk)
        slot = lax.rem(q, X_SLOTS)
        return pltpu.make_async_copy(x_hbm.at[b, rows, :], x_buf.at[slot], x_sem.at[slot])

    @pl.when(step == 0)
    def _():
        for q in range(X_AHEAD):
            x_copy(q).start()

    def x_arrive(rc):
        q = step * n_chunks + rc
        x_copy(q).wait()

        @pl.when(q + X_AHEAD < total_chunks)
        def _():
            x_copy(q + X_AHEAD).start()

    def proj_chunk(rc, carry):
        r0 = pl.multiple_of(rc * row_chunk, row_chunk)
        q = step * n_chunks + rc
        xb = x_buf[lax.rem(q, X_SLOTS)].astype(BF16)
        res = jnp.dot(xb, w_s[...], preferred_element_type=F32)
        u = res[:, :C] + bu_ref[...]
        for s in range(n_slabs):
            u_pad[s, pl.ds(r0 + SUBLANES, row_chunk), :] = u[:, s * LANES:(s + 1) * LANES]
        gl = res[:, C:] + bl_ref[...]
        gate_s[pl.ds(r0, row_chunk), :] = gl * jax.nn.sigmoid(gl)
        return carry

    neg_lam = -lam_ref[...]
    softplus = jnp.maximum(neg_lam, 0.0) + jnp.log1p(jnp.exp(-jnp.abs(neg_lam)))
    decay_rate = LRU_C * softplus

    def gate_chunk(rc, carry):
        r0 = pl.multiple_of(rc * row_chunk, row_chunk)
        parts = []
        for s in range(n_slabs):
            lanes = slice(s * LANES, (s + 1) * LANES)
            acc = cb_ref[:, lanes]
            for j in range(CONV_W):
                off = SUBLANES - (CONV_W - 1) + j
                acc = acc + u_pad[s, pl.ds(r0 + off, row_chunk), :] * cw_ref[j:j + 1, lanes]
            parts.append(acc)
        uc = jnp.concatenate(parts, axis=1)
        gates = jnp.dot(uc.astype(BF16), wg_s[...], preferred_element_type=F32)
        r = jax.nn.sigmoid(gates[:, :C] + br_ref[...])
        i = jax.nn.sigmoid(gates[:, C:] + bi_ref[...])
        neg_log_a = r * decay_rate
        a = jnp.exp(-neg_log_a)
        one_minus_a2 = jnp.maximum(jnp.tanh(neg_log_a) * (a * a + 1.0), 0.0)
        mult = jnp.exp2(jnp.log(one_minus_a2) * (0.5 * LOG2_E))
        u = mult * (i * uc)
        for k in range(seg_per_chunk):
            start = (rc * seg_per_chunk + k) * pitch
            rows = slice(k * seg_len, (k + 1) * seg_len)
            for s in range(n_slabs):
                lanes = slice(s * LANES, (s + 1) * LANES)
                a_p[s, pl.ds(start, seg_len), :] = a[rows, lanes]
                h_p[s, pl.ds(start, seg_len), :] = u[rows, lanes]
        return carry

    def gate_then_proj(rc, carry):
        x_arrive(rc + 1)
        gate_chunk(rc, carry)
        return proj_chunk(rc + 1, carry)

    x_arrive(0)
    proj_chunk(0, 0)
    lax.fori_loop(0, n_chunks - 1, gate_then_proj, 0)
    gate_chunk(n_chunks - 1, 0)

    chains = [(s, v) for s in range(n_slabs) for v in range(SCAN_VREGS)]

    def scan_step(j, carry):
        out = []
        for c, (s, v) in enumerate(chains):
            rows = pl.ds(v * SUBLANES * pitch + j, SUBLANES, stride=pitch)
            h_prev, p_prev = carry[2 * c], carry[2 * c + 1]
            a = a_p[s, rows, :]
            h = a * h_prev + h_p[s, rows, :]
            p = a * p_prev
            h_p[s, rows, :] = h
            a_p[s, rows, :] = p
            out += [h, p]
        return tuple(out)

    init = (jnp.zeros((SUBLANES, LANES), F32), jnp.ones((SUBLANES, LANES), F32)) * len(chains)
    ends = lax.fori_loop(0, seg_len, scan_step, init, unroll=8)

    for s in range(n_slabs):
        lanes = slice(s * LANES, (s + 1) * LANES)
        h_in = jnp.zeros((1, LANES), F32)
        for seg in range(n_seg):
            c, sub = chains.index((s, seg // SUBLANES)), seg % SUBLANES
            h_end, p_end = ends[2 * c], ends[2 * c + 1]
            rows = slice(seg * pitch, seg * pitch + seg_len)
            h = h_p[s, rows, :] + a_p[s, rows, :] * h_in
            nat = slice(seg * seg_len, (seg + 1) * seg_len)
            o_ref[nat, lanes] = (h * gate_s[nat, lanes]).astype(BF16)
            h_in = h_end[sub:sub + 1] + p_end[sub:sub + 1] * h_in


def _rglru(x, w_in, b_in, conv_w, conv_b, lru_wr, lru_br, lru_wi, lru_bi, lam):
    B, S, D = x.shape
    C = LRU_SLAB
    n_slabs = LRU_W // C
    per = C // LRU_BLOCK
    n_seg = SCAN_VREGS * SUBLANES
    pitch = S // n_seg + SUBLANES // 2
    row_chunk = 512
    kern = functools.partial(_rglru_kernel, S=S, row_chunk=row_chunk, pitch=pitch)
    col = lambda c0: (lambda b, c: (0, c0 // C + c))
    row1 = lambda: pl.BlockSpec((1, C), lambda b, c: (0, c))
    blocks = lambda: pl.BlockSpec((per, LRU_BLOCK, LRU_BLOCK), lambda b, c: (c, 0, 0))
    return pl.pallas_call(
        kern,
        grid=(B, n_slabs),
        in_specs=[
            pl.BlockSpec(memory_space=pl.ANY),
            pl.BlockSpec((D, C), col(COL_U)), pl.BlockSpec((D, C), col(COL_GATE_L)),
            pl.BlockSpec((1, C), col(COL_U)), pl.BlockSpec((1, C), col(COL_GATE_L)),
            pl.BlockSpec((CONV_W, C), lambda b, c: (0, c)), row1(),
            blocks(), blocks(), row1(), row1(), row1(),
        ],
        out_specs=pl.BlockSpec((None, S, C), lambda b, c: (b, 0, c)),
        out_shape=jax.ShapeDtypeStruct((B, S, LRU_W), BF16),
        scratch_shapes=[
            pltpu.VMEM((D, 2 * C), BF16),
            pltpu.VMEM((C, 2 * C), BF16),
            pltpu.VMEM((C // LANES, S + SUBLANES, LANES), F32),
            pltpu.VMEM((S, C), F32),
            pltpu.VMEM((C // LANES, n_seg * pitch, LANES), F32),
            pltpu.VMEM((C // LANES, n_seg * pitch, LANES), F32),
            pltpu.VMEM((X_SLOTS, row_chunk, D), F32),
            pltpu.SemaphoreType.DMA((X_SLOTS,)),
        ],
        compiler_params=_cparams(2),
        name="rglru",
    )(x, w_in, w_in, b_in, b_in, conv_w, conv_b, lru_wr, lru_wi, lru_br, lru_bi, lam)


MERGE_SUBTILES = 2


def _merge_out_kernel(x_ref, attn_ref, hg_ref, wga_ref, wgb_ref, bga_ref, bgb_ref,
                      wa_ref, wl_ref, wo_ref, bo_ref, lng_ref, lnb_ref, o_ref,
                      wg_s, wa_s, wl_s, wo_s):
    D = x_ref.shape[1]

    @pl.when(pl.program_id(0) == 0)
    def _():
        rows = 256
        for r0 in range(0, D, rows):
            sl = slice(r0, r0 + rows)
            wg_s[sl, :D] = wga_ref[sl, :].astype(BF16)
            wg_s[sl, D:] = wgb_ref[sl, :].astype(BF16)
            wl_s[sl, :] = wl_ref[sl, :].astype(BF16)
            wo_s[sl, :] = wo_ref[sl, :].astype(BF16)
        for r0 in range(0, ATTN_W, rows):
            sl = slice(r0, r0 + rows)
            wa_s[sl, :] = wa_ref[sl, :].astype(BF16)

    half = x_ref.shape[0] // MERGE_SUBTILES
    for hb in range(MERGE_SUBTILES):
        rows = slice(hb * half, (hb + 1) * half)
        x = x_ref[rows, :]
        xb = x.astype(BF16)
        gate_a = jax.nn.sigmoid(
            jnp.dot(xb, wg_s[:, :D], preferred_element_type=F32) + bga_ref[...])
        gate_b = jax.nn.sigmoid(
            jnp.dot(xb, wg_s[:, D:], preferred_element_type=F32) + bgb_ref[...])
        y_a = jnp.dot(attn_ref[rows, :].astype(BF16), wa_s[...], preferred_element_type=F32)
        y_b = jnp.dot(hg_ref[rows, :], wl_s[...], preferred_element_type=F32)
        merged = gate_a * y_a + gate_b * y_b
        out = jnp.dot(merged.astype(BF16), wo_s[...], preferred_element_type=F32) + bo_ref[...]
        y = ALPHA * x + out
        mu = jnp.mean(y, axis=1, keepdims=True)
        yc = y - mu
        var = jnp.mean(yc * yc, axis=1, keepdims=True)
        o_ref[rows, :] = yc * lax.rsqrt(var + LN_EPS) * lng_ref[...] + lnb_ref[...]


def _merge_out(x2, attn2, hg2, w_in, b_in, w_a, w_l, w_o, b_o, ln_g, ln_b, tm=512):
    T, D = x2.shape
    once = lambda shape, idx: pl.BlockSpec(shape, lambda i: idx, pipeline_mode=ONCE)
    merge_blk = COL_MERGE // D
    return pl.pallas_call(
        _merge_out_kernel,
        grid=(T // tm,),
        in_specs=[
            pl.BlockSpec((tm, D), lambda i: (i, 0)),
            pl.BlockSpec((tm, ATTN_W), lambda i: (i, 0)),
            pl.BlockSpec((tm, LRU_W), lambda i: (i, 0)),
            once((D, D), (0, merge_blk)), once((D, D), (0, merge_blk + 1)),
            once((1, D), (0, merge_blk)), once((1, D), (0, merge_blk + 1)),
            once((ATTN_W, D), (0, 0)), once((LRU_W, D), (0, 0)), once((D, D), (0, 0)),
            once((1, D), (0, 0)), once((1, D), (0, 0)), once((1, D), (0, 0)),
        ],
        out_specs=pl.BlockSpec((tm, D), lambda i: (i, 0)),
        out_shape=jax.ShapeDtypeStruct((T, D), F32),
        scratch_shapes=[
            pltpu.VMEM((D, 2 * D), BF16), pltpu.VMEM((ATTN_W, D), BF16),
            pltpu.VMEM((LRU_W, D), BF16), pltpu.VMEM((D, D), BF16),
        ],
        compiler_params=_cparams(1),
        name="merge_out",
    )(x2, attn2, hg2, w_in, w_in, b_in, b_in, w_a, w_l, w_o, b_o, ln_g, ln_b)


def kernel(x, w_in, b_in, conv_w, conv_b, lru_wr, lru_br, lru_wi, lru_bi, lru_lambda,
           w_attn_proj, w_lru_proj, w_out, b_out, ln_gain, ln_bias):
    B, S, D = x.shape
    assert w_in.shape[0] == DEPTH and S % (CLASSES * BAND) == 0 and D == D_MODEL
    assert COL_MERGE % D == 0 and w_in.shape[2] == COL_MERGE + 2 * D
    row = lambda v: v.reshape(1, -1)
    for l in range(DEPTH):
        wl, bl = w_in[l], row(b_in[l])
        attn = _attention(_permute_cast(x), *_attn_weights(wl, bl))
        hg = _rglru(x, wl, bl, conv_w[l], row(conv_b[l]), lru_wr[l], row(lru_br[l]),
                    lru_wi[l], row(lru_bi[l]), row(lru_lambda[l]))
        out = _merge_out(
            x.reshape(B * S, D), attn.reshape(B * S, ATTN_W), hg.reshape(B * S, LRU_W),
            wl, bl, w_attn_proj[l], w_lru_proj[l], w_out[l],
            row(b_out[l]), row(ln_gain[l]), row(ln_bias[l]))
        x = out.reshape(B, S, D)
    return x
```

```python
import functools
import math

import jax
import jax.numpy as jnp
import numpy as np
from jax import lax
from jax.experimental import pallas as pl
from jax.experimental.pallas import tpu as pltpu

LANES = 128
SUBLANES = 8
BF16_ROWS = 16
VMEM_LIMIT_BYTES = 56 * 1024 * 1024

D_MODEL = 1024
HEAD_DIM = 64
HALF = HEAD_DIM // 2
HEADS_PER_GROUP = 8
DILATIONS = (1, 4, 16)
N_GROUPS = len(DILATIONS)
BAND = 128
CLASSES = 16
QKV_W = N_GROUPS * HEADS_PER_GROUP * HEAD_DIM
ATTN_W = HEADS_PER_GROUP * HEAD_DIM
N_PAIRS = ATTN_W // LANES
LRU_W = 1024
LRU_BLOCK = 64
LRU_SLAB = 256
CONV_W = 4
LRU_C = 8.0
ROPE_THETA = 10000.0
LOG2_E = math.log2(math.e)
NEG_INF = -1e30
LN_EPS = 1e-5
DEPTH = 1
ALPHA = (2.0 * DEPTH) ** 0.25
GATE_SLAB = 3
ATTN_SLAB_LAYOUT = ([(0, 0), (0, 1), (0, 2), (None, GATE_SLAB)]
                    + [(g, kind) for g in range(1, N_GROUPS) for kind in range(3)])
N_ATTN_SLABS = len(ATTN_SLAB_LAYOUT)
ATTN_COLS = N_ATTN_SLABS * LANES
COL_K = QKV_W
COL_V = 2 * QKV_W
COL_GATE_A = 3 * QKV_W
COL_U = COL_GATE_A + ATTN_W
COL_GATE_L = COL_U + LRU_W
COL_MERGE = COL_GATE_L + LRU_W

BF16 = jnp.bfloat16
F32 = jnp.float32
ONCE = pl.Buffered(1)


def _cparams(n_axes, flags=None):
    return pltpu.CompilerParams(dimension_semantics=("arbitrary",) * n_axes,
                                vmem_limit_bytes=VMEM_LIMIT_BYTES, flags=flags)


PERMUTE_CLASSES_PER_STEP = 4


def _permute_cast_kernel(x_hbm, xp_ref, buf, sem, *, rows_per_class):
    per = PERMUTE_CLASSES_PER_STEP
    steps_per_batch = CLASSES // per
    step = pl.program_id(0) * steps_per_batch + pl.program_id(1)
    n_steps = pl.num_programs(0) * steps_per_batch

    def copies(st, slot):
        b = lax.div(st, steps_per_batch)
        q = lax.rem(st, steps_per_batch)
        return [pltpu.make_async_copy(x_hbm.at[b, :, q * per + i, :], buf.at[slot, i],
                                      sem.at[slot, i]) for i in range(per)]

    @pl.when(step == 0)
    def _():
        for cp in copies(step, 0):
            cp.start()

    slot = lax.rem(step, 2)

    @pl.when(step + 1 < n_steps)
    def _():
        for cp in copies(step + 1, 1 - slot):
            cp.start()

    for i, cp in enumerate(copies(step, slot)):
        cp.wait()
        xp_ref[i * rows_per_class:(i + 1) * rows_per_class, :] = buf[slot, i].astype(BF16)


def _permute_cast(x):
    B, S, D = x.shape
    A = S // CLASSES
    per = PERMUTE_CLASSES_PER_STEP
    kern = functools.partial(_permute_cast_kernel, rows_per_class=A)
    return pl.pallas_call(
        kern,
        grid=(B, CLASSES // per),
        in_specs=[pl.BlockSpec(memory_space=pl.ANY)],
        out_specs=pl.BlockSpec((None, per * A, D), lambda b, q: (b, q, 0)),
        out_shape=jax.ShapeDtypeStruct((B, S, D), BF16),
        scratch_shapes=[pltpu.VMEM((2, per, A, D), F32), pltpu.SemaphoreType.DMA((2, per))],
        compiler_params=_cparams(2),
        name="permute_cast",
    )(x.reshape(B, A, CLASSES, D))


def _band_bias():
    out = np.zeros((N_GROUPS, 2, 2 * BAND, 2 * BAND), np.float32)
    i = np.arange(BAND)
    for g, d in enumerate(DILATIONS):
        R = CLASSES // d
        L = BAND // R
        m = R * (i % L) + i // L
        m_k = np.concatenate([m - BAND, m])
        dist = m[:, None] - m_k[None, :]
        band = (dist >= 0) & (dist <= BAND)
        for first in range(2):
            valid = band & ((np.arange(2 * BAND) >= BAND)[None, :] | (first == 0))
            bias = np.where(valid, 0.0, NEG_INF).astype(np.float32)
            out[g, first] = np.concatenate([bias, bias], axis=0)
    return out


def _rope_tables(S):
    A = S // CLASSES
    r = np.arange(S)
    pos = (CLASSES * (r % A) + r // A).astype(np.float64)
    inv_freq = ROPE_THETA ** (-np.arange(HALF, dtype=np.float64) / HALF)
    ang = pos[:, None] * inv_freq[None, :]
    cos_t = np.tile(np.cos(ang), (1, LANES // HALF))
    sin = np.sin(ang)
    sin_t = np.concatenate([-sin, -sin, sin, sin], axis=1)
    return cos_t.astype(np.float32), sin_t.astype(np.float32)


def _pair_interleave(w):
    lane = lax.broadcasted_iota(jnp.int32, w.shape, 1)
    from_right = pltpu.roll(w, LANES - HALF, axis=1)
    from_left = pltpu.roll(w, HALF, axis=1)
    quarter = lane // HALF
    return jnp.where(quarter == 1, from_right, jnp.where(quarter == 2, from_left, w))


def _attn_weights_kernel(*refs):
    w_refs = refs[:N_ATTN_SLABS]
    b_refs = refs[N_ATTN_SLABS:2 * N_ATTN_SLABS]
    w_out, b_out = refs[2 * N_ATTN_SLABS:]
    D = w_out.shape[0]
    w_rows = 256
    for i, (g, kind) in enumerate(ATTN_SLAB_LAYOUT):
        is_qk = kind in (0, 1)
        cols = slice(i * LANES, (i + 1) * LANES)
        for r0 in range(0, D, w_rows):
            w = w_refs[i][r0:r0 + w_rows, :]
            w_out[r0:r0 + w_rows, cols] = (_pair_interleave(w) if is_qk else w).astype(BF16)
        b = jnp.broadcast_to(b_refs[i][...], (SUBLANES, LANES))
        b_out[:, cols] = _pair_interleave(b) if is_qk else b


def _attn_weights(w_in, b_in):
    D = w_in.shape[0]
    kind_col = (0, COL_K, COL_V)
    slab0 = [COL_GATE_A // LANES if kind == GATE_SLAB else (kind_col[kind] + g * ATTN_W) // LANES
             for g, kind in ATTN_SLAB_LAYOUT]
    w_specs = [pl.BlockSpec((D, LANES), functools.partial(lambda p, s: (0, s + p), s=s))
               for s in slab0]
    b_specs = [pl.BlockSpec((1, LANES), functools.partial(lambda p, s: (0, s + p), s=s))
               for s in slab0]
    return pl.pallas_call(
        _attn_weights_kernel,
        grid=(N_PAIRS,),
        in_specs=w_specs + b_specs,
        out_specs=[pl.BlockSpec((None, D, ATTN_COLS), lambda p: (p, 0, 0)),
                   pl.BlockSpec((None, SUBLANES, ATTN_COLS), lambda p: (p, 0, 0))],
        out_shape=[jax.ShapeDtypeStruct((N_PAIRS, D, ATTN_COLS), BF16),
                   jax.ShapeDtypeStruct((N_PAIRS, SUBLANES, ATTN_COLS), F32)],
        compiler_params=_cparams(1),
        name="attn_weights",
    )(*([w_in] * N_ATTN_SLABS), *([b_in] * N_ATTN_SLABS))


def _attention_kernel(x_ref, w_s, b_s, cos_ref, sin_ref, bias_ref, o_ref,
                      qkv0_s, qkv_s, gate_s, acc_s, m_s, l_s, *, S, row_chunk):
    A = S // CLASSES

    def qkv_ref(g, i):
        return qkv0_s.at[i] if g == 0 else qkv_s.at[3 * (g - 1) + i]

    n_tiles = S // BAND
    lane = lax.broadcasted_iota(jnp.int32, (BAND, LANES), 1)
    head0_q = ((lane // HALF) % 2) == 0
    head0_v = lane < HEAD_DIM
    head0_kv = lax.broadcasted_iota(jnp.int32, (2 * BAND, LANES), 1) < HEAD_DIM
    ones_h0 = head0_kv.astype(BF16)
    ones_h1 = 1 - ones_h0
    q_scale = (HEAD_DIM ** -0.5) * LOG2_E

    def proj_chunk(rc, slab_lo, slab_hi):
        rows = pl.ds(pl.multiple_of(rc * row_chunk, row_chunk), row_chunk)
        cols = slice(slab_lo * LANES, slab_hi * LANES)
        res = jnp.dot(x_ref[rows, :], w_s[:, cols], preferred_element_type=F32) + b_s[0:1, cols]
        cos = cos_ref[rows, :]
        sin = sin_ref[rows, :]
        for i, (g, kind) in enumerate(ATTN_SLAB_LAYOUT[slab_lo:slab_hi]):
            val = res[:, i * LANES:(i + 1) * LANES]
            if kind == GATE_SLAB:
                gate_s[rows, :] = (val * jax.nn.sigmoid(val)).astype(BF16)
                continue
            if kind in (0, 1):
                val = val * cos + pltpu.roll(val, LANES // 2, axis=1) * sin
            if kind == 0:
                val = val * q_scale
            ref = qkv_ref(g, kind)
            ref[rows, :] = val.astype(ref.dtype)

    def tiles(g, is_first_group):
        d = DILATIONS[g]
        R = CLASSES // d
        L = BAND // R
        nb = S // (d * BAND)

        def make_run(e):
            rho, n = divmod(e, nb)

            def run(j, nn):
                start = (rho + d * j) * A + L * nn
                return slice(start, start + L)
            return n, run

        def gather(ref, run, nn):
            return jnp.concatenate([ref[run(j, nn), :] for j in range(R)], axis=0)

        def scores(n, run):
            q_t = gather(qkv_ref(g, 0), run, n).astype(BF16)
            zero = jnp.zeros_like(q_t)
            qm = jnp.concatenate([jnp.where(head0_q, q_t, zero),
                                  jnp.where(head0_q, zero, q_t)], axis=0)
            n_prev = max(n - 1, 0)
            k_ref = qkv_ref(g, 1)
            kwin = jnp.concatenate([gather(k_ref, run, n_prev), gather(k_ref, run, n)],
                                   axis=0).astype(BF16)
            s = lax.dot_general(qm, kwin, (((1,), (1,)), ((), ())),
                                preferred_element_type=F32)
            return s + bias_ref[g, int(n == 0)]

        def softmax(s):
            m_blk = jnp.max(s, axis=1, keepdims=True)
            p = jnp.exp2(s - m_blk)
            return p.astype(BF16), m_blk

        def finish(n, run, p, m_blk):
            n_prev = max(n - 1, 0)
            v_ref = qkv_ref(g, 2)
            vwin = jnp.concatenate([gather(v_ref, run, n_prev), gather(v_ref, run, n)],
                                   axis=0).astype(BF16)
            zero_v = jnp.zeros_like(vwin)
            v_bd = jnp.concatenate([
                jnp.concatenate([jnp.where(head0_kv, vwin, zero_v), ones_h0], axis=1),
                jnp.concatenate([jnp.where(head0_kv, zero_v, vwin), ones_h1], axis=1)], axis=0)
            p_cat = jnp.concatenate([p[:BAND], p[BAND:]], axis=1)
            pv = jnp.dot(p_cat, v_bd, preferred_element_type=F32)
            o_t = pv[:, :LANES]
            m_t = jnp.where(head0_v, m_blk[:BAND], m_blk[BAND:])
            l_t = pv[:, LANES:]
            if not is_first_group:
                m_old = gather(m_s, run, n)
                m_new = jnp.maximum(m_old, m_t)
                w_old = jnp.exp2(m_old - m_new)
                w_new = jnp.exp2(m_t - m_new)
                m_t = m_new
                l_t = gather(l_s, run, n) * w_old + l_t * w_new
                o_t = gather(acc_s, run, n) * w_old + o_t * w_new
            for j in range(R):
                m_s[run(j, n), :] = m_t[j * L:(j + 1) * L]
                l_s[run(j, n), :] = l_t[j * L:(j + 1) * L]
                acc_s[run(j, n), :] = o_t[j * L:(j + 1) * L]

        for e in range(n_tiles):
            n, run = make_run(e)
            finish(n, run, *softmax(scores(n, run)))

    def proj_all(rc, carry):
        proj_chunk(rc, 0, N_ATTN_SLABS)
        return carry

    lax.fori_loop(0, S // row_chunk, proj_all, 0)
    for g in range(N_GROUPS):
        tiles(g, g == 0)

    for c in range(CLASSES):
        rows = slice(c * A, (c + 1) * A)
        res = (acc_s[rows, :] / l_s[rows, :]) * gate_s[rows, :].astype(F32)
        o_ref[pl.ds(c, A, stride=CLASSES), :] = res


def _attention(xp, w_attn, b_attn):
    B, S, D = xp.shape
    kern = functools.partial(_attention_kernel, S=S, row_chunk=1024)
    cos_t, sin_t = _rope_tables(S)
    const2 = pl.BlockSpec((S, LANES), lambda b, p: (0, 0), pipeline_mode=ONCE)
    scratch = [
        pltpu.VMEM((3, S, LANES), F32),
        pltpu.VMEM((3 * (N_GROUPS - 1), S, LANES), BF16),
        pltpu.VMEM((S, LANES), BF16),
        pltpu.VMEM((S, LANES), F32),
        pltpu.VMEM((S, LANES), F32),
        pltpu.VMEM((S, LANES), F32),
    ]
    return pl.pallas_call(
        kern,
        grid=(B, N_PAIRS),
        in_specs=[
            pl.BlockSpec((None, S, D), lambda b, p: (b, 0, 0)),
            pl.BlockSpec((None, D, ATTN_COLS), lambda b, p: (p, 0, 0)),
            pl.BlockSpec((None, SUBLANES, ATTN_COLS), lambda b, p: (p, 0, 0)),
            const2, const2,
            pl.BlockSpec((N_GROUPS, 2, 2 * BAND, 2 * BAND), lambda b, p: (0, 0, 0, 0),
                         pipeline_mode=ONCE),
        ],
        out_specs=pl.BlockSpec((None, S, LANES), lambda b, p: (b, 0, p)),
        out_shape=jax.ShapeDtypeStruct((B, S, ATTN_W), F32),
        scratch_shapes=scratch,
        compiler_params=_cparams(2),
        name="attention",
    )(xp, w_attn, b_attn, cos_t, sin_t, _band_bias())


X_AHEAD = 2
X_SLOTS = X_AHEAD + 1
SCAN_VREGS = 2


def _rglru_kernel(x_hbm, wu_ref, wl_ref, wr_ref, wi_ref, b_in_ref, cw_ref, cb_ref,
                  br_ref, bi_ref, lam_ref, o_ref,
                  w_s, wg_s, u_pad, gate_s, a_p, h_p, x_buf, x_sem, *, S, row_chunk, pitch):
    C = LRU_SLAB
    n_slabs = C // LANES
    n_chunks = S // row_chunk
    n_seg = SCAN_VREGS * SUBLANES
    seg_len = S // n_seg
    seg_per_chunk = row_chunk // seg_len
    u_pad[:, 0:SUBLANES, :] = jnp.zeros((n_slabs, SUBLANES, LANES), F32)

    def slab_cols(ref, col0=0):
        start = pl.multiple_of(col0 + pl.program_id(1) * C, LANES)
        return ref[:, pl.ds(start, C)]

    bu, bl = slab_cols(b_in_ref, COL_U), slab_cols(b_in_ref, COL_GATE_L)
    cw, cb = slab_cols(cw_ref), slab_cols(cb_ref)
    br, bi = slab_cols(br_ref), slab_cols(bi_ref)

    w_s[:, :C] = wu_ref[...].astype(BF16)
    w_s[:, C:] = wl_ref[...].astype(BF16)
    wg_s[...] = jnp.zeros(wg_s.shape, BF16)
    for blk in range(C // LRU_BLOCK):
        rows = slice(blk * LRU_BLOCK, (blk + 1) * LRU_BLOCK)
        wg_s[rows, blk * LRU_BLOCK:(blk + 1) * LRU_BLOCK] = wr_ref[blk].astype(BF16)
        wg_s[rows, C + blk * LRU_BLOCK:C + (blk + 1) * LRU_BLOCK] = wi_ref[blk].astype(BF16)

    step = pl.program_id(0) * pl.num_programs(1) + pl.program_id(1)
    total_chunks = pl.num_programs(0) * pl.num_programs(1) * n_chunks
    chunks_per_batch = pl.num_programs(1) * n_chunks

    def x_copy(q):
        b = lax.div(q, chunks_per_batch)
        rows = pl.ds(pl.multiple_of(lax.rem(q, n_chunks) * row_chunk, row_chunk), row_chunk)
        slot = lax.rem(q, X_SLOTS)
        return pltpu.make_async_copy(x_hbm.at[b, rows, :], x_buf.at[slot], x_sem.at[slot])

    @pl.when(step == 0)
    def _():
        for q in range(X_AHEAD):
            x_copy(q).start()

    def x_arrive(rc):
        q = step * n_chunks + rc
        x_copy(q).wait()

        @pl.when(q + X_AHEAD < total_chunks)
        def _():
            x_copy(q + X_AHEAD).start()

    def proj_chunk(rc, carry):
        r0 = pl.multiple_of(rc * row_chunk, row_chunk)
        q = step * n_chunks + rc
        xb = x_buf[lax.rem(q, X_SLOTS)].astype(BF16)
        res = jnp.dot(xb, w_s[...], preferred_element_type=F32)
        u = res[:, :C] + bu
        for s in range(n_slabs):
            u_pad[s, pl.ds(r0 + SUBLANES, row_chunk), :] = u[:, s * LANES:(s + 1) * LANES]
        gl = res[:, C:] + bl
        gate_s[pl.ds(r0, row_chunk), :] = gl * jax.nn.sigmoid(gl)
        return carry

    neg_lam = -slab_cols(lam_ref)
    softplus = jnp.maximum(neg_lam, 0.0) + jnp.log1p(jnp.exp(-jnp.abs(neg_lam)))
    decay_rate = LRU_C * softplus

    def gate_chunk(rc, carry):
        r0 = pl.multiple_of(rc * row_chunk, row_chunk)
        parts = []
        for s in range(n_slabs):
            lanes = slice(s * LANES, (s + 1) * LANES)
            acc = cb[:, lanes]
            for j in range(CONV_W):
                off = SUBLANES - (CONV_W - 1) + j
                acc = acc + u_pad[s, pl.ds(r0 + off, row_chunk), :] * cw[j:j + 1, lanes]
            parts.append(acc)
        uc = jnp.concatenate(parts, axis=1)
        gates = jnp.dot(uc.astype(BF16), wg_s[...], preferred_element_type=F32)
        r = jax.nn.sigmoid(gates[:, :C] + br)
        i = jax.nn.sigmoid(gates[:, C:] + bi)
        neg_log_a = r * decay_rate
        a = jnp.exp(-neg_log_a)
        one_minus_a2 = jnp.maximum(jnp.tanh(neg_log_a) * (a * a + 1.0), 0.0)
        mult = jnp.exp2(jnp.log(one_minus_a2) * (0.5 * LOG2_E))
        u = mult * (i * uc)
        for k in range(seg_per_chunk):
            start = (rc * seg_per_chunk + k) * pitch
            rows = slice(k * seg_len, (k + 1) * seg_len)
            for s in range(n_slabs):
                lanes = slice(s * LANES, (s + 1) * LANES)
                a_p[s, pl.ds(start, seg_len), :] = a[rows, lanes]
                h_p[s, pl.ds(start, seg_len), :] = u[rows, lanes]
        return carry

    def gate_then_proj(rc, carry):
        x_arrive(rc + 1)
        gate_chunk(rc, carry)
        return proj_chunk(rc + 1, carry)

    x_arrive(0)
    proj_chunk(0, 0)
    lax.fori_loop(0, n_chunks - 1, gate_then_proj, 0)
    gate_chunk(n_chunks - 1, 0)

    chains = [(s, v) for s in range(n_slabs) for v in range(SCAN_VREGS)]

    def scan_step(j, carry):
        out = []
        for c, (s, v) in enumerate(chains):
            rows = pl.ds(v * SUBLANES * pitch + j, SUBLANES, stride=pitch)
            h_prev, p_prev = carry[2 * c], carry[2 * c + 1]
            a = a_p[s, rows, :]
            h = a * h_prev + h_p[s, rows, :]
            p = a * p_prev
            h_p[s, rows, :] = h
            a_p[s, rows, :] = p
            out += [h, p]
        return tuple(out)

    init = (jnp.zeros((SUBLANES, LANES), F32), jnp.ones((SUBLANES, LANES), F32)) * len(chains)
    ends = lax.fori_loop(0, seg_len, scan_step, init, unroll=8)

    for s in range(n_slabs):
        lanes = slice(s * LANES, (s + 1) * LANES)
        h_in = jnp.zeros((1, LANES), F32)
        for seg in range(n_seg):
            c, sub = chains.index((s, seg // SUBLANES)), seg % SUBLANES
            h_end, p_end = ends[2 * c], ends[2 * c + 1]
            rows = slice(seg * pitch, seg * pitch + seg_len)
            h = h_p[s, rows, :] + a_p[s, rows, :] * h_in
            nat = slice(seg * seg_len, (seg + 1) * seg_len)
            o_ref[nat, lanes] = (h * gate_s[nat, lanes]).astype(BF16)
            h_in = h_end[sub:sub + 1] + p_end[sub:sub + 1] * h_in


def _rglru(x, w_in, b_in, conv_w, conv_b, lru_wr, lru_br, lru_wi, lru_bi, lam):
    B, S, D = x.shape
    C = LRU_SLAB
    n_slabs = LRU_W // C
    per = C // LRU_BLOCK
    n_seg = SCAN_VREGS * SUBLANES
    pitch = S // n_seg + SUBLANES // 2
    row_chunk = 512
    kern = functools.partial(_rglru_kernel, S=S, row_chunk=row_chunk, pitch=pitch)
    col = lambda c0: (lambda b, c: (0, c0 // C + c))
    whole = lambda a: pl.BlockSpec(a.shape, lambda b, c: (0, 0), pipeline_mode=ONCE)
    blocks = lambda: pl.BlockSpec((per, LRU_BLOCK, LRU_BLOCK), lambda b, c: (c, 0, 0))
    return pl.pallas_call(
        kern,
        grid=(B, n_slabs),
        in_specs=[
            pl.BlockSpec(memory_space=pl.ANY),
            pl.BlockSpec((D, C), col(COL_U)), pl.BlockSpec((D, C), col(COL_GATE_L)),
            blocks(), blocks(),
            whole(b_in), whole(conv_w), whole(conv_b), whole(lru_br), whole(lru_bi), whole(lam),
        ],
        out_specs=pl.BlockSpec((None, S, C), lambda b, c: (b, 0, c)),
        out_shape=jax.ShapeDtypeStruct((B, S, LRU_W), BF16),
        scratch_shapes=[
            pltpu.VMEM((D, 2 * C), BF16),
            pltpu.VMEM((C, 2 * C), BF16),
            pltpu.VMEM((C // LANES, S + SUBLANES, LANES), F32),
            pltpu.VMEM((S, C), F32),
            pltpu.VMEM((C // LANES, n_seg * pitch, LANES), F32),
            pltpu.VMEM((C // LANES, n_seg * pitch, LANES), F32),
            pltpu.VMEM((X_SLOTS, row_chunk, D), F32),
            pltpu.SemaphoreType.DMA((X_SLOTS,)),
        ],
        compiler_params=_cparams(2),
        name="rglru",
    )(x, w_in, w_in, lru_wr, lru_wi, b_in, conv_w, conv_b, lru_br, lru_bi, lam)


MERGE_SUBTILES = 2


def _merge_out_kernel(x_ref, attn_ref, hg_ref, wga_ref, wgb_ref, bga_ref, bgb_ref,
                      wa_ref, wl_ref, wo_ref, bo_ref, lng_ref, lnb_ref, o_ref,
                      wg_s, wa_s, wl_s, wo_s):
    D = x_ref.shape[1]

    @pl.when(pl.program_id(0) == 0)
    def _():
        rows = 256
        for r0 in range(0, D, rows):
            sl = slice(r0, r0 + rows)
            wg_s[sl, :D] = wga_ref[sl, :].astype(BF16)
            wg_s[sl, D:] = wgb_ref[sl, :].astype(BF16)
            wl_s[sl, :] = wl_ref[sl, :].astype(BF16)
            wo_s[sl, :] = wo_ref[sl, :].astype(BF16)
        for r0 in range(0, ATTN_W, rows):
            sl = slice(r0, r0 + rows)
            wa_s[sl, :] = wa_ref[sl, :].astype(BF16)

    half = x_ref.shape[0] // MERGE_SUBTILES
    for hb in range(MERGE_SUBTILES):
        rows = slice(hb * half, (hb + 1) * half)
        x = x_ref[rows, :]
        xb = x.astype(BF16)
        gate_a = jax.nn.sigmoid(
            jnp.dot(xb, wg_s[:, :D], preferred_element_type=F32) + bga_ref[...])
        gate_b = jax.nn.sigmoid(
            jnp.dot(xb, wg_s[:, D:], preferred_element_type=F32) + bgb_ref[...])
        y_a = jnp.dot(attn_ref[rows, :].astype(BF16), wa_s[...], preferred_element_type=F32)
        y_b = jnp.dot(hg_ref[rows, :], wl_s[...], preferred_element_type=F32)
        merged = gate_a * y_a + gate_b * y_b
        out = jnp.dot(merged.astype(BF16), wo_s[...], preferred_element_type=F32) + bo_ref[...]
        y = ALPHA * x + out
        mu = jnp.mean(y, axis=1, keepdims=True)
        yc = y - mu
        var = jnp.mean(yc * yc, axis=1, keepdims=True)
        o_ref[rows, :] = yc * lax.rsqrt(var + LN_EPS) * lng_ref[...] + lnb_ref[...]


def _merge_out(x2, attn2, hg2, w_in, b_in, w_a, w_l, w_o, b_o, ln_g, ln_b, tm=512):
    T, D = x2.shape
    once = lambda shape, idx: pl.BlockSpec(shape, lambda i: idx, pipeline_mode=ONCE)
    merge_blk = COL_MERGE // D
    return pl.pallas_call(
        _merge_out_kernel,
        grid=(T // tm,),
        in_specs=[
            pl.BlockSpec((tm, D), lambda i: (i, 0)),
            pl.BlockSpec((tm, ATTN_W), lambda i: (i, 0)),
            pl.BlockSpec((tm, LRU_W), lambda i: (i, 0)),
            once((D, D), (0, merge_blk)), once((D, D), (0, merge_blk + 1)),
            once((1, D), (0, merge_blk)), once((1, D), (0, merge_blk + 1)),
            once((ATTN_W, D), (0, 0)), once((LRU_W, D), (0, 0)), once((D, D), (0, 0)),
            once((1, D), (0, 0)), once((1, D), (0, 0)), once((1, D), (0, 0)),
        ],
        out_specs=pl.BlockSpec((tm, D), lambda i: (i, 0)),
        out_shape=jax.ShapeDtypeStruct((T, D), F32),
        scratch_shapes=[
            pltpu.VMEM((D, 2 * D), BF16), pltpu.VMEM((ATTN_W, D), BF16),
            pltpu.VMEM((LRU_W, D), BF16), pltpu.VMEM((D, D), BF16),
        ],
        compiler_params=_cparams(1),
        name="merge_out",
    )(x2, attn2, hg2, w_in, w_in, b_in, b_in, w_a, w_l, w_o, b_o, ln_g, ln_b)


def kernel(x, w_in, b_in, conv_w, conv_b, lru_wr, lru_br, lru_wi, lru_bi, lru_lambda,
           w_attn_proj, w_lru_proj, w_out, b_out, ln_gain, ln_bias):
    B, S, D = x.shape
    assert w_in.shape[0] == DEPTH and S % (CLASSES * BAND) == 0 and D == D_MODEL
    assert COL_MERGE % D == 0 and w_in.shape[2] == COL_MERGE + 2 * D
    row = lambda v: v.reshape(1, -1)
    for l in range(DEPTH):
        wl, bl = w_in[l], row(b_in[l])
        attn = _attention(_permute_cast(x), *_attn_weights(wl, bl))
        hg = _rglru(x, wl, bl, conv_w[l], row(conv_b[l]), lru_wr[l], row(lru_br[l]),
                    lru_wi[l], row(lru_bi[l]), row(lru_lambda[l]))
        out = _merge_out(
            x.reshape(B * S, D), attn.reshape(B * S, ATTN_W), hg.reshape(B * S, LRU_W),
            wl, bl, w_attn_proj[l], w_lru_proj[l], w_out[l],
            row(b_out[l]), row(ln_gain[l]), row(ln_bias[l]))
        x = out.reshape(B, S, D)
    return x
```

```python
import functools
import math

import jax
import jax.numpy as jnp
import numpy as np
from jax import lax
from jax.experimental import pallas as pl
from jax.experimental.pallas import tpu as pltpu

LANES = 128
SUBLANES = 8
BF16_ROWS = 16
VMEM_LIMIT_BYTES = 56 * 1024 * 1024

D_MODEL = 1024
HEAD_DIM = 64
HALF = HEAD_DIM // 2
HEADS_PER_GROUP = 8
DILATIONS = (1, 4, 16)
N_GROUPS = len(DILATIONS)
BAND = 128
CLASSES = 16
QKV_W = N_GROUPS * HEADS_PER_GROUP * HEAD_DIM
ATTN_W = HEADS_PER_GROUP * HEAD_DIM
N_PAIRS = ATTN_W // LANES
LRU_W = 1024
LRU_BLOCK = 64
LRU_SLAB = 256
CONV_W = 4
LRU_C = 8.0
ROPE_THETA = 10000.0
LOG2_E = math.log2(math.e)
NEG_INF = -1e30
LN_EPS = 1e-5
DEPTH = 1
ALPHA = (2.0 * DEPTH) ** 0.25
GATE_SLAB = 3
ATTN_SLAB_LAYOUT = ([(0, 0), (0, 1), (0, 2), (None, GATE_SLAB)]
                    + [(g, kind) for g in range(1, N_GROUPS) for kind in range(3)])
N_ATTN_SLABS = len(ATTN_SLAB_LAYOUT)
ATTN_COLS = N_ATTN_SLABS * LANES
COL_K = QKV_W
COL_V = 2 * QKV_W
COL_GATE_A = 3 * QKV_W
COL_U = COL_GATE_A + ATTN_W
COL_GATE_L = COL_U + LRU_W
COL_MERGE = COL_GATE_L + LRU_W

BF16 = jnp.bfloat16
F32 = jnp.float32
ONCE = pl.Buffered(1)


def _cparams(n_axes, flags=None):
    return pltpu.CompilerParams(dimension_semantics=("arbitrary",) * n_axes,
                                vmem_limit_bytes=VMEM_LIMIT_BYTES, flags=flags)


def _band_bias():
    out = np.zeros((N_GROUPS, 2, 2 * BAND, 2 * BAND), np.float32)
    i = np.arange(BAND)
    for g, d in enumerate(DILATIONS):
        R = CLASSES // d
        L = BAND // R
        m = R * (i % L) + i // L
        m_k = np.concatenate([m - BAND, m])
        dist = m[:, None] - m_k[None, :]
        band = (dist >= 0) & (dist <= BAND)
        for first in range(2):
            valid = band & ((np.arange(2 * BAND) >= BAND)[None, :] | (first == 0))
            bias = np.where(valid, 0.0, NEG_INF).astype(np.float32)
            out[g, first] = np.concatenate([bias, bias], axis=0)
    return out


def _rope_tables(S):
    A = S // CLASSES
    r = np.arange(S)
    pos = (CLASSES * (r % A) + r // A).astype(np.float64)
    inv_freq = ROPE_THETA ** (-np.arange(HALF, dtype=np.float64) / HALF)
    ang = pos[:, None] * inv_freq[None, :]
    cos_t = np.tile(np.cos(ang), (1, LANES // HALF))
    sin = np.sin(ang)
    sin_t = np.concatenate([-sin, -sin, sin, sin], axis=1)
    return cos_t.astype(np.float32), sin_t.astype(np.float32)


def _pair_interleave(w):
    lane = lax.broadcasted_iota(jnp.int32, w.shape, 1)
    from_right = pltpu.roll(w, LANES - HALF, axis=1)
    from_left = pltpu.roll(w, HALF, axis=1)
    quarter = lane // HALF
    return jnp.where(quarter == 1, from_right, jnp.where(quarter == 2, from_left, w))


def _attn_weights_kernel(*refs):
    w_refs = refs[:N_ATTN_SLABS]
    b_refs = refs[N_ATTN_SLABS:2 * N_ATTN_SLABS]
    w_out, b_out = refs[2 * N_ATTN_SLABS:]
    D = w_out.shape[0]
    w_rows = 256
    for i, (g, kind) in enumerate(ATTN_SLAB_LAYOUT):
        is_qk = kind in (0, 1)
        cols = slice(i * LANES, (i + 1) * LANES)
        for r0 in range(0, D, w_rows):
            w = w_refs[i][r0:r0 + w_rows, :]
            w_out[r0:r0 + w_rows, cols] = (_pair_interleave(w) if is_qk else w).astype(BF16)
        b = jnp.broadcast_to(b_refs[i][...], (SUBLANES, LANES))
        b_out[:, cols] = _pair_interleave(b) if is_qk else b


def _attn_weights(w_in, b_in):
    D = w_in.shape[0]
    kind_col = (0, COL_K, COL_V)
    slab0 = [COL_GATE_A // LANES if kind == GATE_SLAB else (kind_col[kind] + g * ATTN_W) // LANES
             for g, kind in ATTN_SLAB_LAYOUT]
    w_specs = [pl.BlockSpec((D, LANES), functools.partial(lambda p, s: (0, s + p), s=s))
               for s in slab0]
    b_specs = [pl.BlockSpec((1, LANES), functools.partial(lambda p, s: (0, s + p), s=s))
               for s in slab0]
    return pl.pallas_call(
        _attn_weights_kernel,
        grid=(N_PAIRS,),
        in_specs=w_specs + b_specs,
        out_specs=[pl.BlockSpec((None, D, ATTN_COLS), lambda p: (p, 0, 0)),
                   pl.BlockSpec((None, SUBLANES, ATTN_COLS), lambda p: (p, 0, 0))],
        out_shape=[jax.ShapeDtypeStruct((N_PAIRS, D, ATTN_COLS), BF16),
                   jax.ShapeDtypeStruct((N_PAIRS, SUBLANES, ATTN_COLS), F32)],
        compiler_params=_cparams(1),
        name="attn_weights",
    )(*([w_in] * N_ATTN_SLABS), *([b_in] * N_ATTN_SLABS))


def _attention_kernel(x_hbm, w_s, b_s, cos_ref, sin_ref, bias_ref, o_ref,
                      qkv0_s, qkv_s, gate_s, acc_s, m_s, l_s, x_buf, x_sem, *, S, row_chunk):
    A = S // CLASSES
    per = row_chunk // A
    n_chunks = S // row_chunk

    step = pl.program_id(0) * pl.num_programs(1) + pl.program_id(1)
    total_chunks = pl.num_programs(0) * pl.num_programs(1) * n_chunks

    def x_copies(q):
        b = lax.div(q, pl.num_programs(1) * n_chunks)
        c0 = lax.rem(q, n_chunks) * per
        slot = lax.rem(q, 2)
        return [pltpu.make_async_copy(x_hbm.at[b, :, c0 + i, :], x_buf.at[slot, i],
                                      x_sem.at[slot, i]) for i in range(per)]

    @pl.when(step == 0)
    def _():
        for cp in x_copies(0):
            cp.start()

    def qkv_ref(g, i):
        return qkv0_s.at[i] if g == 0 else qkv_s.at[3 * (g - 1) + i]

    n_tiles = S // BAND
    lane = lax.broadcasted_iota(jnp.int32, (BAND, LANES), 1)
    head0_q = ((lane // HALF) % 2) == 0
    head0_v = lane < HEAD_DIM
    head0_kv = lax.broadcasted_iota(jnp.int32, (2 * BAND, LANES), 1) < HEAD_DIM
    ones_h0 = head0_kv.astype(BF16)
    ones_h1 = 1 - ones_h0
    q_scale = (HEAD_DIM ** -0.5) * LOG2_E

    def proj_chunk(rc, slab_lo, slab_hi):
        q = step * n_chunks + rc
        for cp in x_copies(q):
            cp.wait()

        @pl.when(q + 1 < total_chunks)
        def _():
            for cp in x_copies(q + 1):
                cp.start()

        rows = pl.ds(pl.multiple_of(rc * row_chunk, row_chunk), row_chunk)
        cols = slice(slab_lo * LANES, slab_hi * LANES)
        xb = x_buf[lax.rem(q, 2)].reshape(row_chunk, x_buf.shape[-1]).astype(BF16)
        res = jnp.dot(xb, w_s[:, cols], preferred_element_type=F32) + b_s[0:1, cols]
        cos = cos_ref[rows, :]
        sin = sin_ref[rows, :]
        for i, (g, kind) in enumerate(ATTN_SLAB_LAYOUT[slab_lo:slab_hi]):
            val = res[:, i * LANES:(i + 1) * LANES]
            if kind == GATE_SLAB:
                gate_s[rows, :] = (val * jax.nn.sigmoid(val)).astype(BF16)
                continue
            if kind in (0, 1):
                val = val * cos + pltpu.roll(val, LANES // 2, axis=1) * sin
            if kind == 0:
                val = val * q_scale
            ref = qkv_ref(g, kind)
            ref[rows, :] = val.astype(ref.dtype)

    def tiles(g, is_first_group):
        d = DILATIONS[g]
        R = CLASSES // d
        L = BAND // R
        nb = S // (d * BAND)

        def make_run(e):
            rho, n = divmod(e, nb)

            def run(j, nn):
                start = (rho + d * j) * A + L * nn
                return slice(start, start + L)
            return n, run

        def gather(ref, run, nn):
            return jnp.concatenate([ref[run(j, nn), :] for j in range(R)], axis=0)

        def scores(n, run):
            q_t = gather(qkv_ref(g, 0), run, n).astype(BF16)
            zero = jnp.zeros_like(q_t)
            qm = jnp.concatenate([jnp.where(head0_q, q_t, zero),
                                  jnp.where(head0_q, zero, q_t)], axis=0)
            n_prev = max(n - 1, 0)
            k_ref = qkv_ref(g, 1)
            kwin = jnp.concatenate([gather(k_ref, run, n_prev), gather(k_ref, run, n)],
                                   axis=0).astype(BF16)
            s = lax.dot_general(qm, kwin, (((1,), (1,)), ((), ())),
                                preferred_element_type=F32)
            return s + bias_ref[g, int(n == 0)]

        def softmax(s):
            m_blk = jnp.max(s, axis=1, keepdims=True)
            p = jnp.exp2(s - m_blk)
            return p.astype(BF16), m_blk

        def finish(n, run, p, m_blk):
            n_prev = max(n - 1, 0)
            v_ref = qkv_ref(g, 2)
            vwin = jnp.concatenate([gather(v_ref, run, n_prev), gather(v_ref, run, n)],
                                   axis=0).astype(BF16)
            zero_v = jnp.zeros_like(vwin)
            v_bd = jnp.concatenate([
                jnp.concatenate([jnp.where(head0_kv, vwin, zero_v), ones_h0], axis=1),
                jnp.concatenate([jnp.where(head0_kv, zero_v, vwin), ones_h1], axis=1)], axis=0)
            p_cat = jnp.concatenate([p[:BAND], p[BAND:]], axis=1)
            pv = jnp.dot(p_cat, v_bd, preferred_element_type=F32)
            o_t = pv[:, :LANES]
            m_t = jnp.where(head0_v, m_blk[:BAND], m_blk[BAND:])
            l_t = pv[:, LANES:]
            if not is_first_group:
                m_old = gather(m_s, run, n)
                m_new = jnp.maximum(m_old, m_t)
                w_old = jnp.exp2(m_old - m_new)
                w_new = jnp.exp2(m_t - m_new)
                m_t = m_new
                l_t = gather(l_s, run, n) * w_old + l_t * w_new
                o_t = gather(acc_s, run, n) * w_old + o_t * w_new
            for j in range(R):
                m_s[run(j, n), :] = m_t[j * L:(j + 1) * L]
                l_s[run(j, n), :] = l_t[j * L:(j + 1) * L]
                acc_s[run(j, n), :] = o_t[j * L:(j + 1) * L]

        for e in range(n_tiles):
            n, run = make_run(e)
            finish(n, run, *softmax(scores(n, run)))

    def proj_all(rc, carry):
        proj_chunk(rc, 0, N_ATTN_SLABS)
        return carry

    lax.fori_loop(0, S // row_chunk, proj_all, 0)
    for g in range(N_GROUPS):
        tiles(g, g == 0)

    for c in range(CLASSES):
        rows = slice(c * A, (c + 1) * A)
        res = (acc_s[rows, :] / l_s[rows, :]) * gate_s[rows, :].astype(F32)
        o_ref[pl.ds(c, A, stride=CLASSES), :] = res


def _attention(x, w_attn, b_attn):
    B, S, D = x.shape
    A = S // CLASSES
    row_chunk = 1024
    kern = functools.partial(_attention_kernel, S=S, row_chunk=row_chunk)
    cos_t, sin_t = _rope_tables(S)
    const2 = pl.BlockSpec((S, LANES), lambda b, p: (0, 0), pipeline_mode=ONCE)
    scratch = [
        pltpu.VMEM((3, S, LANES), F32),
        pltpu.VMEM((3 * (N_GROUPS - 1), S, LANES), BF16),
        pltpu.VMEM((S, LANES), BF16),
        pltpu.VMEM((S, LANES), F32),
        pltpu.VMEM((S, LANES), F32),
        pltpu.VMEM((S, LANES), F32),
        pltpu.VMEM((2, row_chunk // A, A, D), F32),
        pltpu.SemaphoreType.DMA((2, row_chunk // A)),
    ]
    return pl.pallas_call(
        kern,
        grid=(B, N_PAIRS),
        in_specs=[
            pl.BlockSpec(memory_space=pl.ANY),
            pl.BlockSpec((None, D, ATTN_COLS), lambda b, p: (p, 0, 0)),
            pl.BlockSpec((None, SUBLANES, ATTN_COLS), lambda b, p: (p, 0, 0)),
            const2, const2,
            pl.BlockSpec((N_GROUPS, 2, 2 * BAND, 2 * BAND), lambda b, p: (0, 0, 0, 0),
                         pipeline_mode=ONCE),
        ],
        out_specs=pl.BlockSpec((None, S, LANES), lambda b, p: (b, 0, p)),
        out_shape=jax.ShapeDtypeStruct((B, S, ATTN_W), F32),
        scratch_shapes=scratch,
        compiler_params=_cparams(2),
        name="attention",
    )(x.reshape(B, A, CLASSES, D), w_attn, b_attn, cos_t, sin_t, _band_bias())


X_AHEAD = 2
X_SLOTS = X_AHEAD + 1
SCAN_VREGS = 2


def _rglru_kernel(x_hbm, wu_ref, wl_ref, wr_ref, wi_ref, b_in_ref, cw_ref, cb_ref,
                  br_ref, bi_ref, lam_ref, o_ref,
                  w_s, wg_s, u_pad, gate_s, a_p, h_p, x_buf, x_sem, *, S, row_chunk, pitch):
    C = LRU_SLAB
    n_slabs = C // LANES
    n_chunks = S // row_chunk
    n_seg = SCAN_VREGS * SUBLANES
    seg_len = S // n_seg
    seg_per_chunk = row_chunk // seg_len
    u_pad[:, 0:SUBLANES, :] = jnp.zeros((n_slabs, SUBLANES, LANES), F32)

    def slab_cols(ref, col0=0):
        start = pl.multiple_of(col0 + pl.program_id(1) * C, LANES)
        return ref[:, pl.ds(start, C)]

    bu, bl = slab_cols(b_in_ref, COL_U), slab_cols(b_in_ref, COL_GATE_L)
    cw, cb = slab_cols(cw_ref), slab_cols(cb_ref)
    br, bi = slab_cols(br_ref), slab_cols(bi_ref)

    w_s[:, :C] = wu_ref[...].astype(BF16)
    w_s[:, C:] = wl_ref[...].astype(BF16)
    wg_s[...] = jnp.zeros(wg_s.shape, BF16)
    for blk in range(C // LRU_BLOCK):
        rows = slice(blk * LRU_BLOCK, (blk + 1) * LRU_BLOCK)
        wg_s[rows, blk * LRU_BLOCK:(blk + 1) * LRU_BLOCK] = wr_ref[blk].astype(BF16)
        wg_s[rows, C + blk * LRU_BLOCK:C + (blk + 1) * LRU_BLOCK] = wi_ref[blk].astype(BF16)

    step = pl.program_id(0) * pl.num_programs(1) + pl.program_id(1)
    total_chunks = pl.num_programs(0) * pl.num_programs(1) * n_chunks
    chunks_per_batch = pl.num_programs(1) * n_chunks

    def x_copy(q):
        b = lax.div(q, chunks_per_batch)
        rows = pl.ds(pl.multiple_of(lax.rem(q, n_chunks) * row_chunk, row_chunk), row_chunk)
        slot = lax.rem(q, X_SLOTS)
        return pltpu.make_async_copy(x_hbm.at[b, rows, :], x_buf.at[slot], x_sem.at[slot])

    @pl.when(step == 0)
    def _():
        for q in range(X_AHEAD):
            x_copy(q).start()

    def x_arrive(rc):
        q = step * n_chunks + rc
        x_copy(q).wait()

        @pl.when(q + X_AHEAD < total_chunks)
        def _():
            x_copy(q + X_AHEAD).start()

    def proj_chunk(rc, carry):
        r0 = pl.multiple_of(rc * row_chunk, row_chunk)
        q = step * n_chunks + rc
        xb = x_buf[lax.rem(q, X_SLOTS)].astype(BF16)
        res = jnp.dot(xb, w_s[...], preferred_element_type=F32)
        u = res[:, :C] + bu
        for s in range(n_slabs):
            u_pad[s, pl.ds(r0 + SUBLANES, row_chunk), :] = u[:, s * LANES:(s + 1) * LANES]
        gl = res[:, C:] + bl
        gate_s[pl.ds(r0, row_chunk), :] = gl * jax.nn.sigmoid(gl)
        return carry

    neg_lam = -slab_cols(lam_ref)
    softplus = jnp.maximum(neg_lam, 0.0) + jnp.log1p(jnp.exp(-jnp.abs(neg_lam)))
    decay_rate = LRU_C * softplus

    def gate_chunk(rc, carry):
        r0 = pl.multiple_of(rc * row_chunk, row_chunk)
        parts = []
        for s in range(n_slabs):
            lanes = slice(s * LANES, (s + 1) * LANES)
            acc = cb[:, lanes]
            for j in range(CONV_W):
                off = SUBLANES - (CONV_W - 1) + j
                acc = acc + u_pad[s, pl.ds(r0 + off, row_chunk), :] * cw[j:j + 1, lanes]
            parts.append(acc)
        uc = jnp.concatenate(parts, axis=1)
        gates = jnp.dot(uc.astype(BF16), wg_s[...], preferred_element_type=F32)
        r = jax.nn.sigmoid(gates[:, :C] + br)
        i = jax.nn.sigmoid(gates[:, C:] + bi)
        neg_log_a = r * decay_rate
        a = jnp.exp(-neg_log_a)
        one_minus_a2 = jnp.maximum(jnp.tanh(neg_log_a) * (a * a + 1.0), 0.0)
        mult = jnp.exp2(jnp.log(one_minus_a2) * (0.5 * LOG2_E))
        u = mult * (i * uc)
        for k in range(seg_per_chunk):
            start = (rc * seg_per_chunk + k) * pitch
            rows = slice(k * seg_len, (k + 1) * seg_len)
            for s in range(n_slabs):
                lanes = slice(s * LANES, (s + 1) * LANES)
                a_p[s, pl.ds(start, seg_len), :] = a[rows, lanes]
                h_p[s, pl.ds(start, seg_len), :] = u[rows, lanes]
        return carry

    def gate_then_proj(rc, carry):
        x_arrive(rc + 1)
        gate_chunk(rc, carry)
        return proj_chunk(rc + 1, carry)

    x_arrive(0)
    proj_chunk(0, 0)
    lax.fori_loop(0, n_chunks - 1, gate_then_proj, 0)
    gate_chunk(n_chunks - 1, 0)

    chains = [(s, v) for s in range(n_slabs) for v in range(SCAN_VREGS)]

    def scan_step(j, carry):
        out = []
        for c, (s, v) in enumerate(chains):
            rows = pl.ds(v * SUBLANES * pitch + j, SUBLANES, stride=pitch)
            h_prev, p_prev = carry[2 * c], carry[2 * c + 1]
            a = a_p[s, rows, :]
            h = a * h_prev + h_p[s, rows, :]
            p = a * p_prev
            h_p[s, rows, :] = h
            a_p[s, rows, :] = p
            out += [h, p]
        return tuple(out)

    init = (jnp.zeros((SUBLANES, LANES), F32), jnp.ones((SUBLANES, LANES), F32)) * len(chains)
    ends = lax.fori_loop(0, seg_len, scan_step, init, unroll=8)

    for s in range(n_slabs):
        lanes = slice(s * LANES, (s + 1) * LANES)
        h_in = jnp.zeros((1, LANES), F32)
        for seg in range(n_seg):
            c, sub = chains.index((s, seg // SUBLANES)), seg % SUBLANES
            h_end, p_end = ends[2 * c], ends[2 * c + 1]
            rows = slice(seg * pitch, seg * pitch + seg_len)
            h = h_p[s, rows, :] + a_p[s, rows, :] * h_in
            nat = slice(seg * seg_len, (seg + 1) * seg_len)
            o_ref[nat, lanes] = (h * gate_s[nat, lanes]).astype(BF16)
            h_in = h_end[sub:sub + 1] + p_end[sub:sub + 1] * h_in


def _rglru(x, w_in, b_in, conv_w, conv_b, lru_wr, lru_br, lru_wi, lru_bi, lam):
    B, S, D = x.shape
    C = LRU_SLAB
    n_slabs = LRU_W // C
    per = C // LRU_BLOCK
    n_seg = SCAN_VREGS * SUBLANES
    pitch = S // n_seg + SUBLANES // 2
    row_chunk = 512
    kern = functools.partial(_rglru_kernel, S=S, row_chunk=row_chunk, pitch=pitch)
    col = lambda c0: (lambda b, c: (0, c0 // C + c))
    whole = lambda a: pl.BlockSpec(a.shape, lambda b, c: (0, 0), pipeline_mode=ONCE)
    blocks = lambda: pl.BlockSpec((per, LRU_BLOCK, LRU_BLOCK), lambda b, c: (c, 0, 0))
    return pl.pallas_call(
        kern,
        grid=(B, n_slabs),
        in_specs=[
            pl.BlockSpec(memory_space=pl.ANY),
            pl.BlockSpec((D, C), col(COL_U)), pl.BlockSpec((D, C), col(COL_GATE_L)),
            blocks(), blocks(),
            whole(b_in), whole(conv_w), whole(conv_b), whole(lru_br), whole(lru_bi), whole(lam),
        ],
        out_specs=pl.BlockSpec((None, S, C), lambda b, c: (b, 0, c)),
        out_shape=jax.ShapeDtypeStruct((B, S, LRU_W), BF16),
        scratch_shapes=[
            pltpu.VMEM((D, 2 * C), BF16),
            pltpu.VMEM((C, 2 * C), BF16),
            pltpu.VMEM((C // LANES, S + SUBLANES, LANES), F32),
            pltpu.VMEM((S, C), F32),
            pltpu.VMEM((C // LANES, n_seg * pitch, LANES), F32),
            pltpu.VMEM((C // LANES, n_seg * pitch, LANES), F32),
            pltpu.VMEM((X_SLOTS, row_chunk, D), F32),
            pltpu.SemaphoreType.DMA((X_SLOTS,)),
        ],
        compiler_params=_cparams(2),
        name="rglru",
    )(x, w_in, w_in, lru_wr, lru_wi, b_in, conv_w, conv_b, lru_br, lru_bi, lam)


MERGE_SUBTILES = 2


def _merge_out_kernel(x_ref, attn_ref, hg_ref, wga_ref, wgb_ref, bga_ref, bgb_ref,
                      wa_ref, wl_ref, wo_ref, bo_ref, lng_ref, lnb_ref, o_ref,
                      wg_s, wa_s, wl_s, wo_s):
    D = x_ref.shape[1]

    @pl.when(pl.program_id(0) == 0)
    def _():
        rows = 256
        for r0 in range(0, D, rows):
            sl = slice(r0, r0 + rows)
            wg_s[sl, :D] = wga_ref[sl, :].astype(BF16)
            wg_s[sl, D:] = wgb_ref[sl, :].astype(BF16)
            wl_s[sl, :] = wl_ref[sl, :].astype(BF16)
            wo_s[sl, :] = wo_ref[sl, :].astype(BF16)
        for r0 in range(0, ATTN_W, rows):
            sl = slice(r0, r0 + rows)
            wa_s[sl, :] = wa_ref[sl, :].astype(BF16)

    half = x_ref.shape[0] // MERGE_SUBTILES
    for hb in range(MERGE_SUBTILES):
        rows = slice(hb * half, (hb + 1) * half)
        x = x_ref[rows, :]
        xb = x.astype(BF16)
        gate_a = jax.nn.sigmoid(
            jnp.dot(xb, wg_s[:, :D], preferred_element_type=F32) + bga_ref[...])
        gate_b = jax.nn.sigmoid(
            jnp.dot(xb, wg_s[:, D:], preferred_element_type=F32) + bgb_ref[...])
        y_a = jnp.dot(attn_ref[rows, :].astype(BF16), wa_s[...], preferred_element_type=F32)
        y_b = jnp.dot(hg_ref[rows, :], wl_s[...], preferred_element_type=F32)
        merged = gate_a * y_a + gate_b * y_b
        out = jnp.dot(merged.astype(BF16), wo_s[...], preferred_element_type=F32) + bo_ref[...]
        y = ALPHA * x + out
        mu = jnp.mean(y, axis=1, keepdims=True)
        yc = y - mu
        var = jnp.mean(yc * yc, axis=1, keepdims=True)
        o_ref[rows, :] = yc * lax.rsqrt(var + LN_EPS) * lng_ref[...] + lnb_ref[...]


def _merge_out(x2, attn2, hg2, w_in, b_in, w_a, w_l, w_o, b_o, ln_g, ln_b, tm=512):
    T, D = x2.shape
    once = lambda shape, idx: pl.BlockSpec(shape, lambda i: idx, pipeline_mode=ONCE)
    merge_blk = COL_MERGE // D
    return pl.pallas_call(
        _merge_out_kernel,
        grid=(T // tm,),
        in_specs=[
            pl.BlockSpec((tm, D), lambda i: (i, 0)),
            pl.BlockSpec((tm, ATTN_W), lambda i: (i, 0)),
            pl.BlockSpec((tm, LRU_W), lambda i: (i, 0)),
            once((D, D), (0, merge_blk)), once((D, D), (0, merge_blk + 1)),
            once((1, D), (0, merge_blk)), once((1, D), (0, merge_blk + 1)),
            once((ATTN_W, D), (0, 0)), once((LRU_W, D), (0, 0)), once((D, D), (0, 0)),
            once((1, D), (0, 0)), once((1, D), (0, 0)), once((1, D), (0, 0)),
        ],
        out_specs=pl.BlockSpec((tm, D), lambda i: (i, 0)),
        out_shape=jax.ShapeDtypeStruct((T, D), F32),
        scratch_shapes=[
            pltpu.VMEM((D, 2 * D), BF16), pltpu.VMEM((ATTN_W, D), BF16),
            pltpu.VMEM((LRU_W, D), BF16), pltpu.VMEM((D, D), BF16),
        ],
        compiler_params=_cparams(1),
        name="merge_out",
    )(x2, attn2, hg2, w_in, w_in, b_in, b_in, w_a, w_l, w_o, b_o, ln_g, ln_b)


def kernel(x, w_in, b_in, conv_w, conv_b, lru_wr, lru_br, lru_wi, lru_bi, lru_lambda,
           w_attn_proj, w_lru_proj, w_out, b_out, ln_gain, ln_bias):
    B, S, D = x.shape
    assert w_in.shape[0] == DEPTH and S % (CLASSES * BAND) == 0 and D == D_MODEL
    assert COL_MERGE % D == 0 and w_in.shape[2] == COL_MERGE + 2 * D
    row = lambda v: v.reshape(1, -1)
    for l in range(DEPTH):
        wl, bl = w_in[l], row(b_in[l])
        attn = _attention(x, *_attn_weights(wl, bl))
        hg = _rglru(x, wl, bl, conv_w[l], row(conv_b[l]), lru_wr[l], row(lru_br[l]),
                    lru_wi[l], row(lru_bi[l]), row(lru_lambda[l]))
        out = _merge_out(
            x.reshape(B * S, D), attn.reshape(B * S, ATTN_W), hg.reshape(B * S, LRU_W),
            wl, bl, w_attn_proj[l], w_lru_proj[l], w_out[l],
            row(b_out[l]), row(ln_gain[l]), row(ln_bias[l]))
        x = out.reshape(B, S, D)
    return x
```

```python
import functools
import math

import jax
import jax.numpy as jnp
import numpy as np
from jax import lax
from jax.experimental import pallas as pl
from jax.experimental.pallas import tpu as pltpu

LANES = 128
SUBLANES = 8
BF16_ROWS = 16
VMEM_LIMIT_BYTES = 56 * 1024 * 1024

D_MODEL = 1024
HEAD_DIM = 64
HALF = HEAD_DIM // 2
HEADS_PER_GROUP = 8
DILATIONS = (1, 4, 16)
N_GROUPS = len(DILATIONS)
BAND = 128
CLASSES = 16
QKV_W = N_GROUPS * HEADS_PER_GROUP * HEAD_DIM
ATTN_W = HEADS_PER_GROUP * HEAD_DIM
N_PAIRS = ATTN_W // LANES
LRU_W = 1024
LRU_BLOCK = 64
LRU_SLAB = 256
CONV_W = 4
LRU_C = 8.0
ROPE_THETA = 10000.0
LOG2_E = math.log2(math.e)
NEG_INF = -1e30
LN_EPS = 1e-5
DEPTH = 1
ALPHA = (2.0 * DEPTH) ** 0.25
GATE_SLAB = 3
ATTN_SLAB_LAYOUT = ([(0, 0), (0, 1), (0, 2), (None, GATE_SLAB)]
                    + [(g, kind) for g in range(1, N_GROUPS) for kind in range(3)])
N_ATTN_SLABS = len(ATTN_SLAB_LAYOUT)
ATTN_COLS = N_ATTN_SLABS * LANES
COL_K = QKV_W
COL_V = 2 * QKV_W
COL_GATE_A = 3 * QKV_W
COL_U = COL_GATE_A + ATTN_W
COL_GATE_L = COL_U + LRU_W
COL_MERGE = COL_GATE_L + LRU_W

BF16 = jnp.bfloat16
F32 = jnp.float32
ONCE = pl.Buffered(1)


def _cparams(n_axes, flags=None):
    return pltpu.CompilerParams(dimension_semantics=("arbitrary",) * n_axes,
                                vmem_limit_bytes=VMEM_LIMIT_BYTES, flags=flags)


def _band_bias():
    out = np.zeros((N_GROUPS, 2, 2 * BAND, 2 * BAND), np.float32)
    i = np.arange(BAND)
    for g, d in enumerate(DILATIONS):
        R = CLASSES // d
        L = BAND // R
        m = R * (i % L) + i // L
        m_k = np.concatenate([m - BAND, m])
        dist = m[:, None] - m_k[None, :]
        band = (dist >= 0) & (dist <= BAND)
        for first in range(2):
            valid = band & ((np.arange(2 * BAND) >= BAND)[None, :] | (first == 0))
            bias = np.where(valid, 0.0, NEG_INF).astype(np.float32)
            out[g, first] = np.concatenate([bias, bias], axis=0)
    return out


def _rope_tables(S):
    A = S // CLASSES
    r = np.arange(S)
    pos = (CLASSES * (r % A) + r // A).astype(np.float64)
    inv_freq = ROPE_THETA ** (-np.arange(HALF, dtype=np.float64) / HALF)
    ang = pos[:, None] * inv_freq[None, :]
    cos_t = np.tile(np.cos(ang), (1, LANES // HALF))
    sin = np.sin(ang)
    sin_t = np.concatenate([-sin, -sin, sin, sin], axis=1)
    return cos_t.astype(np.float32), sin_t.astype(np.float32)


def _pair_interleave(w):
    lane = lax.broadcasted_iota(jnp.int32, w.shape, 1)
    from_right = pltpu.roll(w, LANES - HALF, axis=1)
    from_left = pltpu.roll(w, HALF, axis=1)
    quarter = lane // HALF
    return jnp.where(quarter == 1, from_right, jnp.where(quarter == 2, from_left, w))


def _attn_weights_kernel(*refs):
    w_refs = refs[:N_ATTN_SLABS]
    b_refs = refs[N_ATTN_SLABS:2 * N_ATTN_SLABS]
    w_out, b_out = refs[2 * N_ATTN_SLABS:]
    D = w_out.shape[0]
    w_rows = 256
    for i, (g, kind) in enumerate(ATTN_SLAB_LAYOUT):
        is_qk = kind in (0, 1)
        cols = slice(i * LANES, (i + 1) * LANES)
        for r0 in range(0, D, w_rows):
            w = w_refs[i][r0:r0 + w_rows, :]
            w_out[r0:r0 + w_rows, cols] = (_pair_interleave(w) if is_qk else w).astype(BF16)
        b = jnp.broadcast_to(b_refs[i][...], (SUBLANES, LANES))
        b_out[:, cols] = _pair_interleave(b) if is_qk else b


def _attn_weights(w_in, b_in):
    D = w_in.shape[0]
    kind_col = (0, COL_K, COL_V)
    slab0 = [COL_GATE_A // LANES if kind == GATE_SLAB else (kind_col[kind] + g * ATTN_W) // LANES
             for g, kind in ATTN_SLAB_LAYOUT]
    w_specs = [pl.BlockSpec((D, LANES), functools.partial(lambda p, s: (0, s + p), s=s))
               for s in slab0]
    b_specs = [pl.BlockSpec((1, LANES), functools.partial(lambda p, s: (0, s + p), s=s))
               for s in slab0]
    return pl.pallas_call(
        _attn_weights_kernel,
        grid=(N_PAIRS,),
        in_specs=w_specs + b_specs,
        out_specs=[pl.BlockSpec((None, D, ATTN_COLS), lambda p: (p, 0, 0)),
                   pl.BlockSpec((None, SUBLANES, ATTN_COLS), lambda p: (p, 0, 0))],
        out_shape=[jax.ShapeDtypeStruct((N_PAIRS, D, ATTN_COLS), BF16),
                   jax.ShapeDtypeStruct((N_PAIRS, SUBLANES, ATTN_COLS), F32)],
        compiler_params=_cparams(1),
        name="attn_weights",
    )(*([w_in] * N_ATTN_SLABS), *([b_in] * N_ATTN_SLABS))


def _attention_kernel(x_hbm, w_s, b_s, cos_ref, sin_ref, bias_ref, o_ref,
                      qkv0_s, qkv_s, gate_s, acc_s, m_s, l_s, x_buf, x_sem, *, S, row_chunk):
    A = S // CLASSES
    per = row_chunk // A
    n_chunks = S // row_chunk

    step = pl.program_id(0) * pl.num_programs(1) + pl.program_id(1)
    total_chunks = pl.num_programs(0) * pl.num_programs(1) * n_chunks

    def x_copies(q):
        b = lax.div(q, pl.num_programs(1) * n_chunks)
        c0 = lax.rem(q, n_chunks) * per
        slot = lax.rem(q, 2)
        return [pltpu.make_async_copy(x_hbm.at[b, :, c0 + i, :], x_buf.at[slot, i],
                                      x_sem.at[slot, i]) for i in range(per)]

    @pl.when(step == 0)
    def _():
        for cp in x_copies(0):
            cp.start()

    def qkv_ref(g, i):
        return qkv0_s.at[i] if g == 0 else qkv_s.at[3 * (g - 1) + i]

    n_tiles = S // BAND
    lane = lax.broadcasted_iota(jnp.int32, (BAND, LANES), 1)
    head0_q = ((lane // HALF) % 2) == 0
    head0_v = lane < HEAD_DIM

    def head_masks(keys):
        h0 = lax.broadcasted_iota(jnp.int32, (keys, LANES), 1) < HEAD_DIM
        return h0, h0.astype(BF16), 1 - h0.astype(BF16)
    q_scale = (HEAD_DIM ** -0.5) * LOG2_E

    def proj_chunk(rc, slab_lo, slab_hi):
        q = step * n_chunks + rc
        for cp in x_copies(q):
            cp.wait()

        @pl.when(q + 1 < total_chunks)
        def _():
            for cp in x_copies(q + 1):
                cp.start()

        rows = pl.ds(pl.multiple_of(rc * row_chunk, row_chunk), row_chunk)
        cols = slice(slab_lo * LANES, slab_hi * LANES)
        xb = x_buf[lax.rem(q, 2)].reshape(row_chunk, x_buf.shape[-1]).astype(BF16)
        res = jnp.dot(xb, w_s[:, cols], preferred_element_type=F32) + b_s[0:1, cols]
        cos = cos_ref[rows, :]
        sin = sin_ref[rows, :]
        for i, (g, kind) in enumerate(ATTN_SLAB_LAYOUT[slab_lo:slab_hi]):
            val = res[:, i * LANES:(i + 1) * LANES]
            if kind == GATE_SLAB:
                gate_s[rows, :] = (val * jax.nn.sigmoid(val)).astype(BF16)
                continue
            if kind in (0, 1):
                val = val * cos + pltpu.roll(val, LANES // 2, axis=1) * sin
            if kind == 0:
                val = val * q_scale
            ref = qkv_ref(g, kind)
            ref[rows, :] = val.astype(ref.dtype)

    def tiles(g, is_first_group):
        d = DILATIONS[g]
        R = CLASSES // d
        L = BAND // R
        nb = S // (d * BAND)

        def make_run(e):
            rho, n = divmod(e, nb)

            def run(j, nn):
                start = (rho + d * j) * A + L * nn
                return slice(start, start + L)
            return n, run

        def gather(ref, run, nn):
            return jnp.concatenate([ref[run(j, nn), :] for j in range(R)], axis=0)

        def key_window(ref, run, n):
            cur = gather(ref, run, n)
            if n == 0:
                return cur.astype(BF16)
            return jnp.concatenate([gather(ref, run, n - 1), cur], axis=0).astype(BF16)

        def scores(n, run):
            q_t = gather(qkv_ref(g, 0), run, n).astype(BF16)
            zero = jnp.zeros_like(q_t)
            qm = jnp.concatenate([jnp.where(head0_q, q_t, zero),
                                  jnp.where(head0_q, zero, q_t)], axis=0)
            kwin = key_window(qkv_ref(g, 1), run, n)
            s = lax.dot_general(qm, kwin, (((1,), (1,)), ((), ())),
                                preferred_element_type=F32)
            if n == 0:
                return s + bias_ref[g, 1, :, BAND:]
            return s + bias_ref[g, 0]

        def softmax(s):
            m_blk = jnp.max(s, axis=1, keepdims=True)
            p = jnp.exp2(s - m_blk)
            return p.astype(BF16), m_blk

        def finish(n, run, p, m_blk):
            vwin = key_window(qkv_ref(g, 2), run, n)
            keys = vwin.shape[0]
            zero_v = jnp.zeros_like(vwin)
            h0, ones_h0, ones_h1 = head_masks(keys)
            v_bd = jnp.concatenate([
                jnp.concatenate([jnp.where(h0, vwin, zero_v), ones_h0], axis=1),
                jnp.concatenate([jnp.where(h0, zero_v, vwin), ones_h1], axis=1)], axis=0)
            p_cat = jnp.concatenate([p[:BAND], p[BAND:]], axis=1)
            pv = jnp.dot(p_cat, v_bd, preferred_element_type=F32)
            o_t = pv[:, :LANES]
            m_t = jnp.where(head0_v, m_blk[:BAND], m_blk[BAND:])
            l_t = pv[:, LANES:]
            if not is_first_group:
                m_old = gather(m_s, run, n)
                m_new = jnp.maximum(m_old, m_t)
                w_old = jnp.exp2(m_old - m_new)
                w_new = jnp.exp2(m_t - m_new)
                m_t = m_new
                l_t = gather(l_s, run, n) * w_old + l_t * w_new
                o_t = gather(acc_s, run, n) * w_old + o_t * w_new
            for j in range(R):
                m_s[run(j, n), :] = m_t[j * L:(j + 1) * L]
                l_s[run(j, n), :] = l_t[j * L:(j + 1) * L]
                acc_s[run(j, n), :] = o_t[j * L:(j + 1) * L]

        for e in range(n_tiles):
            n, run = make_run(e)
            finish(n, run, *softmax(scores(n, run)))

    def proj_all(rc, carry):
        proj_chunk(rc, 0, N_ATTN_SLABS)
        return carry

    lax.fori_loop(0, S // row_chunk, proj_all, 0)
    for g in range(N_GROUPS):
        tiles(g, g == 0)

    for c in range(CLASSES):
        rows = slice(c * A, (c + 1) * A)
        res = (acc_s[rows, :] / l_s[rows, :]) * gate_s[rows, :].astype(F32)
        o_ref[pl.ds(c, A, stride=CLASSES), :] = res


def _attention(x, w_attn, b_attn):
    B, S, D = x.shape
    A = S // CLASSES
    row_chunk = 1024
    kern = functools.partial(_attention_kernel, S=S, row_chunk=row_chunk)
    cos_t, sin_t = _rope_tables(S)
    const2 = pl.BlockSpec((S, LANES), lambda b, p: (0, 0), pipeline_mode=ONCE)
    scratch = [
        pltpu.VMEM((3, S, LANES), F32),
        pltpu.VMEM((3 * (N_GROUPS - 1), S, LANES), BF16),
        pltpu.VMEM((S, LANES), BF16),
        pltpu.VMEM((S, LANES), F32),
        pltpu.VMEM((S, LANES), F32),
        pltpu.VMEM((S, LANES), F32),
        pltpu.VMEM((2, row_chunk // A, A, D), F32),
        pltpu.SemaphoreType.DMA((2, row_chunk // A)),
    ]
    return pl.pallas_call(
        kern,
        grid=(B, N_PAIRS),
        in_specs=[
            pl.BlockSpec(memory_space=pl.ANY),
            pl.BlockSpec((None, D, ATTN_COLS), lambda b, p: (p, 0, 0)),
            pl.BlockSpec((None, SUBLANES, ATTN_COLS), lambda b, p: (p, 0, 0)),
            const2, const2,
            pl.BlockSpec((N_GROUPS, 2, 2 * BAND, 2 * BAND), lambda b, p: (0, 0, 0, 0),
                         pipeline_mode=ONCE),
        ],
        out_specs=pl.BlockSpec((None, S, LANES), lambda b, p: (b, 0, p)),
        out_shape=jax.ShapeDtypeStruct((B, S, ATTN_W), F32),
        scratch_shapes=scratch,
        compiler_params=_cparams(2),
        name="attention",
    )(x.reshape(B, A, CLASSES, D), w_attn, b_attn, cos_t, sin_t, _band_bias())


X_AHEAD = 2
X_SLOTS = X_AHEAD + 1
SCAN_VREGS = 2


def _rglru_kernel(x_hbm, wu_ref, wl_ref, wr_ref, wi_ref, b_in_ref, cw_ref, cb_ref,
                  br_ref, bi_ref, lam_ref, o_ref,
                  w_s, wg_s, u_pad, gate_s, a_p, h_p, x_buf, x_sem, *, S, row_chunk, pitch):
    C = LRU_SLAB
    n_slabs = C // LANES
    n_chunks = S // row_chunk
    n_seg = SCAN_VREGS * SUBLANES
    seg_len = S // n_seg
    seg_per_chunk = row_chunk // seg_len
    u_pad[:, 0:SUBLANES, :] = jnp.zeros((n_slabs, SUBLANES, LANES), F32)

    def slab_cols(ref, col0=0):
        start = pl.multiple_of(col0 + pl.program_id(1) * C, LANES)
        return ref[:, pl.ds(start, C)]

    bu, bl = slab_cols(b_in_ref, COL_U), slab_cols(b_in_ref, COL_GATE_L)
    cw, cb = slab_cols(cw_ref), slab_cols(cb_ref)
    br, bi = slab_cols(br_ref), slab_cols(bi_ref)

    w_s[:, :C] = wu_ref[...].astype(BF16)
    w_s[:, C:] = wl_ref[...].astype(BF16)
    wg_s[...] = jnp.zeros(wg_s.shape, BF16)
    for blk in range(C // LRU_BLOCK):
        rows = slice(blk * LRU_BLOCK, (blk + 1) * LRU_BLOCK)
        wg_s[rows, blk * LRU_BLOCK:(blk + 1) * LRU_BLOCK] = wr_ref[blk].astype(BF16)
        wg_s[rows, C + blk * LRU_BLOCK:C + (blk + 1) * LRU_BLOCK] = wi_ref[blk].astype(BF16)

    step = pl.program_id(0) * pl.num_programs(1) + pl.program_id(1)
    total_chunks = pl.num_programs(0) * pl.num_programs(1) * n_chunks
    chunks_per_batch = pl.num_programs(1) * n_chunks

    def x_copy(q):
        b = lax.div(q, chunks_per_batch)
        rows = pl.ds(pl.multiple_of(lax.rem(q, n_chunks) * row_chunk, row_chunk), row_chunk)
        slot = lax.rem(q, X_SLOTS)
        return pltpu.make_async_copy(x_hbm.at[b, rows, :], x_buf.at[slot], x_sem.at[slot])

    @pl.when(step == 0)
    def _():
        for q in range(X_AHEAD):
            x_copy(q).start()

    def x_arrive(rc):
        q = step * n_chunks + rc
        x_copy(q).wait()

        @pl.when(q + X_AHEAD < total_chunks)
        def _():
            x_copy(q + X_AHEAD).start()

    def proj_chunk(rc, carry):
        r0 = pl.multiple_of(rc * row_chunk, row_chunk)
        q = step * n_chunks + rc
        xb = x_buf[lax.rem(q, X_SLOTS)].astype(BF16)
        res = jnp.dot(xb, w_s[...], preferred_element_type=F32)
        u = res[:, :C] + bu
        for s in range(n_slabs):
            u_pad[s, pl.ds(r0 + SUBLANES, row_chunk), :] = u[:, s * LANES:(s + 1) * LANES]
        gl = res[:, C:] + bl
        gate_s[pl.ds(r0, row_chunk), :] = gl * jax.nn.sigmoid(gl)
        return carry

    neg_lam = -slab_cols(lam_ref)
    softplus = jnp.maximum(neg_lam, 0.0) + jnp.log1p(jnp.exp(-jnp.abs(neg_lam)))
    decay_rate = LRU_C * softplus

    def gate_chunk(rc, carry):
        r0 = pl.multiple_of(rc * row_chunk, row_chunk)
        parts = []
        for s in range(n_slabs):
            lanes = slice(s * LANES, (s + 1) * LANES)
            acc = cb[:, lanes]
            for j in range(CONV_W):
                off = SUBLANES - (CONV_W - 1) + j
                acc = acc + u_pad[s, pl.ds(r0 + off, row_chunk), :] * cw[j:j + 1, lanes]
            parts.append(acc)
        uc = jnp.concatenate(parts, axis=1)
        gates = jnp.dot(uc.astype(BF16), wg_s[...], preferred_element_type=F32)
        r = jax.nn.sigmoid(gates[:, :C] + br)
        i = jax.nn.sigmoid(gates[:, C:] + bi)
        neg_log_a = r * decay_rate
        a = jnp.exp(-neg_log_a)
        one_minus_a2 = jnp.maximum(jnp.tanh(neg_log_a) * (a * a + 1.0), 0.0)
        mult = jnp.exp2(jnp.log(one_minus_a2) * (0.5 * LOG2_E))
        u = mult * (i * uc)
        for k in range(seg_per_chunk):
            start = (rc * seg_per_chunk + k) * pitch
            rows = slice(k * seg_len, (k + 1) * seg_len)
            for s in range(n_slabs):
                lanes = slice(s * LANES, (s + 1) * LANES)
                a_p[s, pl.ds(start, seg_len), :] = a[rows, lanes]
                h_p[s, pl.ds(start, seg_len), :] = u[rows, lanes]
        return carry

    def gate_then_proj(rc, carry):
        x_arrive(rc + 1)
        gate_chunk(rc, carry)
        return proj_chunk(rc + 1, carry)

    x_arrive(0)
    proj_chunk(0, 0)
    lax.fori_loop(0, n_chunks - 1, gate_then_proj, 0)
    gate_chunk(n_chunks - 1, 0)

    chains = [(s, v) for s in range(n_slabs) for v in range(SCAN_VREGS)]

    def scan_step(j, carry):
        out = []
        for c, (s, v) in enumerate(chains):
            rows = pl.ds(v * SUBLANES * pitch + j, SUBLANES, stride=pitch)
            h_prev, p_prev = carry[2 * c], carry[2 * c + 1]
            a = a_p[s, rows, :]
            h = a * h_prev + h_p[s, rows, :]
            p = a * p_prev
            h_p[s, rows, :] = h
            a_p[s, rows, :] = p
            out += [h, p]
        return tuple(out)

    init = (jnp.zeros((SUBLANES, LANES), F32), jnp.ones((SUBLANES, LANES), F32)) * len(chains)
    ends = lax.fori_loop(0, seg_len, scan_step, init, unroll=8)

    for s in range(n_slabs):
        lanes = slice(s * LANES, (s + 1) * LANES)
        h_in = jnp.zeros((1, LANES), F32)
        for seg in range(n_seg):
            c, sub = chains.index((s, seg // SUBLANES)), seg % SUBLANES
            h_end, p_end = ends[2 * c], ends[2 * c + 1]
            rows = slice(seg * pitch, seg * pitch + seg_len)
            h = h_p[s, rows, :] + a_p[s, rows, :] * h_in
            nat = slice(seg * seg_len, (seg + 1) * seg_len)
            o_ref[nat, lanes] = (h * gate_s[nat, lanes]).astype(BF16)
            h_in = h_end[sub:sub + 1] + p_end[sub:sub + 1] * h_in


def _rglru(x, w_in, b_in, conv_w, conv_b, lru_wr, lru_br, lru_wi, lru_bi, lam):
    B, S, D = x.shape
    C = LRU_SLAB
    n_slabs = LRU_W // C
    per = C // LRU_BLOCK
    n_seg = SCAN_VREGS * SUBLANES
    pitch = S // n_seg + SUBLANES // 2
    row_chunk = 512
    kern = functools.partial(_rglru_kernel, S=S, row_chunk=row_chunk, pitch=pitch)
    col = lambda c0: (lambda b, c: (0, c0 // C + c))
    whole = lambda a: pl.BlockSpec(a.shape, lambda b, c: (0, 0), pipeline_mode=ONCE)
    blocks = lambda: pl.BlockSpec((per, LRU_BLOCK, LRU_BLOCK), lambda b, c: (c, 0, 0))
    return pl.pallas_call(
        kern,
        grid=(B, n_slabs),
        in_specs=[
            pl.BlockSpec(memory_space=pl.ANY),
            pl.BlockSpec((D, C), col(COL_U)), pl.BlockSpec((D, C), col(COL_GATE_L)),
            blocks(), blocks(),
            whole(b_in), whole(conv_w), whole(conv_b), whole(lru_br), whole(lru_bi), whole(lam),
        ],
        out_specs=pl.BlockSpec((None, S, C), lambda b, c: (b, 0, c)),
        out_shape=jax.ShapeDtypeStruct((B, S, LRU_W), BF16),
        scratch_shapes=[
            pltpu.VMEM((D, 2 * C), BF16),
            pltpu.VMEM((C, 2 * C), BF16),
            pltpu.VMEM((C // LANES, S + SUBLANES, LANES), F32),
            pltpu.VMEM((S, C), F32),
            pltpu.VMEM((C // LANES, n_seg * pitch, LANES), F32),
            pltpu.VMEM((C // LANES, n_seg * pitch, LANES), F32),
            pltpu.VMEM((X_SLOTS, row_chunk, D), F32),
            pltpu.SemaphoreType.DMA((X_SLOTS,)),
        ],
        compiler_params=_cparams(2),
        name="rglru",
    )(x, w_in, w_in, lru_wr, lru_wi, b_in, conv_w, conv_b, lru_br, lru_bi, lam)


MERGE_SUBTILES = 2


def _merge_out_kernel(x_ref, attn_ref, hg_ref, wga_ref, wgb_ref, bga_ref, bgb_ref,
                      wa_ref, wl_ref, wo_ref, bo_ref, lng_ref, lnb_ref, o_ref,
                      wg_s, wa_s, wl_s, wo_s):
    D = x_ref.shape[1]

    @pl.when(pl.program_id(0) == 0)
    def _():
        rows = 256
        for r0 in range(0, D, rows):
            sl = slice(r0, r0 + rows)
            wg_s[sl, :D] = wga_ref[sl, :].astype(BF16)
            wg_s[sl, D:] = wgb_ref[sl, :].astype(BF16)
            wl_s[sl, :] = wl_ref[sl, :].astype(BF16)
            wo_s[sl, :] = wo_ref[sl, :].astype(BF16)
        for r0 in range(0, ATTN_W, rows):
            sl = slice(r0, r0 + rows)
            wa_s[sl, :] = wa_ref[sl, :].astype(BF16)

    half = x_ref.shape[0] // MERGE_SUBTILES
    for hb in range(MERGE_SUBTILES):
        rows = slice(hb * half, (hb + 1) * half)
        x = x_ref[rows, :]
        xb = x.astype(BF16)
        gate_a = jax.nn.sigmoid(
            jnp.dot(xb, wg_s[:, :D], preferred_element_type=F32) + bga_ref[...])
        gate_b = jax.nn.sigmoid(
            jnp.dot(xb, wg_s[:, D:], preferred_element_type=F32) + bgb_ref[...])
        y_a = jnp.dot(attn_ref[rows, :].astype(BF16), wa_s[...], preferred_element_type=F32)
        y_b = jnp.dot(hg_ref[rows, :], wl_s[...], preferred_element_type=F32)
        merged = gate_a * y_a + gate_b * y_b
        out = jnp.dot(merged.astype(BF16), wo_s[...], preferred_element_type=F32) + bo_ref[...]
        y = ALPHA * x + out
        mu = jnp.mean(y, axis=1, keepdims=True)
        yc = y - mu
        var = jnp.mean(yc * yc, axis=1, keepdims=True)
        o_ref[rows, :] = yc * lax.rsqrt(var + LN_EPS) * lng_ref[...] + lnb_ref[...]


def _merge_out(x2, attn2, hg2, w_in, b_in, w_a, w_l, w_o, b_o, ln_g, ln_b, tm=512):
    T, D = x2.shape
    once = lambda shape, idx: pl.BlockSpec(shape, lambda i: idx, pipeline_mode=ONCE)
    merge_blk = COL_MERGE // D
    return pl.pallas_call(
        _merge_out_kernel,
        grid=(T // tm,),
        in_specs=[
            pl.BlockSpec((tm, D), lambda i: (i, 0)),
            pl.BlockSpec((tm, ATTN_W), lambda i: (i, 0)),
            pl.BlockSpec((tm, LRU_W), lambda i: (i, 0)),
            once((D, D), (0, merge_blk)), once((D, D), (0, merge_blk + 1)),
            once((1, D), (0, merge_blk)), once((1, D), (0, merge_blk + 1)),
            once((ATTN_W, D), (0, 0)), once((LRU_W, D), (0, 0)), once((D, D), (0, 0)),
            once((1, D), (0, 0)), once((1, D), (0, 0)), once((1, D), (0, 0)),
        ],
        out_specs=pl.BlockSpec((tm, D), lambda i: (i, 0)),
        out_shape=jax.ShapeDtypeStruct((T, D), F32),
        scratch_shapes=[
            pltpu.VMEM((D, 2 * D), BF16), pltpu.VMEM((ATTN_W, D), BF16),
            pltpu.VMEM((LRU_W, D), BF16), pltpu.VMEM((D, D), BF16),
        ],
        compiler_params=_cparams(1),
        name="merge_out",
    )(x2, attn2, hg2, w_in, w_in, b_in, b_in, w_a, w_l, w_o, b_o, ln_g, ln_b)


def kernel(x, w_in, b_in, conv_w, conv_b, lru_wr, lru_br, lru_wi, lru_bi, lru_lambda,
           w_attn_proj, w_lru_proj, w_out, b_out, ln_gain, ln_bias):
    B, S, D = x.shape
    assert w_in.shape[0] == DEPTH and S % (CLASSES * BAND) == 0 and D == D_MODEL
    assert COL_MERGE % D == 0 and w_in.shape[2] == COL_MERGE + 2 * D
    row = lambda v: v.reshape(1, -1)
    for l in range(DEPTH):
        wl, bl = w_in[l], row(b_in[l])
        attn = _attention(x, *_attn_weights(wl, bl))
        hg = _rglru(x, wl, bl, conv_w[l], row(conv_b[l]), lru_wr[l], row(lru_br[l]),
                    lru_wi[l], row(lru_bi[l]), row(lru_lambda[l]))
        out = _merge_out(
            x.reshape(B * S, D), attn.reshape(B * S, ATTN_W), hg.reshape(B * S, LRU_W),
            wl, bl, w_attn_proj[l], w_lru_proj[l], w_out[l],
            row(b_out[l]), row(ln_gain[l]), row(ln_bias[l]))
        x = out.reshape(B, S, D)
    return x
```

```python
import functools
import math

import jax
import jax.numpy as jnp
import numpy as np
from jax import lax
from jax.experimental import pallas as pl
from jax.experimental.pallas import tpu as pltpu

LANES = 128
SUBLANES = 8
BF16_ROWS = 16
VMEM_LIMIT_BYTES = 56 * 1024 * 1024

D_MODEL = 1024
HEAD_DIM = 64
HALF = HEAD_DIM // 2
HEADS_PER_GROUP = 8
DILATIONS = (1, 4, 16)
N_GROUPS = len(DILATIONS)
BAND = 128
CLASSES = 16
QKV_W = N_GROUPS * HEADS_PER_GROUP * HEAD_DIM
ATTN_W = HEADS_PER_GROUP * HEAD_DIM
N_PAIRS = ATTN_W // LANES
LRU_W = 1024
LRU_BLOCK = 64
LRU_SLAB = 256
CONV_W = 4
LRU_C = 8.0
ROPE_THETA = 10000.0
LOG2_E = math.log2(math.e)
NEG_INF = -1e30
LN_EPS = 1e-5
DEPTH = 1
ALPHA = (2.0 * DEPTH) ** 0.25
GATE_SLAB = 3
ATTN_SLAB_LAYOUT = ([(0, 0), (0, 1), (0, 2), (None, GATE_SLAB)]
                    + [(g, kind) for g in range(1, N_GROUPS) for kind in range(3)])
N_ATTN_SLABS = len(ATTN_SLAB_LAYOUT)
ATTN_COLS = N_ATTN_SLABS * LANES
COL_K = QKV_W
COL_V = 2 * QKV_W
COL_GATE_A = 3 * QKV_W
COL_U = COL_GATE_A + ATTN_W
COL_GATE_L = COL_U + LRU_W
COL_MERGE = COL_GATE_L + LRU_W

BF16 = jnp.bfloat16
F32 = jnp.float32
ONCE = pl.Buffered(1)


def _cparams(n_axes, flags=None):
    return pltpu.CompilerParams(dimension_semantics=("arbitrary",) * n_axes,
                                vmem_limit_bytes=VMEM_LIMIT_BYTES, flags=flags)


def _band_bias():
    out = np.zeros((N_GROUPS, 2, 2 * BAND, 2 * BAND), np.float32)
    i = np.arange(BAND)
    for g, d in enumerate(DILATIONS):
        R = CLASSES // d
        L = BAND // R
        m = R * (i % L) + i // L
        m_k = np.concatenate([m - BAND, m])
        dist = m[:, None] - m_k[None, :]
        band = (dist >= 0) & (dist <= BAND)
        for first in range(2):
            valid = band & ((np.arange(2 * BAND) >= BAND)[None, :] | (first == 0))
            bias = np.where(valid, 0.0, NEG_INF).astype(np.float32)
            out[g, first] = np.concatenate([bias, bias], axis=0)
    return out


def _rope_tables(S):
    A = S // CLASSES
    r = np.arange(S)
    pos = (CLASSES * (r % A) + r // A).astype(np.float64)
    inv_freq = ROPE_THETA ** (-np.arange(HALF, dtype=np.float64) / HALF)
    ang = pos[:, None] * inv_freq[None, :]
    cos_t = np.tile(np.cos(ang), (1, LANES // HALF))
    sin = np.sin(ang)
    sin_t = np.concatenate([-sin, -sin, sin, sin], axis=1)
    return cos_t.astype(np.float32), sin_t.astype(np.float32)


def _pair_interleave(w):
    lane = lax.broadcasted_iota(jnp.int32, w.shape, 1)
    from_right = pltpu.roll(w, LANES - HALF, axis=1)
    from_left = pltpu.roll(w, HALF, axis=1)
    quarter = lane // HALF
    return jnp.where(quarter == 1, from_right, jnp.where(quarter == 2, from_left, w))


def _attn_weights_kernel(*refs):
    w_refs = refs[:N_ATTN_SLABS]
    b_refs = refs[N_ATTN_SLABS:2 * N_ATTN_SLABS]
    w_out, b_out = refs[2 * N_ATTN_SLABS:]
    D = w_out.shape[0]
    w_rows = 256
    for i, (g, kind) in enumerate(ATTN_SLAB_LAYOUT):
        is_qk = kind in (0, 1)
        cols = slice(i * LANES, (i + 1) * LANES)
        for r0 in range(0, D, w_rows):
            w = w_refs[i][r0:r0 + w_rows, :]
            w_out[r0:r0 + w_rows, cols] = (_pair_interleave(w) if is_qk else w).astype(BF16)
        b = jnp.broadcast_to(b_refs[i][...], (SUBLANES, LANES))
        b_out[:, cols] = _pair_interleave(b) if is_qk else b


def _attn_weights(w_in, b_in):
    D = w_in.shape[0]
    kind_col = (0, COL_K, COL_V)
    slab0 = [COL_GATE_A // LANES if kind == GATE_SLAB else (kind_col[kind] + g * ATTN_W) // LANES
             for g, kind in ATTN_SLAB_LAYOUT]
    w_specs = [pl.BlockSpec((D, LANES), functools.partial(lambda p, s: (0, s + p), s=s))
               for s in slab0]
    b_specs = [pl.BlockSpec((1, LANES), functools.partial(lambda p, s: (0, s + p), s=s))
               for s in slab0]
    return pl.pallas_call(
        _attn_weights_kernel,
        grid=(N_PAIRS,),
        in_specs=w_specs + b_specs,
        out_specs=[pl.BlockSpec((None, D, ATTN_COLS), lambda p: (p, 0, 0)),
                   pl.BlockSpec((None, SUBLANES, ATTN_COLS), lambda p: (p, 0, 0))],
        out_shape=[jax.ShapeDtypeStruct((N_PAIRS, D, ATTN_COLS), BF16),
                   jax.ShapeDtypeStruct((N_PAIRS, SUBLANES, ATTN_COLS), F32)],
        compiler_params=_cparams(1),
        name="attn_weights",
    )(*([w_in] * N_ATTN_SLABS), *([b_in] * N_ATTN_SLABS))


def _attention_kernel(x_hbm, w_s, b_s, cos_ref, sin_ref, bias_ref, o_ref,
                      qkv0_s, qkv_s, gate_s, acc_s, m_s, l_s, x_buf, x_sem, *, S, row_chunk):
    A = S // CLASSES
    per = row_chunk // A
    n_chunks = S // row_chunk

    step = pl.program_id(0) * pl.num_programs(1) + pl.program_id(1)
    total_chunks = pl.num_programs(0) * pl.num_programs(1) * n_chunks

    def x_copies(q):
        b = lax.div(q, pl.num_programs(1) * n_chunks)
        c0 = lax.rem(q, n_chunks) * per
        slot = lax.rem(q, 2)
        return [pltpu.make_async_copy(x_hbm.at[b, :, c0 + i, :], x_buf.at[slot, i],
                                      x_sem.at[slot, i]) for i in range(per)]

    @pl.when(step == 0)
    def _():
        for cp in x_copies(0):
            cp.start()

    def qkv_ref(g, i):
        return qkv0_s.at[i] if g == 0 else qkv_s.at[3 * (g - 1) + i]

    n_tiles = S // BAND
    lane = lax.broadcasted_iota(jnp.int32, (BAND, LANES), 1)
    head0_q = ((lane // HALF) % 2) == 0
    head0_v = lane < HEAD_DIM

    def head_masks(keys):
        h0 = lax.broadcasted_iota(jnp.int32, (keys, LANES), 1) < HEAD_DIM
        return h0, h0.astype(BF16), 1 - h0.astype(BF16)
    q_scale = (HEAD_DIM ** -0.5) * LOG2_E

    def proj_chunk(rc, slab_lo, slab_hi):
        q = step * n_chunks + rc
        for cp in x_copies(q):
            cp.wait()

        @pl.when(q + 1 < total_chunks)
        def _():
            for cp in x_copies(q + 1):
                cp.start()

        rows = pl.ds(pl.multiple_of(rc * row_chunk, row_chunk), row_chunk)
        cols = slice(slab_lo * LANES, slab_hi * LANES)
        xb = x_buf[lax.rem(q, 2)].reshape(row_chunk, x_buf.shape[-1]).astype(BF16)
        res = jnp.dot(xb, w_s[:, cols], preferred_element_type=F32) + b_s[0:1, cols]
        cos = cos_ref[rows, :]
        sin = sin_ref[rows, :]
        for i, (g, kind) in enumerate(ATTN_SLAB_LAYOUT[slab_lo:slab_hi]):
            val = res[:, i * LANES:(i + 1) * LANES]
            if kind == GATE_SLAB:
                gate_s[rows, :] = (val * jax.nn.sigmoid(val)).astype(BF16)
                continue
            if kind in (0, 1):
                val = val * cos + pltpu.roll(val, LANES // 2, axis=1) * sin
            if kind == 0:
                val = val * q_scale
            ref = qkv_ref(g, kind)
            ref[rows, :] = val.astype(ref.dtype)

    def tiles(g, is_first_group):
        d = DILATIONS[g]
        R = CLASSES // d
        L = BAND // R
        nb = S // (d * BAND)

        def make_run(e):
            rho, n = divmod(e, nb)

            def run(j, nn):
                start = (rho + d * j) * A + L * nn
                return slice(start, start + L)
            return n, run

        def gather(ref, run, nn):
            return jnp.concatenate([ref[run(j, nn), :] for j in range(R)], axis=0)

        def key_window(ref, run, n):
            cur = gather(ref, run, n)
            if n == 0:
                return cur.astype(BF16)
            return jnp.concatenate([gather(ref, run, n - 1), cur], axis=0).astype(BF16)

        def scores(n, run):
            q_t = gather(qkv_ref(g, 0), run, n).astype(BF16)
            zero = jnp.zeros_like(q_t)
            qm = jnp.concatenate([jnp.where(head0_q, q_t, zero),
                                  jnp.where(head0_q, zero, q_t)], axis=0)
            kwin = key_window(qkv_ref(g, 1), run, n)
            s = lax.dot_general(qm, kwin, (((1,), (1,)), ((), ())),
                                preferred_element_type=F32)
            if n == 0:
                return s + bias_ref[g, 1, :, BAND:]
            return s + bias_ref[g, 0]

        def softmax(s):
            m_blk = jnp.max(s, axis=1, keepdims=True)
            p = jnp.exp2(s - m_blk)
            return p.astype(BF16), m_blk

        def finish(n, run, p, m_blk):
            vwin = key_window(qkv_ref(g, 2), run, n)
            keys = vwin.shape[0]
            zero_v = jnp.zeros_like(vwin)
            h0, ones_h0, ones_h1 = head_masks(keys)
            v_bd = jnp.concatenate([
                jnp.concatenate([jnp.where(h0, vwin, zero_v), ones_h0], axis=1),
                jnp.concatenate([jnp.where(h0, zero_v, vwin), ones_h1], axis=1)], axis=0)
            p_cat = jnp.concatenate([p[:BAND], p[BAND:]], axis=1)
            pv = jnp.dot(p_cat, v_bd, preferred_element_type=F32)
            o_t = pv[:, :LANES]
            m_t = jnp.where(head0_v, m_blk[:BAND], m_blk[BAND:])
            l_t = pv[:, LANES:]
            if not is_first_group:
                m_old = gather(m_s, run, n)
                m_new = jnp.maximum(m_old, m_t)
                w_old = jnp.exp2(m_old - m_new)
                w_new = jnp.exp2(m_t - m_new)
                m_t = m_new
                l_t = gather(l_s, run, n) * w_old + l_t * w_new
                o_t = gather(acc_s, run, n) * w_old + o_t * w_new
            for j in range(R):
                m_s[run(j, n), :] = m_t[j * L:(j + 1) * L]
                l_s[run(j, n), :] = l_t[j * L:(j + 1) * L]
                acc_s[run(j, n), :] = o_t[j * L:(j + 1) * L]

        for e in range(n_tiles):
            n, run = make_run(e)
            finish(n, run, *softmax(scores(n, run)))

    def proj_all(rc, carry):
        proj_chunk(rc, 0, N_ATTN_SLABS)
        return carry

    lax.fori_loop(0, S // row_chunk, proj_all, 0)
    for g in range(N_GROUPS):
        tiles(g, g == 0)

    for c in range(CLASSES):
        rows = slice(c * A, (c + 1) * A)
        res = (acc_s[rows, :] / l_s[rows, :]) * gate_s[rows, :].astype(F32)
        o_ref[pl.ds(c, A, stride=CLASSES), :] = res


def _attention(x, w_attn, b_attn):
    B, S, D = x.shape
    A = S // CLASSES
    row_chunk = 1024
    kern = functools.partial(_attention_kernel, S=S, row_chunk=row_chunk)
    cos_t, sin_t = _rope_tables(S)
    const2 = pl.BlockSpec((S, LANES), lambda b, p: (0, 0), pipeline_mode=ONCE)
    scratch = [
        pltpu.VMEM((3, S, LANES), F32),
        pltpu.VMEM((3 * (N_GROUPS - 1), S, LANES), BF16),
        pltpu.VMEM((S, LANES), BF16),
        pltpu.VMEM((S, LANES), F32),
        pltpu.VMEM((S, LANES), F32),
        pltpu.VMEM((S, LANES), F32),
        pltpu.VMEM((2, row_chunk // A, A, D), F32),
        pltpu.SemaphoreType.DMA((2, row_chunk // A)),
    ]
    return pl.pallas_call(
        kern,
        grid=(B, N_PAIRS),
        in_specs=[
            pl.BlockSpec(memory_space=pl.ANY),
            pl.BlockSpec((None, D, ATTN_COLS), lambda b, p: (p, 0, 0)),
            pl.BlockSpec((None, SUBLANES, ATTN_COLS), lambda b, p: (p, 0, 0)),
            const2, const2,
            pl.BlockSpec((N_GROUPS, 2, 2 * BAND, 2 * BAND), lambda b, p: (0, 0, 0, 0),
                         pipeline_mode=ONCE),
        ],
        out_specs=pl.BlockSpec((None, S, LANES), lambda b, p: (b, 0, p)),
        out_shape=jax.ShapeDtypeStruct((B, S, ATTN_W), F32),
        scratch_shapes=scratch,
        compiler_params=_cparams(2),
        name="attention",
    )(x.reshape(B, A, CLASSES, D), w_attn, b_attn, cos_t, sin_t, _band_bias())


X_AHEAD = 2
X_SLOTS = X_AHEAD + 1
SCAN_VREGS = 2


def _rglru_kernel(x_hbm, wu_ref, wl_ref, wr_ref, wi_ref, b_in_ref, cw_ref, cb_ref,
                  br_ref, bi_ref, lam_ref, o_ref,
                  w_s, wg_s, u_pad, gate_s, a_p, h_p, p_o, h_o, x_buf, x_sem, *, S, row_chunk, pitch):
    C = LRU_SLAB
    n_slabs = C // LANES
    n_chunks = S // row_chunk
    n_seg = SCAN_VREGS * SUBLANES
    seg_len = S // n_seg
    seg_per_chunk = row_chunk // seg_len
    u_pad[:, 0:SUBLANES, :] = jnp.zeros((n_slabs, SUBLANES, LANES), F32)

    def slab_cols(ref, col0=0):
        start = pl.multiple_of(col0 + pl.program_id(1) * C, LANES)
        return ref[:, pl.ds(start, C)]

    bu, bl = slab_cols(b_in_ref, COL_U), slab_cols(b_in_ref, COL_GATE_L)
    cw, cb = slab_cols(cw_ref), slab_cols(cb_ref)
    br, bi = slab_cols(br_ref), slab_cols(bi_ref)

    w_s[:, :C] = wu_ref[...].astype(BF16)
    w_s[:, C:] = wl_ref[...].astype(BF16)
    wg_s[...] = jnp.zeros(wg_s.shape, BF16)
    for blk in range(C // LRU_BLOCK):
        rows = slice(blk * LRU_BLOCK, (blk + 1) * LRU_BLOCK)
        wg_s[rows, blk * LRU_BLOCK:(blk + 1) * LRU_BLOCK] = wr_ref[blk].astype(BF16)
        wg_s[rows, C + blk * LRU_BLOCK:C + (blk + 1) * LRU_BLOCK] = wi_ref[blk].astype(BF16)

    step = pl.program_id(0) * pl.num_programs(1) + pl.program_id(1)
    total_chunks = pl.num_programs(0) * pl.num_programs(1) * n_chunks
    chunks_per_batch = pl.num_programs(1) * n_chunks

    def x_copy(q):
        b = lax.div(q, chunks_per_batch)
        rows = pl.ds(pl.multiple_of(lax.rem(q, n_chunks) * row_chunk, row_chunk), row_chunk)
        slot = lax.rem(q, X_SLOTS)
        return pltpu.make_async_copy(x_hbm.at[b, rows, :], x_buf.at[slot], x_sem.at[slot])

    @pl.when(step == 0)
    def _():
        for q in range(X_AHEAD):
            x_copy(q).start()

    def x_arrive(rc):
        q = step * n_chunks + rc
        x_copy(q).wait()

        @pl.when(q + X_AHEAD < total_chunks)
        def _():
            x_copy(q + X_AHEAD).start()

    def proj_chunk(rc, carry):
        r0 = pl.multiple_of(rc * row_chunk, row_chunk)
        q = step * n_chunks + rc
        xb = x_buf[lax.rem(q, X_SLOTS)].astype(BF16)
        res = jnp.dot(xb, w_s[...], preferred_element_type=F32)
        u = res[:, :C] + bu
        for s in range(n_slabs):
            u_pad[s, pl.ds(r0 + SUBLANES, row_chunk), :] = u[:, s * LANES:(s + 1) * LANES]
        gl = res[:, C:] + bl
        gate_s[pl.ds(r0, row_chunk), :] = gl * jax.nn.sigmoid(gl)
        return carry

    neg_lam = -slab_cols(lam_ref)
    softplus = jnp.maximum(neg_lam, 0.0) + jnp.log1p(jnp.exp(-jnp.abs(neg_lam)))
    decay_rate = LRU_C * softplus

    def gate_chunk(rc, carry):
        r0 = pl.multiple_of(rc * row_chunk, row_chunk)
        parts = []
        for s in range(n_slabs):
            lanes = slice(s * LANES, (s + 1) * LANES)
            acc = cb[:, lanes]
            for j in range(CONV_W):
                off = SUBLANES - (CONV_W - 1) + j
                acc = acc + u_pad[s, pl.ds(r0 + off, row_chunk), :] * cw[j:j + 1, lanes]
            parts.append(acc)
        uc = jnp.concatenate(parts, axis=1)
        gates = jnp.dot(uc.astype(BF16), wg_s[...], preferred_element_type=F32)
        r = jax.nn.sigmoid(gates[:, :C] + br)
        i = jax.nn.sigmoid(gates[:, C:] + bi)
        neg_log_a = r * decay_rate
        a = jnp.exp(-neg_log_a)
        one_minus_a2 = jnp.maximum(jnp.tanh(neg_log_a) * (a * a + 1.0), 0.0)
        mult = jnp.exp2(jnp.log(one_minus_a2) * (0.5 * LOG2_E))
        u = mult * (i * uc)
        for k in range(seg_per_chunk):
            start = (rc * seg_per_chunk + k) * pitch
            rows = slice(k * seg_len, (k + 1) * seg_len)
            for s in range(n_slabs):
                lanes = slice(s * LANES, (s + 1) * LANES)
                a_p[s, pl.ds(start, seg_len), :] = a[rows, lanes]
                h_p[s, pl.ds(start, seg_len), :] = u[rows, lanes]
        return carry

    def gate_then_proj(rc, carry):
        x_arrive(rc + 1)
        gate_chunk(rc, carry)
        return proj_chunk(rc + 1, carry)

    x_arrive(0)
    proj_chunk(0, 0)
    lax.fori_loop(0, n_chunks - 1, gate_then_proj, 0)
    gate_chunk(n_chunks - 1, 0)

    chains = [(s, v) for s in range(n_slabs) for v in range(SCAN_VREGS)]

    def scan_step(j, carry):
        out = []
        for c, (s, v) in enumerate(chains):
            rows = pl.ds(v * SUBLANES * pitch + j, SUBLANES, stride=pitch)
            h_prev, p_prev = carry[2 * c], carry[2 * c + 1]
            a = a_p[s, rows, :]
            h = a * h_prev + h_p[s, rows, :]
            p = a * p_prev
            h_o[s, rows, :] = h
            p_o[s, rows, :] = p
            out += [h, p]
        return tuple(out)

    init = (jnp.zeros((SUBLANES, LANES), F32), jnp.ones((SUBLANES, LANES), F32)) * len(chains)
    ends = lax.fori_loop(0, seg_len, scan_step, init, unroll=8)

    for s in range(n_slabs):
        lanes = slice(s * LANES, (s + 1) * LANES)
        h_in = jnp.zeros((1, LANES), F32)
        for seg in range(n_seg):
            c, sub = chains.index((s, seg // SUBLANES)), seg % SUBLANES
            h_end, p_end = ends[2 * c], ends[2 * c + 1]
            rows = slice(seg * pitch, seg * pitch + seg_len)
            h = h_o[s, rows, :] + p_o[s, rows, :] * h_in
            nat = slice(seg * seg_len, (seg + 1) * seg_len)
            o_ref[nat, lanes] = (h * gate_s[nat, lanes]).astype(BF16)
            h_in = h_end[sub:sub + 1] + p_end[sub:sub + 1] * h_in


def _rglru(x, w_in, b_in, conv_w, conv_b, lru_wr, lru_br, lru_wi, lru_bi, lam):
    B, S, D = x.shape
    C = LRU_SLAB
    n_slabs = LRU_W // C
    per = C // LRU_BLOCK
    n_seg = SCAN_VREGS * SUBLANES
    pitch = S // n_seg + SUBLANES // 2
    row_chunk = 512
    kern = functools.partial(_rglru_kernel, S=S, row_chunk=row_chunk, pitch=pitch)
    col = lambda c0: (lambda b, c: (0, c0 // C + c))
    whole = lambda a: pl.BlockSpec(a.shape, lambda b, c: (0, 0), pipeline_mode=ONCE)
    blocks = lambda: pl.BlockSpec((per, LRU_BLOCK, LRU_BLOCK), lambda b, c: (c, 0, 0))
    return pl.pallas_call(
        kern,
        grid=(B, n_slabs),
        in_specs=[
            pl.BlockSpec(memory_space=pl.ANY),
            pl.BlockSpec((D, C), col(COL_U)), pl.BlockSpec((D, C), col(COL_GATE_L)),
            blocks(), blocks(),
            whole(b_in), whole(conv_w), whole(conv_b), whole(lru_br), whole(lru_bi), whole(lam),
        ],
        out_specs=pl.BlockSpec((None, S, C), lambda b, c: (b, 0, c)),
        out_shape=jax.ShapeDtypeStruct((B, S, LRU_W), BF16),
        scratch_shapes=[
            pltpu.VMEM((D, 2 * C), BF16),
            pltpu.VMEM((C, 2 * C), BF16),
            pltpu.VMEM((C // LANES, S + SUBLANES, LANES), F32),
            pltpu.VMEM((S, C), F32),
            pltpu.VMEM((C // LANES, n_seg * pitch, LANES), F32),
            pltpu.VMEM((C // LANES, n_seg * pitch, LANES), F32),
            pltpu.VMEM((C // LANES, n_seg * pitch, LANES), F32),
            pltpu.VMEM((C // LANES, n_seg * pitch, LANES), F32),
            pltpu.VMEM((X_SLOTS, row_chunk, D), F32),
            pltpu.SemaphoreType.DMA((X_SLOTS,)),
        ],
        compiler_params=_cparams(2),
        name="rglru",
    )(x, w_in, w_in, lru_wr, lru_wi, b_in, conv_w, conv_b, lru_br, lru_bi, lam)


MERGE_SUBTILES = 2


def _merge_out_kernel(x_ref, attn_ref, hg_ref, wga_ref, wgb_ref, bga_ref, bgb_ref,
                      wa_ref, wl_ref, wo_ref, bo_ref, lng_ref, lnb_ref, o_ref,
                      wg_s, wa_s, wl_s, wo_s):
    D = x_ref.shape[1]

    @pl.when(pl.program_id(0) == 0)
    def _():
        rows = 256
        for r0 in range(0, D, rows):
            sl = slice(r0, r0 + rows)
            wg_s[sl, :D] = wga_ref[sl, :].astype(BF16)
            wg_s[sl, D:] = wgb_ref[sl, :].astype(BF16)
            wl_s[sl, :] = wl_ref[sl, :].astype(BF16)
            wo_s[sl, :] = wo_ref[sl, :].astype(BF16)
        for r0 in range(0, ATTN_W, rows):
            sl = slice(r0, r0 + rows)
            wa_s[sl, :] = wa_ref[sl, :].astype(BF16)

    half = x_ref.shape[0] // MERGE_SUBTILES
    for hb in range(MERGE_SUBTILES):
        rows = slice(hb * half, (hb + 1) * half)
        x = x_ref[rows, :]
        xb = x.astype(BF16)
        gate_a = jax.nn.sigmoid(
            jnp.dot(xb, wg_s[:, :D], preferred_element_type=F32) + bga_ref[...])
        gate_b = jax.nn.sigmoid(
            jnp.dot(xb, wg_s[:, D:], preferred_element_type=F32) + bgb_ref[...])
        y_a = jnp.dot(attn_ref[rows, :].astype(BF16), wa_s[...], preferred_element_type=F32)
        y_b = jnp.dot(hg_ref[rows, :], wl_s[...], preferred_element_type=F32)
        merged = gate_a * y_a + gate_b * y_b
        out = jnp.dot(merged.astype(BF16), wo_s[...], preferred_element_type=F32) + bo_ref[...]
        y = ALPHA * x + out
        mu = jnp.mean(y, axis=1, keepdims=True)
        yc = y - mu
        var = jnp.mean(yc * yc, axis=1, keepdims=True)
        o_ref[rows, :] = yc * lax.rsqrt(var + LN_EPS) * lng_ref[...] + lnb_ref[...]


def _merge_out(x2, attn2, hg2, w_in, b_in, w_a, w_l, w_o, b_o, ln_g, ln_b, tm=512):
    T, D = x2.shape
    once = lambda shape, idx: pl.BlockSpec(shape, lambda i: idx, pipeline_mode=ONCE)
    merge_blk = COL_MERGE // D
    return pl.pallas_call(
        _merge_out_kernel,
        grid=(T // tm,),
        in_specs=[
            pl.BlockSpec((tm, D), lambda i: (i, 0)),
            pl.BlockSpec((tm, ATTN_W), lambda i: (i, 0)),
            pl.BlockSpec((tm, LRU_W), lambda i: (i, 0)),
            once((D, D), (0, merge_blk)), once((D, D), (0, merge_blk + 1)),
            once((1, D), (0, merge_blk)), once((1, D), (0, merge_blk + 1)),
            once((ATTN_W, D), (0, 0)), once((LRU_W, D), (0, 0)), once((D, D), (0, 0)),
            once((1, D), (0, 0)), once((1, D), (0, 0)), once((1, D), (0, 0)),
        ],
        out_specs=pl.BlockSpec((tm, D), lambda i: (i, 0)),
        out_shape=jax.ShapeDtypeStruct((T, D), F32),
        scratch_shapes=[
            pltpu.VMEM((D, 2 * D), BF16), pltpu.VMEM((ATTN_W, D), BF16),
            pltpu.VMEM((LRU_W, D), BF16), pltpu.VMEM((D, D), BF16),
        ],
        compiler_params=_cparams(1),
        name="merge_out",
    )(x2, attn2, hg2, w_in, w_in, b_in, b_in, w_a, w_l, w_o, b_o, ln_g, ln_b)


def kernel(x, w_in, b_in, conv_w, conv_b, lru_wr, lru_br, lru_wi, lru_bi, lru_lambda,
           w_attn_proj, w_lru_proj, w_out, b_out, ln_gain, ln_bias):
    B, S, D = x.shape
    assert w_in.shape[0] == DEPTH and S % (CLASSES * BAND) == 0 and D == D_MODEL
    assert COL_MERGE % D == 0 and w_in.shape[2] == COL_MERGE + 2 * D
    row = lambda v: v.reshape(1, -1)
    for l in range(DEPTH):
        wl, bl = w_in[l], row(b_in[l])
        attn = _attention(x, *_attn_weights(wl, bl))
        hg = _rglru(x, wl, bl, conv_w[l], row(conv_b[l]), lru_wr[l], row(lru_br[l]),
                    lru_wi[l], row(lru_bi[l]), row(lru_lambda[l]))
        out = _merge_out(
            x.reshape(B * S, D), attn.reshape(B * S, ATTN_W), hg.reshape(B * S, LRU_W),
            wl, bl, w_attn_proj[l], w_lru_proj[l], w_out[l],
            row(b_out[l]), row(ln_gain[l]), row(ln_bias[l]))
        x = out.reshape(B, S, D)
    return x
```

```python
import functools
import math

import jax
import jax.numpy as jnp
import numpy as np
from jax import lax
from jax.experimental import pallas as pl
from jax.experimental.pallas import tpu as pltpu

LANES = 128
SUBLANES = 8
BF16_ROWS = 16
VMEM_LIMIT_BYTES = 56 * 1024 * 1024

D_MODEL = 1024
HEAD_DIM = 64
HALF = HEAD_DIM // 2
HEADS_PER_GROUP = 8
DILATIONS = (1, 4, 16)
N_GROUPS = len(DILATIONS)
BAND = 128
CLASSES = 16
QKV_W = N_GROUPS * HEADS_PER_GROUP * HEAD_DIM
ATTN_W = HEADS_PER_GROUP * HEAD_DIM
N_PAIRS = ATTN_W // LANES
LRU_W = 1024
LRU_BLOCK = 64
LRU_SLAB = 256
CONV_W = 4
LRU_C = 8.0
ROPE_THETA = 10000.0
LOG2_E = math.log2(math.e)
NEG_INF = -1e30
LN_EPS = 1e-5
DEPTH = 1
ALPHA = (2.0 * DEPTH) ** 0.25
GATE_SLAB = 3
ATTN_SLAB_LAYOUT = ([(0, 0), (0, 1), (0, 2), (None, GATE_SLAB)]
                    + [(g, kind) for g in range(1, N_GROUPS) for kind in range(3)])
N_ATTN_SLABS = len(ATTN_SLAB_LAYOUT)
ATTN_COLS = N_ATTN_SLABS * LANES
COL_K = QKV_W
COL_V = 2 * QKV_W
COL_GATE_A = 3 * QKV_W
COL_U = COL_GATE_A + ATTN_W
COL_GATE_L = COL_U + LRU_W
COL_MERGE = COL_GATE_L + LRU_W

ATTN_ROW_CHUNK = 1024
LRU_ROW_CHUNK = 512
MERGE_TILE_ROWS = 512
MERGE_SUBTILES = 2
CAST_ROWS = 256
X_AHEAD = 2
X_SLOTS = X_AHEAD + 1
SCAN_VREGS = 2

BF16 = jnp.bfloat16
F32 = jnp.float32
ONCE = pl.Buffered(1)


def _cparams(n_axes, flags=None):
    return pltpu.CompilerParams(dimension_semantics=("arbitrary",) * n_axes,
                                vmem_limit_bytes=VMEM_LIMIT_BYTES, flags=flags)


def _band_bias():
    out = np.zeros((N_GROUPS, 2, 2 * BAND, 2 * BAND), np.float32)
    i = np.arange(BAND)
    for g, d in enumerate(DILATIONS):
        R = CLASSES // d
        L = BAND // R
        m = R * (i % L) + i // L
        m_k = np.concatenate([m - BAND, m])
        dist = m[:, None] - m_k[None, :]
        band = (dist >= 0) & (dist <= BAND)
        for first in range(2):
            valid = band & ((np.arange(2 * BAND) >= BAND)[None, :] | (first == 0))
            bias = np.where(valid, 0.0, NEG_INF).astype(np.float32)
            out[g, first] = np.concatenate([bias, bias], axis=0)
    return out


def _rope_tables(S):
    A = S // CLASSES
    r = np.arange(S)
    pos = (CLASSES * (r % A) + r // A).astype(np.float64)
    inv_freq = ROPE_THETA ** (-np.arange(HALF, dtype=np.float64) / HALF)
    ang = pos[:, None] * inv_freq[None, :]
    cos_t = np.tile(np.cos(ang), (1, LANES // HALF))
    sin = np.sin(ang)
    sin_t = np.concatenate([-sin, -sin, sin, sin], axis=1)
    return cos_t.astype(np.float32), sin_t.astype(np.float32)


def _pair_interleave(w):
    lane = lax.broadcasted_iota(jnp.int32, w.shape, 1)
    from_right = pltpu.roll(w, LANES - HALF, axis=1)
    from_left = pltpu.roll(w, HALF, axis=1)
    quarter = lane // HALF
    return jnp.where(quarter == 1, from_right, jnp.where(quarter == 2, from_left, w))


def _attn_weights_kernel(*refs):
    w_refs = refs[:N_ATTN_SLABS]
    b_refs = refs[N_ATTN_SLABS:2 * N_ATTN_SLABS]
    w_out, b_out = refs[2 * N_ATTN_SLABS:]
    D = w_out.shape[0]
    w_rows = CAST_ROWS
    for i, (g, kind) in enumerate(ATTN_SLAB_LAYOUT):
        is_qk = kind in (0, 1)
        cols = slice(i * LANES, (i + 1) * LANES)
        for r0 in range(0, D, w_rows):
            w = w_refs[i][r0:r0 + w_rows, :]
            w_out[r0:r0 + w_rows, cols] = (_pair_interleave(w) if is_qk else w).astype(BF16)
        b = jnp.broadcast_to(b_refs[i][...], (SUBLANES, LANES))
        b_out[:, cols] = _pair_interleave(b) if is_qk else b


def _attn_weights(w_in, b_in):
    D = w_in.shape[0]
    kind_col = (0, COL_K, COL_V)
    slab0 = [COL_GATE_A // LANES if kind == GATE_SLAB else (kind_col[kind] + g * ATTN_W) // LANES
             for g, kind in ATTN_SLAB_LAYOUT]
    w_specs = [pl.BlockSpec((D, LANES), functools.partial(lambda p, s: (0, s + p), s=s))
               for s in slab0]
    b_specs = [pl.BlockSpec((1, LANES), functools.partial(lambda p, s: (0, s + p), s=s))
               for s in slab0]
    return pl.pallas_call(
        _attn_weights_kernel,
        grid=(N_PAIRS,),
        in_specs=w_specs + b_specs,
        out_specs=[pl.BlockSpec((None, D, ATTN_COLS), lambda p: (p, 0, 0)),
                   pl.BlockSpec((None, SUBLANES, ATTN_COLS), lambda p: (p, 0, 0))],
        out_shape=[jax.ShapeDtypeStruct((N_PAIRS, D, ATTN_COLS), BF16),
                   jax.ShapeDtypeStruct((N_PAIRS, SUBLANES, ATTN_COLS), F32)],
        compiler_params=_cparams(1),
        name="attn_weights",
    )(*([w_in] * N_ATTN_SLABS), *([b_in] * N_ATTN_SLABS))


def _attention_kernel(x_hbm, w_s, b_s, cos_ref, sin_ref, bias_ref, o_ref,
                      qkv0_s, qkv_s, gate_s, acc_s, m_s, l_s, x_buf, x_sem, *, S, row_chunk):
    A = S // CLASSES
    per = row_chunk // A
    n_chunks = S // row_chunk

    step = pl.program_id(0) * pl.num_programs(1) + pl.program_id(1)
    total_chunks = pl.num_programs(0) * pl.num_programs(1) * n_chunks

    def x_copies(q):
        b = lax.div(q, pl.num_programs(1) * n_chunks)
        c0 = lax.rem(q, n_chunks) * per
        slot = lax.rem(q, 2)
        return [pltpu.make_async_copy(x_hbm.at[b, :, c0 + i, :], x_buf.at[slot, i],
                                      x_sem.at[slot, i]) for i in range(per)]

    @pl.when(step == 0)
    def _():
        for cp in x_copies(0):
            cp.start()

    def qkv_ref(g, i):
        return qkv0_s.at[i] if g == 0 else qkv_s.at[3 * (g - 1) + i]

    n_tiles = S // BAND
    lane = lax.broadcasted_iota(jnp.int32, (BAND, LANES), 1)
    head0_q = ((lane // HALF) % 2) == 0
    head0_v = lane < HEAD_DIM

    def head_masks(keys):
        h0 = lax.broadcasted_iota(jnp.int32, (keys, LANES), 1) < HEAD_DIM
        return h0, h0.astype(BF16), 1 - h0.astype(BF16)
    q_scale = (HEAD_DIM ** -0.5) * LOG2_E

    def proj_chunk(rc, slab_lo, slab_hi):
        q = step * n_chunks + rc
        for cp in x_copies(q):
            cp.wait()

        @pl.when(q + 1 < total_chunks)
        def _():
            for cp in x_copies(q + 1):
                cp.start()

        rows = pl.ds(pl.multiple_of(rc * row_chunk, row_chunk), row_chunk)
        cols = slice(slab_lo * LANES, slab_hi * LANES)
        xb = x_buf[lax.rem(q, 2)].reshape(row_chunk, x_buf.shape[-1]).astype(BF16)
        res = jnp.dot(xb, w_s[:, cols], preferred_element_type=F32) + b_s[0:1, cols]
        cos = cos_ref[rows, :]
        sin = sin_ref[rows, :]
        for i, (g, kind) in enumerate(ATTN_SLAB_LAYOUT[slab_lo:slab_hi]):
            val = res[:, i * LANES:(i + 1) * LANES]
            if kind == GATE_SLAB:
                gate_s[rows, :] = (val * jax.nn.sigmoid(val)).astype(BF16)
                continue
            if kind in (0, 1):
                val = val * cos + pltpu.roll(val, LANES // 2, axis=1) * sin
            if kind == 0:
                val = val * q_scale
            ref = qkv_ref(g, kind)
            ref[rows, :] = val.astype(ref.dtype)

    def tiles(g, is_first_group):
        d = DILATIONS[g]
        R = CLASSES // d
        L = BAND // R
        nb = S // (d * BAND)

        def make_run(e):
            rho, n = divmod(e, nb)

            def run(j, nn):
                start = (rho + d * j) * A + L * nn
                return slice(start, start + L)
            return n, run

        def gather(ref, run, nn):
            return jnp.concatenate([ref[run(j, nn), :] for j in range(R)], axis=0)

        def key_window(ref, run, n):
            cur = gather(ref, run, n)
            if n == 0:
                return cur.astype(BF16)
            return jnp.concatenate([gather(ref, run, n - 1), cur], axis=0).astype(BF16)

        def scores(n, run):
            q_t = gather(qkv_ref(g, 0), run, n).astype(BF16)
            zero = jnp.zeros_like(q_t)
            qm = jnp.concatenate([jnp.where(head0_q, q_t, zero),
                                  jnp.where(head0_q, zero, q_t)], axis=0)
            kwin = key_window(qkv_ref(g, 1), run, n)
            s = lax.dot_general(qm, kwin, (((1,), (1,)), ((), ())),
                                preferred_element_type=F32)
            if n == 0:
                return s + bias_ref[g, 1, :, BAND:]
            return s + bias_ref[g, 0]

        def softmax(s):
            m_blk = jnp.max(s, axis=1, keepdims=True)
            p = jnp.exp2(s - m_blk)
            return p.astype(BF16), m_blk

        def finish(n, run, p, m_blk):
            vwin = key_window(qkv_ref(g, 2), run, n)
            keys = vwin.shape[0]
            zero_v = jnp.zeros_like(vwin)
            h0, ones_h0, ones_h1 = head_masks(keys)
            v_bd = jnp.concatenate([
                jnp.concatenate([jnp.where(h0, vwin, zero_v), ones_h0], axis=1),
                jnp.concatenate([jnp.where(h0, zero_v, vwin), ones_h1], axis=1)], axis=0)
            p_cat = jnp.concatenate([p[:BAND], p[BAND:]], axis=1)
            pv = jnp.dot(p_cat, v_bd, preferred_element_type=F32)
            o_t = pv[:, :LANES]
            m_t = jnp.where(head0_v, m_blk[:BAND], m_blk[BAND:])
            l_t = pv[:, LANES:]
            if not is_first_group:
                m_old = gather(m_s, run, n)
                m_new = jnp.maximum(m_old, m_t)
                w_old = jnp.exp2(m_old - m_new)
                w_new = jnp.exp2(m_t - m_new)
                m_t = m_new
                l_t = gather(l_s, run, n) * w_old + l_t * w_new
                o_t = gather(acc_s, run, n) * w_old + o_t * w_new
            for j in range(R):
                m_s[run(j, n), :] = m_t[j * L:(j + 1) * L]
                l_s[run(j, n), :] = l_t[j * L:(j + 1) * L]
                acc_s[run(j, n), :] = o_t[j * L:(j + 1) * L]

        for e in range(n_tiles):
            n, run = make_run(e)
            finish(n, run, *softmax(scores(n, run)))

    def proj_all(rc, carry):
        proj_chunk(rc, 0, N_ATTN_SLABS)
        return carry

    lax.fori_loop(0, S // row_chunk, proj_all, 0)
    for g in range(N_GROUPS):
        tiles(g, g == 0)

    for c in range(CLASSES):
        rows = slice(c * A, (c + 1) * A)
        res = (acc_s[rows, :] / l_s[rows, :]) * gate_s[rows, :].astype(F32)
        o_ref[pl.ds(c, A, stride=CLASSES), :] = res


def _attention(x, w_attn, b_attn):
    B, S, D = x.shape
    A = S // CLASSES
    row_chunk = ATTN_ROW_CHUNK
    kern = functools.partial(_attention_kernel, S=S, row_chunk=row_chunk)
    cos_t, sin_t = _rope_tables(S)
    const2 = pl.BlockSpec((S, LANES), lambda b, p: (0, 0), pipeline_mode=ONCE)
    scratch = [
        pltpu.VMEM((3, S, LANES), F32),
        pltpu.VMEM((3 * (N_GROUPS - 1), S, LANES), BF16),
        pltpu.VMEM((S, LANES), BF16),
        pltpu.VMEM((S, LANES), F32),
        pltpu.VMEM((S, LANES), F32),
        pltpu.VMEM((S, LANES), F32),
        pltpu.VMEM((2, row_chunk // A, A, D), F32),
        pltpu.SemaphoreType.DMA((2, row_chunk // A)),
    ]
    return pl.pallas_call(
        kern,
        grid=(B, N_PAIRS),
        in_specs=[
            pl.BlockSpec(memory_space=pl.ANY),
            pl.BlockSpec((None, D, ATTN_COLS), lambda b, p: (p, 0, 0)),
            pl.BlockSpec((None, SUBLANES, ATTN_COLS), lambda b, p: (p, 0, 0)),
            const2, const2,
            pl.BlockSpec((N_GROUPS, 2, 2 * BAND, 2 * BAND), lambda b, p: (0, 0, 0, 0),
                         pipeline_mode=ONCE),
        ],
        out_specs=pl.BlockSpec((None, S, LANES), lambda b, p: (b, 0, p)),
        out_shape=jax.ShapeDtypeStruct((B, S, ATTN_W), F32),
        scratch_shapes=scratch,
        compiler_params=_cparams(2),
        name="attention",
    )(x.reshape(B, A, CLASSES, D), w_attn, b_attn, cos_t, sin_t, _band_bias())


def _rglru_kernel(x_hbm, wu_ref, wl_ref, wr_ref, wi_ref, b_in_ref, cw_ref, cb_ref,
                  br_ref, bi_ref, lam_ref, o_ref,
                  w_s, wg_s, u_pad, gate_s, a_p, h_p, x_buf, x_sem, *, S, row_chunk, pitch):
    C = LRU_SLAB
    n_slabs = C // LANES
    n_chunks = S // row_chunk
    n_seg = SCAN_VREGS * SUBLANES
    seg_len = S // n_seg
    seg_per_chunk = row_chunk // seg_len
    u_pad[:, 0:SUBLANES, :] = jnp.zeros((n_slabs, SUBLANES, LANES), F32)

    def slab_cols(ref, col0=0):
        start = pl.multiple_of(col0 + pl.program_id(1) * C, LANES)
        return ref[:, pl.ds(start, C)]

    bu, bl = slab_cols(b_in_ref, COL_U), slab_cols(b_in_ref, COL_GATE_L)
    cw, cb = slab_cols(cw_ref), slab_cols(cb_ref)
    br, bi = slab_cols(br_ref), slab_cols(bi_ref)

    w_s[:, :C] = wu_ref[...].astype(BF16)
    w_s[:, C:] = wl_ref[...].astype(BF16)
    wg_s[...] = jnp.zeros(wg_s.shape, BF16)
    for blk in range(C // LRU_BLOCK):
        rows = slice(blk * LRU_BLOCK, (blk + 1) * LRU_BLOCK)
        wg_s[rows, blk * LRU_BLOCK:(blk + 1) * LRU_BLOCK] = wr_ref[blk].astype(BF16)
        wg_s[rows, C + blk * LRU_BLOCK:C + (blk + 1) * LRU_BLOCK] = wi_ref[blk].astype(BF16)

    step = pl.program_id(0) * pl.num_programs(1) + pl.program_id(1)
    total_chunks = pl.num_programs(0) * pl.num_programs(1) * n_chunks
    chunks_per_batch = pl.num_programs(1) * n_chunks

    def x_copy(q):
        b = lax.div(q, chunks_per_batch)
        rows = pl.ds(pl.multiple_of(lax.rem(q, n_chunks) * row_chunk, row_chunk), row_chunk)
        slot = lax.rem(q, X_SLOTS)
        return pltpu.make_async_copy(x_hbm.at[b, rows, :], x_buf.at[slot], x_sem.at[slot])

    @pl.when(step == 0)
    def _():
        for q in range(X_AHEAD):
            x_copy(q).start()

    def x_arrive(rc):
        q = step * n_chunks + rc
        x_copy(q).wait()

        @pl.when(q + X_AHEAD < total_chunks)
        def _():
            x_copy(q + X_AHEAD).start()

    def proj_chunk(rc, carry):
        r0 = pl.multiple_of(rc * row_chunk, row_chunk)
        q = step * n_chunks + rc
        xb = x_buf[lax.rem(q, X_SLOTS)].astype(BF16)
        res = jnp.dot(xb, w_s[...], preferred_element_type=F32)
        u = res[:, :C] + bu
        for s in range(n_slabs):
            u_pad[s, pl.ds(r0 + SUBLANES, row_chunk), :] = u[:, s * LANES:(s + 1) * LANES]
        gl = res[:, C:] + bl
        gate_s[pl.ds(r0, row_chunk), :] = gl * jax.nn.sigmoid(gl)
        return carry

    neg_lam = -slab_cols(lam_ref)
    softplus = jnp.maximum(neg_lam, 0.0) + jnp.log1p(jnp.exp(-jnp.abs(neg_lam)))
    decay_rate = LRU_C * softplus

    def gate_chunk(rc, carry):
        r0 = pl.multiple_of(rc * row_chunk, row_chunk)
        parts = []
        for s in range(n_slabs):
            lanes = slice(s * LANES, (s + 1) * LANES)
            acc = cb[:, lanes]
            for j in range(CONV_W):
                off = SUBLANES - (CONV_W - 1) + j
                acc = acc + u_pad[s, pl.ds(r0 + off, row_chunk), :] * cw[j:j + 1, lanes]
            parts.append(acc)
        uc = jnp.concatenate(parts, axis=1)
        gates = jnp.dot(uc.astype(BF16), wg_s[...], preferred_element_type=F32)
        r = jax.nn.sigmoid(gates[:, :C] + br)
        i = jax.nn.sigmoid(gates[:, C:] + bi)
        neg_log_a = r * decay_rate
        a = jnp.exp(-neg_log_a)
        one_minus_a2 = jnp.maximum(jnp.tanh(neg_log_a) * (a * a + 1.0), 0.0)
        mult = jnp.exp2(jnp.log(one_minus_a2) * (0.5 * LOG2_E))
        u = mult * (i * uc)
        for k in range(seg_per_chunk):
            start = (rc * seg_per_chunk + k) * pitch
            rows = slice(k * seg_len, (k + 1) * seg_len)
            for s in range(n_slabs):
                lanes = slice(s * LANES, (s + 1) * LANES)
                a_p[s, pl.ds(start, seg_len), :] = a[rows, lanes]
                h_p[s, pl.ds(start, seg_len), :] = u[rows, lanes]
        return carry

    def gate_then_proj(rc, carry):
        x_arrive(rc + 1)
        gate_chunk(rc, carry)
        return proj_chunk(rc + 1, carry)

    x_arrive(0)
    proj_chunk(0, 0)
    lax.fori_loop(0, n_chunks - 1, gate_then_proj, 0)
    gate_chunk(n_chunks - 1, 0)

    chains = [(s, v) for s in range(n_slabs) for v in range(SCAN_VREGS)]

    def scan_step(j, carry):
        out = []
        for c, (s, v) in enumerate(chains):
            rows = pl.ds(v * SUBLANES * pitch + j, SUBLANES, stride=pitch)
            h_prev, p_prev = carry[2 * c], carry[2 * c + 1]
            a = a_p[s, rows, :]
            h = a * h_prev + h_p[s, rows, :]
            p = a * p_prev
            h_p[s, rows, :] = h
            a_p[s, rows, :] = p
            out += [h, p]
        return tuple(out)

    init = (jnp.zeros((SUBLANES, LANES), F32), jnp.ones((SUBLANES, LANES), F32)) * len(chains)
    ends = lax.fori_loop(0, seg_len, scan_step, init, unroll=8)

    for s in range(n_slabs):
        lanes = slice(s * LANES, (s + 1) * LANES)
        h_in = jnp.zeros((1, LANES), F32)
        for seg in range(n_seg):
            c, sub = chains.index((s, seg // SUBLANES)), seg % SUBLANES
            h_end, p_end = ends[2 * c], ends[2 * c + 1]
            rows = slice(seg * pitch, seg * pitch + seg_len)
            h = h_p[s, rows, :] + a_p[s, rows, :] * h_in
            nat = slice(seg * seg_len, (seg + 1) * seg_len)
            o_ref[nat, lanes] = (h * gate_s[nat, lanes]).astype(BF16)
            h_in = h_end[sub:sub + 1] + p_end[sub:sub + 1] * h_in


def _rglru(x, w_in, b_in, conv_w, conv_b, lru_wr, lru_br, lru_wi, lru_bi, lam):
    B, S, D = x.shape
    C = LRU_SLAB
    n_slabs = LRU_W // C
    per = C // LRU_BLOCK
    n_seg = SCAN_VREGS * SUBLANES
    pitch = S // n_seg + SUBLANES // 2
    row_chunk = LRU_ROW_CHUNK
    kern = functools.partial(_rglru_kernel, S=S, row_chunk=row_chunk, pitch=pitch)
    col = lambda c0: (lambda b, c: (0, c0 // C + c))
    whole = lambda a: pl.BlockSpec(a.shape, lambda b, c: (0, 0), pipeline_mode=ONCE)
    blocks = lambda: pl.BlockSpec((per, LRU_BLOCK, LRU_BLOCK), lambda b, c: (c, 0, 0))
    return pl.pallas_call(
        kern,
        grid=(B, n_slabs),
        in_specs=[
            pl.BlockSpec(memory_space=pl.ANY),
            pl.BlockSpec((D, C), col(COL_U)), pl.BlockSpec((D, C), col(COL_GATE_L)),
            blocks(), blocks(),
            whole(b_in), whole(conv_w), whole(conv_b), whole(lru_br), whole(lru_bi), whole(lam),
        ],
        out_specs=pl.BlockSpec((None, S, C), lambda b, c: (b, 0, c)),
        out_shape=jax.ShapeDtypeStruct((B, S, LRU_W), BF16),
        scratch_shapes=[
            pltpu.VMEM((D, 2 * C), BF16),
            pltpu.VMEM((C, 2 * C), BF16),
            pltpu.VMEM((C // LANES, S + SUBLANES, LANES), F32),
            pltpu.VMEM((S, C), F32),
            pltpu.VMEM((C // LANES, n_seg * pitch, LANES), F32),
            pltpu.VMEM((C // LANES, n_seg * pitch, LANES), F32),
            pltpu.VMEM((X_SLOTS, row_chunk, D), F32),
            pltpu.SemaphoreType.DMA((X_SLOTS,)),
        ],
        compiler_params=_cparams(2),
        name="rglru",
    )(x, w_in, w_in, lru_wr, lru_wi, b_in, conv_w, conv_b, lru_br, lru_bi, lam)


def _merge_out_kernel(x_ref, attn_ref, hg_ref, wga_ref, wgb_ref, bga_ref, bgb_ref,
                      wa_ref, wl_ref, wo_ref, bo_ref, lng_ref, lnb_ref, o_ref,
                      wg_s, wa_s, wl_s, wo_s):
    D = x_ref.shape[1]

    @pl.when(pl.program_id(0) == 0)
    def _():
        rows = CAST_ROWS
        for r0 in range(0, D, rows):
            sl = slice(r0, r0 + rows)
            wg_s[sl, :D] = wga_ref[sl, :].astype(BF16)
            wg_s[sl, D:] = wgb_ref[sl, :].astype(BF16)
            wl_s[sl, :] = wl_ref[sl, :].astype(BF16)
            wo_s[sl, :] = wo_ref[sl, :].astype(BF16)
        for r0 in range(0, ATTN_W, rows):
            sl = slice(r0, r0 + rows)
            wa_s[sl, :] = wa_ref[sl, :].astype(BF16)

    half = x_ref.shape[0] // MERGE_SUBTILES
    for hb in range(MERGE_SUBTILES):
        rows = slice(hb * half, (hb + 1) * half)
        x = x_ref[rows, :]
        xb = x.astype(BF16)
        gate_a = jax.nn.sigmoid(
            jnp.dot(xb, wg_s[:, :D], preferred_element_type=F32) + bga_ref[...])
        gate_b = jax.nn.sigmoid(
            jnp.dot(xb, wg_s[:, D:], preferred_element_type=F32) + bgb_ref[...])
        y_a = jnp.dot(attn_ref[rows, :].astype(BF16), wa_s[...], preferred_element_type=F32)
        y_b = jnp.dot(hg_ref[rows, :], wl_s[...], preferred_element_type=F32)
        merged = gate_a * y_a + gate_b * y_b
        out = jnp.dot(merged.astype(BF16), wo_s[...], preferred_element_type=F32) + bo_ref[...]
        y = ALPHA * x + out
        mu = jnp.mean(y, axis=1, keepdims=True)
        yc = y - mu
        var = jnp.mean(yc * yc, axis=1, keepdims=True)
        o_ref[rows, :] = yc * lax.rsqrt(var + LN_EPS) * lng_ref[...] + lnb_ref[...]


def _merge_out(x2, attn2, hg2, w_in, b_in, w_a, w_l, w_o, b_o, ln_g, ln_b):
    T, D = x2.shape
    tm = MERGE_TILE_ROWS
    once = lambda shape, idx: pl.BlockSpec(shape, lambda i: idx, pipeline_mode=ONCE)
    merge_blk = COL_MERGE // D
    return pl.pallas_call(
        _merge_out_kernel,
        grid=(T // tm,),
        in_specs=[
            pl.BlockSpec((tm, D), lambda i: (i, 0)),
            pl.BlockSpec((tm, ATTN_W), lambda i: (i, 0)),
            pl.BlockSpec((tm, LRU_W), lambda i: (i, 0)),
            once((D, D), (0, merge_blk)), once((D, D), (0, merge_blk + 1)),
            once((1, D), (0, merge_blk)), once((1, D), (0, merge_blk + 1)),
            once((ATTN_W, D), (0, 0)), once((LRU_W, D), (0, 0)), once((D, D), (0, 0)),
            once((1, D), (0, 0)), once((1, D), (0, 0)), once((1, D), (0, 0)),
        ],
        out_specs=pl.BlockSpec((tm, D), lambda i: (i, 0)),
        out_shape=jax.ShapeDtypeStruct((T, D), F32),
        scratch_shapes=[
            pltpu.VMEM((D, 2 * D), BF16), pltpu.VMEM((ATTN_W, D), BF16),
            pltpu.VMEM((LRU_W, D), BF16), pltpu.VMEM((D, D), BF16),
        ],
        compiler_params=_cparams(1),
        name="merge_out",
    )(x2, attn2, hg2, w_in, w_in, b_in, b_in, w_a, w_l, w_o, b_o, ln_g, ln_b)


def kernel(x, w_in, b_in, conv_w, conv_b, lru_wr, lru_br, lru_wi, lru_bi, lru_lambda,
           w_attn_proj, w_lru_proj, w_out, b_out, ln_gain, ln_bias):
    B, S, D = x.shape
    assert w_in.shape[0] == DEPTH and S % (CLASSES * BAND) == 0 and D == D_MODEL
    assert COL_MERGE % D == 0 and w_in.shape[2] == COL_MERGE + 2 * D
    row = lambda v: v.reshape(1, -1)
    for l in range(DEPTH):
        wl, bl = w_in[l], row(b_in[l])
        attn = _attention(x, *_attn_weights(wl, bl))
        hg = _rglru(x, wl, bl, conv_w[l], row(conv_b[l]), lru_wr[l], row(lru_br[l]),
                    lru_wi[l], row(lru_bi[l]), row(lru_lambda[l]))
        out = _merge_out(
            x.reshape(B * S, D), attn.reshape(B * S, ATTN_W), hg.reshape(B * S, LRU_W),
            wl, bl, w_attn_proj[l], w_lru_proj[l], w_out[l],
            row(b_out[l]), row(ln_gain[l]), row(ln_bias[l]))
        x = out.reshape(B, S, D)
    return x
```

```python
import functools
import math

import jax
import jax.numpy as jnp
import numpy as np
from jax import lax
from jax.experimental import pallas as pl
from jax.experimental.pallas import tpu as pltpu

LANES = 128
SUBLANES = 8
BF16_ROWS = 16
VMEM_LIMIT_BYTES = 56 * 1024 * 1024

D_MODEL = 1024
HEAD_DIM = 64
HALF = HEAD_DIM // 2
HEADS_PER_GROUP = 8
DILATIONS = (1, 4, 16)
N_GROUPS = len(DILATIONS)
BAND = 128
CLASSES = 16
QKV_W = N_GROUPS * HEADS_PER_GROUP * HEAD_DIM
ATTN_W = HEADS_PER_GROUP * HEAD_DIM
N_PAIRS = ATTN_W // LANES
LRU_W = 1024
LRU_BLOCK = 64
LRU_SLAB = 256
CONV_W = 4
LRU_C = 8.0
ROPE_THETA = 10000.0
LOG2_E = math.log2(math.e)
NEG_INF = -1e30
LN_EPS = 1e-5
DEPTH = 1
ALPHA = (2.0 * DEPTH) ** 0.25
GATE_SLAB = 3
ATTN_SLAB_LAYOUT = ([(0, 0), (0, 1), (0, 2), (None, GATE_SLAB)]
                    + [(g, kind) for g in range(1, N_GROUPS) for kind in range(3)])
N_ATTN_SLABS = len(ATTN_SLAB_LAYOUT)
ATTN_COLS = N_ATTN_SLABS * LANES
COL_K = QKV_W
COL_V = 2 * QKV_W
COL_GATE_A = 3 * QKV_W
COL_U = COL_GATE_A + ATTN_W
COL_GATE_L = COL_U + LRU_W
COL_MERGE = COL_GATE_L + LRU_W

ATTN_ROW_CHUNK = 1024
LRU_ROW_CHUNK = 1024
MERGE_TILE_ROWS = 512
MERGE_SUBTILES = 2
CAST_ROWS = 256
X_AHEAD = 2
X_SLOTS = X_AHEAD + 1
SCAN_VREGS = 2

BF16 = jnp.bfloat16
F32 = jnp.float32
ONCE = pl.Buffered(1)


def _cparams(n_axes, flags=None):
    return pltpu.CompilerParams(dimension_semantics=("arbitrary",) * n_axes,
                                vmem_limit_bytes=VMEM_LIMIT_BYTES, flags=flags)


def _band_bias():
    out = np.zeros((N_GROUPS, 2, 2 * BAND, 2 * BAND), np.float32)
    i = np.arange(BAND)
    for g, d in enumerate(DILATIONS):
        R = CLASSES // d
        L = BAND // R
        m = R * (i % L) + i // L
        m_k = np.concatenate([m - BAND, m])
        dist = m[:, None] - m_k[None, :]
        band = (dist >= 0) & (dist <= BAND)
        for first in range(2):
            valid = band & ((np.arange(2 * BAND) >= BAND)[None, :] | (first == 0))
            bias = np.where(valid, 0.0, NEG_INF).astype(np.float32)
            out[g, first] = np.concatenate([bias, bias], axis=0)
    return out


def _rope_tables(S):
    A = S // CLASSES
    r = np.arange(S)
    pos = (CLASSES * (r % A) + r // A).astype(np.float64)
    inv_freq = ROPE_THETA ** (-np.arange(HALF, dtype=np.float64) / HALF)
    ang = pos[:, None] * inv_freq[None, :]
    cos_t = np.tile(np.cos(ang), (1, LANES // HALF))
    sin = np.sin(ang)
    sin_t = np.concatenate([-sin, -sin, sin, sin], axis=1)
    return cos_t.astype(np.float32), sin_t.astype(np.float32)


def _pair_interleave(w):
    lane = lax.broadcasted_iota(jnp.int32, w.shape, 1)
    from_right = pltpu.roll(w, LANES - HALF, axis=1)
    from_left = pltpu.roll(w, HALF, axis=1)
    quarter = lane // HALF
    return jnp.where(quarter == 1, from_right, jnp.where(quarter == 2, from_left, w))


def _attn_weights_kernel(*refs):
    w_refs = refs[:N_ATTN_SLABS]
    b_refs = refs[N_ATTN_SLABS:2 * N_ATTN_SLABS]
    w_out, b_out = refs[2 * N_ATTN_SLABS:]
    D = w_out.shape[0]
    w_rows = CAST_ROWS
    for i, (g, kind) in enumerate(ATTN_SLAB_LAYOUT):
        is_qk = kind in (0, 1)
        cols = slice(i * LANES, (i + 1) * LANES)
        for r0 in range(0, D, w_rows):
            w = w_refs[i][r0:r0 + w_rows, :]
            w_out[r0:r0 + w_rows, cols] = (_pair_interleave(w) if is_qk else w).astype(BF16)
        b = jnp.broadcast_to(b_refs[i][...], (SUBLANES, LANES))
        b_out[:, cols] = _pair_interleave(b) if is_qk else b


def _attn_weights(w_in, b_in):
    D = w_in.shape[0]
    kind_col = (0, COL_K, COL_V)
    slab0 = [COL_GATE_A // LANES if kind == GATE_SLAB else (kind_col[kind] + g * ATTN_W) // LANES
             for g, kind in ATTN_SLAB_LAYOUT]
    w_specs = [pl.BlockSpec((D, LANES), functools.partial(lambda p, s: (0, s + p), s=s))
               for s in slab0]
    b_specs = [pl.BlockSpec((1, LANES), functools.partial(lambda p, s: (0, s + p), s=s))
               for s in slab0]
    return pl.pallas_call(
        _attn_weights_kernel,
        grid=(N_PAIRS,),
        in_specs=w_specs + b_specs,
        out_specs=[pl.BlockSpec((None, D, ATTN_COLS), lambda p: (p, 0, 0)),
                   pl.BlockSpec((None, SUBLANES, ATTN_COLS), lambda p: (p, 0, 0))],
        out_shape=[jax.ShapeDtypeStruct((N_PAIRS, D, ATTN_COLS), BF16),
                   jax.ShapeDtypeStruct((N_PAIRS, SUBLANES, ATTN_COLS), F32)],
        compiler_params=_cparams(1),
        name="attn_weights",
    )(*([w_in] * N_ATTN_SLABS), *([b_in] * N_ATTN_SLABS))


def _attention_kernel(x_hbm, w_s, b_s, cos_ref, sin_ref, bias_ref, o_ref,
                      qkv0_s, qkv_s, gate_s, acc_s, m_s, l_s, x_buf, x_sem, *, S, row_chunk):
    A = S // CLASSES
    per = row_chunk // A
    n_chunks = S // row_chunk

    step = pl.program_id(0) * pl.num_programs(1) + pl.program_id(1)
    total_chunks = pl.num_programs(0) * pl.num_programs(1) * n_chunks

    def x_copies(q):
        b = lax.div(q, pl.num_programs(1) * n_chunks)
        c0 = lax.rem(q, n_chunks) * per
        slot = lax.rem(q, 2)
        return [pltpu.make_async_copy(x_hbm.at[b, :, c0 + i, :], x_buf.at[slot, i],
                                      x_sem.at[slot, i]) for i in range(per)]

    @pl.when(step == 0)
    def _():
        for cp in x_copies(0):
            cp.start()

    def qkv_ref(g, i):
        return qkv0_s.at[i] if g == 0 else qkv_s.at[3 * (g - 1) + i]

    n_tiles = S // BAND
    lane = lax.broadcasted_iota(jnp.int32, (BAND, LANES), 1)
    head0_q = ((lane // HALF) % 2) == 0
    head0_v = lane < HEAD_DIM

    def head_masks(keys):
        h0 = lax.broadcasted_iota(jnp.int32, (keys, LANES), 1) < HEAD_DIM
        return h0, h0.astype(BF16), 1 - h0.astype(BF16)
    q_scale = (HEAD_DIM ** -0.5) * LOG2_E

    def proj_chunk(rc, slab_lo, slab_hi):
        q = step * n_chunks + rc
        for cp in x_copies(q):
            cp.wait()

        @pl.when(q + 1 < total_chunks)
        def _():
            for cp in x_copies(q + 1):
                cp.start()

        rows = pl.ds(pl.multiple_of(rc * row_chunk, row_chunk), row_chunk)
        cols = slice(slab_lo * LANES, slab_hi * LANES)
        xb = x_buf[lax.rem(q, 2)].reshape(row_chunk, x_buf.shape[-1]).astype(BF16)
        res = jnp.dot(xb, w_s[:, cols], preferred_element_type=F32) + b_s[0:1, cols]
        cos = cos_ref[rows, :]
        sin = sin_ref[rows, :]
        for i, (g, kind) in enumerate(ATTN_SLAB_LAYOUT[slab_lo:slab_hi]):
            val = res[:, i * LANES:(i + 1) * LANES]
            if kind == GATE_SLAB:
                gate_s[rows, :] = (val * jax.nn.sigmoid(val)).astype(BF16)
                continue
            if kind in (0, 1):
                val = val * cos + pltpu.roll(val, LANES // 2, axis=1) * sin
            if kind == 0:
                val = val * q_scale
            ref = qkv_ref(g, kind)
            ref[rows, :] = val.astype(ref.dtype)

    def tiles(g, is_first_group):
        d = DILATIONS[g]
        R = CLASSES // d
        L = BAND // R
        nb = S // (d * BAND)

        def make_run(e):
            rho, n = divmod(e, nb)

            def run(j, nn):
                start = (rho + d * j) * A + L * nn
                return slice(start, start + L)
            return n, run

        def gather(ref, run, nn):
            return jnp.concatenate([ref[run(j, nn), :] for j in range(R)], axis=0)

        def key_window(ref, run, n):
            cur = gather(ref, run, n)
            if n == 0:
                return cur.astype(BF16)
            return jnp.concatenate([gather(ref, run, n - 1), cur], axis=0).astype(BF16)

        def scores(n, run):
            q_t = gather(qkv_ref(g, 0), run, n).astype(BF16)
            zero = jnp.zeros_like(q_t)
            qm = jnp.concatenate([jnp.where(head0_q, q_t, zero),
                                  jnp.where(head0_q, zero, q_t)], axis=0)
            kwin = key_window(qkv_ref(g, 1), run, n)
            s = lax.dot_general(qm, kwin, (((1,), (1,)), ((), ())),
                                preferred_element_type=F32)
            if n == 0:
                return s + bias_ref[g, 1, :, BAND:]
            return s + bias_ref[g, 0]

        def softmax(s):
            m_blk = jnp.max(s, axis=1, keepdims=True)
            p = jnp.exp2(s - m_blk)
            return p.astype(BF16), m_blk

        def finish(n, run, p, m_blk):
            vwin = key_window(qkv_ref(g, 2), run, n)
            keys = vwin.shape[0]
            zero_v = jnp.zeros_like(vwin)
            h0, ones_h0, ones_h1 = head_masks(keys)
            v_bd = jnp.concatenate([
                jnp.concatenate([jnp.where(h0, vwin, zero_v), ones_h0], axis=1),
                jnp.concatenate([jnp.where(h0, zero_v, vwin), ones_h1], axis=1)], axis=0)
            p_cat = jnp.concatenate([p[:BAND], p[BAND:]], axis=1)
            pv = jnp.dot(p_cat, v_bd, preferred_element_type=F32)
            o_t = pv[:, :LANES]
            m_t = jnp.where(head0_v, m_blk[:BAND], m_blk[BAND:])
            l_t = pv[:, LANES:]
            if not is_first_group:
                m_old = gather(m_s, run, n)
                m_new = jnp.maximum(m_old, m_t)
                w_old = jnp.exp2(m_old - m_new)
                w_new = jnp.exp2(m_t - m_new)
                m_t = m_new
                l_t = gather(l_s, run, n) * w_old + l_t * w_new
                o_t = gather(acc_s, run, n) * w_old + o_t * w_new
            for j in range(R):
                m_s[run(j, n), :] = m_t[j * L:(j + 1) * L]
                l_s[run(j, n), :] = l_t[j * L:(j + 1) * L]
                acc_s[run(j, n), :] = o_t[j * L:(j + 1) * L]

        for e in range(n_tiles):
            n, run = make_run(e)
            finish(n, run, *softmax(scores(n, run)))

    def proj_all(rc, carry):
        proj_chunk(rc, 0, N_ATTN_SLABS)
        return carry

    lax.fori_loop(0, S // row_chunk, proj_all, 0)
    for g in range(N_GROUPS):
        tiles(g, g == 0)

    for c in range(CLASSES):
        rows = slice(c * A, (c + 1) * A)
        res = (acc_s[rows, :] / l_s[rows, :]) * gate_s[rows, :].astype(F32)
        o_ref[pl.ds(c, A, stride=CLASSES), :] = res


def _attention(x, w_attn, b_attn):
    B, S, D = x.shape
    A = S // CLASSES
    row_chunk = ATTN_ROW_CHUNK
    kern = functools.partial(_attention_kernel, S=S, row_chunk=row_chunk)
    cos_t, sin_t = _rope_tables(S)
    const2 = pl.BlockSpec((S, LANES), lambda b, p: (0, 0), pipeline_mode=ONCE)
    scratch = [
        pltpu.VMEM((3, S, LANES), F32),
        pltpu.VMEM((3 * (N_GROUPS - 1), S, LANES), BF16),
        pltpu.VMEM((S, LANES), BF16),
        pltpu.VMEM((S, LANES), F32),
        pltpu.VMEM((S, LANES), F32),
        pltpu.VMEM((S, LANES), F32),
        pltpu.VMEM((2, row_chunk // A, A, D), F32),
        pltpu.SemaphoreType.DMA((2, row_chunk // A)),
    ]
    return pl.pallas_call(
        kern,
        grid=(B, N_PAIRS),
        in_specs=[
            pl.BlockSpec(memory_space=pl.ANY),
            pl.BlockSpec((None, D, ATTN_COLS), lambda b, p: (p, 0, 0)),
            pl.BlockSpec((None, SUBLANES, ATTN_COLS), lambda b, p: (p, 0, 0)),
            const2, const2,
            pl.BlockSpec((N_GROUPS, 2, 2 * BAND, 2 * BAND), lambda b, p: (0, 0, 0, 0),
                         pipeline_mode=ONCE),
        ],
        out_specs=pl.BlockSpec((None, S, LANES), lambda b, p: (b, 0, p)),
        out_shape=jax.ShapeDtypeStruct((B, S, ATTN_W), F32),
        scratch_shapes=scratch,
        compiler_params=_cparams(2),
        name="attention",
    )(x.reshape(B, A, CLASSES, D), w_attn, b_attn, cos_t, sin_t, _band_bias())


def _rglru_kernel(x_hbm, wu_ref, wl_ref, wr_ref, wi_ref, b_in_ref, cw_ref, cb_ref,
                  br_ref, bi_ref, lam_ref, o_ref,
                  w_s, wg_s, u_pad, gate_s, a_p, h_p, x_buf, x_sem, *, S, row_chunk, pitch):
    C = LRU_SLAB
    n_slabs = C // LANES
    n_chunks = S // row_chunk
    n_seg = SCAN_VREGS * SUBLANES
    seg_len = S // n_seg
    seg_per_chunk = row_chunk // seg_len
    u_pad[:, 0:SUBLANES, :] = jnp.zeros((n_slabs, SUBLANES, LANES), F32)

    def slab_cols(ref, col0=0):
        start = pl.multiple_of(col0 + pl.program_id(1) * C, LANES)
        return ref[:, pl.ds(start, C)]

    bu, bl = slab_cols(b_in_ref, COL_U), slab_cols(b_in_ref, COL_GATE_L)
    cw, cb = slab_cols(cw_ref), slab_cols(cb_ref)
    br, bi = slab_cols(br_ref), slab_cols(bi_ref)

    w_s[:, :C] = wu_ref[...].astype(BF16)
    w_s[:, C:] = wl_ref[...].astype(BF16)
    wg_s[...] = jnp.zeros(wg_s.shape, BF16)
    for blk in range(C // LRU_BLOCK):
        rows = slice(blk * LRU_BLOCK, (blk + 1) * LRU_BLOCK)
        wg_s[rows, blk * LRU_BLOCK:(blk + 1) * LRU_BLOCK] = wr_ref[blk].astype(BF16)
        wg_s[rows, C + blk * LRU_BLOCK:C + (blk + 1) * LRU_BLOCK] = wi_ref[blk].astype(BF16)

    step = pl.program_id(0) * pl.num_programs(1) + pl.program_id(1)
    total_chunks = pl.num_programs(0) * pl.num_programs(1) * n_chunks
    chunks_per_batch = pl.num_programs(1) * n_chunks

    def x_copy(q):
        b = lax.div(q, chunks_per_batch)
        rows = pl.ds(pl.multiple_of(lax.rem(q, n_chunks) * row_chunk, row_chunk), row_chunk)
        slot = lax.rem(q, X_SLOTS)
        return pltpu.make_async_copy(x_hbm.at[b, rows, :], x_buf.at[slot], x_sem.at[slot])

    @pl.when(step == 0)
    def _():
        for q in range(X_AHEAD):
            x_copy(q).start()

    def x_arrive(rc):
        q = step * n_chunks + rc
        x_copy(q).wait()

        @pl.when(q + X_AHEAD < total_chunks)
        def _():
            x_copy(q + X_AHEAD).start()

    def proj_chunk(rc, carry):
        r0 = pl.multiple_of(rc * row_chunk, row_chunk)
        q = step * n_chunks + rc
        xb = x_buf[lax.rem(q, X_SLOTS)].astype(BF16)
        res = jnp.dot(xb, w_s[...], preferred_element_type=F32)
        u = res[:, :C] + bu
        for s in range(n_slabs):
            u_pad[s, pl.ds(r0 + SUBLANES, row_chunk), :] = u[:, s * LANES:(s + 1) * LANES]
        gl = res[:, C:] + bl
        gate_s[pl.ds(r0, row_chunk), :] = gl * jax.nn.sigmoid(gl)
        return carry

    neg_lam = -slab_cols(lam_ref)
    softplus = jnp.maximum(neg_lam, 0.0) + jnp.log1p(jnp.exp(-jnp.abs(neg_lam)))
    decay_rate = LRU_C * softplus

    def gate_chunk(rc, carry):
        r0 = pl.multiple_of(rc * row_chunk, row_chunk)
        parts = []
        for s in range(n_slabs):
            lanes = slice(s * LANES, (s + 1) * LANES)
            acc = cb[:, lanes]
            for j in range(CONV_W):
                off = SUBLANES - (CONV_W - 1) + j
                acc = acc + u_pad[s, pl.ds(r0 + off, row_chunk), :] * cw[j:j + 1, lanes]
            parts.append(acc)
        uc = jnp.concatenate(parts, axis=1)
        gates = jnp.dot(uc.astype(BF16), wg_s[...], preferred_element_type=F32)
        r = jax.nn.sigmoid(gates[:, :C] + br)
        i = jax.nn.sigmoid(gates[:, C:] + bi)
        neg_log_a = r * decay_rate
        a = jnp.exp(-neg_log_a)
        one_minus_a2 = jnp.maximum(jnp.tanh(neg_log_a) * (a * a + 1.0), 0.0)
        mult = jnp.exp2(jnp.log(one_minus_a2) * (0.5 * LOG2_E))
        u = mult * (i * uc)
        for k in range(seg_per_chunk):
            start = (rc * seg_per_chunk + k) * pitch
            rows = slice(k * seg_len, (k + 1) * seg_len)
            for s in range(n_slabs):
                lanes = slice(s * LANES, (s + 1) * LANES)
                a_p[s, pl.ds(start, seg_len), :] = a[rows, lanes]
                h_p[s, pl.ds(start, seg_len), :] = u[rows, lanes]
        return carry

    def gate_then_proj(rc, carry):
        x_arrive(rc + 1)
        gate_chunk(rc, carry)
        return proj_chunk(rc + 1, carry)

    x_arrive(0)
    proj_chunk(0, 0)
    lax.fori_loop(0, n_chunks - 1, gate_then_proj, 0)
    gate_chunk(n_chunks - 1, 0)

    chains = [(s, v) for s in range(n_slabs) for v in range(SCAN_VREGS)]

    def scan_step(j, carry):
        out = []
        for c, (s, v) in enumerate(chains):
            rows = pl.ds(v * SUBLANES * pitch + j, SUBLANES, stride=pitch)
            h_prev, p_prev = carry[2 * c], carry[2 * c + 1]
            a = a_p[s, rows, :]
            h = a * h_prev + h_p[s, rows, :]
            p = a * p_prev
            h_p[s, rows, :] = h
            a_p[s, rows, :] = p
            out += [h, p]
        return tuple(out)

    init = (jnp.zeros((SUBLANES, LANES), F32), jnp.ones((SUBLANES, LANES), F32)) * len(chains)
    ends = lax.fori_loop(0, seg_len, scan_step, init, unroll=8)

    for s in range(n_slabs):
        lanes = slice(s * LANES, (s + 1) * LANES)
        h_in = jnp.zeros((1, LANES), F32)
        for seg in range(n_seg):
            c, sub = chains.index((s, seg // SUBLANES)), seg % SUBLANES
            h_end, p_end = ends[2 * c], ends[2 * c + 1]
            rows = slice(seg * pitch, seg * pitch + seg_len)
            h = h_p[s, rows, :] + a_p[s, rows, :] * h_in
            nat = slice(seg * seg_len, (seg + 1) * seg_len)
            o_ref[nat, lanes] = (h * gate_s[nat, lanes]).astype(BF16)
            h_in = h_end[sub:sub + 1] + p_end[sub:sub + 1] * h_in


def _rglru(x, w_in, b_in, conv_w, conv_b, lru_wr, lru_br, lru_wi, lru_bi, lam):
    B, S, D = x.shape
    C = LRU_SLAB
    n_slabs = LRU_W // C
    per = C // LRU_BLOCK
    n_seg = SCAN_VREGS * SUBLANES
    pitch = S // n_seg + SUBLANES // 2
    row_chunk = LRU_ROW_CHUNK
    kern = functools.partial(_rglru_kernel, S=S, row_chunk=row_chunk, pitch=pitch)
    col = lambda c0: (lambda b, c: (0, c0 // C + c))
    whole = lambda a: pl.BlockSpec(a.shape, lambda b, c: (0, 0), pipeline_mode=ONCE)
    blocks = lambda: pl.BlockSpec((per, LRU_BLOCK, LRU_BLOCK), lambda b, c: (c, 0, 0))
    return pl.pallas_call(
        kern,
        grid=(B, n_slabs),
        in_specs=[
            pl.BlockSpec(memory_space=pl.ANY),
            pl.BlockSpec((D, C), col(COL_U)), pl.BlockSpec((D, C), col(COL_GATE_L)),
            blocks(), blocks(),
            whole(b_in), whole(conv_w), whole(conv_b), whole(lru_br), whole(lru_bi), whole(lam),
        ],
        out_specs=pl.BlockSpec((None, S, C), lambda b, c: (b, 0, c)),
        out_shape=jax.ShapeDtypeStruct((B, S, LRU_W), BF16),
        scratch_shapes=[
            pltpu.VMEM((D, 2 * C), BF16),
            pltpu.VMEM((C, 2 * C), BF16),
            pltpu.VMEM((C // LANES, S + SUBLANES, LANES), F32),
            pltpu.VMEM((S, C), F32),
            pltpu.VMEM((C // LANES, n_seg * pitch, LANES), F32),
            pltpu.VMEM((C // LANES, n_seg * pitch, LANES), F32),
            pltpu.VMEM((X_SLOTS, row_chunk, D), F32),
            pltpu.SemaphoreType.DMA((X_SLOTS,)),
        ],
        compiler_params=_cparams(2),
        name="rglru",
    )(x, w_in, w_in, lru_wr, lru_wi, b_in, conv_w, conv_b, lru_br, lru_bi, lam)


def _merge_out_kernel(x_ref, attn_ref, hg_ref, wga_ref, wgb_ref, bga_ref, bgb_ref,
                      wa_ref, wl_ref, wo_ref, bo_ref, lng_ref, lnb_ref, o_ref,
                      wg_s, wa_s, wl_s, wo_s):
    D = x_ref.shape[1]

    @pl.when(pl.program_id(0) == 0)
    def _():
        rows = CAST_ROWS
        for r0 in range(0, D, rows):
            sl = slice(r0, r0 + rows)
            wg_s[sl, :D] = wga_ref[sl, :].astype(BF16)
            wg_s[sl, D:] = wgb_ref[sl, :].astype(BF16)
            wl_s[sl, :] = wl_ref[sl, :].astype(BF16)
            wo_s[sl, :] = wo_ref[sl, :].astype(BF16)
        for r0 in range(0, ATTN_W, rows):
            sl = slice(r0, r0 + rows)
            wa_s[sl, :] = wa_ref[sl, :].astype(BF16)

    half = x_ref.shape[0] // MERGE_SUBTILES
    for hb in range(MERGE_SUBTILES):
        rows = slice(hb * half, (hb + 1) * half)
        x = x_ref[rows, :]
        xb = x.astype(BF16)
        gate_a = jax.nn.sigmoid(
            jnp.dot(xb, wg_s[:, :D], preferred_element_type=F32) + bga_ref[...])
        gate_b = jax.nn.sigmoid(
            jnp.dot(xb, wg_s[:, D:], preferred_element_type=F32) + bgb_ref[...])
        y_a = jnp.dot(attn_ref[rows, :].astype(BF16), wa_s[...], preferred_element_type=F32)
        y_b = jnp.dot(hg_ref[rows, :], wl_s[...], preferred_element_type=F32)
        merged = gate_a * y_a + gate_b * y_b
        out = jnp.dot(merged.astype(BF16), wo_s[...], preferred_element_type=F32) + bo_ref[...]
        y = ALPHA * x + out
        mu = jnp.mean(y, axis=1, keepdims=True)
        yc = y - mu
        var = jnp.mean(yc * yc, axis=1, keepdims=True)
        o_ref[rows, :] = yc * lax.rsqrt(var + LN_EPS) * lng_ref[...] + lnb_ref[...]


def _merge_out(x2, attn2, hg2, w_in, b_in, w_a, w_l, w_o, b_o, ln_g, ln_b):
    T, D = x2.shape
    tm = MERGE_TILE_ROWS
    once = lambda shape, idx: pl.BlockSpec(shape, lambda i: idx, pipeline_mode=ONCE)
    merge_blk = COL_MERGE // D
    return pl.pallas_call(
        _merge_out_kernel,
        grid=(T // tm,),
        in_specs=[
            pl.BlockSpec((tm, D), lambda i: (i, 0)),
            pl.BlockSpec((tm, ATTN_W), lambda i: (i, 0)),
            pl.BlockSpec((tm, LRU_W), lambda i: (i, 0)),
            once((D, D), (0, merge_blk)), once((D, D), (0, merge_blk + 1)),
            once((1, D), (0, merge_blk)), once((1, D), (0, merge_blk + 1)),
            once((ATTN_W, D), (0, 0)), once((LRU_W, D), (0, 0)), once((D, D), (0, 0)),
            once((1, D), (0, 0)), once((1, D), (0, 0)), once((1, D), (0, 0)),
        ],
        out_specs=pl.BlockSpec((tm, D), lambda i: (i, 0)),
        out_shape=jax.ShapeDtypeStruct((T, D), F32),
        scratch_shapes=[
            pltpu.VMEM((D, 2 * D), BF16), pltpu.VMEM((ATTN_W, D), BF16),
            pltpu.VMEM((LRU_W, D), BF16), pltpu.VMEM((D, D), BF16),
        ],
        compiler_params=_cparams(1),
        name="merge_out",
    )(x2, attn2, hg2, w_in, w_in, b_in, b_in, w_a, w_l, w_o, b_o, ln_g, ln_b)


def kernel(x, w_in, b_in, conv_w, conv_b, lru_wr, lru_br, lru_wi, lru_bi, lru_lambda,
           w_attn_proj, w_lru_proj, w_out, b_out, ln_gain, ln_bias):
    B, S, D = x.shape
    assert w_in.shape[0] == DEPTH and S % (CLASSES * BAND) == 0 and D == D_MODEL
    assert COL_MERGE % D == 0 and w_in.shape[2] == COL_MERGE + 2 * D
    row = lambda v: v.reshape(1, -1)
    for l in range(DEPTH):
        wl, bl = w_in[l], row(b_in[l])
        attn = _attention(x, *_attn_weights(wl, bl))
        hg = _rglru(x, wl, bl, conv_w[l], row(conv_b[l]), lru_wr[l], row(lru_br[l]),
                    lru_wi[l], row(lru_bi[l]), row(lru_lambda[l]))
        out = _merge_out(
            x.reshape(B * S, D), attn.reshape(B * S, ATTN_W), hg.reshape(B * S, LRU_W),
            wl, bl, w_attn_proj[l], w_lru_proj[l], w_out[l],
            row(b_out[l]), row(ln_gain[l]), row(ln_bias[l]))
        x = out.reshape(B, S, D)
    return x
```

```python
import functools
import math

import jax
import jax.numpy as jnp
import numpy as np
from jax import lax
from jax.experimental import pallas as pl
from jax.experimental.pallas import tpu as pltpu

LANES = 128
SUBLANES = 8
BF16_ROWS = 16
VMEM_LIMIT_BYTES = 56 * 1024 * 1024

D_MODEL = 1024
HEAD_DIM = 64
HALF = HEAD_DIM // 2
HEADS_PER_GROUP = 8
DILATIONS = (1, 4, 16)
N_GROUPS = len(DILATIONS)
BAND = 128
CLASSES = 16
QKV_W = N_GROUPS * HEADS_PER_GROUP * HEAD_DIM
ATTN_W = HEADS_PER_GROUP * HEAD_DIM
N_PAIRS = ATTN_W // LANES
LRU_W = 1024
LRU_BLOCK = 64
LRU_SLAB = 256
CONV_W = 4
LRU_C = 8.0
ROPE_THETA = 10000.0
LOG2_E = math.log2(math.e)
NEG_INF = -1e30
LN_EPS = 1e-5
DEPTH = 1
ALPHA = (2.0 * DEPTH) ** 0.25
GATE_SLAB = 3
ATTN_SLAB_LAYOUT = ([(0, 0), (0, 1), (0, 2), (None, GATE_SLAB)]
                    + [(g, kind) for g in range(1, N_GROUPS) for kind in range(3)])
N_ATTN_SLABS = len(ATTN_SLAB_LAYOUT)
ATTN_COLS = N_ATTN_SLABS * LANES
COL_K = QKV_W
COL_V = 2 * QKV_W
COL_GATE_A = 3 * QKV_W
COL_U = COL_GATE_A + ATTN_W
COL_GATE_L = COL_U + LRU_W
COL_MERGE = COL_GATE_L + LRU_W

ATTN_ROW_CHUNK = 1024
LRU_ROW_CHUNK = 1024
MERGE_TILE_ROWS = 512
MERGE_SUBTILES = 2
CAST_ROWS = 256
X_AHEAD = 2
X_SLOTS = X_AHEAD + 1
SCAN_VREGS = 2

BF16 = jnp.bfloat16
F32 = jnp.float32
ONCE = pl.Buffered(1)


def _cparams(n_axes, flags=None):
    return pltpu.CompilerParams(dimension_semantics=("arbitrary",) * n_axes,
                                vmem_limit_bytes=VMEM_LIMIT_BYTES, flags=flags)


def _band_bias():
    out = np.zeros((N_GROUPS, 2, 2 * BAND, 2 * BAND), np.float32)
    i = np.arange(BAND)
    for g, d in enumerate(DILATIONS):
        R = CLASSES // d
        L = BAND // R
        m = R * (i % L) + i // L
        m_k = np.concatenate([m - BAND, m])
        dist = m[:, None] - m_k[None, :]
        band = (dist >= 0) & (dist <= BAND)
        for first in range(2):
            valid = band & ((np.arange(2 * BAND) >= BAND)[None, :] | (first == 0))
            bias = np.where(valid, 0.0, NEG_INF).astype(np.float32)
            out[g, first] = np.concatenate([bias, bias], axis=0)
    return out


def _rope_tables(S):
    A = S // CLASSES
    r = np.arange(S)
    pos = (CLASSES * (r % A) + r // A).astype(np.float64)
    inv_freq = ROPE_THETA ** (-np.arange(HALF, dtype=np.float64) / HALF)
    ang = pos[:, None] * inv_freq[None, :]
    cos_t = np.tile(np.cos(ang), (1, LANES // HALF))
    sin = np.sin(ang)
    sin_t = np.concatenate([-sin, -sin, sin, sin], axis=1)
    return cos_t.astype(np.float32), sin_t.astype(np.float32)


def _pair_interleave(w):
    lane = lax.broadcasted_iota(jnp.int32, w.shape, 1)
    from_right = pltpu.roll(w, LANES - HALF, axis=1)
    from_left = pltpu.roll(w, HALF, axis=1)
    quarter = lane // HALF
    return jnp.where(quarter == 1, from_right, jnp.where(quarter == 2, from_left, w))


def _attn_weights_kernel(*refs):
    w_refs = refs[:N_ATTN_SLABS]
    b_refs = refs[N_ATTN_SLABS:2 * N_ATTN_SLABS]
    w_out, b_out = refs[2 * N_ATTN_SLABS:]
    D = w_out.shape[0]
    w_rows = CAST_ROWS
    for i, (g, kind) in enumerate(ATTN_SLAB_LAYOUT):
        is_qk = kind in (0, 1)
        cols = slice(i * LANES, (i + 1) * LANES)
        for r0 in range(0, D, w_rows):
            w = w_refs[i][r0:r0 + w_rows, :]
            w_out[r0:r0 + w_rows, cols] = (_pair_interleave(w) if is_qk else w).astype(BF16)
        b = jnp.broadcast_to(b_refs[i][...], (SUBLANES, LANES))
        b_out[:, cols] = _pair_interleave(b) if is_qk else b


def _attn_weights(w_in, b_in):
    D = w_in.shape[0]
    kind_col = (0, COL_K, COL_V)
    slab0 = [COL_GATE_A // LANES if kind == GATE_SLAB else (kind_col[kind] + g * ATTN_W) // LANES
             for g, kind in ATTN_SLAB_LAYOUT]
    w_specs = [pl.BlockSpec((D, LANES), functools.partial(lambda p, s: (0, s + p), s=s))
               for s in slab0]
    b_specs = [pl.BlockSpec((1, LANES), functools.partial(lambda p, s: (0, s + p), s=s))
               for s in slab0]
    return pl.pallas_call(
        _attn_weights_kernel,
        grid=(N_PAIRS,),
        in_specs=w_specs + b_specs,
        out_specs=[pl.BlockSpec((None, D, ATTN_COLS), lambda p: (p, 0, 0)),
                   pl.BlockSpec((None, SUBLANES, ATTN_COLS), lambda p: (p, 0, 0))],
        out_shape=[jax.ShapeDtypeStruct((N_PAIRS, D, ATTN_COLS), BF16),
                   jax.ShapeDtypeStruct((N_PAIRS, SUBLANES, ATTN_COLS), F32)],
        compiler_params=_cparams(1),
        name="attn_weights",
    )(*([w_in] * N_ATTN_SLABS), *([b_in] * N_ATTN_SLABS))


def _attention_kernel(x_hbm, w_s, b_s, cos_ref, sin_ref, bias_ref, o_ref,
                      qkv0_s, qkv_s, gate_s, acc_s, m_s, l_s, x_buf, x_sem, *, S, row_chunk):
    A = S // CLASSES
    per = row_chunk // A
    n_chunks = S // row_chunk

    step = pl.program_id(0) * pl.num_programs(1) + pl.program_id(1)
    total_chunks = pl.num_programs(0) * pl.num_programs(1) * n_chunks

    def x_copies(q):
        b = lax.div(q, pl.num_programs(1) * n_chunks)
        c0 = lax.rem(q, n_chunks) * per
        slot = lax.rem(q, X_SLOTS)
        return [pltpu.make_async_copy(x_hbm.at[b, :, c0 + i, :], x_buf.at[slot, i],
                                      x_sem.at[slot, i]) for i in range(per)]

    @pl.when(step == 0)
    def _():
        for q0 in range(X_AHEAD):
            for cp in x_copies(q0):
                cp.start()

    def qkv_ref(g, i):
        return qkv0_s.at[i] if g == 0 else qkv_s.at[3 * (g - 1) + i]

    n_tiles = S // BAND
    lane = lax.broadcasted_iota(jnp.int32, (BAND, LANES), 1)
    head0_q = ((lane // HALF) % 2) == 0
    head0_v = lane < HEAD_DIM

    def head_masks(keys):
        h0 = lax.broadcasted_iota(jnp.int32, (keys, LANES), 1) < HEAD_DIM
        return h0, h0.astype(BF16), 1 - h0.astype(BF16)
    q_scale = (HEAD_DIM ** -0.5) * LOG2_E

    def proj_chunk(rc, slab_lo, slab_hi):
        q = step * n_chunks + rc
        for cp in x_copies(q):
            cp.wait()

        @pl.when(q + X_AHEAD < total_chunks)
        def _():
            for cp in x_copies(q + X_AHEAD):
                cp.start()

        rows = pl.ds(pl.multiple_of(rc * row_chunk, row_chunk), row_chunk)
        cols = slice(slab_lo * LANES, slab_hi * LANES)
        xb = x_buf[lax.rem(q, X_SLOTS)].reshape(row_chunk, x_buf.shape[-1]).astype(BF16)
        res = jnp.dot(xb, w_s[:, cols], preferred_element_type=F32) + b_s[0:1, cols]
        cos = cos_ref[rows, :]
        sin = sin_ref[rows, :]
        for i, (g, kind) in enumerate(ATTN_SLAB_LAYOUT[slab_lo:slab_hi]):
            val = res[:, i * LANES:(i + 1) * LANES]
            if kind == GATE_SLAB:
                gate_s[rows, :] = (val * jax.nn.sigmoid(val)).astype(BF16)
                continue
            if kind in (0, 1):
                val = val * cos + pltpu.roll(val, LANES // 2, axis=1) * sin
            if kind == 0:
                val = val * q_scale
            ref = qkv_ref(g, kind)
            ref[rows, :] = val.astype(ref.dtype)

    def tiles(g, is_first_group):
        d = DILATIONS[g]
        R = CLASSES // d
        L = BAND // R
        nb = S // (d * BAND)

        def make_run(e):
            rho, n = divmod(e, nb)

            def run(j, nn):
                start = (rho + d * j) * A + L * nn
                return slice(start, start + L)
            return n, run

        def gather(ref, run, nn):
            return jnp.concatenate([ref[run(j, nn), :] for j in range(R)], axis=0)

        def key_window(ref, run, n):
            cur = gather(ref, run, n)
            if n == 0:
                return cur.astype(BF16)
            return jnp.concatenate([gather(ref, run, n - 1), cur], axis=0).astype(BF16)

        def scores(n, run):
            q_t = gather(qkv_ref(g, 0), run, n).astype(BF16)
            zero = jnp.zeros_like(q_t)
            qm = jnp.concatenate([jnp.where(head0_q, q_t, zero),
                                  jnp.where(head0_q, zero, q_t)], axis=0)
            kwin = key_window(qkv_ref(g, 1), run, n)
            s = lax.dot_general(qm, kwin, (((1,), (1,)), ((), ())),
                                preferred_element_type=F32)
            if n == 0:
                return s + bias_ref[g, 1, :, BAND:]
            return s + bias_ref[g, 0]

        def softmax(s):
            m_blk = jnp.max(s, axis=1, keepdims=True)
            p = jnp.exp2(s - m_blk)
            return p.astype(BF16), m_blk

        def finish(n, run, p, m_blk):
            vwin = key_window(qkv_ref(g, 2), run, n)
            keys = vwin.shape[0]
            zero_v = jnp.zeros_like(vwin)
            h0, ones_h0, ones_h1 = head_masks(keys)
            v_bd = jnp.concatenate([
                jnp.concatenate([jnp.where(h0, vwin, zero_v), ones_h0], axis=1),
                jnp.concatenate([jnp.where(h0, zero_v, vwin), ones_h1], axis=1)], axis=0)
            p_cat = jnp.concatenate([p[:BAND], p[BAND:]], axis=1)
            pv = jnp.dot(p_cat, v_bd, preferred_element_type=F32)
            o_t = pv[:, :LANES]
            m_t = jnp.where(head0_v, m_blk[:BAND], m_blk[BAND:])
            l_t = pv[:, LANES:]
            if not is_first_group:
                m_old = gather(m_s, run, n)
                m_new = jnp.maximum(m_old, m_t)
                w_old = jnp.exp2(m_old - m_new)
                w_new = jnp.exp2(m_t - m_new)
                m_t = m_new
                l_t = gather(l_s, run, n) * w_old + l_t * w_new
                o_t = gather(acc_s, run, n) * w_old + o_t * w_new
            for j in range(R):
                m_s[run(j, n), :] = m_t[j * L:(j + 1) * L]
                l_s[run(j, n), :] = l_t[j * L:(j + 1) * L]
                acc_s[run(j, n), :] = o_t[j * L:(j + 1) * L]

        for e in range(n_tiles):
            n, run = make_run(e)
            finish(n, run, *softmax(scores(n, run)))

    def proj_all(rc, carry):
        proj_chunk(rc, 0, N_ATTN_SLABS)
        return carry

    lax.fori_loop(0, S // row_chunk, proj_all, 0)
    for g in range(N_GROUPS):
        tiles(g, g == 0)

    for c in range(CLASSES):
        rows = slice(c * A, (c + 1) * A)
        res = (acc_s[rows, :] / l_s[rows, :]) * gate_s[rows, :].astype(F32)
        o_ref[pl.ds(c, A, stride=CLASSES), :] = res


def _attention(x, w_attn, b_attn):
    B, S, D = x.shape
    A = S // CLASSES
    row_chunk = ATTN_ROW_CHUNK
    kern = functools.partial(_attention_kernel, S=S, row_chunk=row_chunk)
    cos_t, sin_t = _rope_tables(S)
    const2 = pl.BlockSpec((S, LANES), lambda b, p: (0, 0), pipeline_mode=ONCE)
    scratch = [
        pltpu.VMEM((3, S, LANES), F32),
        pltpu.VMEM((3 * (N_GROUPS - 1), S, LANES), BF16),
        pltpu.VMEM((S, LANES), BF16),
        pltpu.VMEM((S, LANES), F32),
        pltpu.VMEM((S, LANES), F32),
        pltpu.VMEM((S, LANES), F32),
        pltpu.VMEM((X_SLOTS, row_chunk // A, A, D), F32),
        pltpu.SemaphoreType.DMA((X_SLOTS, row_chunk // A)),
    ]
    return pl.pallas_call(
        kern,
        grid=(B, N_PAIRS),
        in_specs=[
            pl.BlockSpec(memory_space=pl.ANY),
            pl.BlockSpec((None, D, ATTN_COLS), lambda b, p: (p, 0, 0)),
            pl.BlockSpec((None, SUBLANES, ATTN_COLS), lambda b, p: (p, 0, 0)),
            const2, const2,
            pl.BlockSpec((N_GROUPS, 2, 2 * BAND, 2 * BAND), lambda b, p: (0, 0, 0, 0),
                         pipeline_mode=ONCE),
        ],
        out_specs=pl.BlockSpec((None, S, LANES), lambda b, p: (b, 0, p)),
        out_shape=jax.ShapeDtypeStruct((B, S, ATTN_W), F32),
        scratch_shapes=scratch,
        compiler_params=_cparams(2),
        name="attention",
    )(x.reshape(B, A, CLASSES, D), w_attn, b_attn, cos_t, sin_t, _band_bias())


def _rglru_kernel(x_hbm, wu_ref, wl_ref, wr_ref, wi_ref, b_in_ref, cw_ref, cb_ref,
                  br_ref, bi_ref, lam_ref, o_ref,
                  w_s, wg_s, u_pad, gate_s, a_p, h_p, x_buf, x_sem, *, S, row_chunk, pitch):
    C = LRU_SLAB
    n_slabs = C // LANES
    n_chunks = S // row_chunk
    n_seg = SCAN_VREGS * SUBLANES
    seg_len = S // n_seg
    seg_per_chunk = row_chunk // seg_len
    u_pad[:, 0:SUBLANES, :] = jnp.zeros((n_slabs, SUBLANES, LANES), F32)

    def slab_cols(ref, col0=0):
        start = pl.multiple_of(col0 + pl.program_id(1) * C, LANES)
        return ref[:, pl.ds(start, C)]

    bu, bl = slab_cols(b_in_ref, COL_U), slab_cols(b_in_ref, COL_GATE_L)
    cw, cb = slab_cols(cw_ref), slab_cols(cb_ref)
    br, bi = slab_cols(br_ref), slab_cols(bi_ref)

    w_s[:, :C] = wu_ref[...].astype(BF16)
    w_s[:, C:] = wl_ref[...].astype(BF16)
    wg_s[...] = jnp.zeros(wg_s.shape, BF16)
    for blk in range(C // LRU_BLOCK):
        rows = slice(blk * LRU_BLOCK, (blk + 1) * LRU_BLOCK)
        wg_s[rows, blk * LRU_BLOCK:(blk + 1) * LRU_BLOCK] = wr_ref[blk].astype(BF16)
        wg_s[rows, C + blk * LRU_BLOCK:C + (blk + 1) * LRU_BLOCK] = wi_ref[blk].astype(BF16)

    step = pl.program_id(0) * pl.num_programs(1) + pl.program_id(1)
    total_chunks = pl.num_programs(0) * pl.num_programs(1) * n_chunks
    chunks_per_batch = pl.num_programs(1) * n_chunks

    def x_copy(q):
        b = lax.div(q, chunks_per_batch)
        rows = pl.ds(pl.multiple_of(lax.rem(q, n_chunks) * row_chunk, row_chunk), row_chunk)
        slot = lax.rem(q, X_SLOTS)
        return pltpu.make_async_copy(x_hbm.at[b, rows, :], x_buf.at[slot], x_sem.at[slot])

    @pl.when(step == 0)
    def _():
        for q in range(X_AHEAD):
            x_copy(q).start()

    def x_arrive(rc):
        q = step * n_chunks + rc
        x_copy(q).wait()

        @pl.when(q + X_AHEAD < total_chunks)
        def _():
            x_copy(q + X_AHEAD).start()

    def proj_chunk(rc, carry):
        r0 = pl.multiple_of(rc * row_chunk, row_chunk)
        q = step * n_chunks + rc
        xb = x_buf[lax.rem(q, X_SLOTS)].astype(BF16)
        res = jnp.dot(xb, w_s[...], preferred_element_type=F32)
        u = res[:, :C] + bu
        for s in range(n_slabs):
            u_pad[s, pl.ds(r0 + SUBLANES, row_chunk), :] = u[:, s * LANES:(s + 1) * LANES]
        gl = res[:, C:] + bl
        gate_s[pl.ds(r0, row_chunk), :] = gl * jax.nn.sigmoid(gl)
        return carry

    neg_lam = -slab_cols(lam_ref)
    softplus = jnp.maximum(neg_lam, 0.0) + jnp.log1p(jnp.exp(-jnp.abs(neg_lam)))
    decay_rate = LRU_C * softplus

    def gate_chunk(rc, carry):
        r0 = pl.multiple_of(rc * row_chunk, row_chunk)
        parts = []
        for s in range(n_slabs):
            lanes = slice(s * LANES, (s + 1) * LANES)
            acc = cb[:, lanes]
            for j in range(CONV_W):
                off = SUBLANES - (CONV_W - 1) + j
                acc = acc + u_pad[s, pl.ds(r0 + off, row_chunk), :] * cw[j:j + 1, lanes]
            parts.append(acc)
        uc = jnp.concatenate(parts, axis=1)
        gates = jnp.dot(uc.astype(BF16), wg_s[...], preferred_element_type=F32)
        r = jax.nn.sigmoid(gates[:, :C] + br)
        i = jax.nn.sigmoid(gates[:, C:] + bi)
        neg_log_a = r * decay_rate
        a = jnp.exp(-neg_log_a)
        one_minus_a2 = jnp.maximum(jnp.tanh(neg_log_a) * (a * a + 1.0), 0.0)
        mult = jnp.exp2(jnp.log(one_minus_a2) * (0.5 * LOG2_E))
        u = mult * (i * uc)
        for k in range(seg_per_chunk):
            start = (rc * seg_per_chunk + k) * pitch
            rows = slice(k * seg_len, (k + 1) * seg_len)
            for s in range(n_slabs):
                lanes = slice(s * LANES, (s + 1) * LANES)
                a_p[s, pl.ds(start, seg_len), :] = a[rows, lanes]
                h_p[s, pl.ds(start, seg_len), :] = u[rows, lanes]
        return carry

    def gate_then_proj(rc, carry):
        x_arrive(rc + 1)
        gate_chunk(rc, carry)
        return proj_chunk(rc + 1, carry)

    x_arrive(0)
    proj_chunk(0, 0)
    lax.fori_loop(0, n_chunks - 1, gate_then_proj, 0)
    gate_chunk(n_chunks - 1, 0)

    chains = [(s, v) for s in range(n_slabs) for v in range(SCAN_VREGS)]

    def scan_step(j, carry):
        out = []
        for c, (s, v) in enumerate(chains):
            rows = pl.ds(v * SUBLANES * pitch + j, SUBLANES, stride=pitch)
            h_prev, p_prev = carry[2 * c], carry[2 * c + 1]
            a = a_p[s, rows, :]
            h = a * h_prev + h_p[s, rows, :]
            p = a * p_prev
            h_p[s, rows, :] = h
            a_p[s, rows, :] = p
            out += [h, p]
        return tuple(out)

    init = (jnp.zeros((SUBLANES, LANES), F32), jnp.ones((SUBLANES, LANES), F32)) * len(chains)
    ends = lax.fori_loop(0, seg_len, scan_step, init, unroll=8)

    for s in range(n_slabs):
        lanes = slice(s * LANES, (s + 1) * LANES)
        h_in = jnp.zeros((1, LANES), F32)
        for seg in range(n_seg):
            c, sub = chains.index((s, seg // SUBLANES)), seg % SUBLANES
            h_end, p_end = ends[2 * c], ends[2 * c + 1]
            rows = slice(seg * pitch, seg * pitch + seg_len)
            h = h_p[s, rows, :] + a_p[s, rows, :] * h_in
            nat = slice(seg * seg_len, (seg + 1) * seg_len)
            o_ref[nat, lanes] = (h * gate_s[nat, lanes]).astype(BF16)
            h_in = h_end[sub:sub + 1] + p_end[sub:sub + 1] * h_in


def _rglru(x, w_in, b_in, conv_w, conv_b, lru_wr, lru_br, lru_wi, lru_bi, lam):
    B, S, D = x.shape
    C = LRU_SLAB
    n_slabs = LRU_W // C
    per = C // LRU_BLOCK
    n_seg = SCAN_VREGS * SUBLANES
    pitch = S // n_seg + SUBLANES // 2
    row_chunk = LRU_ROW_CHUNK
    kern = functools.partial(_rglru_kernel, S=S, row_chunk=row_chunk, pitch=pitch)
    col = lambda c0: (lambda b, c: (0, c0 // C + c))
    whole = lambda a: pl.BlockSpec(a.shape, lambda b, c: (0, 0), pipeline_mode=ONCE)
    blocks = lambda: pl.BlockSpec((per, LRU_BLOCK, LRU_BLOCK), lambda b, c: (c, 0, 0))
    return pl.pallas_call(
        kern,
        grid=(B, n_slabs),
        in_specs=[
            pl.BlockSpec(memory_space=pl.ANY),
            pl.BlockSpec((D, C), col(COL_U)), pl.BlockSpec((D, C), col(COL_GATE_L)),
            blocks(), blocks(),
            whole(b_in), whole(conv_w), whole(conv_b), whole(lru_br), whole(lru_bi), whole(lam),
        ],
        out_specs=pl.BlockSpec((None, S, C), lambda b, c: (b, 0, c)),
        out_shape=jax.ShapeDtypeStruct((B, S, LRU_W), BF16),
        scratch_shapes=[
            pltpu.VMEM((D, 2 * C), BF16),
            pltpu.VMEM((C, 2 * C), BF16),
            pltpu.VMEM((C // LANES, S + SUBLANES, LANES), F32),
            pltpu.VMEM((S, C), F32),
            pltpu.VMEM((C // LANES, n_seg * pitch, LANES), F32),
            pltpu.VMEM((C // LANES, n_seg * pitch, LANES), F32),
            pltpu.VMEM((X_SLOTS, row_chunk, D), F32),
            pltpu.SemaphoreType.DMA((X_SLOTS,)),
        ],
        compiler_params=_cparams(2),
        name="rglru",
    )(x, w_in, w_in, lru_wr, lru_wi, b_in, conv_w, conv_b, lru_br, lru_bi, lam)


def _merge_out_kernel(x_ref, attn_ref, hg_ref, wga_ref, wgb_ref, bga_ref, bgb_ref,
                      wa_ref, wl_ref, wo_ref, bo_ref, lng_ref, lnb_ref, o_ref,
                      wg_s, wa_s, wl_s, wo_s):
    D = x_ref.shape[1]

    @pl.when(pl.program_id(0) == 0)
    def _():
        rows = CAST_ROWS
        for r0 in range(0, D, rows):
            sl = slice(r0, r0 + rows)
            wg_s[sl, :D] = wga_ref[sl, :].astype(BF16)
            wg_s[sl, D:] = wgb_ref[sl, :].astype(BF16)
            wl_s[sl, :] = wl_ref[sl, :].astype(BF16)
            wo_s[sl, :] = wo_ref[sl, :].astype(BF16)
        for r0 in range(0, ATTN_W, rows):
            sl = slice(r0, r0 + rows)
            wa_s[sl, :] = wa_ref[sl, :].astype(BF16)

    half = x_ref.shape[0] // MERGE_SUBTILES
    for hb in range(MERGE_SUBTILES):
        rows = slice(hb * half, (hb + 1) * half)
        x = x_ref[rows, :]
        xb = x.astype(BF16)
        gate_a = jax.nn.sigmoid(
            jnp.dot(xb, wg_s[:, :D], preferred_element_type=F32) + bga_ref[...])
        gate_b = jax.nn.sigmoid(
            jnp.dot(xb, wg_s[:, D:], preferred_element_type=F32) + bgb_ref[...])
        y_a = jnp.dot(attn_ref[rows, :].astype(BF16), wa_s[...], preferred_element_type=F32)
        y_b = jnp.dot(hg_ref[rows, :], wl_s[...], preferred_element_type=F32)
        merged = gate_a * y_a + gate_b * y_b
        out = jnp.dot(merged.astype(BF16), wo_s[...], preferred_element_type=F32) + bo_ref[...]
        y = ALPHA * x + out
        mu = jnp.mean(y, axis=1, keepdims=True)
        yc = y - mu
        var = jnp.mean(yc * yc, axis=1, keepdims=True)
        o_ref[rows, :] = yc * lax.rsqrt(var + LN_EPS) * lng_ref[...] + lnb_ref[...]


def _merge_out(x2, attn2, hg2, w_in, b_in, w_a, w_l, w_o, b_o, ln_g, ln_b):
    T, D = x2.shape
    tm = MERGE_TILE_ROWS
    once = lambda shape, idx: pl.BlockSpec(shape, lambda i: idx, pipeline_mode=ONCE)
    merge_blk = COL_MERGE // D
    return pl.pallas_call(
        _merge_out_kernel,
        grid=(T // tm,),
        in_specs=[
            pl.BlockSpec((tm, D), lambda i: (i, 0)),
            pl.BlockSpec((tm, ATTN_W), lambda i: (i, 0)),
            pl.BlockSpec((tm, LRU_W), lambda i: (i, 0)),
            once((D, D), (0, merge_blk)), once((D, D), (0, merge_blk + 1)),
            once((1, D), (0, merge_blk)), once((1, D), (0, merge_blk + 1)),
            once((ATTN_W, D), (0, 0)), once((LRU_W, D), (0, 0)), once((D, D), (0, 0)),
            once((1, D), (0, 0)), once((1, D), (0, 0)), once((1, D), (0, 0)),
        ],
        out_specs=pl.BlockSpec((tm, D), lambda i: (i, 0)),
        out_shape=jax.ShapeDtypeStruct((T, D), F32),
        scratch_shapes=[
            pltpu.VMEM((D, 2 * D), BF16), pltpu.VMEM((ATTN_W, D), BF16),
            pltpu.VMEM((LRU_W, D), BF16), pltpu.VMEM((D, D), BF16),
        ],
        compiler_params=_cparams(1),
        name="merge_out",
    )(x2, attn2, hg2, w_in, w_in, b_in, b_in, w_a, w_l, w_o, b_o, ln_g, ln_b)


def kernel(x, w_in, b_in, conv_w, conv_b, lru_wr, lru_br, lru_wi, lru_bi, lru_lambda,
           w_attn_proj, w_lru_proj, w_out, b_out, ln_gain, ln_bias):
    B, S, D = x.shape
    assert w_in.shape[0] == DEPTH and S % (CLASSES * BAND) == 0 and D == D_MODEL
    assert COL_MERGE % D == 0 and w_in.shape[2] == COL_MERGE + 2 * D
    row = lambda v: v.reshape(1, -1)
    for l in range(DEPTH):
        wl, bl = w_in[l], row(b_in[l])
        attn = _attention(x, *_attn_weights(wl, bl))
        hg = _rglru(x, wl, bl, conv_w[l], row(conv_b[l]), lru_wr[l], row(lru_br[l]),
                    lru_wi[l], row(lru_bi[l]), row(lru_lambda[l]))
        out = _merge_out(
            x.reshape(B * S, D), attn.reshape(B * S, ATTN_W), hg.reshape(B * S, LRU_W),
            wl, bl, w_attn_proj[l], w_lru_proj[l], w_out[l],
            row(b_out[l]), row(ln_gain[l]), row(ln_bias[l]))
        x = out.reshape(B, S, D)
    return x
```

```python
import functools
import math

import jax
import jax.numpy as jnp
import numpy as np
from jax import lax
from jax.experimental import pallas as pl
from jax.experimental.pallas import tpu as pltpu

LANES = 128
SUBLANES = 8
BF16_ROWS = 16
VMEM_LIMIT_BYTES = 56 * 1024 * 1024

D_MODEL = 1024
HEAD_DIM = 64
HALF = HEAD_DIM // 2
HEADS_PER_GROUP = 8
DILATIONS = (1, 4, 16)
N_GROUPS = len(DILATIONS)
BAND = 128
CLASSES = 16
QKV_W = N_GROUPS * HEADS_PER_GROUP * HEAD_DIM
ATTN_W = HEADS_PER_GROUP * HEAD_DIM
N_PAIRS = ATTN_W // LANES
LRU_W = 1024
LRU_BLOCK = 64
LRU_SLAB = 256
CONV_W = 4
LRU_C = 8.0
ROPE_THETA = 10000.0
LOG2_E = math.log2(math.e)
NEG_INF = -1e30
LN_EPS = 1e-5
DEPTH = 1
ALPHA = (2.0 * DEPTH) ** 0.25
GATE_SLAB = 3
ATTN_SLAB_LAYOUT = ([(0, 0), (0, 1), (0, 2), (None, GATE_SLAB)]
                    + [(g, kind) for g in range(1, N_GROUPS) for kind in range(3)])
N_ATTN_SLABS = len(ATTN_SLAB_LAYOUT)
ATTN_COLS = N_ATTN_SLABS * LANES
COL_K = QKV_W
COL_V = 2 * QKV_W
COL_GATE_A = 3 * QKV_W
COL_U = COL_GATE_A + ATTN_W
COL_GATE_L = COL_U + LRU_W
COL_MERGE = COL_GATE_L + LRU_W

ATTN_ROW_CHUNK = 1024
LRU_ROW_CHUNK = 1024
MERGE_TILE_ROWS = 512
MERGE_SUBTILES = 2
CAST_ROWS = 256
X_AHEAD = 2
X_SLOTS = X_AHEAD + 1
SCAN_VREGS = 2

BF16 = jnp.bfloat16
F32 = jnp.float32
ONCE = pl.Buffered(1)


def _cparams(n_axes, flags=None):
    return pltpu.CompilerParams(dimension_semantics=("arbitrary",) * n_axes,
                                vmem_limit_bytes=VMEM_LIMIT_BYTES, flags=flags)


def _band_bias():
    out = np.zeros((N_GROUPS, 2, 2 * BAND, 2 * BAND), np.float32)
    i = np.arange(BAND)
    for g, d in enumerate(DILATIONS):
        R = CLASSES // d
        L = BAND // R
        m = R * (i % L) + i // L
        m_k = np.concatenate([m - BAND, m])
        dist = m[:, None] - m_k[None, :]
        band = (dist >= 0) & (dist <= BAND)
        for first in range(2):
            valid = band & ((np.arange(2 * BAND) >= BAND)[None, :] | (first == 0))
            bias = np.where(valid, 0.0, NEG_INF).astype(np.float32)
            out[g, first] = np.concatenate([bias, bias], axis=0)
    return out


def _rope_tables(S):
    A = S // CLASSES
    r = np.arange(S)
    pos = (CLASSES * (r % A) + r // A).astype(np.float64)
    inv_freq = ROPE_THETA ** (-np.arange(HALF, dtype=np.float64) / HALF)
    ang = pos[:, None] * inv_freq[None, :]
    cos_t = np.tile(np.cos(ang), (1, LANES // HALF))
    sin = np.sin(ang)
    sin_t = np.concatenate([-sin, -sin, sin, sin], axis=1)
    return cos_t.astype(np.float32), sin_t.astype(np.float32)


def _pair_interleave(w):
    lane = lax.broadcasted_iota(jnp.int32, w.shape, 1)
    from_right = pltpu.roll(w, LANES - HALF, axis=1)
    from_left = pltpu.roll(w, HALF, axis=1)
    quarter = lane // HALF
    return jnp.where(quarter == 1, from_right, jnp.where(quarter == 2, from_left, w))


def _attn_weights_kernel(*refs):
    w_refs = refs[:N_ATTN_SLABS]
    b_refs = refs[N_ATTN_SLABS:2 * N_ATTN_SLABS]
    w_out, b_out = refs[2 * N_ATTN_SLABS:]
    D = w_out.shape[0]
    w_rows = CAST_ROWS
    for i, (g, kind) in enumerate(ATTN_SLAB_LAYOUT):
        is_qk = kind in (0, 1)
        cols = slice(i * LANES, (i + 1) * LANES)
        for r0 in range(0, D, w_rows):
            w = w_refs[i][r0:r0 + w_rows, :]
            w_out[r0:r0 + w_rows, cols] = (_pair_interleave(w) if is_qk else w).astype(BF16)
        b = jnp.broadcast_to(b_refs[i][...], (SUBLANES, LANES))
        b_out[:, cols] = _pair_interleave(b) if is_qk else b


def _attn_weights(w_in, b_in):
    D = w_in.shape[0]
    kind_col = (0, COL_K, COL_V)
    slab0 = [COL_GATE_A // LANES if kind == GATE_SLAB else (kind_col[kind] + g * ATTN_W) // LANES
             for g, kind in ATTN_SLAB_LAYOUT]
    w_specs = [pl.BlockSpec((D, LANES), functools.partial(lambda p, s: (0, s + p), s=s))
               for s in slab0]
    b_specs = [pl.BlockSpec((1, LANES), functools.partial(lambda p, s: (0, s + p), s=s))
               for s in slab0]
    return pl.pallas_call(
        _attn_weights_kernel,
        grid=(N_PAIRS,),
        in_specs=w_specs + b_specs,
        out_specs=[pl.BlockSpec((None, D, ATTN_COLS), lambda p: (p, 0, 0)),
                   pl.BlockSpec((None, SUBLANES, ATTN_COLS), lambda p: (p, 0, 0))],
        out_shape=[jax.ShapeDtypeStruct((N_PAIRS, D, ATTN_COLS), BF16),
                   jax.ShapeDtypeStruct((N_PAIRS, SUBLANES, ATTN_COLS), F32)],
        compiler_params=_cparams(1),
        name="attn_weights",
    )(*([w_in] * N_ATTN_SLABS), *([b_in] * N_ATTN_SLABS))


def _attention_kernel(x_hbm, w_s, b_s, cos_ref, sin_ref, bias_ref, o_ref,
                      qkv0_s, qkv_s, gate_s, acc_s, m_s, l_s, x_buf, x_sem, *, S, row_chunk):
    A = S // CLASSES
    per = row_chunk // A
    n_chunks = S // row_chunk

    step = pl.program_id(0) * pl.num_programs(1) + pl.program_id(1)
    total_chunks = pl.num_programs(0) * pl.num_programs(1) * n_chunks

    def x_copies(q):
        b = lax.div(q, pl.num_programs(1) * n_chunks)
        c0 = lax.rem(q, n_chunks) * per
        slot = lax.rem(q, X_SLOTS)
        return [pltpu.make_async_copy(x_hbm.at[b, :, c0 + i, :], x_buf.at[slot, i],
                                      x_sem.at[slot, i]) for i in range(per)]

    @pl.when(step == 0)
    def _():
        for q0 in range(X_AHEAD):
            for i, cp in enumerate(x_copies(q0)):
                cp.start(priority=i % 2)

    def qkv_ref(g, i):
        return qkv0_s.at[i] if g == 0 else qkv_s.at[3 * (g - 1) + i]

    n_tiles = S // BAND
    lane = lax.broadcasted_iota(jnp.int32, (BAND, LANES), 1)
    head0_q = ((lane // HALF) % 2) == 0
    head0_v = lane < HEAD_DIM

    def head_masks(keys):
        h0 = lax.broadcasted_iota(jnp.int32, (keys, LANES), 1) < HEAD_DIM
        return h0, h0.astype(BF16), 1 - h0.astype(BF16)
    q_scale = (HEAD_DIM ** -0.5) * LOG2_E

    def proj_chunk(rc, slab_lo, slab_hi):
        q = step * n_chunks + rc
        for cp in x_copies(q):
            cp.wait()

        @pl.when(q + X_AHEAD < total_chunks)
        def _():
            for i, cp in enumerate(x_copies(q + X_AHEAD)):
                cp.start(priority=i % 2)

        rows = pl.ds(pl.multiple_of(rc * row_chunk, row_chunk), row_chunk)
        cols = slice(slab_lo * LANES, slab_hi * LANES)
        xb = x_buf[lax.rem(q, X_SLOTS)].reshape(row_chunk, x_buf.shape[-1]).astype(BF16)
        res = jnp.dot(xb, w_s[:, cols], preferred_element_type=F32) + b_s[0:1, cols]
        cos = cos_ref[rows, :]
        sin = sin_ref[rows, :]
        for i, (g, kind) in enumerate(ATTN_SLAB_LAYOUT[slab_lo:slab_hi]):
            val = res[:, i * LANES:(i + 1) * LANES]
            if kind == GATE_SLAB:
                gate_s[rows, :] = (val * jax.nn.sigmoid(val)).astype(BF16)
                continue
            if kind in (0, 1):
                val = val * cos + pltpu.roll(val, LANES // 2, axis=1) * sin
            if kind == 0:
                val = val * q_scale
            ref = qkv_ref(g, kind)
            ref[rows, :] = val.astype(ref.dtype)

    def tiles(g, is_first_group):
        d = DILATIONS[g]
        R = CLASSES // d
        L = BAND // R
        nb = S // (d * BAND)

        def make_run(e):
            rho, n = divmod(e, nb)

            def run(j, nn):
                start = (rho + d * j) * A + L * nn
                return slice(start, start + L)
            return n, run

        def gather(ref, run, nn):
            return jnp.concatenate([ref[run(j, nn), :] for j in range(R)], axis=0)

        def key_window(ref, run, n):
            cur = gather(ref, run, n)
            if n == 0:
                return cur.astype(BF16)
            return jnp.concatenate([gather(ref, run, n - 1), cur], axis=0).astype(BF16)

        def scores(n, run):
            q_t = gather(qkv_ref(g, 0), run, n).astype(BF16)
            zero = jnp.zeros_like(q_t)
            qm = jnp.concatenate([jnp.where(head0_q, q_t, zero),
                                  jnp.where(head0_q, zero, q_t)], axis=0)
            kwin = key_window(qkv_ref(g, 1), run, n)
            s = lax.dot_general(qm, kwin, (((1,), (1,)), ((), ())),
                                preferred_element_type=F32)
            if n == 0:
                return s + bias_ref[g, 1, :, BAND:]
            return s + bias_ref[g, 0]

        def softmax(s):
            m_blk = jnp.max(s, axis=1, keepdims=True)
            p = jnp.exp2(s - m_blk)
            return p.astype(BF16), m_blk

        def finish(n, run, p, m_blk):
            vwin = key_window(qkv_ref(g, 2), run, n)
            keys = vwin.shape[0]
            zero_v = jnp.zeros_like(vwin)
            h0, ones_h0, ones_h1 = head_masks(keys)
            v_bd = jnp.concatenate([
                jnp.concatenate([jnp.where(h0, vwin, zero_v), ones_h0], axis=1),
                jnp.concatenate([jnp.where(h0, zero_v, vwin), ones_h1], axis=1)], axis=0)
            p_cat = jnp.concatenate([p[:BAND], p[BAND:]], axis=1)
            pv = jnp.dot(p_cat, v_bd, preferred_element_type=F32)
            o_t = pv[:, :LANES]
            m_t = jnp.where(head0_v, m_blk[:BAND], m_blk[BAND:])
            l_t = pv[:, LANES:]
            if not is_first_group:
                m_old = gather(m_s, run, n)
                m_new = jnp.maximum(m_old, m_t)
                w_old = jnp.exp2(m_old - m_new)
                w_new = jnp.exp2(m_t - m_new)
                m_t = m_new
                l_t = gather(l_s, run, n) * w_old + l_t * w_new
                o_t = gather(acc_s, run, n) * w_old + o_t * w_new
            for j in range(R):
                m_s[run(j, n), :] = m_t[j * L:(j + 1) * L]
                l_s[run(j, n), :] = l_t[j * L:(j + 1) * L]
                acc_s[run(j, n), :] = o_t[j * L:(j + 1) * L]

        for e in range(n_tiles):
            n, run = make_run(e)
            finish(n, run, *softmax(scores(n, run)))

    def proj_all(rc, carry):
        proj_chunk(rc, 0, N_ATTN_SLABS)
        return carry

    lax.fori_loop(0, S // row_chunk, proj_all, 0)
    for g in range(N_GROUPS):
        tiles(g, g == 0)

    for c in range(CLASSES):
        rows = slice(c * A, (c + 1) * A)
        res = (acc_s[rows, :] / l_s[rows, :]) * gate_s[rows, :].astype(F32)
        o_ref[pl.ds(c, A, stride=CLASSES), :] = res


def _attention(x, w_attn, b_attn):
    B, S, D = x.shape
    A = S // CLASSES
    row_chunk = ATTN_ROW_CHUNK
    kern = functools.partial(_attention_kernel, S=S, row_chunk=row_chunk)
    cos_t, sin_t = _rope_tables(S)
    const2 = pl.BlockSpec((S, LANES), lambda b, p: (0, 0), pipeline_mode=ONCE)
    scratch = [
        pltpu.VMEM((3, S, LANES), F32),
        pltpu.VMEM((3 * (N_GROUPS - 1), S, LANES), BF16),
        pltpu.VMEM((S, LANES), BF16),
        pltpu.VMEM((S, LANES), F32),
        pltpu.VMEM((S, LANES), F32),
        pltpu.VMEM((S, LANES), F32),
        pltpu.VMEM((X_SLOTS, row_chunk // A, A, D), F32),
        pltpu.SemaphoreType.DMA((X_SLOTS, row_chunk // A)),
    ]
    return pl.pallas_call(
        kern,
        grid=(B, N_PAIRS),
        in_specs=[
            pl.BlockSpec(memory_space=pl.ANY),
            pl.BlockSpec((None, D, ATTN_COLS), lambda b, p: (p, 0, 0)),
            pl.BlockSpec((None, SUBLANES, ATTN_COLS), lambda b, p: (p, 0, 0)),
            const2, const2,
            pl.BlockSpec((N_GROUPS, 2, 2 * BAND, 2 * BAND), lambda b, p: (0, 0, 0, 0),
                         pipeline_mode=ONCE),
        ],
        out_specs=pl.BlockSpec((None, S, LANES), lambda b, p: (b, 0, p)),
        out_shape=jax.ShapeDtypeStruct((B, S, ATTN_W), F32),
        scratch_shapes=scratch,
        compiler_params=_cparams(2),
        name="attention",
    )(x.reshape(B, A, CLASSES, D), w_attn, b_attn, cos_t, sin_t, _band_bias())


def _rglru_kernel(x_hbm, wu_ref, wl_ref, wr_ref, wi_ref, b_in_ref, cw_ref, cb_ref,
                  br_ref, bi_ref, lam_ref, o_ref,
                  w_s, wg_s, u_pad, gate_s, a_p, h_p, x_buf, x_sem, *, S, row_chunk, pitch):
    C = LRU_SLAB
    n_slabs = C // LANES
    n_chunks = S // row_chunk
    n_seg = SCAN_VREGS * SUBLANES
    seg_len = S // n_seg
    seg_per_chunk = row_chunk // seg_len
    u_pad[:, 0:SUBLANES, :] = jnp.zeros((n_slabs, SUBLANES, LANES), F32)

    def slab_cols(ref, col0=0):
        start = pl.multiple_of(col0 + pl.program_id(1) * C, LANES)
        return ref[:, pl.ds(start, C)]

    bu, bl = slab_cols(b_in_ref, COL_U), slab_cols(b_in_ref, COL_GATE_L)
    cw, cb = slab_cols(cw_ref), slab_cols(cb_ref)
    br, bi = slab_cols(br_ref), slab_cols(bi_ref)

    w_s[:, :C] = wu_ref[...].astype(BF16)
    w_s[:, C:] = wl_ref[...].astype(BF16)
    wg_s[...] = jnp.zeros(wg_s.shape, BF16)
    for blk in range(C // LRU_BLOCK):
        rows = slice(blk * LRU_BLOCK, (blk + 1) * LRU_BLOCK)
        wg_s[rows, blk * LRU_BLOCK:(blk + 1) * LRU_BLOCK] = wr_ref[blk].astype(BF16)
        wg_s[rows, C + blk * LRU_BLOCK:C + (blk + 1) * LRU_BLOCK] = wi_ref[blk].astype(BF16)

    step = pl.program_id(0) * pl.num_programs(1) + pl.program_id(1)
    total_chunks = pl.num_programs(0) * pl.num_programs(1) * n_chunks
    chunks_per_batch = pl.num_programs(1) * n_chunks

    def x_copy(q):
        b = lax.div(q, chunks_per_batch)
        rows = pl.ds(pl.multiple_of(lax.rem(q, n_chunks) * row_chunk, row_chunk), row_chunk)
        slot = lax.rem(q, X_SLOTS)
        return pltpu.make_async_copy(x_hbm.at[b, rows, :], x_buf.at[slot], x_sem.at[slot])

    @pl.when(step == 0)
    def _():
        for q in range(X_AHEAD):
            x_copy(q).start()

    def x_arrive(rc):
        q = step * n_chunks + rc
        x_copy(q).wait()

        @pl.when(q + X_AHEAD < total_chunks)
        def _():
            x_copy(q + X_AHEAD).start()

    def proj_chunk(rc, carry):
        r0 = pl.multiple_of(rc * row_chunk, row_chunk)
        q = step * n_chunks + rc
        xb = x_buf[lax.rem(q, X_SLOTS)].astype(BF16)
        res = jnp.dot(xb, w_s[...], preferred_element_type=F32)
        u = res[:, :C] + bu
        for s in range(n_slabs):
            u_pad[s, pl.ds(r0 + SUBLANES, row_chunk), :] = u[:, s * LANES:(s + 1) * LANES]
        gl = res[:, C:] + bl
        gate_s[pl.ds(r0, row_chunk), :] = gl * jax.nn.sigmoid(gl)
        return carry

    neg_lam = -slab_cols(lam_ref)
    softplus = jnp.maximum(neg_lam, 0.0) + jnp.log1p(jnp.exp(-jnp.abs(neg_lam)))
    decay_rate = LRU_C * softplus

    def gate_chunk(rc, carry):
        r0 = pl.multiple_of(rc * row_chunk, row_chunk)
        parts = []
        for s in range(n_slabs):
            lanes = slice(s * LANES, (s + 1) * LANES)
            acc = cb[:, lanes]
            for j in range(CONV_W):
                off = SUBLANES - (CONV_W - 1) + j
                acc = acc + u_pad[s, pl.ds(r0 + off, row_chunk), :] * cw[j:j + 1, lanes]
            parts.append(acc)
        uc = jnp.concatenate(parts, axis=1)
        gates = jnp.dot(uc.astype(BF16), wg_s[...], preferred_element_type=F32)
        r = jax.nn.sigmoid(gates[:, :C] + br)
        i = jax.nn.sigmoid(gates[:, C:] + bi)
        neg_log_a = r * decay_rate
        a = jnp.exp(-neg_log_a)
        one_minus_a2 = jnp.maximum(jnp.tanh(neg_log_a) * (a * a + 1.0), 0.0)
        mult = jnp.exp2(jnp.log(one_minus_a2) * (0.5 * LOG2_E))
        u = mult * (i * uc)
        for k in range(seg_per_chunk):
            start = (rc * seg_per_chunk + k) * pitch
            rows = slice(k * seg_len, (k + 1) * seg_len)
            for s in range(n_slabs):
                lanes = slice(s * LANES, (s + 1) * LANES)
                a_p[s, pl.ds(start, seg_len), :] = a[rows, lanes]
                h_p[s, pl.ds(start, seg_len), :] = u[rows, lanes]
        return carry

    def gate_then_proj(rc, carry):
        x_arrive(rc + 1)
        gate_chunk(rc, carry)
        return proj_chunk(rc + 1, carry)

    x_arrive(0)
    proj_chunk(0, 0)
    lax.fori_loop(0, n_chunks - 1, gate_then_proj, 0)
    gate_chunk(n_chunks - 1, 0)

    chains = [(s, v) for s in range(n_slabs) for v in range(SCAN_VREGS)]

    def scan_step(j, carry):
        out = []
        for c, (s, v) in enumerate(chains):
            rows = pl.ds(v * SUBLANES * pitch + j, SUBLANES, stride=pitch)
            h_prev, p_prev = carry[2 * c], carry[2 * c + 1]
            a = a_p[s, rows, :]
            h = a * h_prev + h_p[s, rows, :]
            p = a * p_prev
            h_p[s, rows, :] = h
            a_p[s, rows, :] = p
            out += [h, p]
        return tuple(out)

    init = (jnp.zeros((SUBLANES, LANES), F32), jnp.ones((SUBLANES, LANES), F32)) * len(chains)
    ends = lax.fori_loop(0, seg_len, scan_step, init, unroll=8)

    for s in range(n_slabs):
        lanes = slice(s * LANES, (s + 1) * LANES)
        h_in = jnp.zeros((1, LANES), F32)
        for seg in range(n_seg):
            c, sub = chains.index((s, seg // SUBLANES)), seg % SUBLANES
            h_end, p_end = ends[2 * c], ends[2 * c + 1]
            rows = slice(seg * pitch, seg * pitch + seg_len)
            h = h_p[s, rows, :] + a_p[s, rows, :] * h_in
            nat = slice(seg * seg_len, (seg + 1) * seg_len)
            o_ref[nat, lanes] = (h * gate_s[nat, lanes]).astype(BF16)
            h_in = h_end[sub:sub + 1] + p_end[sub:sub + 1] * h_in


def _rglru(x, w_in, b_in, conv_w, conv_b, lru_wr, lru_br, lru_wi, lru_bi, lam):
    B, S, D = x.shape
    C = LRU_SLAB
    n_slabs = LRU_W // C
    per = C // LRU_BLOCK
    n_seg = SCAN_VREGS * SUBLANES
    pitch = S // n_seg + SUBLANES // 2
    row_chunk = LRU_ROW_CHUNK
    kern = functools.partial(_rglru_kernel, S=S, row_chunk=row_chunk, pitch=pitch)
    col = lambda c0: (lambda b, c: (0, c0 // C + c))
    whole = lambda a: pl.BlockSpec(a.shape, lambda b, c: (0, 0), pipeline_mode=ONCE)
    blocks = lambda: pl.BlockSpec((per, LRU_BLOCK, LRU_BLOCK), lambda b, c: (c, 0, 0))
    return pl.pallas_call(
        kern,
        grid=(B, n_slabs),
        in_specs=[
            pl.BlockSpec(memory_space=pl.ANY),
            pl.BlockSpec((D, C), col(COL_U)), pl.BlockSpec((D, C), col(COL_GATE_L)),
            blocks(), blocks(),
            whole(b_in), whole(conv_w), whole(conv_b), whole(lru_br), whole(lru_bi), whole(lam),
        ],
        out_specs=pl.BlockSpec((None, S, C), lambda b, c: (b, 0, c)),
        out_shape=jax.ShapeDtypeStruct((B, S, LRU_W), BF16),
        scratch_shapes=[
            pltpu.VMEM((D, 2 * C), BF16),
            pltpu.VMEM((C, 2 * C), BF16),
            pltpu.VMEM((C // LANES, S + SUBLANES, LANES), F32),
            pltpu.VMEM((S, C), F32),
            pltpu.VMEM((C // LANES, n_seg * pitch, LANES), F32),
            pltpu.VMEM((C // LANES, n_seg * pitch, LANES), F32),
            pltpu.VMEM((X_SLOTS, row_chunk, D), F32),
            pltpu.SemaphoreType.DMA((X_SLOTS,)),
        ],
        compiler_params=_cparams(2),
        name="rglru",
    )(x, w_in, w_in, lru_wr, lru_wi, b_in, conv_w, conv_b, lru_br, lru_bi, lam)


def _merge_out_kernel(x_ref, attn_ref, hg_ref, wga_ref, wgb_ref, bga_ref, bgb_ref,
                      wa_ref, wl_ref, wo_ref, bo_ref, lng_ref, lnb_ref, o_ref,
                      wg_s, wa_s, wl_s, wo_s):
    D = x_ref.shape[1]

    @pl.when(pl.program_id(0) == 0)
    def _():
        rows = CAST_ROWS
        for r0 in range(0, D, rows):
            sl = slice(r0, r0 + rows)
            wg_s[sl, :D] = wga_ref[sl, :].astype(BF16)
            wg_s[sl, D:] = wgb_ref[sl, :].astype(BF16)
            wl_s[sl, :] = wl_ref[sl, :].astype(BF16)
            wo_s[sl, :] = wo_ref[sl, :].astype(BF16)
        for r0 in range(0, ATTN_W, rows):
            sl = slice(r0, r0 + rows)
            wa_s[sl, :] = wa_ref[sl, :].astype(BF16)

    half = x_ref.shape[0] // MERGE_SUBTILES
    for hb in range(MERGE_SUBTILES):
        rows = slice(hb * half, (hb + 1) * half)
        x = x_ref[rows, :]
        xb = x.astype(BF16)
        gate_a = jax.nn.sigmoid(
            jnp.dot(xb, wg_s[:, :D], preferred_element_type=F32) + bga_ref[...])
        gate_b = jax.nn.sigmoid(
            jnp.dot(xb, wg_s[:, D:], preferred_element_type=F32) + bgb_ref[...])
        y_a = jnp.dot(attn_ref[rows, :].astype(BF16), wa_s[...], preferred_element_type=F32)
        y_b = jnp.dot(hg_ref[rows, :], wl_s[...], preferred_element_type=F32)
        merged = gate_a * y_a + gate_b * y_b
        out = jnp.dot(merged.astype(BF16), wo_s[...], preferred_element_type=F32) + bo_ref[...]
        y = ALPHA * x + out
        mu = jnp.mean(y, axis=1, keepdims=True)
        yc = y - mu
        var = jnp.mean(yc * yc, axis=1, keepdims=True)
        o_ref[rows, :] = yc * lax.rsqrt(var + LN_EPS) * lng_ref[...] + lnb_ref[...]


def _merge_out(x2, attn2, hg2, w_in, b_in, w_a, w_l, w_o, b_o, ln_g, ln_b):
    T, D = x2.shape
    tm = MERGE_TILE_ROWS
    once = lambda shape, idx: pl.BlockSpec(shape, lambda i: idx, pipeline_mode=ONCE)
    merge_blk = COL_MERGE // D
    return pl.pallas_call(
        _merge_out_kernel,
        grid=(T // tm,),
        in_specs=[
            pl.BlockSpec((tm, D), lambda i: (i, 0)),
            pl.BlockSpec((tm, ATTN_W), lambda i: (i, 0)),
            pl.BlockSpec((tm, LRU_W), lambda i: (i, 0)),
            once((D, D), (0, merge_blk)), once((D, D), (0, merge_blk + 1)),
            once((1, D), (0, merge_blk)), once((1, D), (0, merge_blk + 1)),
            once((ATTN_W, D), (0, 0)), once((LRU_W, D), (0, 0)), once((D, D), (0, 0)),
            once((1, D), (0, 0)), once((1, D), (0, 0)), once((1, D), (0, 0)),
        ],
        out_specs=pl.BlockSpec((tm, D), lambda i: (i, 0)),
        out_shape=jax.ShapeDtypeStruct((T, D), F32),
        scratch_shapes=[
            pltpu.VMEM((D, 2 * D), BF16), pltpu.VMEM((ATTN_W, D), BF16),
            pltpu.VMEM((LRU_W, D), BF16), pltpu.VMEM((D, D), BF16),
        ],
        compiler_params=_cparams(1),
        name="merge_out",
    )(x2, attn2, hg2, w_in, w_in, b_in, b_in, w_a, w_l, w_o, b_o, ln_g, ln_b)


def kernel(x, w_in, b_in, conv_w, conv_b, lru_wr, lru_br, lru_wi, lru_bi, lru_lambda,
           w_attn_proj, w_lru_proj, w_out, b_out, ln_gain, ln_bias):
    B, S, D = x.shape
    assert w_in.shape[0] == DEPTH and S % (CLASSES * BAND) == 0 and D == D_MODEL
    assert COL_MERGE % D == 0 and w_in.shape[2] == COL_MERGE + 2 * D
    row = lambda v: v.reshape(1, -1)
    for l in range(DEPTH):
        wl, bl = w_in[l], row(b_in[l])
        attn = _attention(x, *_attn_weights(wl, bl))
        hg = _rglru(x, wl, bl, conv_w[l], row(conv_b[l]), lru_wr[l], row(lru_br[l]),
                    lru_wi[l], row(lru_bi[l]), row(lru_lambda[l]))
        out = _merge_out(
            x.reshape(B * S, D), attn.reshape(B * S, ATTN_W), hg.reshape(B * S, LRU_W),
            wl, bl, w_attn_proj[l], w_lru_proj[l], w_out[l],
            row(b_out[l]), row(ln_gain[l]), row(ln_bias[l]))
        x = out.reshape(B, S, D)
    return x
```
